```python
import math
import jax
import jax.numpy as jnp
from jax import lax
import numpy as np

D_MODEL = 2048
BATCH = 2
SEQ = 4096
DEPTH = 4

SSM_GROUPS = 32
SSM_CH = 16
SSM_STATE = 64
SSM_WIDTH = SSM_GROUPS * SSM_CH
SSM_DT_MIN = 1e-3
SSM_DT_MAX = 1e-1
DN_HEADS = 6
DN_HEAD_DIM = 128
DN_WIDTH = DN_HEADS * DN_HEAD_DIM
DN_CONV = 4
DN_CHUNK = 64
ATTN_HEADS = 6
ATTN_HEAD_DIM = 128
ATTN_WIDTH = ATTN_HEADS * ATTN_HEAD_DIM
DILATED_PAIRS = ((128, 1), (512, 4), (2048, 16))
ATTN_BLOCK = 128
N_BUCKETS = 32
REL_MAX_DIST = 2048
D_MIX = SSM_WIDTH + DN_WIDTH + ATTN_WIDTH
IN_SPLITS = (SSM_WIDTH, ATTN_WIDTH, ATTN_WIDTH, ATTN_WIDTH, 3 * DN_WIDTH, DN_WIDTH, DN_HEADS, DN_HEADS)
N_IN_COLS = SSM_WIDTH + 3 * ATTN_WIDTH + 4 * DN_WIDTH + 2 * DN_HEADS
D_FF = 5632
NORM_EPS = 1e-6
NEG_INF = -1e30

kernel_name = 'hybrid_s5_gdn_dilated_macaron'


def rms_norm(x, gain):
    xf = x.astype(jnp.float32)
    y = xf * lax.rsqrt(jnp.mean(xf * xf, axis=-1, keepdims=True) + NORM_EPS)
    return (y * gain.astype(jnp.float32)).astype(x.dtype)


def l2_normalize(x):
    return x * lax.rsqrt(jnp.sum(x * x, axis=-1, keepdims=True) + NORM_EPS)


def swiglu(h, w_gate, w_up, w_down):
    return (jax.nn.silu(h @ w_gate) * (h @ w_up)) @ w_down


def causal_depthwise_conv(x, w):
    k_width, channels = w.shape
    return lax.conv_general_dilated(x, w[:, None, :], window_strides=(1,), padding=((k_width - 1, 0),),
                                    dimension_numbers=('NWC', 'WIO', 'NWC'), feature_group_count=channels)


def s5_layer(u, lam_re, lam_im, b_re, b_im, c_re, c_im, d_skip, log_dt, glu_w, glu_b):
    bsz, seq, _ = u.shape
    f32 = jnp.float32
    uf = u.astype(f32).reshape(bsz, seq, SSM_GROUPS, SSM_CH)
    lam = lax.complex(lam_re.astype(f32), lam_im.astype(f32))
    dt = jnp.exp(log_dt.astype(f32))[:, None]
    lam_bar = jnp.exp(lam * dt)
    b = lax.complex(b_re.astype(f32), b_im.astype(f32))
    b_bar = ((lam_bar - 1.0) / lam)[..., None] * b
    c = lax.complex(c_re.astype(f32), c_im.astype(f32))
    bu = jnp.einsum('gpc,bsgc->bsgp', b_bar, uf.astype(jnp.complex64))
    a = jnp.broadcast_to(lam_bar, bu.shape)

    def combine(e1, e2):
        a1, b1 = e1
        a2, b2 = e2
        return a1 * a2, a2 * b1 + b2

    _, states = lax.associative_scan(combine, (a, bu), axis=1)
    y = jnp.einsum('gcp,bsgp->bsgc', c, states).real + d_skip.astype(f32).reshape(SSM_GROUPS, SSM_CH) * uf
    y = jax.nn.gelu(y.reshape(bsz, seq, SSM_WIDTH))
    return y * jax.nn.sigmoid(y @ glu_w.astype(f32) + glu_b.astype(f32))


def to_chunks(t, chunk):
    bsz, seq, heads = t.shape[:3]
    t = t.reshape(bsz, seq // chunk, chunk, heads, *t.shape[3:])
    return jnp.moveaxis(t, 3, 1)


def gated_delta_rule(q, k, v, g, beta):
    bsz, seq, heads, dk = q.shape
    dv = v.shape[-1]
    q = to_chunks(q * dk ** -0.5, DN_CHUNK)
    k, v = to_chunks(k, DN_CHUNK), to_chunks(v, DN_CHUNK)
    g, beta = to_chunks(g, DN_CHUNK), to_chunks(beta, DN_CHUNK)
    gc = jnp.cumsum(g, axis=-1)
    idx = jnp.arange(DN_CHUNK)
    causal = idx[:, None] >= idx[None, :]
    strict = idx[:, None] > idx[None, :]
    decay = jnp.exp(jnp.where(causal, gc[..., :, None] - gc[..., None, :], NEG_INF))
    k_beta = k * beta[..., None]
    a_mat = jnp.where(strict, jnp.einsum('bhnck,bhnek->bhnce', k_beta, k) * decay, 0.0)
    eye = jnp.eye(DN_CHUNK, dtype=jnp.float32)
    t_inv = lax.linalg.triangular_solve(eye + a_mat, jnp.broadcast_to(eye, a_mat.shape),
                                        left_side=True, lower=True, unit_diagonal=True)
    u = jnp.einsum('bhnce,bhnev->bhncv', t_inv, v * beta[..., None])
    w = jnp.einsum('bhnce,bhnek->bhnck', t_inv, k_beta * jnp.exp(gc)[..., None])
    attn = jnp.einsum('bhnck,bhnek->bhnce', q, k) * decay
    q_dec = q * jnp.exp(gc)[..., None]
    k_tail = k * jnp.exp(gc[..., -1:] - gc)[..., None]
    chunk_decay = jnp.exp(gc[..., -1])
    xs = tuple(jnp.moveaxis(t, 2, 0) for t in (q_dec, k_tail, u, w, attn, chunk_decay))

    def step(state, inp):
        qd, kt, un, wn, an, dec = inp
        v_new = un - jnp.einsum('bhck,bhkv->bhcv', wn, state)
        o = jnp.einsum('bhck,bhkv->bhcv', qd, state) + jnp.einsum('bhce,bhev->bhcv', an, v_new)
        state = state * dec[..., None, None] + jnp.einsum('bhck,bhcv->bhkv', kt, v_new)
        return state, o

    state0 = jnp.zeros((bsz, heads, dk, dv), jnp.float32)
    _, o = lax.scan(step, state0, xs)
    o = jnp.moveaxis(o, 0, 2).reshape(bsz, heads, seq, dv)
    return jnp.swapaxes(o, 1, 2)


def t5_bucket(dist):
    max_exact = N_BUCKETS // 2
    d = jnp.maximum(dist, 1).astype(jnp.float32)
    large = max_exact + jnp.log(d / max_exact) / math.log(REL_MAX_DIST / max_exact) * (N_BUCKETS - max_exact)
    large = jnp.minimum(large.astype(jnp.int32), N_BUCKETS - 1)
    return jnp.where(dist < max_exact, dist, large)


def dilated_branch(q, k, v, rel_bias, window, dil):
    bsz, seq, heads, dh = q.shape
    blk = ATTN_BLOCK
    sub_len = seq // dil
    n_blocks = -(-sub_len // blk)
    sub_pad = n_blocks * blk

    def to_sub(t):
        t = t.astype(jnp.float32).reshape(bsz, sub_len, dil, heads, dh).transpose(0, 2, 1, 3, 4)
        return jnp.pad(t, ((0, 0), (0, 0), (0, sub_pad - sub_len), (0, 0), (0, 0)))

    def band(t):
        tp = jnp.pad(t, ((0, 0), (0, 0), (blk, 0), (0, 0), (0, 0)))
        prev = tp[:, :, :sub_pad].reshape(bsz, dil, n_blocks, blk, heads, dh)
        cur = tp[:, :, blk:].reshape(bsz, dil, n_blocks, blk, heads, dh)
        return jnp.concatenate([prev, cur], axis=3)

    qb = to_sub(q).reshape(bsz, dil, n_blocks, blk, heads, dh) * dh ** -0.5
    kb = band(to_sub(k))
    vb = band(to_sub(v))
    rel = blk + jnp.arange(blk)[:, None] - jnp.arange(2 * blk)[None, :]
    key_idx = jnp.arange(n_blocks)[:, None] * blk + jnp.arange(2 * blk)[None, :] - blk
    valid = ((rel >= 0) & (rel <= window // dil))[None] & (key_idx >= 0)[:, None, :]
    bias = jnp.moveaxis(rel_bias.astype(jnp.float32)[t5_bucket(jnp.maximum(rel, 0) * dil)], -1, 0)
    logits = jnp.einsum('bgnqhd,bgnkhd->bhgnqk', qb, kb) + bias[None, :, None, None]
    logits = jnp.where(valid[None, None, None], logits, NEG_INF)
    m = jnp.max(logits, axis=-1, keepdims=True)
    p = jnp.exp(logits - m)
    s = jnp.sum(p, axis=-1, keepdims=True)
    o = jnp.einsum('bhgnqk,bgnkhd->bgnqhd', p / s, vb)
    lse = (m + jnp.log(s))[..., 0].reshape(bsz, heads, dil, sub_pad)[..., :sub_len]
    o = o.reshape(bsz, dil, sub_pad, heads, dh)[:, :, :sub_len].transpose(0, 2, 1, 3, 4).reshape(bsz, seq, heads, dh)
    lse = lse.transpose(0, 3, 2, 1).reshape(bsz, seq, heads)
    return o, lse


def dilated_attention(q, k, v, rel_bias):
    outs, lses = [], []
    for window, dil in DILATED_PAIRS:
        o, lse = dilated_branch(q, k, v, rel_bias, window, dil)
        outs.append(o)
        lses.append(lse)
    weights = jax.nn.softmax(jnp.stack(lses, axis=0), axis=0)
    return jnp.einsum('gbsh,gbshd->bshd', weights, jnp.stack(outs, axis=0))


def hybrid_mixer(h, w_in, w_out, ssm_lambda_re, ssm_lambda_im, ssm_b_re, ssm_b_im, ssm_c_re, ssm_c_im,
                 ssm_d, ssm_log_dt, ssm_glu_w, ssm_glu_b, ssm_out_gain, dn_conv_w, dn_a_log, dn_dt_bias,
                 dn_norm_gain, attn_out_gain, rel_bias):
    bsz, seq, _ = h.shape
    f32 = jnp.float32
    proj = h @ w_in
    cuts = [sum(IN_SPLITS[:i + 1]) for i in range(len(IN_SPLITS) - 1)]
    u_ssm, a_q, a_k, a_v, dn_qkv, dn_z, dn_a, dn_b = jnp.split(proj, cuts, axis=-1)

    y_ssm = rms_norm(s5_layer(u_ssm, ssm_lambda_re, ssm_lambda_im, ssm_b_re, ssm_b_im, ssm_c_re, ssm_c_im,
                              ssm_d, ssm_log_dt, ssm_glu_w, ssm_glu_b), ssm_out_gain)

    qkv = jax.nn.silu(causal_depthwise_conv(dn_qkv, dn_conv_w)).astype(f32)
    d_q, d_k, d_v = [t.reshape(bsz, seq, DN_HEADS, DN_HEAD_DIM) for t in jnp.split(qkv, 3, axis=-1)]
    d_q, d_k = l2_normalize(d_q), l2_normalize(d_k)
    beta = jax.nn.sigmoid(dn_b.astype(f32))
    g = -jnp.exp(dn_a_log.astype(f32)) * jax.nn.softplus(dn_a.astype(f32) + dn_dt_bias.astype(f32))
    o_dn = gated_delta_rule(d_q, d_k, d_v, g, beta)
    o_dn = rms_norm(o_dn, dn_norm_gain) * jax.nn.silu(dn_z.astype(f32).reshape(bsz, seq, DN_HEADS, DN_HEAD_DIM))
    y_dn = o_dn.reshape(bsz, seq, DN_WIDTH)

    at_q, at_k, at_v = [t.reshape(bsz, seq, ATTN_HEADS, ATTN_HEAD_DIM) for t in (a_q, a_k, a_v)]
    o_at = dilated_attention(at_q, at_k, at_v, rel_bias).reshape(bsz, seq, ATTN_WIDTH)
    y_at = rms_norm(o_at, attn_out_gain)

    mix = jnp.concatenate([y_ssm.astype(h.dtype), y_dn.astype(h.dtype), y_at.astype(h.dtype)], axis=-1)
    return mix @ w_out


def setup_inputs(seed: int = 0) -> dict:
    key = jax.random.key(seed)
    ks = jax.random.split(key, 24)
    f32 = jnp.float32

    def nrm(k, shape, scale):
        return scale * jax.random.normal(k, shape, f32)

    x = jax.random.normal(ks[0], (BATCH, SEQ, D_MODEL), f32)
    norm_gains = 1.0 + nrm(ks[1], (DEPTH, 6, D_MODEL), 0.05)
    ffn_w_gate = nrm(ks[2], (DEPTH, 2, D_MODEL, D_FF), D_MODEL ** -0.5)
    ffn_w_up = nrm(ks[3], (DEPTH, 2, D_MODEL, D_FF), D_MODEL ** -0.5)
    ffn_w_down = nrm(ks[4], (DEPTH, 2, D_FF, D_MODEL), D_FF ** -0.5)
    w_in = nrm(ks[5], (DEPTH, D_MODEL, N_IN_COLS), D_MODEL ** -0.5)
    w_out = nrm(ks[6], (DEPTH, D_MIX, D_MODEL), D_MIX ** -0.5)
    n_idx = jnp.arange(SSM_STATE, dtype=f32)
    ssm_lambda_re = -0.5 + nrm(ks[7], (DEPTH, SSM_GROUPS, SSM_STATE), 0.01)
    ssm_lambda_im = math.pi * n_idx + nrm(ks[8], (DEPTH, SSM_GROUPS, SSM_STATE), 0.01)
    ssm_b_re = nrm(ks[9], (DEPTH, SSM_GROUPS, SSM_STATE, SSM_CH), (2 * SSM_CH) ** -0.5)
    ssm_b_im = nrm(ks[10], (DEPTH, SSM_GROUPS, SSM_STATE, SSM_CH), (2 * SSM_CH) ** -0.5)
    ssm_c_re = nrm(ks[11], (DEPTH, SSM_GROUPS, SSM_CH, SSM_STATE), (2 * SSM_STATE) ** -0.5)
    ssm_c_im = nrm(ks[12], (DEPTH, SSM_GROUPS, SSM_CH, SSM_STATE), (2 * SSM_STATE) ** -0.5)
    ssm_d = nrm(ks[13], (DEPTH, SSM_WIDTH), 1.0)
    ssm_log_dt = jax.random.uniform(ks[14], (DEPTH, SSM_GROUPS), f32, math.log(SSM_DT_MIN), math.log(SSM_DT_MAX))
    ssm_glu_w = nrm(ks[15], (DEPTH, SSM_WIDTH, SSM_WIDTH), SSM_WIDTH ** -0.5)
    ssm_glu_b = nrm(ks[16], (DEPTH, SSM_WIDTH), 0.02)
    ssm_out_gain = 1.0 + nrm(ks[17], (DEPTH, SSM_WIDTH), 0.05)
    dn_conv_w = nrm(ks[18], (DEPTH, DN_CONV, 3 * DN_WIDTH), DN_CONV ** -0.5)
    dn_a_log = jnp.log(jax.random.uniform(ks[19], (DEPTH, DN_HEADS), f32, 1.0, 16.0))
    dt = jnp.exp(jax.random.uniform(ks[20], (DEPTH, DN_HEADS), f32, math.log(1e-3), math.log(1e-1)))
    dn_dt_bias = dt + jnp.log(-jnp.expm1(-dt))
    dn_norm_gain = 1.0 + nrm(ks[21], (DEPTH, DN_HEAD_DIM), 0.05)
    attn_out_gain = 1.0 + nrm(ks[22], (DEPTH, ATTN_WIDTH), 0.05)
    rel_bias = nrm(ks[23], (N_BUCKETS, ATTN_HEADS), 0.5)
    return {'x': x, 'norm_gains': norm_gains, 'ffn_w_gate': ffn_w_gate, 'ffn_w_up': ffn_w_up,
            'ffn_w_down': ffn_w_down, 'w_in': w_in, 'w_out': w_out,
            'ssm_lambda_re': ssm_lambda_re, 'ssm_lambda_im': ssm_lambda_im,
            'ssm_b_re': ssm_b_re, 'ssm_b_im': ssm_b_im, 'ssm_c_re': ssm_c_re, 'ssm_c_im': ssm_c_im,
            'ssm_d': ssm_d, 'ssm_log_dt': ssm_log_dt, 'ssm_glu_w': ssm_glu_w, 'ssm_glu_b': ssm_glu_b,
            'ssm_out_gain': ssm_out_gain, 'dn_conv_w': dn_conv_w, 'dn_a_log': dn_a_log,
            'dn_dt_bias': dn_dt_bias, 'dn_norm_gain': dn_norm_gain, 'attn_out_gain': attn_out_gain,
            'rel_bias': rel_bias}


def reference(x, norm_gains, ffn_w_gate, ffn_w_up, ffn_w_down, w_in, w_out,
              ssm_lambda_re, ssm_lambda_im, ssm_b_re, ssm_b_im, ssm_c_re, ssm_c_im,
              ssm_d, ssm_log_dt, ssm_glu_w, ssm_glu_b, ssm_out_gain, dn_conv_w, dn_a_log,
              dn_dt_bias, dn_norm_gain, attn_out_gain, rel_bias):
    for l in range(DEPTH):
        gains = norm_gains[l]
        h = rms_norm(x, gains[0])
        x = x + 0.5 * rms_norm(swiglu(h, ffn_w_gate[l, 0], ffn_w_up[l, 0], ffn_w_down[l, 0]), gains[1])
        h = rms_norm(x, gains[2])
        mix = hybrid_mixer(h, w_in[l], w_out[l], ssm_lambda_re[l], ssm_lambda_im[l], ssm_b_re[l], ssm_b_im[l],
                           ssm_c_re[l], ssm_c_im[l], ssm_d[l], ssm_log_dt[l], ssm_glu_w[l], ssm_glu_b[l],
                           ssm_out_gain[l], dn_conv_w[l], dn_a_log[l], dn_dt_bias[l], dn_norm_gain[l],
                           attn_out_gain[l], rel_bias)
        x = x + rms_norm(mix, gains[3])
        h = rms_norm(x, gains[4])
        x = x + 0.5 * rms_norm(swiglu(h, ffn_w_gate[l, 1], ffn_w_up[l, 1], ffn_w_down[l, 1]), gains[5])
    return x
```

```python
import functools
import math

import jax
import jax.numpy as jnp
import numpy as np
from jax import lax
from jax.experimental import pallas as pl
from jax.experimental.pallas import tpu as pltpu

D_MODEL = 2048
BATCH = 2
SEQ = 4096
DEPTH = 4
SSM_GROUPS = 32
SSM_CH = 16
SSM_STATE = 64
SSM_WIDTH = SSM_GROUPS * SSM_CH
DN_HEADS = 6
DN_HEAD_DIM = 128
DN_WIDTH = DN_HEADS * DN_HEAD_DIM
DN_CONV = 4
DN_CHUNK = 64
ATTN_HEADS = 6
ATTN_HEAD_DIM = 128
ATTN_WIDTH = ATTN_HEADS * ATTN_HEAD_DIM
DILATED_PAIRS = ((128, 1), (512, 4), (2048, 16))
ATTN_BLOCK = 128
N_BUCKETS = 32
REL_MAX_DIST = 2048
D_MIX = SSM_WIDTH + DN_WIDTH + ATTN_WIDTH
IN_SPLITS = (SSM_WIDTH, ATTN_WIDTH, ATTN_WIDTH, ATTN_WIDTH, 3 * DN_WIDTH, DN_WIDTH, DN_HEADS, DN_HEADS)
N_IN_COLS = sum(IN_SPLITS)
D_FF = 5632
NORM_EPS = 1e-6
NEG_INF = -1e30

LANES = 128
N_IN_PAD = 6144
VMEM_LIMIT = 56 * 1024 * 1024
N_TOK = BATCH * SEQ

BF16 = jnp.bfloat16
F32 = jnp.float32


def _rms(x, gain):
    return x * lax.rsqrt(jnp.mean(x * x, axis=-1, keepdims=True) + NORM_EPS) * gain


FFN_TM = 512
FFN_TF = 512


def _ffn_kernel(x_ref, gpre_ref, gpost_ref, wg_ref, wu_ref, wd_ref, o_ref, h_scr, acc_scr):
    f = pl.program_id(1)

    @pl.when(f == 0)
    def _():
        h_scr[...] = _rms(x_ref[...], gpre_ref[...]).astype(BF16)
        acc_scr[...] = jnp.zeros_like(acc_scr)

    h = h_scr[...]
    gate = jnp.dot(h, wg_ref[...], preferred_element_type=F32)
    up = jnp.dot(h, wu_ref[...], preferred_element_type=F32)
    act = (gate * jax.nn.sigmoid(gate) * up).astype(BF16)
    acc_scr[...] += jnp.dot(act, wd_ref[...], preferred_element_type=F32)

    @pl.when(f == pl.num_programs(1) - 1)
    def _():
        o_ref[...] = x_ref[...] + 0.5 * _rms(acc_scr[...], gpost_ref[...])


def _ffn(x, g_pre, g_post, wg, wu, wd):
    return pl.pallas_call(
        _ffn_kernel,
        grid=(N_TOK // FFN_TM, D_FF // FFN_TF),
        in_specs=[
            pl.BlockSpec((FFN_TM, D_MODEL), lambda i, f: (i, 0)),
            pl.BlockSpec((1, D_MODEL), lambda i, f: (0, 0)),
            pl.BlockSpec((1, D_MODEL), lambda i, f: (0, 0)),
            pl.BlockSpec((D_MODEL, FFN_TF), lambda i, f: (0, f)),
            pl.BlockSpec((D_MODEL, FFN_TF), lambda i, f: (0, f)),
            pl.BlockSpec((FFN_TF, D_MODEL), lambda i, f: (f, 0)),
        ],
        out_specs=pl.BlockSpec((FFN_TM, D_MODEL), lambda i, f: (i, 0)),
        out_shape=jax.ShapeDtypeStruct((N_TOK, D_MODEL), F32),
        scratch_shapes=[pltpu.VMEM((FFN_TM, D_MODEL), BF16), pltpu.VMEM((FFN_TM, D_MODEL), F32)],
        compiler_params=pltpu.CompilerParams(
            dimension_semantics=("parallel", "arbitrary"), vmem_limit_bytes=VMEM_LIMIT),
        name="ffn",
    )(x, g_pre, g_post, wg, wu, wd)


INP_TM = 512
INP_TN = 768


def _inproj_kernel(x_ref, g_ref, w_ref, o_ref, h_scr):
    @pl.when(pl.program_id(1) == 0)
    def _():
        h_scr[...] = _rms(x_ref[...], g_ref[...]).astype(BF16)

    o_ref[...] = jnp.dot(h_scr[...], w_ref[...], preferred_element_type=F32)


def _inproj(x, g, w):
    return pl.pallas_call(
        _inproj_kernel,
        grid=(N_TOK // INP_TM, N_IN_PAD // INP_TN),
        in_specs=[
            pl.BlockSpec((INP_TM, D_MODEL), lambda i, n: (i, 0)),
            pl.BlockSpec((1, D_MODEL), lambda i, n: (0, 0)),
            pl.BlockSpec((D_MODEL, INP_TN), lambda i, n: (0, n)),
        ],
        out_specs=pl.BlockSpec((INP_TM, INP_TN), lambda i, n: (i, n)),
        out_shape=jax.ShapeDtypeStruct((N_TOK, N_IN_PAD), F32),
        scratch_shapes=[pltpu.VMEM((INP_TM, D_MODEL), BF16)],
        compiler_params=pltpu.CompilerParams(
            dimension_semantics=("parallel", "arbitrary"), vmem_limit_bytes=VMEM_LIMIT),
        name="inproj",
    )(x, g, w)


OUT_TM = 256


def _outproj_kernel(x_ref, ys_ref, yd_ref, oa_ref, ga_ref, gpost_ref, w_ref, o_ref):
    ya = _rms(oa_ref[...], ga_ref[...]).astype(BF16)
    mix = jnp.dot(ys_ref[...].astype(BF16), w_ref[0:SSM_WIDTH, :], preferred_element_type=F32)
    mix += jnp.dot(yd_ref[...].astype(BF16), w_ref[SSM_WIDTH:SSM_WIDTH + DN_WIDTH, :],
                   preferred_element_type=F32)
    mix += jnp.dot(ya, w_ref[SSM_WIDTH + DN_WIDTH:D_MIX, :], preferred_element_type=F32)
    o_ref[...] = x_ref[...] + _rms(mix, gpost_ref[...])


def _outproj(x, y_ssm, y_dn, o_at, g_attn, g_post, w):
    row = lambda i: (i, 0)
    fixed = lambda i: (0, 0)
    return pl.pallas_call(
        _outproj_kernel,
        grid=(N_TOK // OUT_TM,),
        in_specs=[
            pl.BlockSpec((OUT_TM, D_MODEL), row),
            pl.BlockSpec((OUT_TM, SSM_WIDTH), row),
            pl.BlockSpec((OUT_TM, DN_WIDTH), row),
            pl.BlockSpec((OUT_TM, ATTN_WIDTH), row),
            pl.BlockSpec((1, ATTN_WIDTH), fixed),
            pl.BlockSpec((1, D_MODEL), fixed),
            pl.BlockSpec((D_MIX, D_MODEL), fixed),
        ],
        out_specs=pl.BlockSpec((OUT_TM, D_MODEL), row),
        out_shape=jax.ShapeDtypeStruct((N_TOK, D_MODEL), F32),
        compiler_params=pltpu.CompilerParams(
            dimension_semantics=("parallel",), vmem_limit_bytes=VMEM_LIMIT),
        name="outproj",
    )(x, y_ssm, y_dn, o_at, g_attn, g_post, w)


def _rms_norm(x, gain):
    xf = x.astype(jnp.float32)
    y = xf * lax.rsqrt(jnp.mean(xf * xf, axis=-1, keepdims=True) + NORM_EPS)
    return (y * gain.astype(jnp.float32)).astype(x.dtype)


def _l2n(x):
    return x * lax.rsqrt(jnp.sum(x * x, axis=-1, keepdims=True) + NORM_EPS)


def _s5(u, lam_re, lam_im, b_re, b_im, c_re, c_im, d_skip, log_dt, glu_w, glu_b):
    bsz, seq, _ = u.shape
    f32 = jnp.float32
    uf = u.astype(f32).reshape(bsz, seq, SSM_GROUPS, SSM_CH)
    lam = lax.complex(lam_re.astype(f32), lam_im.astype(f32))
    dt = jnp.exp(log_dt.astype(f32))[:, None]
    lam_bar = jnp.exp(lam * dt)
    b = lax.complex(b_re.astype(f32), b_im.astype(f32))
    b_bar = ((lam_bar - 1.0) / lam)[..., None] * b
    c = lax.complex(c_re.astype(f32), c_im.astype(f32))
    bu = jnp.einsum('gpc,bsgc->bsgp', b_bar, uf.astype(jnp.complex64))
    a = jnp.broadcast_to(lam_bar, bu.shape)

    def combine(e1, e2):
        a1, b1 = e1
        a2, b2 = e2
        return a1 * a2, a2 * b1 + b2

    _, states = lax.associative_scan(combine, (a, bu), axis=1)
    y = jnp.einsum('gcp,bsgp->bsgc', c, states).real + d_skip.astype(f32).reshape(SSM_GROUPS, SSM_CH) * uf
    y = jax.nn.gelu(y.reshape(bsz, seq, SSM_WIDTH))
    return y * jax.nn.sigmoid(y @ glu_w.astype(f32) + glu_b.astype(f32))


def _to_chunks(t, chunk):
    bsz, seq, heads = t.shape[:3]
    t = t.reshape(bsz, seq // chunk, chunk, heads, *t.shape[3:])
    return jnp.moveaxis(t, 3, 1)


def _gdr(q, k, v, g, beta):
    bsz, seq, heads, dk = q.shape
    dv = v.shape[-1]
    q = _to_chunks(q * dk ** -0.5, DN_CHUNK)
    k, v = _to_chunks(k, DN_CHUNK), _to_chunks(v, DN_CHUNK)
    g, beta = _to_chunks(g, DN_CHUNK), _to_chunks(beta, DN_CHUNK)
    gc = jnp.cumsum(g, axis=-1)
    idx = jnp.arange(DN_CHUNK)
    causal = idx[:, None] >= idx[None, :]
    strict = idx[:, None] > idx[None, :]
    decay = jnp.exp(jnp.where(causal, gc[..., :, None] - gc[..., None, :], NEG_INF))
    k_beta = k * beta[..., None]
    a_mat = jnp.where(strict, jnp.einsum('bhnck,bhnek->bhnce', k_beta, k) * decay, 0.0)
    eye = jnp.eye(DN_CHUNK, dtype=jnp.float32)
    t_inv = lax.linalg.triangular_solve(eye + a_mat, jnp.broadcast_to(eye, a_mat.shape),
                                        left_side=True, lower=True, unit_diagonal=True)
    u = jnp.einsum('bhnce,bhnev->bhncv', t_inv, v * beta[..., None])
    w = jnp.einsum('bhnce,bhnek->bhnck', t_inv, k_beta * jnp.exp(gc)[..., None])
    attn = jnp.einsum('bhnck,bhnek->bhnce', q, k) * decay
    q_dec = q * jnp.exp(gc)[..., None]
    k_tail = k * jnp.exp(gc[..., -1:] - gc)[..., None]
    chunk_decay = jnp.exp(gc[..., -1])
    xs = tuple(jnp.moveaxis(t, 2, 0) for t in (q_dec, k_tail, u, w, attn, chunk_decay))

    def step(state, inp):
        qd, kt, un, wn, an, dec = inp
        v_new = un - jnp.einsum('bhck,bhkv->bhcv', wn, state)
        o = jnp.einsum('bhck,bhkv->bhcv', qd, state) + jnp.einsum('bhce,bhev->bhcv', an, v_new)
        state = state * dec[..., None, None] + jnp.einsum('bhck,bhcv->bhkv', kt, v_new)
        return state, o

    state0 = jnp.zeros((bsz, heads, dk, dv), jnp.float32)
    _, o = lax.scan(step, state0, xs)
    o = jnp.moveaxis(o, 0, 2).reshape(bsz, heads, seq, dv)
    return jnp.swapaxes(o, 1, 2)


def _t5_bucket(dist):
    max_exact = N_BUCKETS // 2
    d = jnp.maximum(dist, 1).astype(jnp.float32)
    large = max_exact + jnp.log(d / max_exact) / math.log(REL_MAX_DIST / max_exact) * (N_BUCKETS - max_exact)
    large = jnp.minimum(large.astype(jnp.int32), N_BUCKETS - 1)
    return jnp.where(dist < max_exact, dist, large)


def _dil_branch(q, k, v, rel_bias, window, dil):
    bsz, seq, heads, dh = q.shape
    blk = ATTN_BLOCK
    sub_len = seq // dil
    n_blocks = -(-sub_len // blk)
    sub_pad = n_blocks * blk

    def to_sub(t):
        t = t.astype(jnp.float32).reshape(bsz, sub_len, dil, heads, dh).transpose(0, 2, 1, 3, 4)
        return jnp.pad(t, ((0, 0), (0, 0), (0, sub_pad - sub_len), (0, 0), (0, 0)))

    def band(t):
        tp = jnp.pad(t, ((0, 0), (0, 0), (blk, 0), (0, 0), (0, 0)))
        prev = tp[:, :, :sub_pad].reshape(bsz, dil, n_blocks, blk, heads, dh)
        cur = tp[:, :, blk:].reshape(bsz, dil, n_blocks, blk, heads, dh)
        return jnp.concatenate([prev, cur], axis=3)

    qb = to_sub(q).reshape(bsz, dil, n_blocks, blk, heads, dh) * dh ** -0.5
    kb = band(to_sub(k))
    vb = band(to_sub(v))
    rel = blk + jnp.arange(blk)[:, None] - jnp.arange(2 * blk)[None, :]
    key_idx = jnp.arange(n_blocks)[:, None] * blk + jnp.arange(2 * blk)[None, :] - blk
    valid = ((rel >= 0) & (rel <= window // dil))[None] & (key_idx >= 0)[:, None, :]
    bias = jnp.moveaxis(rel_bias.astype(jnp.float32)[_t5_bucket(jnp.maximum(rel, 0) * dil)], -1, 0)
    logits = jnp.einsum('bgnqhd,bgnkhd->bhgnqk', qb, kb) + bias[None, :, None, None]
    logits = jnp.where(valid[None, None, None], logits, NEG_INF)
    m = jnp.max(logits, axis=-1, keepdims=True)
    p = jnp.exp(logits - m)
    s = jnp.sum(p, axis=-1, keepdims=True)
    o = jnp.einsum('bhgnqk,bgnkhd->bgnqhd', p / s, vb)
    lse = (m + jnp.log(s))[..., 0].reshape(bsz, heads, dil, sub_pad)[..., :sub_len]
    o = o.reshape(bsz, dil, sub_pad, heads, dh)[:, :, :sub_len].transpose(0, 2, 1, 3, 4).reshape(bsz, seq, heads, dh)
    lse = lse.transpose(0, 3, 2, 1).reshape(bsz, seq, heads)
    return o, lse


def _dil_attn(q, k, v, rel_bias):
    outs, lses = [], []
    for window, dil in DILATED_PAIRS:
        o, lse = _dil_branch(q, k, v, rel_bias, window, dil)
        outs.append(o)
        lses.append(lse)
    weights = jax.nn.softmax(jnp.stack(lses, axis=0), axis=0)
    return jnp.einsum('gbsh,gbshd->bshd', weights, jnp.stack(outs, axis=0))


def _mixers_jnp(proj, p, l, rel_bias):
    bsz, seq = BATCH, SEQ
    f32 = jnp.float32
    proj = proj[:, :N_IN_COLS].reshape(bsz, seq, N_IN_COLS)
    cuts = [sum(IN_SPLITS[:i + 1]) for i in range(len(IN_SPLITS) - 1)]
    u_ssm, a_q, a_k, a_v, dn_qkv, dn_z, dn_a, dn_b = jnp.split(proj, cuts, axis=-1)
    y_ssm = _rms_norm(_s5(u_ssm, p['ssm_lambda_re'][l], p['ssm_lambda_im'][l], p['ssm_b_re'][l],
                          p['ssm_b_im'][l], p['ssm_c_re'][l], p['ssm_c_im'][l], p['ssm_d'][l],
                          p['ssm_log_dt'][l], p['ssm_glu_w'][l], p['ssm_glu_b'][l]), p['ssm_out_gain'][l])
    qkv = jax.nn.silu(lax.conv_general_dilated(
        dn_qkv, p['dn_conv_w'][l][:, None, :], window_strides=(1,), padding=((DN_CONV - 1, 0),),
        dimension_numbers=('NWC', 'WIO', 'NWC'), feature_group_count=3 * DN_WIDTH)).astype(f32)
    d_q, d_k, d_v = [t.reshape(bsz, seq, DN_HEADS, DN_HEAD_DIM) for t in jnp.split(qkv, 3, axis=-1)]
    d_q, d_k = _l2n(d_q), _l2n(d_k)
    beta = jax.nn.sigmoid(dn_b.astype(f32))
    g = -jnp.exp(p['dn_a_log'][l].astype(f32)) * jax.nn.softplus(dn_a.astype(f32) + p['dn_dt_bias'][l].astype(f32))
    o_dn = _gdr(d_q, d_k, d_v, g, beta)
    o_dn = _rms_norm(o_dn, p['dn_norm_gain'][l]) * jax.nn.silu(dn_z.astype(f32).reshape(bsz, seq, DN_HEADS, DN_HEAD_DIM))
    y_dn = o_dn.reshape(bsz * seq, DN_WIDTH)
    at_q, at_k, at_v = [t.reshape(bsz, seq, ATTN_HEADS, ATTN_HEAD_DIM) for t in (a_q, a_k, a_v)]
    o_at = _dil_attn(at_q, at_k, at_v, rel_bias).reshape(bsz * seq, ATTN_WIDTH)
    return y_ssm.reshape(bsz * seq, SSM_WIDTH), y_dn, o_at


def kernel(x, norm_gains, ffn_w_gate, ffn_w_up, ffn_w_down, w_in, w_out, ssm_lambda_re, ssm_lambda_im,
           ssm_b_re, ssm_b_im, ssm_c_re, ssm_c_im, ssm_d, ssm_log_dt, ssm_glu_w, ssm_glu_b, ssm_out_gain,
           dn_conv_w, dn_a_log, dn_dt_bias, dn_norm_gain, attn_out_gain, rel_bias):
    p = dict(ssm_lambda_re=ssm_lambda_re, ssm_lambda_im=ssm_lambda_im, ssm_b_re=ssm_b_re, ssm_b_im=ssm_b_im,
             ssm_c_re=ssm_c_re, ssm_c_im=ssm_c_im, ssm_d=ssm_d, ssm_log_dt=ssm_log_dt, ssm_glu_w=ssm_glu_w,
             ssm_glu_b=ssm_glu_b, ssm_out_gain=ssm_out_gain, dn_conv_w=dn_conv_w, dn_a_log=dn_a_log,
             dn_dt_bias=dn_dt_bias, dn_norm_gain=dn_norm_gain)
    wg = ffn_w_gate.astype(BF16)
    wu = ffn_w_up.astype(BF16)
    wd = ffn_w_down.astype(BF16)
    w_in_p = jnp.pad(w_in, ((0, 0), (0, 0), (0, N_IN_PAD - N_IN_COLS))).astype(BF16)
    w_out_b = w_out.astype(BF16)
    gains = norm_gains.reshape(DEPTH, 6, 1, D_MODEL)
    x = x.reshape(N_TOK, D_MODEL)
    for l in range(DEPTH):
        x = _ffn(x, gains[l, 0], gains[l, 1], wg[l, 0], wu[l, 0], wd[l, 0])
        proj = _inproj(x, gains[l, 2], w_in_p[l])
        y_ssm, y_dn, o_at = _mixers_jnp(proj, p, l, rel_bias)
        x = _outproj(x, y_ssm, y_dn, o_at, attn_out_gain[l].reshape(1, ATTN_WIDTH), gains[l, 3], w_out_b[l])
        x = _ffn(x, gains[l, 4], gains[l, 5], wg[l, 1], wu[l, 1], wd[l, 1])
    return x.reshape(BATCH, SEQ, D_MODEL)
```

```python
import functools
import math

import jax
import jax.numpy as jnp
import numpy as np
from jax import lax
from jax.experimental import pallas as pl
from jax.experimental.pallas import tpu as pltpu

D_MODEL = 2048
BATCH = 2
SEQ = 4096
DEPTH = 4
SSM_GROUPS = 32
SSM_CH = 16
SSM_STATE = 64
SSM_WIDTH = SSM_GROUPS * SSM_CH
DN_HEADS = 6
DN_HEAD_DIM = 128
DN_WIDTH = DN_HEADS * DN_HEAD_DIM
DN_CONV = 4
DN_CHUNK = 64
ATTN_HEADS = 6
ATTN_HEAD_DIM = 128
ATTN_WIDTH = ATTN_HEADS * ATTN_HEAD_DIM
DILATED_PAIRS = ((128, 1), (512, 4), (2048, 16))
ATTN_BLOCK = 128
N_BUCKETS = 32
REL_MAX_DIST = 2048
D_MIX = SSM_WIDTH + DN_WIDTH + ATTN_WIDTH
IN_SPLITS = (SSM_WIDTH, ATTN_WIDTH, ATTN_WIDTH, ATTN_WIDTH, 3 * DN_WIDTH, DN_WIDTH, DN_HEADS, DN_HEADS)
N_IN_COLS = sum(IN_SPLITS)
D_FF = 5632
NORM_EPS = 1e-6
NEG_INF = -1e30

LANES = 128
N_IN_PAD = 6144
VMEM_LIMIT = 56 * 1024 * 1024
N_TOK = BATCH * SEQ

BF16 = jnp.bfloat16
F32 = jnp.float32


def _rms(x, gain):
    return x * lax.rsqrt(jnp.mean(x * x, axis=-1, keepdims=True) + NORM_EPS) * gain


FFN_TM = 512
FFN_TF = 512


def _ffn_kernel(x_ref, gpre_ref, gpost_ref, wg_ref, wu_ref, wd_ref, o_ref, h_scr, acc_scr):
    f = pl.program_id(1)

    @pl.when(f == 0)
    def _():
        h_scr[...] = _rms(x_ref[...], gpre_ref[...]).astype(BF16)
        acc_scr[...] = jnp.zeros_like(acc_scr)

    h = h_scr[...]
    gate = jnp.dot(h, wg_ref[...], preferred_element_type=F32)
    up = jnp.dot(h, wu_ref[...], preferred_element_type=F32)
    act = (gate * jax.nn.sigmoid(gate) * up).astype(BF16)
    acc_scr[...] += jnp.dot(act, wd_ref[...], preferred_element_type=F32)

    @pl.when(f == pl.num_programs(1) - 1)
    def _():
        o_ref[...] = x_ref[...] + 0.5 * _rms(acc_scr[...], gpost_ref[...])


def _ffn(x, g_pre, g_post, wg, wu, wd):
    return pl.pallas_call(
        _ffn_kernel,
        grid=(N_TOK // FFN_TM, D_FF // FFN_TF),
        in_specs=[
            pl.BlockSpec((FFN_TM, D_MODEL), lambda i, f: (i, 0)),
            pl.BlockSpec((1, D_MODEL), lambda i, f: (0, 0)),
            pl.BlockSpec((1, D_MODEL), lambda i, f: (0, 0)),
            pl.BlockSpec((D_MODEL, FFN_TF), lambda i, f: (0, f)),
            pl.BlockSpec((D_MODEL, FFN_TF), lambda i, f: (0, f)),
            pl.BlockSpec((FFN_TF, D_MODEL), lambda i, f: (f, 0)),
        ],
        out_specs=pl.BlockSpec((FFN_TM, D_MODEL), lambda i, f: (i, 0)),
        out_shape=jax.ShapeDtypeStruct((N_TOK, D_MODEL), F32),
        scratch_shapes=[pltpu.VMEM((FFN_TM, D_MODEL), BF16), pltpu.VMEM((FFN_TM, D_MODEL), F32)],
        compiler_params=pltpu.CompilerParams(
            dimension_semantics=("parallel", "arbitrary"), vmem_limit_bytes=VMEM_LIMIT),
        name="ffn",
    )(x, g_pre, g_post, wg, wu, wd)


INP_TM = 512
INP_TN = 768


def _inproj_kernel(x_ref, g_ref, w_ref, o_ref, h_scr):
    @pl.when(pl.program_id(1) == 0)
    def _():
        h_scr[...] = _rms(x_ref[...], g_ref[...]).astype(BF16)

    o_ref[...] = jnp.dot(h_scr[...], w_ref[...], preferred_element_type=F32)


def _inproj(x, g, w):
    return pl.pallas_call(
        _inproj_kernel,
        grid=(N_TOK // INP_TM, N_IN_PAD // INP_TN),
        in_specs=[
            pl.BlockSpec((INP_TM, D_MODEL), lambda i, n: (i, 0)),
            pl.BlockSpec((1, D_MODEL), lambda i, n: (0, 0)),
            pl.BlockSpec((D_MODEL, INP_TN), lambda i, n: (0, n)),
        ],
        out_specs=pl.BlockSpec((INP_TM, INP_TN), lambda i, n: (i, n)),
        out_shape=jax.ShapeDtypeStruct((N_TOK, N_IN_PAD), F32),
        scratch_shapes=[pltpu.VMEM((INP_TM, D_MODEL), BF16)],
        compiler_params=pltpu.CompilerParams(
            dimension_semantics=("parallel", "arbitrary"), vmem_limit_bytes=VMEM_LIMIT),
        name="inproj",
    )(x, g, w)


OUT_TM = 256


def _outproj_kernel(x_ref, ys_ref, yd_ref, oa_ref, ga_ref, gpost_ref, w_ref, o_ref):
    ya = _rms(oa_ref[...], ga_ref[...]).astype(BF16)
    mix = jnp.dot(ys_ref[...].astype(BF16), w_ref[0:SSM_WIDTH, :], preferred_element_type=F32)
    mix += jnp.dot(yd_ref[...].astype(BF16), w_ref[SSM_WIDTH:SSM_WIDTH + DN_WIDTH, :],
                   preferred_element_type=F32)
    mix += jnp.dot(ya, w_ref[SSM_WIDTH + DN_WIDTH:D_MIX, :], preferred_element_type=F32)
    o_ref[...] = x_ref[...] + _rms(mix, gpost_ref[...])


def _outproj(x, y_ssm, y_dn, o_at, g_attn, g_post, w):
    row = lambda i: (i, 0)
    fixed = lambda i: (0, 0)
    return pl.pallas_call(
        _outproj_kernel,
        grid=(N_TOK // OUT_TM,),
        in_specs=[
            pl.BlockSpec((OUT_TM, D_MODEL), row),
            pl.BlockSpec((OUT_TM, SSM_WIDTH), row),
            pl.BlockSpec((OUT_TM, DN_WIDTH), row),
            pl.BlockSpec((OUT_TM, ATTN_WIDTH), row),
            pl.BlockSpec((1, ATTN_WIDTH), fixed),
            pl.BlockSpec((1, D_MODEL), fixed),
            pl.BlockSpec((D_MIX, D_MODEL), fixed),
        ],
        out_specs=pl.BlockSpec((OUT_TM, D_MODEL), row),
        out_shape=jax.ShapeDtypeStruct((N_TOK, D_MODEL), F32),
        compiler_params=pltpu.CompilerParams(
            dimension_semantics=("parallel",), vmem_limit_bytes=VMEM_LIMIT),
        name="outproj",
    )(x, y_ssm, y_dn, o_at, g_attn, g_post, w)


S5_SEG = 8
S5_SEGLEN = SEQ // S5_SEG
S5_KB = 64
S5_ROWS = S5_KB * S5_SEG
S5_NS = SSM_GROUPS * SSM_STATE
S5_Q = 4
S5_QS = S5_NS // S5_Q
S5_QC = SSM_WIDTH // S5_Q
S5_LOG2_SEGLEN = 9
assert 1 << S5_LOG2_SEGLEN == S5_SEGLEN


def _s5_kernel(u_ref, wb_ref, are_ref, aim_ref, cre_ref, cim_ref, d_ref, gw_ref, gb_ref, go_ref, o_ref,
               bu_scr, st_scr, carry_scr, y_scr):
    p = pl.program_id(1)
    j = pl.program_id(2)
    ub = u_ref[0].astype(BF16)
    for q in range(S5_Q):
        r = jnp.dot(ub[:, q * S5_QC:(q + 1) * S5_QC], wb_ref[q], preferred_element_type=F32)
        bu_scr[:, q * S5_QS:(q + 1) * S5_QS] = r[:, :S5_QS]
        bu_scr[:, S5_NS + q * S5_QS:S5_NS + (q + 1) * S5_QS] = r[:, S5_QS:]

    @pl.when(jnp.logical_and(p == 0, j == 0))
    def _():
        st_scr[...] = jnp.zeros_like(st_scr)

    @pl.when(jnp.logical_and(p == 1, j == 0))
    def _():
        st_scr[...] = carry_scr[...]

    def scan(store):
        for q in range(S5_Q):
            re_cols = slice(q * S5_QS, (q + 1) * S5_QS)
            im_cols = slice(S5_NS + q * S5_QS, S5_NS + (q + 1) * S5_QS)
            ar = jnp.broadcast_to(are_ref[:, re_cols], (S5_SEG, S5_QS))
            ai = jnp.broadcast_to(aim_ref[:, re_cols], (S5_SEG, S5_QS))

            def body(k, carry):
                sr, si = carry
                rows = pl.ds(pl.multiple_of(k * S5_SEG, S5_SEG), S5_SEG)
                nr = ar * sr - ai * si + bu_scr[rows, re_cols]
                ni = ar * si + ai * sr + bu_scr[rows, im_cols]
                if store:
                    bu_scr[rows, re_cols] = nr
                    bu_scr[rows, im_cols] = ni
                return nr, ni

            sr, si = lax.fori_loop(0, S5_KB, body, (st_scr[:, re_cols], st_scr[:, im_cols]), unroll=8)
            st_scr[:, re_cols] = sr
            st_scr[:, im_cols] = si

    @pl.when(p == 0)
    def _():
        scan(False)

        @pl.when(j == pl.num_programs(2) - 1)
        def _():
            lr, li = are_ref[...], aim_ref[...]
            for _ in range(S5_LOG2_SEGLEN):
                lr, li = lr * lr - li * li, 2.0 * lr * li
            cr = jnp.zeros((1, S5_NS), F32)
            ci = jnp.zeros((1, S5_NS), F32)
            carry_scr[0:1, :] = jnp.zeros((1, 2 * S5_NS), F32)
            for i in range(1, S5_SEG):
                er = st_scr[i - 1:i, 0:S5_NS]
                ei = st_scr[i - 1:i, S5_NS:2 * S5_NS]
                cr, ci = er + lr * cr - li * ci, ei + lr * ci + li * cr
                carry_scr[i:i + 1, 0:S5_NS] = cr
                carry_scr[i:i + 1, S5_NS:2 * S5_NS] = ci

    @pl.when(p == 1)
    def _():
        scan(True)
        for q in range(S5_Q):
            sre = bu_scr[:, q * S5_QS:(q + 1) * S5_QS].astype(BF16)
            sim = bu_scr[:, S5_NS + q * S5_QS:S5_NS + (q + 1) * S5_QS].astype(BF16)
            y_scr[:, q * S5_QC:(q + 1) * S5_QC] = (
                jnp.dot(sre, cre_ref[q], preferred_element_type=F32)
                + jnp.dot(sim, cim_ref[q], preferred_element_type=F32))
        y = jax.nn.gelu(y_scr[...] + d_ref[...] * u_ref[0])
        z = jnp.dot(y.astype(BF16), gw_ref[...], preferred_element_type=F32) + gb_ref[...]
        o_ref[0] = _rms(y * jax.nn.sigmoid(z), go_ref[...])


def _s5_params(lam_re, lam_im, b_re, b_im, c_re, c_im, log_dt):
    lam = lax.complex(lam_re, lam_im)
    lam_bar = jnp.exp(lam * jnp.exp(log_dt)[:, None])
    b_bar = ((lam_bar - 1.0) / lam)[..., None] * lax.complex(b_re, b_im)
    gq = SSM_GROUPS // S5_Q
    eye = jnp.eye(gq, dtype=F32)

    def in_map(t):
        t = t.reshape(S5_Q, gq, SSM_STATE, SSM_CH)
        return jnp.einsum('qgpc,gh->qgchp', t, eye).reshape(S5_Q, S5_QC, S5_QS)

    def out_map(t):
        t = t.reshape(S5_Q, gq, SSM_CH, SSM_STATE)
        return jnp.einsum('qgcp,gh->qgphc', t, eye).reshape(S5_Q, S5_QS, S5_QC)

    wb = jnp.concatenate([in_map(b_bar.real), in_map(b_bar.imag)], axis=-1).astype(BF16)
    return (wb, lam_bar.real.reshape(1, S5_NS), lam_bar.imag.reshape(1, S5_NS),
            out_map(c_re).astype(BF16), out_map(-c_im).astype(BF16))


def _s5(u_perm, wb, a_re, a_im, cre, cim, d_skip, glu_w, glu_b, out_gain):
    nblk = S5_SEGLEN // S5_KB
    fix2 = lambda b, p, j: (0, 0)
    fix3 = lambda b, p, j: (0, 0, 0)
    return pl.pallas_call(
        _s5_kernel,
        grid=(BATCH, 2, nblk),
        in_specs=[
            pl.BlockSpec((1, S5_ROWS, SSM_WIDTH), lambda b, p, j: (b, j, 0)),
            pl.BlockSpec((S5_Q, S5_QC, 2 * S5_QS), fix3),
            pl.BlockSpec((1, S5_NS), fix2),
            pl.BlockSpec((1, S5_NS), fix2),
            pl.BlockSpec((S5_Q, S5_QS, S5_QC), fix3),
            pl.BlockSpec((S5_Q, S5_QS, S5_QC), fix3),
            pl.BlockSpec((1, SSM_WIDTH), fix2),
            pl.BlockSpec((SSM_WIDTH, SSM_WIDTH), fix2),
            pl.BlockSpec((1, SSM_WIDTH), fix2),
            pl.BlockSpec((1, SSM_WIDTH), fix2),
        ],
        out_specs=pl.BlockSpec((1, S5_ROWS, SSM_WIDTH), lambda b, p, j: (b, j * p, 0)),
        out_shape=jax.ShapeDtypeStruct((BATCH, SEQ, SSM_WIDTH), F32),
        scratch_shapes=[
            pltpu.VMEM((S5_ROWS, 2 * S5_NS), F32),
            pltpu.VMEM((S5_SEG, 2 * S5_NS), F32),
            pltpu.VMEM((S5_SEG, 2 * S5_NS), F32),
            pltpu.VMEM((S5_ROWS, SSM_WIDTH), F32),
        ],
        compiler_params=pltpu.CompilerParams(
            dimension_semantics=("parallel", "arbitrary", "arbitrary"), vmem_limit_bytes=VMEM_LIMIT),
        name="s5",
    )(u_perm, wb, a_re, a_im, cre, cim, d_skip, glu_w, glu_b, out_gain)


def _s5_mixer(proj, p, l):
    u = proj[:, :SSM_WIDTH].reshape(BATCH, S5_SEG, S5_SEGLEN, SSM_WIDTH)
    u_perm = jnp.swapaxes(u, 1, 2).reshape(BATCH, SEQ, SSM_WIDTH)
    wb, a_re, a_im, cre, cim = _s5_params(p['ssm_lambda_re'][l], p['ssm_lambda_im'][l], p['ssm_b_re'][l],
                                          p['ssm_b_im'][l], p['ssm_c_re'][l], p['ssm_c_im'][l],
                                          p['ssm_log_dt'][l])
    y = _s5(u_perm, wb, a_re, a_im, cre, cim, p['ssm_d'][l].reshape(1, SSM_WIDTH),
            p['ssm_glu_w'][l].astype(BF16), p['ssm_glu_b'][l].reshape(1, SSM_WIDTH),
            p['ssm_out_gain'][l].reshape(1, SSM_WIDTH))
    y = jnp.swapaxes(y.reshape(BATCH, S5_SEGLEN, S5_SEG, SSM_WIDTH), 1, 2)
    return y.reshape(N_TOK, SSM_WIDTH)


def _rms_norm(x, gain):
    xf = x.astype(jnp.float32)
    y = xf * lax.rsqrt(jnp.mean(xf * xf, axis=-1, keepdims=True) + NORM_EPS)
    return (y * gain.astype(jnp.float32)).astype(x.dtype)


def _l2n(x):
    return x * lax.rsqrt(jnp.sum(x * x, axis=-1, keepdims=True) + NORM_EPS)


def _s5_jnp(u, lam_re, lam_im, b_re, b_im, c_re, c_im, d_skip, log_dt, glu_w, glu_b):
    bsz, seq, _ = u.shape
    f32 = jnp.float32
    uf = u.astype(f32).reshape(bsz, seq, SSM_GROUPS, SSM_CH)
    lam = lax.complex(lam_re.astype(f32), lam_im.astype(f32))
    dt = jnp.exp(log_dt.astype(f32))[:, None]
    lam_bar = jnp.exp(lam * dt)
    b = lax.complex(b_re.astype(f32), b_im.astype(f32))
    b_bar = ((lam_bar - 1.0) / lam)[..., None] * b
    c = lax.complex(c_re.astype(f32), c_im.astype(f32))
    bu = jnp.einsum('gpc,bsgc->bsgp', b_bar, uf.astype(jnp.complex64))
    a = jnp.broadcast_to(lam_bar, bu.shape)

    def combine(e1, e2):
        a1, b1 = e1
        a2, b2 = e2
        return a1 * a2, a2 * b1 + b2

    _, states = lax.associative_scan(combine, (a, bu), axis=1)
    y = jnp.einsum('gcp,bsgp->bsgc', c, states).real + d_skip.astype(f32).reshape(SSM_GROUPS, SSM_CH) * uf
    y = jax.nn.gelu(y.reshape(bsz, seq, SSM_WIDTH))
    return y * jax.nn.sigmoid(y @ glu_w.astype(f32) + glu_b.astype(f32))


def _to_chunks(t, chunk):
    bsz, seq, heads = t.shape[:3]
    t = t.reshape(bsz, seq // chunk, chunk, heads, *t.shape[3:])
    return jnp.moveaxis(t, 3, 1)


def _gdr(q, k, v, g, beta):
    bsz, seq, heads, dk = q.shape
    dv = v.shape[-1]
    q = _to_chunks(q * dk ** -0.5, DN_CHUNK)
    k, v = _to_chunks(k, DN_CHUNK), _to_chunks(v, DN_CHUNK)
    g, beta = _to_chunks(g, DN_CHUNK), _to_chunks(beta, DN_CHUNK)
    gc = jnp.cumsum(g, axis=-1)
    idx = jnp.arange(DN_CHUNK)
    causal = idx[:, None] >= idx[None, :]
    strict = idx[:, None] > idx[None, :]
    decay = jnp.exp(jnp.where(causal, gc[..., :, None] - gc[..., None, :], NEG_INF))
    k_beta = k * beta[..., None]
    a_mat = jnp.where(strict, jnp.einsum('bhnck,bhnek->bhnce', k_beta, k) * decay, 0.0)
    eye = jnp.eye(DN_CHUNK, dtype=jnp.float32)
    t_inv = lax.linalg.triangular_solve(eye + a_mat, jnp.broadcast_to(eye, a_mat.shape),
                                        left_side=True, lower=True, unit_diagonal=True)
    u = jnp.einsum('bhnce,bhnev->bhncv', t_inv, v * beta[..., None])
    w = jnp.einsum('bhnce,bhnek->bhnck', t_inv, k_beta * jnp.exp(gc)[..., None])
    attn = jnp.einsum('bhnck,bhnek->bhnce', q, k) * decay
    q_dec = q * jnp.exp(gc)[..., None]
    k_tail = k * jnp.exp(gc[..., -1:] - gc)[..., None]
    chunk_decay = jnp.exp(gc[..., -1])
    xs = tuple(jnp.moveaxis(t, 2, 0) for t in (q_dec, k_tail, u, w, attn, chunk_decay))

    def step(state, inp):
        qd, kt, un, wn, an, dec = inp
        v_new = un - jnp.einsum('bhck,bhkv->bhcv', wn, state)
        o = jnp.einsum('bhck,bhkv->bhcv', qd, state) + jnp.einsum('bhce,bhev->bhcv', an, v_new)
        state = state * dec[..., None, None] + jnp.einsum('bhck,bhcv->bhkv', kt, v_new)
        return state, o

    state0 = jnp.zeros((bsz, heads, dk, dv), jnp.float32)
    _, o = lax.scan(step, state0, xs)
    o = jnp.moveaxis(o, 0, 2).reshape(bsz, heads, seq, dv)
    return jnp.swapaxes(o, 1, 2)


def _t5_bucket(dist):
    max_exact = N_BUCKETS // 2
    d = jnp.maximum(dist, 1).astype(jnp.float32)
    large = max_exact + jnp.log(d / max_exact) / math.log(REL_MAX_DIST / max_exact) * (N_BUCKETS - max_exact)
    large = jnp.minimum(large.astype(jnp.int32), N_BUCKETS - 1)
    return jnp.where(dist < max_exact, dist, large)


def _dil_branch(q, k, v, rel_bias, window, dil):
    bsz, seq, heads, dh = q.shape
    blk = ATTN_BLOCK
    sub_len = seq // dil
    n_blocks = -(-sub_len // blk)
    sub_pad = n_blocks * blk

    def to_sub(t):
        t = t.astype(jnp.float32).reshape(bsz, sub_len, dil, heads, dh).transpose(0, 2, 1, 3, 4)
        return jnp.pad(t, ((0, 0), (0, 0), (0, sub_pad - sub_len), (0, 0), (0, 0)))

    def band(t):
        tp = jnp.pad(t, ((0, 0), (0, 0), (blk, 0), (0, 0), (0, 0)))
        prev = tp[:, :, :sub_pad].reshape(bsz, dil, n_blocks, blk, heads, dh)
        cur = tp[:, :, blk:].reshape(bsz, dil, n_blocks, blk, heads, dh)
        return jnp.concatenate([prev, cur], axis=3)

    qb = to_sub(q).reshape(bsz, dil, n_blocks, blk, heads, dh) * dh ** -0.5
    kb = band(to_sub(k))
    vb = band(to_sub(v))
    rel = blk + jnp.arange(blk)[:, None] - jnp.arange(2 * blk)[None, :]
    key_idx = jnp.arange(n_blocks)[:, None] * blk + jnp.arange(2 * blk)[None, :] - blk
    valid = ((rel >= 0) & (rel <= window // dil))[None] & (key_idx >= 0)[:, None, :]
    bias = jnp.moveaxis(rel_bias.astype(jnp.float32)[_t5_bucket(jnp.maximum(rel, 0) * dil)], -1, 0)
    logits = jnp.einsum('bgnqhd,bgnkhd->bhgnqk', qb, kb) + bias[None, :, None, None]
    logits = jnp.where(valid[None, None, None], logits, NEG_INF)
    m = jnp.max(logits, axis=-1, keepdims=True)
    p = jnp.exp(logits - m)
    s = jnp.sum(p, axis=-1, keepdims=True)
    o = jnp.einsum('bhgnqk,bgnkhd->bgnqhd', p / s, vb)
    lse = (m + jnp.log(s))[..., 0].reshape(bsz, heads, dil, sub_pad)[..., :sub_len]
    o = o.reshape(bsz, dil, sub_pad, heads, dh)[:, :, :sub_len].transpose(0, 2, 1, 3, 4).reshape(bsz, seq, heads, dh)
    lse = lse.transpose(0, 3, 2, 1).reshape(bsz, seq, heads)
    return o, lse


def _dil_attn(q, k, v, rel_bias):
    outs, lses = [], []
    for window, dil in DILATED_PAIRS:
        o, lse = _dil_branch(q, k, v, rel_bias, window, dil)
        outs.append(o)
        lses.append(lse)
    weights = jax.nn.softmax(jnp.stack(lses, axis=0), axis=0)
    return jnp.einsum('gbsh,gbshd->bshd', weights, jnp.stack(outs, axis=0))


def _mixers_jnp(proj, p, l, rel_bias):
    bsz, seq = BATCH, SEQ
    f32 = jnp.float32
    proj = proj[:, :N_IN_COLS].reshape(bsz, seq, N_IN_COLS)
    cuts = [sum(IN_SPLITS[:i + 1]) for i in range(len(IN_SPLITS) - 1)]
    u_ssm, a_q, a_k, a_v, dn_qkv, dn_z, dn_a, dn_b = jnp.split(proj, cuts, axis=-1)
    qkv = jax.nn.silu(lax.conv_general_dilated(
        dn_qkv, p['dn_conv_w'][l][:, None, :], window_strides=(1,), padding=((DN_CONV - 1, 0),),
        dimension_numbers=('NWC', 'WIO', 'NWC'), feature_group_count=3 * DN_WIDTH)).astype(f32)
    d_q, d_k, d_v = [t.reshape(bsz, seq, DN_HEADS, DN_HEAD_DIM) for t in jnp.split(qkv, 3, axis=-1)]
    d_q, d_k = _l2n(d_q), _l2n(d_k)
    beta = jax.nn.sigmoid(dn_b.astype(f32))
    g = -jnp.exp(p['dn_a_log'][l].astype(f32)) * jax.nn.softplus(dn_a.astype(f32) + p['dn_dt_bias'][l].astype(f32))
    o_dn = _gdr(d_q, d_k, d_v, g, beta)
    o_dn = _rms_norm(o_dn, p['dn_norm_gain'][l]) * jax.nn.silu(dn_z.astype(f32).reshape(bsz, seq, DN_HEADS, DN_HEAD_DIM))
    y_dn = o_dn.reshape(bsz * seq, DN_WIDTH)
    at_q, at_k, at_v = [t.reshape(bsz, seq, ATTN_HEADS, ATTN_HEAD_DIM) for t in (a_q, a_k, a_v)]
    o_at = _dil_attn(at_q, at_k, at_v, rel_bias).reshape(bsz * seq, ATTN_WIDTH)
    return y_dn, o_at


def kernel(x, norm_gains, ffn_w_gate, ffn_w_up, ffn_w_down, w_in, w_out, ssm_lambda_re, ssm_lambda_im,
           ssm_b_re, ssm_b_im, ssm_c_re, ssm_c_im, ssm_d, ssm_log_dt, ssm_glu_w, ssm_glu_b, ssm_out_gain,
           dn_conv_w, dn_a_log, dn_dt_bias, dn_norm_gain, attn_out_gain, rel_bias):
    p = dict(ssm_lambda_re=ssm_lambda_re, ssm_lambda_im=ssm_lambda_im, ssm_b_re=ssm_b_re, ssm_b_im=ssm_b_im,
             ssm_c_re=ssm_c_re, ssm_c_im=ssm_c_im, ssm_d=ssm_d, ssm_log_dt=ssm_log_dt, ssm_glu_w=ssm_glu_w,
             ssm_glu_b=ssm_glu_b, ssm_out_gain=ssm_out_gain, dn_conv_w=dn_conv_w, dn_a_log=dn_a_log,
             dn_dt_bias=dn_dt_bias, dn_norm_gain=dn_norm_gain)
    wg = ffn_w_gate.astype(BF16)
    wu = ffn_w_up.astype(BF16)
    wd = ffn_w_down.astype(BF16)
    w_in_p = jnp.pad(w_in, ((0, 0), (0, 0), (0, N_IN_PAD - N_IN_COLS))).astype(BF16)
    w_out_b = w_out.astype(BF16)
    gains = norm_gains.reshape(DEPTH, 6, 1, D_MODEL)
    x = x.reshape(N_TOK, D_MODEL)
    for l in range(DEPTH):
        x = _ffn(x, gains[l, 0], gains[l, 1], wg[l, 0], wu[l, 0], wd[l, 0])
        proj = _inproj(x, gains[l, 2], w_in_p[l])
        y_ssm = _s5_mixer(proj, p, l)
        y_dn, o_at = _mixers_jnp(proj, p, l, rel_bias)
        x = _outproj(x, y_ssm, y_dn, o_at, attn_out_gain[l].reshape(1, ATTN_WIDTH), gains[l, 3], w_out_b[l])
        x = _ffn(x, gains[l, 4], gains[l, 5], wg[l, 1], wu[l, 1], wd[l, 1])
    return x.reshape(BATCH, SEQ, D_MODEL)
```

```python
import functools
import math

import jax
import jax.numpy as jnp
import numpy as np
from jax import lax
from jax.experimental import pallas as pl
from jax.experimental.pallas import tpu as pltpu

D_MODEL = 2048
BATCH = 2
SEQ = 4096
DEPTH = 4
SSM_GROUPS = 32
SSM_CH = 16
SSM_STATE = 64
SSM_WIDTH = SSM_GROUPS * SSM_CH
DN_HEADS = 6
DN_HEAD_DIM = 128
DN_WIDTH = DN_HEADS * DN_HEAD_DIM
DN_CONV = 4
DN_CHUNK = 64
ATTN_HEADS = 6
ATTN_HEAD_DIM = 128
ATTN_WIDTH = ATTN_HEADS * ATTN_HEAD_DIM
DILATED_PAIRS = ((128, 1), (512, 4), (2048, 16))
ATTN_BLOCK = 128
N_BUCKETS = 32
REL_MAX_DIST = 2048
D_MIX = SSM_WIDTH + DN_WIDTH + ATTN_WIDTH
IN_SPLITS = (SSM_WIDTH, ATTN_WIDTH, ATTN_WIDTH, ATTN_WIDTH, 3 * DN_WIDTH, DN_WIDTH, DN_HEADS, DN_HEADS)
N_IN_COLS = sum(IN_SPLITS)
D_FF = 5632
NORM_EPS = 1e-6
NEG_INF = -1e30

LANES = 128
N_IN_PAD = 6144
VMEM_LIMIT = 56 * 1024 * 1024
N_TOK = BATCH * SEQ

BF16 = jnp.bfloat16
F32 = jnp.float32


def _rms(x, gain):
    return x * lax.rsqrt(jnp.mean(x * x, axis=-1, keepdims=True) + NORM_EPS) * gain


FFN_TM = 512
FFN_TF = 512


def _ffn_kernel(x_ref, gpre_ref, gpost_ref, wg_ref, wu_ref, wd_ref, o_ref, h_scr, acc_scr):
    f = pl.program_id(1)

    @pl.when(f == 0)
    def _():
        h_scr[...] = _rms(x_ref[...], gpre_ref[...]).astype(BF16)
        acc_scr[...] = jnp.zeros_like(acc_scr)

    h = h_scr[...]
    gate = jnp.dot(h, wg_ref[...], preferred_element_type=F32)
    up = jnp.dot(h, wu_ref[...], preferred_element_type=F32)
    act = (gate * jax.nn.sigmoid(gate) * up).astype(BF16)
    acc_scr[...] += jnp.dot(act, wd_ref[...], preferred_element_type=F32)

    @pl.when(f == pl.num_programs(1) - 1)
    def _():
        o_ref[...] = x_ref[...] + 0.5 * _rms(acc_scr[...], gpost_ref[...])


def _ffn(x, g_pre, g_post, wg, wu, wd):
    return pl.pallas_call(
        _ffn_kernel,
        grid=(N_TOK // FFN_TM, D_FF // FFN_TF),
        in_specs=[
            pl.BlockSpec((FFN_TM, D_MODEL), lambda i, f: (i, 0)),
            pl.BlockSpec((1, D_MODEL), lambda i, f: (0, 0)),
            pl.BlockSpec((1, D_MODEL), lambda i, f: (0, 0)),
            pl.BlockSpec((D_MODEL, FFN_TF), lambda i, f: (0, f)),
            pl.BlockSpec((D_MODEL, FFN_TF), lambda i, f: (0, f)),
            pl.BlockSpec((FFN_TF, D_MODEL), lambda i, f: (f, 0)),
        ],
        out_specs=pl.BlockSpec((FFN_TM, D_MODEL), lambda i, f: (i, 0)),
        out_shape=jax.ShapeDtypeStruct((N_TOK, D_MODEL), F32),
        scratch_shapes=[pltpu.VMEM((FFN_TM, D_MODEL), BF16), pltpu.VMEM((FFN_TM, D_MODEL), F32)],
        compiler_params=pltpu.CompilerParams(
            dimension_semantics=("parallel", "arbitrary"), vmem_limit_bytes=VMEM_LIMIT),
        name="ffn",
    )(x, g_pre, g_post, wg, wu, wd)


INP_TM = 512
INP_TN = 768


def _inproj_kernel(x_ref, g_ref, w_ref, o_ref, h_scr):
    @pl.when(pl.program_id(1) == 0)
    def _():
        h_scr[...] = _rms(x_ref[...], g_ref[...]).astype(BF16)

    o_ref[...] = jnp.dot(h_scr[...], w_ref[...], preferred_element_type=F32)


def _inproj(x, g, w):
    return pl.pallas_call(
        _inproj_kernel,
        grid=(N_TOK // INP_TM, N_IN_PAD // INP_TN),
        in_specs=[
            pl.BlockSpec((INP_TM, D_MODEL), lambda i, n: (i, 0)),
            pl.BlockSpec((1, D_MODEL), lambda i, n: (0, 0)),
            pl.BlockSpec((D_MODEL, INP_TN), lambda i, n: (0, n)),
        ],
        out_specs=pl.BlockSpec((INP_TM, INP_TN), lambda i, n: (i, n)),
        out_shape=jax.ShapeDtypeStruct((N_TOK, N_IN_PAD), F32),
        scratch_shapes=[pltpu.VMEM((INP_TM, D_MODEL), BF16)],
        compiler_params=pltpu.CompilerParams(
            dimension_semantics=("parallel", "arbitrary"), vmem_limit_bytes=VMEM_LIMIT),
        name="inproj",
    )(x, g, w)


OUT_TM = 256


def _outproj_kernel(x_ref, ys_ref, yd_ref, oa_ref, ga_ref, gpost_ref, w_ref, o_ref):
    ya = _rms(oa_ref[...], ga_ref[...]).astype(BF16)
    mix = jnp.dot(ys_ref[...].astype(BF16), w_ref[0:SSM_WIDTH, :], preferred_element_type=F32)
    mix += jnp.dot(yd_ref[...].astype(BF16), w_ref[SSM_WIDTH:SSM_WIDTH + DN_WIDTH, :],
                   preferred_element_type=F32)
    mix += jnp.dot(ya, w_ref[SSM_WIDTH + DN_WIDTH:D_MIX, :], preferred_element_type=F32)
    o_ref[...] = x_ref[...] + _rms(mix, gpost_ref[...])


def _outproj(x, y_ssm, y_dn, o_at, g_attn, g_post, w):
    row = lambda i: (i, 0)
    fixed = lambda i: (0, 0)
    return pl.pallas_call(
        _outproj_kernel,
        grid=(N_TOK // OUT_TM,),
        in_specs=[
            pl.BlockSpec((OUT_TM, D_MODEL), row),
            pl.BlockSpec((OUT_TM, SSM_WIDTH), row),
            pl.BlockSpec((OUT_TM, DN_WIDTH), row),
            pl.BlockSpec((OUT_TM, ATTN_WIDTH), row),
            pl.BlockSpec((1, ATTN_WIDTH), fixed),
            pl.BlockSpec((1, D_MODEL), fixed),
            pl.BlockSpec((D_MIX, D_MODEL), fixed),
        ],
        out_specs=pl.BlockSpec((OUT_TM, D_MODEL), row),
        out_shape=jax.ShapeDtypeStruct((N_TOK, D_MODEL), F32),
        compiler_params=pltpu.CompilerParams(
            dimension_semantics=("parallel",), vmem_limit_bytes=VMEM_LIMIT),
        name="outproj",
    )(x, y_ssm, y_dn, o_at, g_attn, g_post, w)


S5_SEG = 8
S5_SEGLEN = SEQ // S5_SEG
S5_KB = 64
S5_ROWS = S5_KB * S5_SEG
S5_NS = SSM_GROUPS * SSM_STATE
S5_Q = 4
S5_QS = S5_NS // S5_Q
S5_QC = SSM_WIDTH // S5_Q
S5_LOG2_SEGLEN = 9
assert 1 << S5_LOG2_SEGLEN == S5_SEGLEN


def _s5_kernel(u_ref, wb_ref, are_ref, aim_ref, cre_ref, cim_ref, d_ref, gw_ref, gb_ref, go_ref, o_ref,
               bu_scr, st_scr, carry_scr, y_scr):
    p = pl.program_id(1)
    j = pl.program_id(2)
    ub = u_ref[0].astype(BF16)
    for q in range(S5_Q):
        r = jnp.dot(ub[:, q * S5_QC:(q + 1) * S5_QC], wb_ref[q], preferred_element_type=F32)
        bu_scr[:, q * S5_QS:(q + 1) * S5_QS] = r[:, :S5_QS]
        bu_scr[:, S5_NS + q * S5_QS:S5_NS + (q + 1) * S5_QS] = r[:, S5_QS:]

    @pl.when(jnp.logical_and(p == 0, j == 0))
    def _():
        st_scr[...] = jnp.zeros_like(st_scr)

    @pl.when(jnp.logical_and(p == 1, j == 0))
    def _():
        st_scr[...] = carry_scr[...]

    def scan(store):
        for q in range(S5_Q):
            re_cols = slice(q * S5_QS, (q + 1) * S5_QS)
            im_cols = slice(S5_NS + q * S5_QS, S5_NS + (q + 1) * S5_QS)
            ar = jnp.broadcast_to(are_ref[:, re_cols], (S5_SEG, S5_QS))
            ai = jnp.broadcast_to(aim_ref[:, re_cols], (S5_SEG, S5_QS))

            def body(k, carry):
                sr, si = carry
                rows = pl.ds(pl.multiple_of(k * S5_SEG, S5_SEG), S5_SEG)
                nr = ar * sr - ai * si + bu_scr[rows, re_cols]
                ni = ar * si + ai * sr + bu_scr[rows, im_cols]
                if store:
                    bu_scr[rows, re_cols] = nr
                    bu_scr[rows, im_cols] = ni
                return nr, ni

            sr, si = lax.fori_loop(0, S5_KB, body, (st_scr[:, re_cols], st_scr[:, im_cols]), unroll=8)
            st_scr[:, re_cols] = sr
            st_scr[:, im_cols] = si

    @pl.when(p == 0)
    def _():
        scan(False)

        @pl.when(j == pl.num_programs(2) - 1)
        def _():
            lr, li = are_ref[...], aim_ref[...]
            for _ in range(S5_LOG2_SEGLEN):
                lr, li = lr * lr - li * li, 2.0 * lr * li
            cr = jnp.zeros((1, S5_NS), F32)
            ci = jnp.zeros((1, S5_NS), F32)
            carry_scr[0:1, :] = jnp.zeros((1, 2 * S5_NS), F32)
            for i in range(1, S5_SEG):
                er = st_scr[i - 1:i, 0:S5_NS]
                ei = st_scr[i - 1:i, S5_NS:2 * S5_NS]
                cr, ci = er + lr * cr - li * ci, ei + lr * ci + li * cr
                carry_scr[i:i + 1, 0:S5_NS] = cr
                carry_scr[i:i + 1, S5_NS:2 * S5_NS] = ci

    @pl.when(p == 1)
    def _():
        scan(True)
        for q in range(S5_Q):
            sre = bu_scr[:, q * S5_QS:(q + 1) * S5_QS].astype(BF16)
            sim = bu_scr[:, S5_NS + q * S5_QS:S5_NS + (q + 1) * S5_QS].astype(BF16)
            y_scr[:, q * S5_QC:(q + 1) * S5_QC] = (
                jnp.dot(sre, cre_ref[q], preferred_element_type=F32)
                + jnp.dot(sim, cim_ref[q], preferred_element_type=F32))
        y = jax.nn.gelu(y_scr[...] + d_ref[...] * u_ref[0])
        z = jnp.dot(y.astype(BF16), gw_ref[...], preferred_element_type=F32) + gb_ref[...]
        o_ref[0] = _rms(y * jax.nn.sigmoid(z), go_ref[...])


def _s5_params(lam_re, lam_im, b_re, b_im, c_re, c_im, log_dt):
    lam = lax.complex(lam_re, lam_im)
    lam_bar = jnp.exp(lam * jnp.exp(log_dt)[:, None])
    b_bar = ((lam_bar - 1.0) / lam)[..., None] * lax.complex(b_re, b_im)
    gq = SSM_GROUPS // S5_Q
    eye = jnp.eye(gq, dtype=F32)

    def in_map(t):
        t = t.reshape(S5_Q, gq, SSM_STATE, SSM_CH)
        return jnp.einsum('qgpc,gh->qgchp', t, eye).reshape(S5_Q, S5_QC, S5_QS)

    def out_map(t):
        t = t.reshape(S5_Q, gq, SSM_CH, SSM_STATE)
        return jnp.einsum('qgcp,gh->qgphc', t, eye).reshape(S5_Q, S5_QS, S5_QC)

    wb = jnp.concatenate([in_map(b_bar.real), in_map(b_bar.imag)], axis=-1).astype(BF16)
    return (wb, lam_bar.real.reshape(1, S5_NS), lam_bar.imag.reshape(1, S5_NS),
            out_map(c_re).astype(BF16), out_map(-c_im).astype(BF16))


def _s5(u_perm, wb, a_re, a_im, cre, cim, d_skip, glu_w, glu_b, out_gain):
    nblk = S5_SEGLEN // S5_KB
    fix2 = lambda b, p, j: (0, 0)
    fix3 = lambda b, p, j: (0, 0, 0)
    return pl.pallas_call(
        _s5_kernel,
        grid=(BATCH, 2, nblk),
        in_specs=[
            pl.BlockSpec((1, S5_ROWS, SSM_WIDTH), lambda b, p, j: (b, j, 0)),
            pl.BlockSpec((S5_Q, S5_QC, 2 * S5_QS), fix3),
            pl.BlockSpec((1, S5_NS), fix2),
            pl.BlockSpec((1, S5_NS), fix2),
            pl.BlockSpec((S5_Q, S5_QS, S5_QC), fix3),
            pl.BlockSpec((S5_Q, S5_QS, S5_QC), fix3),
            pl.BlockSpec((1, SSM_WIDTH), fix2),
            pl.BlockSpec((SSM_WIDTH, SSM_WIDTH), fix2),
            pl.BlockSpec((1, SSM_WIDTH), fix2),
            pl.BlockSpec((1, SSM_WIDTH), fix2),
        ],
        out_specs=pl.BlockSpec((1, S5_ROWS, SSM_WIDTH), lambda b, p, j: (b, j * p, 0)),
        out_shape=jax.ShapeDtypeStruct((BATCH, SEQ, SSM_WIDTH), F32),
        scratch_shapes=[
            pltpu.VMEM((S5_ROWS, 2 * S5_NS), F32),
            pltpu.VMEM((S5_SEG, 2 * S5_NS), F32),
            pltpu.VMEM((S5_SEG, 2 * S5_NS), F32),
            pltpu.VMEM((S5_ROWS, SSM_WIDTH), F32),
        ],
        compiler_params=pltpu.CompilerParams(
            dimension_semantics=("parallel", "arbitrary", "arbitrary"), vmem_limit_bytes=VMEM_LIMIT),
        name="s5",
    )(u_perm, wb, a_re, a_im, cre, cim, d_skip, glu_w, glu_b, out_gain)


def _s5_mixer(proj, p, l):
    u = proj[:, :SSM_WIDTH].reshape(BATCH, S5_SEG, S5_SEGLEN, SSM_WIDTH)
    u_perm = jnp.swapaxes(u, 1, 2).reshape(BATCH, SEQ, SSM_WIDTH)
    wb, a_re, a_im, cre, cim = _s5_params(p['ssm_lambda_re'][l], p['ssm_lambda_im'][l], p['ssm_b_re'][l],
                                          p['ssm_b_im'][l], p['ssm_c_re'][l], p['ssm_c_im'][l],
                                          p['ssm_log_dt'][l])
    y = _s5(u_perm, wb, a_re, a_im, cre, cim, p['ssm_d'][l].reshape(1, SSM_WIDTH),
            p['ssm_glu_w'][l].astype(BF16), p['ssm_glu_b'][l].reshape(1, SSM_WIDTH),
            p['ssm_out_gain'][l].reshape(1, SSM_WIDTH))
    y = jnp.swapaxes(y.reshape(BATCH, S5_SEGLEN, S5_SEG, SSM_WIDTH), 1, 2)
    return y.reshape(N_TOK, SSM_WIDTH)


AT_BLK = ATTN_BLOCK
AT_NBRANCH = len(DILATED_PAIRS)
AT_BLOCKS = SEQ // AT_BLK
AT_COMBINE_ROWS = 256
assert all(w // d == AT_BLK for w, d in DILATED_PAIRS)
AT_Q_COL0 = SSM_WIDTH // LANES
AT_K_COL0 = AT_Q_COL0 + ATTN_HEADS
AT_V_COL0 = AT_K_COL0 + ATTN_HEADS


def _attn_kernel(rb_ref, bkt_ref, q_ref, k_ref, v_ref, o_ref, bias_scr, o_scr, lse_scr):
    h = pl.program_id(1)
    for g in range(AT_NBRANCH):
        bkt = bkt_ref[g]
        bias = jnp.zeros((AT_BLK, 2 * AT_BLK), F32)
        for b in range(N_BUCKETS):
            bias = jnp.where(bkt == b, rb_ref[b, h], bias)
        bias_scr[g] = jnp.where(bkt < 0, NEG_INF, bias)

    lane = lax.broadcasted_iota(jnp.int32, (AT_BLK, 2 * AT_BLK), 1)
    scale = ATTN_HEAD_DIM ** -0.5
    for g, (_, dil) in enumerate(DILATED_PAIRS):
        nb = AT_BLOCKS // dil

        def body(t, carry, g=g, dil=dil, nb=nb):
            r = t // nb
            n = t % nb
            start = r + n * (AT_BLK * dil)
            pstart = r + jnp.maximum(n - 1, 0) * (AT_BLK * dil)
            if dil == 1:
                rows = pl.ds(pl.multiple_of(start, AT_BLK), AT_BLK)
                prows = pl.ds(pl.multiple_of(pstart, AT_BLK), AT_BLK)
            else:
                rows = pl.ds(start, AT_BLK, stride=dil)
                prows = pl.ds(pstart, AT_BLK, stride=dil)
            q = (q_ref[rows, :] * scale).astype(BF16)
            kc = jnp.concatenate([k_ref[prows, :], k_ref[rows, :]], axis=0).astype(BF16)
            vc = jnp.concatenate([v_ref[prows, :], v_ref[rows, :]], axis=0).astype(BF16)
            s = lax.dot_general(q, kc, (((1,), (1,)), ((), ())), preferred_element_type=F32) + bias_scr[g]
            s = jnp.where(jnp.logical_or(n > 0, lane >= AT_BLK), s, NEG_INF)
            m = jnp.max(s, axis=1, keepdims=True)
            p = jnp.exp(s - m)
            l = jnp.sum(p, axis=1, keepdims=True)
            o = jnp.dot(p.astype(BF16), vc, preferred_element_type=F32) / l
            o_scr[g, rows, :] = o
            lse_scr[g, rows, :] = jnp.broadcast_to(m + jnp.log(l), (AT_BLK, LANES))
            return carry

        lax.fori_loop(0, AT_BLOCKS, body, 0)

    def combine(i, carry):
        rows = pl.ds(pl.multiple_of(i * AT_COMBINE_ROWS, AT_COMBINE_ROWS), AT_COMBINE_ROWS)
        l0, l1, l2 = lse_scr[0, rows, :], lse_scr[1, rows, :], lse_scr[2, rows, :]
        m = jnp.maximum(jnp.maximum(l0, l1), l2)
        w0, w1, w2 = jnp.exp(l0 - m), jnp.exp(l1 - m), jnp.exp(l2 - m)
        num = w0 * o_scr[0, rows, :] + w1 * o_scr[1, rows, :] + w2 * o_scr[2, rows, :]
        o_ref[rows, :] = num / (w0 + w1 + w2)
        return carry

    lax.fori_loop(0, SEQ // AT_COMBINE_ROWS, combine, 0)


def _attn_bucket_table():
    qi = jnp.arange(AT_BLK)[:, None]
    kj = jnp.arange(2 * AT_BLK)[None, :]
    rel = AT_BLK + qi - kj
    max_exact = N_BUCKETS // 2
    tables = []
    for window, dil in DILATED_PAIRS:
        dist = jnp.maximum(rel, 0) * dil
        d = jnp.maximum(dist, 1).astype(F32)
        large = max_exact + jnp.log(d / max_exact) / math.log(REL_MAX_DIST / max_exact) * (N_BUCKETS - max_exact)
        large = jnp.minimum(large.astype(jnp.int32), N_BUCKETS - 1)
        bucket = jnp.where(dist < max_exact, dist, large)
        tables.append(jnp.where((rel >= 0) & (rel <= window // dil), bucket, -1))
    return jnp.stack(tables).astype(jnp.int32)


def _attn(proj, rel_bias, bucket_table):
    col = lambda c0: pl.BlockSpec((SEQ, LANES), lambda b, h: (b, c0 + h))
    return pl.pallas_call(
        _attn_kernel,
        grid=(BATCH, ATTN_HEADS),
        in_specs=[
            pl.BlockSpec(memory_space=pltpu.SMEM),
            pl.BlockSpec((AT_NBRANCH, AT_BLK, 2 * AT_BLK), lambda b, h: (0, 0, 0)),
            col(AT_Q_COL0), col(AT_K_COL0), col(AT_V_COL0),
        ],
        out_specs=pl.BlockSpec((SEQ, LANES), lambda b, h: (b, h)),
        out_shape=jax.ShapeDtypeStruct((N_TOK, ATTN_WIDTH), F32),
        scratch_shapes=[
            pltpu.VMEM((AT_NBRANCH, AT_BLK, 2 * AT_BLK), F32),
            pltpu.VMEM((AT_NBRANCH, SEQ, LANES), F32),
            pltpu.VMEM((AT_NBRANCH, SEQ, LANES), F32),
        ],
        compiler_params=pltpu.CompilerParams(
            dimension_semantics=("parallel", "parallel"), vmem_limit_bytes=VMEM_LIMIT),
        name="dilated_attn",
    )(rel_bias, bucket_table, proj, proj, proj)


DN_TILE = 2 * DN_CHUNK
DN_NTILE = SEQ // DN_TILE
DN_BH = BATCH * DN_HEADS
DN_PAD = 8
DN_NEUMANN_STEPS = 5
assert 2 ** (DN_NEUMANN_STEPS + 1) == DN_CHUNK
DN_QKV_COL0 = (SSM_WIDTH + 3 * ATTN_WIDTH) // LANES
DN_Z_COL0 = DN_QKV_COL0 + 3 * DN_HEADS
DN_AB_COL = DN_Z_COL0 + DN_HEADS
HIGHEST = lax.Precision.HIGHEST


def _dot_nt(a, b):
    return lax.dot_general(a, b, (((1,), (1,)), ((), ())), preferred_element_type=F32)


def _split_bf16(x):
    hi = x.astype(BF16)
    return hi, (x - hi.astype(F32)).astype(BF16)


def _mm3(a, b):
    ah, al = _split_bf16(a)
    bh, bl = _split_bf16(b)
    return (jnp.dot(ah, bh, preferred_element_type=F32) + jnp.dot(ah, bl, preferred_element_type=F32)
            + jnp.dot(al, bh, preferred_element_type=F32))


def _dn_prep_kernel(alog_ref, dtb_ref, q_ref, k_ref, v_ref, z_ref, ab_ref, wq_ref, wk_ref, wv_ref,
                    u_o, w_o, qd_o, at_o, kt_o, dec_o, sz_o, qp, kp, vp):
    h = pl.program_id(1)
    for src, dst in ((q_ref, qp), (k_ref, kp), (v_ref, vp)):
        dst[0:DN_PAD, :] = jnp.zeros((DN_PAD, LANES), F32)
        dst[DN_PAD:DN_PAD + SEQ, :] = src[...]

    neg_a = -jnp.exp(jnp.full((1, LANES), alog_ref[h], F32))
    dt_bias = dtb_ref[h]
    row = lax.broadcasted_iota(jnp.int32, (DN_TILE, DN_TILE), 0)
    col = lax.broadcasted_iota(jnp.int32, (DN_TILE, DN_TILE), 1)
    same = (row // DN_CHUNK) == (col // DN_CHUNK)
    causal = jnp.logical_and(same, row >= col)
    strict = jnp.logical_and(same, row > col)
    cumsum_mat = causal.astype(F32)
    pick_a = (row == h).astype(F32)
    pick_b = (row == h + DN_HEADS).astype(F32)
    eye = (row == col).astype(F32)
    scale = DN_HEAD_DIM ** -0.5

    def conv_silu(pad_ref, w_ref, base):
        win = pad_ref[pl.ds(base, DN_TILE + DN_PAD), :]
        acc = None
        for j in range(DN_CONV):
            sh = DN_PAD - (DN_CONV - 1) + j
            term = w_ref[j:j + 1, :] * win[sh:sh + DN_TILE, :]
            acc = term if acc is None else acc + term
        return acc * jax.nn.sigmoid(acc)

    def l2n(x):
        return x * lax.rsqrt(jnp.sum(x * x, axis=1, keepdims=True) + NORM_EPS)

    def body(t, carry):
        base = pl.multiple_of(t * DN_TILE, DN_TILE)
        rows = pl.ds(base, DN_TILE)
        q = l2n(conv_silu(qp, wq_ref, base)) * scale
        k = l2n(conv_silu(kp, wk_ref, base))
        v = conv_silu(vp, wv_ref, base)
        ab = ab_ref[rows, :]
        a_rep = jnp.dot(ab, pick_a, precision=HIGHEST, preferred_element_type=F32)
        b_rep = jnp.dot(ab, pick_b, precision=HIGHEST, preferred_element_type=F32)
        beta = jax.nn.sigmoid(b_rep)
        x = a_rep + dt_bias
        g = neg_a * (jnp.maximum(x, 0.0) + jnp.log1p(jnp.exp(-jnp.abs(x))))
        gc = jnp.dot(cumsum_mat, g, precision=HIGHEST, preferred_element_type=F32)
        decay = jnp.exp(jnp.where(causal, gc - gc.T, NEG_INF))
        kb = k * beta
        k16 = k.astype(BF16)
        x_mat = jnp.where(strict, -(_dot_nt(kb.astype(BF16), k16) * decay), 0.0)
        t_inv = eye + x_mat
        x_pow = x_mat
        for _ in range(DN_NEUMANN_STEPS):
            x_pow = _mm3(x_pow, x_pow)
            t_inv = t_inv + _mm3(t_inv, x_pow)
        egc = jnp.exp(gc)
        rhs = jnp.concatenate([v * beta, kb * egc], axis=1).astype(BF16)
        uw = jnp.dot(t_inv.astype(BF16), rhs, preferred_element_type=F32)
        u_o[0, rows, :] = uw[:, :DN_HEAD_DIM]
        w_o[0, rows, :] = uw[:, DN_HEAD_DIM:].astype(BF16)
        at_o[0, rows, :] = (_dot_nt(q.astype(BF16), k16) * decay).astype(BF16)
        qd_o[0, rows, :] = (q * egc).astype(BF16)
        gc_first, gc_second = gc[DN_CHUNK - 1:DN_CHUNK, :], gc[DN_TILE - 1:DN_TILE, :]
        gc_last = jnp.where(row < DN_CHUNK, gc_first, gc_second)
        kt_o[0, t] = (k * jnp.exp(gc_last - gc)).T.astype(BF16)
        dec_o[0, t] = jnp.concatenate(
            [jnp.exp(gc_first), jnp.exp(gc_second), jnp.zeros((DN_PAD - 2, LANES), F32)], axis=0)
        z = z_ref[rows, :]
        sz_o[0, rows, :] = (z * jax.nn.sigmoid(z)).astype(BF16)
        return carry

    lax.fori_loop(0, DN_NTILE, body, 0)


def _dn_prep(proj, conv_w, a_log, dt_bias):
    col = lambda c0: pl.BlockSpec((SEQ, LANES), lambda b, h: (b, c0 + h))
    cw = lambda c0: pl.BlockSpec((DN_CONV, LANES), lambda b, h: (0, c0 + h))
    bh_rows = lambda: pl.BlockSpec((1, SEQ, LANES), lambda b, h: (b * DN_HEADS + h, 0, 0))
    bh_tile = lambda r: pl.BlockSpec((1, DN_NTILE, r, LANES), lambda b, h: (b * DN_HEADS + h, 0, 0, 0))
    rows_shape = lambda dt: jax.ShapeDtypeStruct((DN_BH, SEQ, LANES), dt)
    return pl.pallas_call(
        _dn_prep_kernel,
        grid=(BATCH, DN_HEADS),
        in_specs=[
            pl.BlockSpec(memory_space=pltpu.SMEM), pl.BlockSpec(memory_space=pltpu.SMEM),
            col(DN_QKV_COL0), col(DN_QKV_COL0 + DN_HEADS), col(DN_QKV_COL0 + 2 * DN_HEADS), col(DN_Z_COL0),
            pl.BlockSpec((SEQ, LANES), lambda b, h: (b, DN_AB_COL)),
            cw(0), cw(DN_HEADS), cw(2 * DN_HEADS),
        ],
        out_specs=[bh_rows(), bh_rows(), bh_rows(), bh_rows(), bh_tile(DN_TILE), bh_tile(DN_PAD), bh_rows()],
        out_shape=[
            rows_shape(F32), rows_shape(BF16), rows_shape(BF16), rows_shape(BF16),
            jax.ShapeDtypeStruct((DN_BH, DN_NTILE, DN_TILE, LANES), BF16),
            jax.ShapeDtypeStruct((DN_BH, DN_NTILE, DN_PAD, LANES), F32),
            rows_shape(BF16),
        ],
        scratch_shapes=[pltpu.VMEM((SEQ + DN_PAD, LANES), F32)] * 3,
        compiler_params=pltpu.CompilerParams(
            dimension_semantics=("parallel", "parallel"), vmem_limit_bytes=VMEM_LIMIT),
        name="dn_prep",
    )(a_log, dt_bias, proj, proj, proj, proj, proj, conv_w, conv_w, conv_w)


def _dn_scan_kernel(u_ref, w_ref, qd_ref, at_ref, kt_ref, dec_ref, sz_ref, gain_ref, o_ref, s_scr):
    @pl.when(pl.program_id(0) == 0)
    def _():
        s_scr[...] = jnp.zeros_like(s_scr)

    lane = lax.broadcasted_iota(jnp.int32, (DN_HEAD_DIM, DN_TILE), 1)
    gain = gain_ref[...]
    for bh in range(DN_BH):
        b, h = divmod(bh, DN_HEADS)
        state = s_scr[bh]
        kt = kt_ref[bh, 0]
        v_done = jnp.zeros((DN_CHUNK, DN_HEAD_DIM), F32)
        for j in range(2):
            rows = slice(j * DN_CHUNK, (j + 1) * DN_CHUNK)
            s16 = state.astype(BF16)
            v_new = u_ref[bh, rows, :] - jnp.dot(w_ref[bh, rows, :], s16, preferred_element_type=F32)
            v_pair = (jnp.concatenate([v_new, v_done], axis=0) if j == 0
                      else jnp.concatenate([v_done, v_new], axis=0)).astype(BF16)
            o = (jnp.dot(qd_ref[bh, rows, :], s16, preferred_element_type=F32)
                 + jnp.dot(at_ref[bh, rows, :], v_pair, preferred_element_type=F32))
            kt_j = jnp.where((lane // DN_CHUNK) == j, kt, jnp.zeros_like(kt))
            state = state * dec_ref[bh, 0, j:j + 1, :] + jnp.dot(kt_j, v_pair, preferred_element_type=F32)
            v_done = v_new
            o_ref[b, rows, h * DN_HEAD_DIM:(h + 1) * DN_HEAD_DIM] = (
                _rms(o, gain) * sz_ref[bh, rows, :].astype(F32))
        s_scr[bh] = state


def _dn_scan(u, w, qd, at, kt, dec, sz, gain):
    rows = pl.BlockSpec((DN_BH, DN_TILE, LANES), lambda t: (0, t, 0))
    tile = lambda r: pl.BlockSpec((DN_BH, 1, r, LANES), lambda t: (0, t, 0, 0))
    return pl.pallas_call(
        _dn_scan_kernel,
        grid=(DN_NTILE,),
        in_specs=[rows, rows, rows, rows, tile(DN_TILE), tile(DN_PAD), rows,
                  pl.BlockSpec((1, LANES), lambda t: (0, 0))],
        out_specs=pl.BlockSpec((BATCH, DN_TILE, DN_WIDTH), lambda t: (0, t, 0)),
        out_shape=jax.ShapeDtypeStruct((BATCH, SEQ, DN_WIDTH), F32),
        scratch_shapes=[pltpu.VMEM((DN_BH, DN_HEAD_DIM, DN_HEAD_DIM), F32)],
        compiler_params=pltpu.CompilerParams(
            dimension_semantics=("arbitrary",), vmem_limit_bytes=VMEM_LIMIT),
        name="dn_scan",
    )(u, w, qd, at, kt, dec, sz, gain)


def _dn_mixer(proj, p, l):
    outs = _dn_prep(proj, p['dn_conv_w'][l], p['dn_a_log'][l], p['dn_dt_bias'][l])
    y = _dn_scan(*outs, p['dn_norm_gain'][l].reshape(1, DN_HEAD_DIM))
    return y.reshape(N_TOK, DN_WIDTH)


def kernel(x, norm_gains, ffn_w_gate, ffn_w_up, ffn_w_down, w_in, w_out, ssm_lambda_re, ssm_lambda_im,
           ssm_b_re, ssm_b_im, ssm_c_re, ssm_c_im, ssm_d, ssm_log_dt, ssm_glu_w, ssm_glu_b, ssm_out_gain,
           dn_conv_w, dn_a_log, dn_dt_bias, dn_norm_gain, attn_out_gain, rel_bias):
    p = dict(ssm_lambda_re=ssm_lambda_re, ssm_lambda_im=ssm_lambda_im, ssm_b_re=ssm_b_re, ssm_b_im=ssm_b_im,
             ssm_c_re=ssm_c_re, ssm_c_im=ssm_c_im, ssm_d=ssm_d, ssm_log_dt=ssm_log_dt, ssm_glu_w=ssm_glu_w,
             ssm_glu_b=ssm_glu_b, ssm_out_gain=ssm_out_gain, dn_conv_w=dn_conv_w, dn_a_log=dn_a_log,
             dn_dt_bias=dn_dt_bias, dn_norm_gain=dn_norm_gain)
    wg = ffn_w_gate.astype(BF16)
    wu = ffn_w_up.astype(BF16)
    wd = ffn_w_down.astype(BF16)
    w_in_p = jnp.pad(w_in, ((0, 0), (0, 0), (0, N_IN_PAD - N_IN_COLS))).astype(BF16)
    w_out_b = w_out.astype(BF16)
    bucket_table = _attn_bucket_table()
    gains = norm_gains.reshape(DEPTH, 6, 1, D_MODEL)
    x = x.reshape(N_TOK, D_MODEL)
    for l in range(DEPTH):
        x = _ffn(x, gains[l, 0], gains[l, 1], wg[l, 0], wu[l, 0], wd[l, 0])
        proj = _inproj(x, gains[l, 2], w_in_p[l])
        y_ssm = _s5_mixer(proj, p, l)
        y_dn = _dn_mixer(proj, p, l)
        o_at = _attn(proj, rel_bias, bucket_table)
        x = _outproj(x, y_ssm, y_dn, o_at, attn_out_gain[l].reshape(1, ATTN_WIDTH), gains[l, 3], w_out_b[l])
        x = _ffn(x, gains[l, 4], gains[l, 5], wg[l, 1], wu[l, 1], wd[l, 1])
    return x.reshape(BATCH, SEQ, D_MODEL)
```

```python
import functools
import math

import jax
import jax.numpy as jnp
import numpy as np
from jax import lax
from jax.experimental import pallas as pl
from jax.experimental.pallas import tpu as pltpu

D_MODEL = 2048
BATCH = 2
SEQ = 4096
DEPTH = 4
SSM_GROUPS = 32
SSM_CH = 16
SSM_STATE = 64
SSM_WIDTH = SSM_GROUPS * SSM_CH
DN_HEADS = 6
DN_HEAD_DIM = 128
DN_WIDTH = DN_HEADS * DN_HEAD_DIM
DN_CONV = 4
DN_CHUNK = 64
ATTN_HEADS = 6
ATTN_HEAD_DIM = 128
ATTN_WIDTH = ATTN_HEADS * ATTN_HEAD_DIM
DILATED_PAIRS = ((128, 1), (512, 4), (2048, 16))
ATTN_BLOCK = 128
N_BUCKETS = 32
REL_MAX_DIST = 2048
D_MIX = SSM_WIDTH + DN_WIDTH + ATTN_WIDTH
IN_SPLITS = (SSM_WIDTH, ATTN_WIDTH, ATTN_WIDTH, ATTN_WIDTH, 3 * DN_WIDTH, DN_WIDTH, DN_HEADS, DN_HEADS)
N_IN_COLS = sum(IN_SPLITS)
D_FF = 5632
NORM_EPS = 1e-6
NEG_INF = -1e30

LANES = 128
N_IN_PAD = 6144
VMEM_LIMIT = 56 * 1024 * 1024
N_TOK = BATCH * SEQ

BF16 = jnp.bfloat16
F32 = jnp.float32


def _rms(x, gain):
    return x * lax.rsqrt(jnp.mean(x * x, axis=-1, keepdims=True) + NORM_EPS) * gain


FFN_TM = 512
FFN_TF = 512


def _ffn_kernel(x_ref, gpre_ref, gpost_ref, wg_ref, wu_ref, wd_ref, o_ref, h_scr, acc_scr):
    f = pl.program_id(1)

    @pl.when(f == 0)
    def _():
        h_scr[...] = _rms(x_ref[...], gpre_ref[...]).astype(BF16)
        acc_scr[...] = jnp.zeros_like(acc_scr)

    h = h_scr[...]
    gate = jnp.dot(h, wg_ref[...], preferred_element_type=F32)
    up = jnp.dot(h, wu_ref[...], preferred_element_type=F32)
    act = (gate * jax.nn.sigmoid(gate) * up).astype(BF16)
    acc_scr[...] += jnp.dot(act, wd_ref[...], preferred_element_type=F32)

    @pl.when(f == pl.num_programs(1) - 1)
    def _():
        o_ref[...] = x_ref[...] + 0.5 * _rms(acc_scr[...], gpost_ref[...])


def _ffn(x, g_pre, g_post, wg, wu, wd):
    return pl.pallas_call(
        _ffn_kernel,
        grid=(N_TOK // FFN_TM, D_FF // FFN_TF),
        in_specs=[
            pl.BlockSpec((FFN_TM, D_MODEL), lambda i, f: (i, 0)),
            pl.BlockSpec((1, D_MODEL), lambda i, f: (0, 0)),
            pl.BlockSpec((1, D_MODEL), lambda i, f: (0, 0)),
            pl.BlockSpec((D_MODEL, FFN_TF), lambda i, f: (0, f)),
            pl.BlockSpec((D_MODEL, FFN_TF), lambda i, f: (0, f)),
            pl.BlockSpec((FFN_TF, D_MODEL), lambda i, f: (f, 0)),
        ],
        out_specs=pl.BlockSpec((FFN_TM, D_MODEL), lambda i, f: (i, 0)),
        out_shape=jax.ShapeDtypeStruct((N_TOK, D_MODEL), F32),
        scratch_shapes=[pltpu.VMEM((FFN_TM, D_MODEL), BF16), pltpu.VMEM((FFN_TM, D_MODEL), F32)],
        compiler_params=pltpu.CompilerParams(
            dimension_semantics=("parallel", "arbitrary"), vmem_limit_bytes=VMEM_LIMIT),
        name="ffn",
    )(x, g_pre, g_post, wg, wu, wd)


INP_TM = 512
INP_TN = 768


def _inproj_kernel(x_ref, g_ref, w_ref, o_ref, h_scr):
    @pl.when(pl.program_id(1) == 0)
    def _():
        h_scr[...] = _rms(x_ref[...], g_ref[...]).astype(BF16)

    o_ref[...] = jnp.dot(h_scr[...], w_ref[...], preferred_element_type=F32)


def _inproj(x, g, w):
    return pl.pallas_call(
        _inproj_kernel,
        grid=(N_TOK // INP_TM, N_IN_PAD // INP_TN),
        in_specs=[
            pl.BlockSpec((INP_TM, D_MODEL), lambda i, n: (i, 0)),
            pl.BlockSpec((1, D_MODEL), lambda i, n: (0, 0)),
            pl.BlockSpec((D_MODEL, INP_TN), lambda i, n: (0, n)),
        ],
        out_specs=pl.BlockSpec((INP_TM, INP_TN), lambda i, n: (i, n)),
        out_shape=jax.ShapeDtypeStruct((N_TOK, N_IN_PAD), F32),
        scratch_shapes=[pltpu.VMEM((INP_TM, D_MODEL), BF16)],
        compiler_params=pltpu.CompilerParams(
            dimension_semantics=("parallel", "arbitrary"), vmem_limit_bytes=VMEM_LIMIT),
        name="inproj",
    )(x, g, w)


OUT_TM = 256


def _outproj_kernel(x_ref, ys_ref, yd_ref, oa_ref, ga_ref, gpost_ref, w_ref, o_ref):
    ya = _rms(oa_ref[...], ga_ref[...]).astype(BF16)
    mix = jnp.dot(ys_ref[...].astype(BF16), w_ref[0:SSM_WIDTH, :], preferred_element_type=F32)
    mix += jnp.dot(yd_ref[...].astype(BF16), w_ref[SSM_WIDTH:SSM_WIDTH + DN_WIDTH, :],
                   preferred_element_type=F32)
    mix += jnp.dot(ya, w_ref[SSM_WIDTH + DN_WIDTH:D_MIX, :], preferred_element_type=F32)
    o_ref[...] = x_ref[...] + _rms(mix, gpost_ref[...])


def _outproj(x, y_ssm, y_dn, o_at, g_attn, g_post, w):
    row = lambda i: (i, 0)
    fixed = lambda i: (0, 0)
    return pl.pallas_call(
        _outproj_kernel,
        grid=(N_TOK // OUT_TM,),
        in_specs=[
            pl.BlockSpec((OUT_TM, D_MODEL), row),
            pl.BlockSpec((OUT_TM, SSM_WIDTH), row),
            pl.BlockSpec((OUT_TM, DN_WIDTH), row),
            pl.BlockSpec((OUT_TM, ATTN_WIDTH), row),
            pl.BlockSpec((1, ATTN_WIDTH), fixed),
            pl.BlockSpec((1, D_MODEL), fixed),
            pl.BlockSpec((D_MIX, D_MODEL), fixed),
        ],
        out_specs=pl.BlockSpec((OUT_TM, D_MODEL), row),
        out_shape=jax.ShapeDtypeStruct((N_TOK, D_MODEL), F32),
        compiler_params=pltpu.CompilerParams(
            dimension_semantics=("parallel",), vmem_limit_bytes=VMEM_LIMIT),
        name="outproj",
    )(x, y_ssm, y_dn, o_at, g_attn, g_post, w)


S5_SEG = 8
S5_SEGLEN = SEQ // S5_SEG
S5_KB = 64
S5_ROWS = S5_KB * S5_SEG
S5_NS = SSM_GROUPS * SSM_STATE
S5_Q = 4
S5_QS = S5_NS // S5_Q
S5_QC = SSM_WIDTH // S5_Q
S5_LOG2_SEGLEN = 9
assert 1 << S5_LOG2_SEGLEN == S5_SEGLEN


def _s5_kernel(u_ref, wb_ref, are_ref, aim_ref, cre_ref, cim_ref, d_ref, gw_ref, gb_ref, go_ref, o_ref,
               bu_scr, st_scr, carry_scr, y_scr):
    p = pl.program_id(1)
    j = pl.program_id(2)
    ub = u_ref[0].astype(BF16)
    for q in range(S5_Q):
        r = jnp.dot(ub[:, q * S5_QC:(q + 1) * S5_QC], wb_ref[q], preferred_element_type=F32)
        bu_scr[:, q * S5_QS:(q + 1) * S5_QS] = r[:, :S5_QS]
        bu_scr[:, S5_NS + q * S5_QS:S5_NS + (q + 1) * S5_QS] = r[:, S5_QS:]

    @pl.when(jnp.logical_and(p == 0, j == 0))
    def _():
        st_scr[...] = jnp.zeros_like(st_scr)

    @pl.when(jnp.logical_and(p == 1, j == 0))
    def _():
        st_scr[...] = carry_scr[...]

    def scan(store):
        for q in range(S5_Q):
            re_cols = slice(q * S5_QS, (q + 1) * S5_QS)
            im_cols = slice(S5_NS + q * S5_QS, S5_NS + (q + 1) * S5_QS)
            ar = jnp.broadcast_to(are_ref[:, re_cols], (S5_SEG, S5_QS))
            ai = jnp.broadcast_to(aim_ref[:, re_cols], (S5_SEG, S5_QS))

            def body(k, carry):
                sr, si = carry
                rows = pl.ds(pl.multiple_of(k * S5_SEG, S5_SEG), S5_SEG)
                nr = ar * sr - ai * si + bu_scr[rows, re_cols]
                ni = ar * si + ai * sr + bu_scr[rows, im_cols]
                if store:
                    bu_scr[rows, re_cols] = nr
                    bu_scr[rows, im_cols] = ni
                return nr, ni

            sr, si = lax.fori_loop(0, S5_KB, body, (st_scr[:, re_cols], st_scr[:, im_cols]), unroll=8)
            st_scr[:, re_cols] = sr
            st_scr[:, im_cols] = si

    @pl.when(p == 0)
    def _():
        scan(False)

        @pl.when(j == pl.num_programs(2) - 1)
        def _():
            lr, li = are_ref[...], aim_ref[...]
            for _ in range(S5_LOG2_SEGLEN):
                lr, li = lr * lr - li * li, 2.0 * lr * li
            cr = jnp.zeros((1, S5_NS), F32)
            ci = jnp.zeros((1, S5_NS), F32)
            carry_scr[0:1, :] = jnp.zeros((1, 2 * S5_NS), F32)
            for i in range(1, S5_SEG):
                er = st_scr[i - 1:i, 0:S5_NS]
                ei = st_scr[i - 1:i, S5_NS:2 * S5_NS]
                cr, ci = er + lr * cr - li * ci, ei + lr * ci + li * cr
                carry_scr[i:i + 1, 0:S5_NS] = cr
                carry_scr[i:i + 1, S5_NS:2 * S5_NS] = ci

    @pl.when(p == 1)
    def _():
        scan(True)
        for q in range(S5_Q):
            sre = bu_scr[:, q * S5_QS:(q + 1) * S5_QS].astype(BF16)
            sim = bu_scr[:, S5_NS + q * S5_QS:S5_NS + (q + 1) * S5_QS].astype(BF16)
            y_scr[:, q * S5_QC:(q + 1) * S5_QC] = (
                jnp.dot(sre, cre_ref[q], preferred_element_type=F32)
                + jnp.dot(sim, cim_ref[q], preferred_element_type=F32))
        y = jax.nn.gelu(y_scr[...] + d_ref[...] * u_ref[0])
        z = jnp.dot(y.astype(BF16), gw_ref[...], preferred_element_type=F32) + gb_ref[...]
        o_ref[0] = _rms(y * jax.nn.sigmoid(z), go_ref[...])


def _s5_params(lam_re, lam_im, b_re, b_im, c_re, c_im, log_dt):
    lam = lax.complex(lam_re, lam_im)
    lam_bar = jnp.exp(lam * jnp.exp(log_dt)[:, None])
    b_bar = ((lam_bar - 1.0) / lam)[..., None] * lax.complex(b_re, b_im)
    gq = SSM_GROUPS // S5_Q
    eye = jnp.eye(gq, dtype=F32)

    def in_map(t):
        t = t.reshape(S5_Q, gq, SSM_STATE, SSM_CH)
        return jnp.einsum('qgpc,gh->qgchp', t, eye).reshape(S5_Q, S5_QC, S5_QS)

    def out_map(t):
        t = t.reshape(S5_Q, gq, SSM_CH, SSM_STATE)
        return jnp.einsum('qgcp,gh->qgphc', t, eye).reshape(S5_Q, S5_QS, S5_QC)

    wb = jnp.concatenate([in_map(b_bar.real), in_map(b_bar.imag)], axis=-1).astype(BF16)
    return (wb, lam_bar.real.reshape(1, S5_NS), lam_bar.imag.reshape(1, S5_NS),
            out_map(c_re).astype(BF16), out_map(-c_im).astype(BF16))


def _s5(u_perm, wb, a_re, a_im, cre, cim, d_skip, glu_w, glu_b, out_gain):
    nblk = S5_SEGLEN // S5_KB
    fix2 = lambda b, p, j: (0, 0)
    fix3 = lambda b, p, j: (0, 0, 0)
    return pl.pallas_call(
        _s5_kernel,
        grid=(BATCH, 2, nblk),
        in_specs=[
            pl.BlockSpec((1, S5_ROWS, SSM_WIDTH), lambda b, p, j: (b, j, 0)),
            pl.BlockSpec((S5_Q, S5_QC, 2 * S5_QS), fix3),
            pl.BlockSpec((1, S5_NS), fix2),
            pl.BlockSpec((1, S5_NS), fix2),
            pl.BlockSpec((S5_Q, S5_QS, S5_QC), fix3),
            pl.BlockSpec((S5_Q, S5_QS, S5_QC), fix3),
            pl.BlockSpec((1, SSM_WIDTH), fix2),
            pl.BlockSpec((SSM_WIDTH, SSM_WIDTH), fix2),
            pl.BlockSpec((1, SSM_WIDTH), fix2),
            pl.BlockSpec((1, SSM_WIDTH), fix2),
        ],
        out_specs=pl.BlockSpec((1, S5_ROWS, SSM_WIDTH), lambda b, p, j: (b, j * p, 0)),
        out_shape=jax.ShapeDtypeStruct((BATCH, SEQ, SSM_WIDTH), F32),
        scratch_shapes=[
            pltpu.VMEM((S5_ROWS, 2 * S5_NS), F32),
            pltpu.VMEM((S5_SEG, 2 * S5_NS), F32),
            pltpu.VMEM((S5_SEG, 2 * S5_NS), F32),
            pltpu.VMEM((S5_ROWS, SSM_WIDTH), F32),
        ],
        compiler_params=pltpu.CompilerParams(
            dimension_semantics=("parallel", "arbitrary", "arbitrary"), vmem_limit_bytes=VMEM_LIMIT),
        name="s5",
    )(u_perm, wb, a_re, a_im, cre, cim, d_skip, glu_w, glu_b, out_gain)


def _s5_mixer(proj, p, l):
    u = proj[:, :SSM_WIDTH].reshape(BATCH, S5_SEG, S5_SEGLEN, SSM_WIDTH)
    u_perm = jnp.swapaxes(u, 1, 2).reshape(BATCH, SEQ, SSM_WIDTH)
    wb, a_re, a_im, cre, cim = _s5_params(p['ssm_lambda_re'][l], p['ssm_lambda_im'][l], p['ssm_b_re'][l],
                                          p['ssm_b_im'][l], p['ssm_c_re'][l], p['ssm_c_im'][l],
                                          p['ssm_log_dt'][l])
    y = _s5(u_perm, wb, a_re, a_im, cre, cim, p['ssm_d'][l].reshape(1, SSM_WIDTH),
            p['ssm_glu_w'][l].astype(BF16), p['ssm_glu_b'][l].reshape(1, SSM_WIDTH),
            p['ssm_out_gain'][l].reshape(1, SSM_WIDTH))
    y = jnp.swapaxes(y.reshape(BATCH, S5_SEGLEN, S5_SEG, SSM_WIDTH), 1, 2)
    return y.reshape(N_TOK, SSM_WIDTH)


AT_BLK = ATTN_BLOCK
AT_NBRANCH = len(DILATED_PAIRS)
AT_BLOCKS = SEQ // AT_BLK
AT_GROUP = 4
AT_COMBINE_ROWS = 256
assert all(w // d == AT_BLK for w, d in DILATED_PAIRS)
AT_Q_COL0 = SSM_WIDTH // LANES
AT_K_COL0 = AT_Q_COL0 + ATTN_HEADS
AT_V_COL0 = AT_K_COL0 + ATTN_HEADS


def _attn_kernel(rb_ref, bkt_ref, q_ref, k_ref, v_ref, o_ref, bias_scr, o_scr, lse_scr):
    h = pl.program_id(1)
    for g in range(AT_NBRANCH):
        bkt = bkt_ref[g]
        bias = jnp.zeros((AT_BLK, 2 * AT_BLK), F32)
        for b in range(N_BUCKETS):
            bias = jnp.where(bkt == b, rb_ref[b, h], bias)
        bias_scr[g] = jnp.where(bkt < 0, NEG_INF, bias)

    lane = lax.broadcasted_iota(jnp.int32, (AT_BLK, 2 * AT_BLK), 1)
    scale = ATTN_HEAD_DIM ** -0.5
    for g, (_, dil) in enumerate(DILATED_PAIRS):
        nb = AT_BLOCKS // dil

        def body(it, carry, g=g, dil=dil, nb=nb):
            ts = [it * AT_GROUP + i for i in range(AT_GROUP)]
            ns = [t % nb for t in ts]
            starts = [t // nb + n * (AT_BLK * dil) for t, n in zip(ts, ns)]
            pstarts = [t // nb + jnp.maximum(n - 1, 0) * (AT_BLK * dil) for t, n in zip(ts, ns)]
            if dil == 1:
                rows = [pl.ds(pl.multiple_of(s, AT_BLK), AT_BLK) for s in starts]
                prows = [pl.ds(pl.multiple_of(s, AT_BLK), AT_BLK) for s in pstarts]
            else:
                rows = [pl.ds(s, AT_BLK, stride=dil) for s in starts]
                prows = [pl.ds(s, AT_BLK, stride=dil) for s in pstarts]
            q = [(q_ref[r, :] * scale).astype(BF16) for r in rows]
            kc = [jnp.concatenate([k_ref[pr, :], k_ref[r, :]], axis=0).astype(BF16) for pr, r in zip(prows, rows)]
            vc = [jnp.concatenate([v_ref[pr, :], v_ref[r, :]], axis=0).astype(BF16) for pr, r in zip(prows, rows)]
            s = [_dot_nt(qi, ki) + bias_scr[g] for qi, ki in zip(q, kc)]
            s = [jnp.where(jnp.logical_or(n > 0, lane >= AT_BLK), si, NEG_INF) for n, si in zip(ns, s)]
            m = [jnp.max(si, axis=1, keepdims=True) for si in s]
            p = [jnp.exp(si - mi) for si, mi in zip(s, m)]
            l = [jnp.sum(pi, axis=1, keepdims=True) for pi in p]
            o = [jnp.dot(pi.astype(BF16), vi, preferred_element_type=F32) / li for pi, vi, li in zip(p, vc, l)]
            for r, oi, mi, li in zip(rows, o, m, l):
                o_scr[g, r, :] = oi
                lse_scr[g, r, :] = jnp.broadcast_to(mi + jnp.log(li), (AT_BLK, LANES))
            return carry

        lax.fori_loop(0, AT_BLOCKS // AT_GROUP, body, 0)

    def combine(i, carry):
        rows = pl.ds(pl.multiple_of(i * AT_COMBINE_ROWS, AT_COMBINE_ROWS), AT_COMBINE_ROWS)
        l0, l1, l2 = lse_scr[0, rows, :], lse_scr[1, rows, :], lse_scr[2, rows, :]
        m = jnp.maximum(jnp.maximum(l0, l1), l2)
        w0, w1, w2 = jnp.exp(l0 - m), jnp.exp(l1 - m), jnp.exp(l2 - m)
        num = w0 * o_scr[0, rows, :] + w1 * o_scr[1, rows, :] + w2 * o_scr[2, rows, :]
        o_ref[rows, :] = num / (w0 + w1 + w2)
        return carry

    lax.fori_loop(0, SEQ // AT_COMBINE_ROWS, combine, 0)


def _attn_bucket_table():
    qi = jnp.arange(AT_BLK)[:, None]
    kj = jnp.arange(2 * AT_BLK)[None, :]
    rel = AT_BLK + qi - kj
    max_exact = N_BUCKETS // 2
    tables = []
    for window, dil in DILATED_PAIRS:
        dist = jnp.maximum(rel, 0) * dil
        d = jnp.maximum(dist, 1).astype(F32)
        large = max_exact + jnp.log(d / max_exact) / math.log(REL_MAX_DIST / max_exact) * (N_BUCKETS - max_exact)
        large = jnp.minimum(large.astype(jnp.int32), N_BUCKETS - 1)
        bucket = jnp.where(dist < max_exact, dist, large)
        tables.append(jnp.where((rel >= 0) & (rel <= window // dil), bucket, -1))
    return jnp.stack(tables).astype(jnp.int32)


def _attn(proj, rel_bias, bucket_table):
    col = lambda c0: pl.BlockSpec((SEQ, LANES), lambda b, h: (b, c0 + h))
    return pl.pallas_call(
        _attn_kernel,
        grid=(BATCH, ATTN_HEADS),
        in_specs=[
            pl.BlockSpec(memory_space=pltpu.SMEM),
            pl.BlockSpec((AT_NBRANCH, AT_BLK, 2 * AT_BLK), lambda b, h: (0, 0, 0)),
            col(AT_Q_COL0), col(AT_K_COL0), col(AT_V_COL0),
        ],
        out_specs=pl.BlockSpec((SEQ, LANES), lambda b, h: (b, h)),
        out_shape=jax.ShapeDtypeStruct((N_TOK, ATTN_WIDTH), F32),
        scratch_shapes=[
            pltpu.VMEM((AT_NBRANCH, AT_BLK, 2 * AT_BLK), F32),
            pltpu.VMEM((AT_NBRANCH, SEQ, LANES), F32),
            pltpu.VMEM((AT_NBRANCH, SEQ, LANES), F32),
        ],
        compiler_params=pltpu.CompilerParams(
            dimension_semantics=("parallel", "parallel"), vmem_limit_bytes=VMEM_LIMIT),
        name="dilated_attn",
    )(rel_bias, bucket_table, proj, proj, proj)


DN_TILE = 2 * DN_CHUNK
DN_NTILE = SEQ // DN_TILE
DN_BH = BATCH * DN_HEADS
DN_PAD = 8
DN_GROUP = 4
DN_INV_LEVELS = 6
assert 2 ** DN_INV_LEVELS == DN_CHUNK
DN_QKV_COL0 = (SSM_WIDTH + 3 * ATTN_WIDTH) // LANES
DN_Z_COL0 = DN_QKV_COL0 + 3 * DN_HEADS
DN_AB_COL = DN_Z_COL0 + DN_HEADS
HIGHEST = lax.Precision.HIGHEST


def _dot_nt(a, b):
    return lax.dot_general(a, b, (((1,), (1,)), ((), ())), preferred_element_type=F32)


def _split3(x):
    hi = x.astype(BF16)
    r = x - hi.astype(F32)
    mid = r.astype(BF16)
    return hi, mid, (r - mid.astype(F32)).astype(BF16)


def _dot_exact_lhs(x, m16):
    return sum(jnp.dot(piece, m16, preferred_element_type=F32) for piece in _split3(x))


def _dot_exact_rhs(m16, x):
    return sum(jnp.dot(m16, piece, preferred_element_type=F32) for piece in _split3(x))


def _dn_prep_kernel(alog_ref, dtb_ref, q_ref, k_ref, v_ref, z_ref, ab_ref, wq_ref, wk_ref, wv_ref,
                    u_o, wq_o, akt_o, dec_o, sz_o, qp, kp, vp):
    h = pl.program_id(1)
    for src, dst in ((q_ref, qp), (k_ref, kp), (v_ref, vp)):
        dst[0:DN_PAD, :] = jnp.zeros((DN_PAD, LANES), F32)
        dst[DN_PAD:DN_PAD + SEQ, :] = src[...]

    neg_a = -jnp.exp(jnp.full((1, LANES), alog_ref[h], F32))
    dt_bias = dtb_ref[h]
    row = lax.broadcasted_iota(jnp.int32, (DN_TILE, DN_TILE), 0)
    col = lax.broadcasted_iota(jnp.int32, (DN_TILE, DN_TILE), 1)
    same = (row // DN_CHUNK) == (col // DN_CHUNK)
    causal = jnp.logical_and(same, row >= col)
    cumsum_mat = causal.astype(BF16)
    pick = jnp.concatenate([row == h, row == h + DN_HEADS], axis=1).astype(BF16)
    eye = (row == col).astype(F32)
    pair_masks = [
        jnp.logical_and(jnp.logical_and((row // (2 * s)) == (col // (2 * s)), (row // s) % 2 == 1),
                        (col // s) % 2 == 0)
        for s in (2 ** i for i in range(DN_INV_LEVELS))]
    first_chunk_col = col < DN_CHUNK
    scale = DN_HEAD_DIM ** -0.5

    def conv_silu(pad_ref, w_ref, base):
        win = pad_ref[pl.ds(base, DN_TILE + DN_PAD), :]
        acc = None
        for j in range(DN_CONV):
            sh = DN_PAD - (DN_CONV - 1) + j
            term = w_ref[j:j + 1, :] * win[sh:sh + DN_TILE, :]
            acc = term if acc is None else acc + term
        return acc * jax.nn.sigmoid(acc)

    def l2n(x):
        return x * lax.rsqrt(jnp.sum(x * x, axis=1, keepdims=True) + NORM_EPS)

    def mm16(a, b):
        return jnp.dot(a.astype(BF16), b.astype(BF16), preferred_element_type=F32)

    def body(it, carry):
        tiles = [it * DN_GROUP + i for i in range(DN_GROUP)]
        bases = [pl.multiple_of(t * DN_TILE, DN_TILE) for t in tiles]
        q = [l2n(conv_silu(qp, wq_ref, b)) * scale for b in bases]
        k = [l2n(conv_silu(kp, wk_ref, b)) for b in bases]
        v = [conv_silu(vp, wv_ref, b) for b in bases]
        ab_rep = [_dot_exact_lhs(ab_ref[pl.ds(b, DN_TILE), :], pick) for b in bases]
        beta = [jax.nn.sigmoid(r[:, DN_TILE:]) for r in ab_rep]
        x = [r[:, :DN_TILE] + dt_bias for r in ab_rep]
        g = [neg_a * (jnp.maximum(xi, 0.0) + jnp.log1p(jnp.exp(-jnp.abs(xi)))) for xi in x]
        gc = [_dot_exact_rhs(cumsum_mat, gi) for gi in g]
        decay = [jnp.exp(jnp.where(causal, c - c.T, NEG_INF)) for c in gc]
        kb = [ki * bi for ki, bi in zip(k, beta)]
        k16 = [ki.astype(BF16) for ki in k]
        a_mat = [_dot_nt(kbi.astype(BF16), ki) * di for kbi, ki, di in zip(kb, k16, decay)]
        t_inv = [eye - jnp.where(pair_masks[0], am, 0.0) for am in a_mat]
        for mask in pair_masks[1:]:
            low = [jnp.where(mask, am, 0.0).astype(BF16) for am in a_mat]
            t_inv = [ti - mm16(mm16(ti, lo), ti) for ti, lo in zip(t_inv, low)]
        egc = [jnp.exp(c) for c in gc]
        uw = [mm16(ti, jnp.concatenate([vi * bi, kbi * ei], axis=1))
              for ti, vi, bi, kbi, ei in zip(t_inv, v, beta, kb, egc)]
        at = [(_dot_nt(qi.astype(BF16), ki) * di).astype(BF16) for qi, ki, di in zip(q, k16, decay)]
        qd = [(qi * ei).astype(BF16) for qi, ei in zip(q, egc)]
        c = DN_CHUNK
        for i, t in enumerate(tiles):
            rows = pl.ds(bases[i], DN_TILE)
            u_o[0, rows, :] = uw[i][:, :DN_HEAD_DIM]
            w = uw[i][:, DN_HEAD_DIM:].astype(BF16)
            wq_o[0, pl.ds(pl.multiple_of(t * (2 * DN_TILE), 2 * DN_TILE), 2 * DN_TILE), :] = jnp.concatenate(
                [w[:c], qd[i][:c], w[c:], qd[i][c:]], axis=0)
            gc_first, gc_second = gc[i][c - 1:c, :], gc[i][DN_TILE - 1:DN_TILE, :]
            gc_last = jnp.where(row < c, gc_first, gc_second)
            kt = (k[i] * jnp.exp(gc_last - gc[i])).T
            akt_o[0, pl.ds(pl.multiple_of(t * (3 * DN_TILE), DN_TILE), 3 * DN_TILE), :] = jnp.concatenate(
                [at[i][:c], jnp.where(first_chunk_col, kt, 0.0).astype(BF16),
                 at[i][c:], jnp.where(first_chunk_col, 0.0, kt).astype(BF16)], axis=0)
            dec_o[0, t] = jnp.concatenate(
                [jnp.exp(gc_first), jnp.exp(gc_second), jnp.zeros((DN_PAD - 2, LANES), F32)], axis=0)
            z = z_ref[rows, :]
            sz_o[0, rows, :] = (z * jax.nn.sigmoid(z)).astype(BF16)
        return carry

    lax.fori_loop(0, DN_NTILE // DN_GROUP, body, 0)


def _dn_prep(proj, conv_w, a_log, dt_bias):
    col = lambda c0: pl.BlockSpec((SEQ, LANES), lambda b, h: (b, c0 + h))
    cw = lambda c0: pl.BlockSpec((DN_CONV, LANES), lambda b, h: (0, c0 + h))
    bh_rows = lambda n: pl.BlockSpec((1, n, LANES), lambda b, h: (b * DN_HEADS + h, 0, 0))
    rows_shape = lambda n, dt: jax.ShapeDtypeStruct((DN_BH, n, LANES), dt)
    return pl.pallas_call(
        _dn_prep_kernel,
        grid=(BATCH, DN_HEADS),
        in_specs=[
            pl.BlockSpec(memory_space=pltpu.SMEM), pl.BlockSpec(memory_space=pltpu.SMEM),
            col(DN_QKV_COL0), col(DN_QKV_COL0 + DN_HEADS), col(DN_QKV_COL0 + 2 * DN_HEADS), col(DN_Z_COL0),
            pl.BlockSpec((SEQ, LANES), lambda b, h: (b, DN_AB_COL)),
            cw(0), cw(DN_HEADS), cw(2 * DN_HEADS),
        ],
        out_specs=[
            bh_rows(SEQ), bh_rows(2 * SEQ), bh_rows(3 * SEQ),
            pl.BlockSpec((1, DN_NTILE, DN_PAD, LANES), lambda b, h: (b * DN_HEADS + h, 0, 0, 0)),
            bh_rows(SEQ),
        ],
        out_shape=[
            rows_shape(SEQ, F32), rows_shape(2 * SEQ, BF16), rows_shape(3 * SEQ, BF16),
            jax.ShapeDtypeStruct((DN_BH, DN_NTILE, DN_PAD, LANES), F32),
            rows_shape(SEQ, BF16),
        ],
        scratch_shapes=[pltpu.VMEM((SEQ + DN_PAD, LANES), F32)] * 3,
        compiler_params=pltpu.CompilerParams(
            dimension_semantics=("parallel", "parallel"), vmem_limit_bytes=VMEM_LIMIT),
        name="dn_prep",
    )(a_log, dt_bias, proj, proj, proj, proj, proj, conv_w, conv_w, conv_w)


def _dn_scan_kernel(u_ref, wq_ref, akt_ref, dec_ref, sz_ref, gain_ref, o_ref, s_scr):
    @pl.when(pl.program_id(0) == 0)
    def _():
        s_scr[...] = jnp.zeros_like(s_scr)

    gain = gain_ref[...]
    c = DN_CHUNK
    chains = range(DN_BH)
    state = [s_scr[bh] for bh in chains]
    v_first = None
    for j in range(2):
        rows = slice(j * c, (j + 1) * c)
        s16 = [s.astype(BF16) for s in state]
        ws_qs = [jnp.dot(wq_ref[bh, j * DN_TILE:(j + 1) * DN_TILE, :], s16[bh], preferred_element_type=F32)
                 for bh in chains]
        v_new = [u_ref[bh, rows, :] - ws_qs[bh][:c] for bh in chains]
        if j == 0:
            v_pair = [jnp.concatenate([v, jnp.zeros_like(v)], axis=0).astype(BF16) for v in v_new]
            v_first = v_new
        else:
            v_pair = [jnp.concatenate([v0, v], axis=0).astype(BF16) for v0, v in zip(v_first, v_new)]
        ov_kv = [jnp.dot(akt_ref[bh, j * 3 * c:(j + 1) * 3 * c, :], v_pair[bh], preferred_element_type=F32)
                 for bh in chains]
        state = [state[bh] * dec_ref[bh, 0, j:j + 1, :] + ov_kv[bh][c:] for bh in chains]
        for bh in chains:
            b, h = divmod(bh, DN_HEADS)
            o = ws_qs[bh][c:] + ov_kv[bh][:c]
            o_ref[b, rows, h * DN_HEAD_DIM:(h + 1) * DN_HEAD_DIM] = (
                _rms(o, gain) * sz_ref[bh, rows, :].astype(F32))
    for bh in chains:
        s_scr[bh] = state[bh]


def _dn_scan(u, wq, akt, dec, sz, gain):
    rows = lambda n: pl.BlockSpec((DN_BH, n * DN_TILE, LANES), lambda t: (0, t, 0))
    return pl.pallas_call(
        _dn_scan_kernel,
        grid=(DN_NTILE,),
        in_specs=[rows(1), rows(2), rows(3),
                  pl.BlockSpec((DN_BH, 1, DN_PAD, LANES), lambda t: (0, t, 0, 0)), rows(1),
                  pl.BlockSpec((1, LANES), lambda t: (0, 0))],
        out_specs=pl.BlockSpec((BATCH, DN_TILE, DN_WIDTH), lambda t: (0, t, 0)),
        out_shape=jax.ShapeDtypeStruct((BATCH, SEQ, DN_WIDTH), F32),
        scratch_shapes=[pltpu.VMEM((DN_BH, DN_HEAD_DIM, DN_HEAD_DIM), F32)],
        compiler_params=pltpu.CompilerParams(
            dimension_semantics=("arbitrary",), vmem_limit_bytes=VMEM_LIMIT),
        name="dn_scan",
    )(u, wq, akt, dec, sz, gain)


def _dn_mixer(proj, p, l):
    outs = _dn_prep(proj, p['dn_conv_w'][l], p['dn_a_log'][l], p['dn_dt_bias'][l])
    y = _dn_scan(*outs, p['dn_norm_gain'][l].reshape(1, DN_HEAD_DIM))
    return y.reshape(N_TOK, DN_WIDTH)


def kernel(x, norm_gains, ffn_w_gate, ffn_w_up, ffn_w_down, w_in, w_out, ssm_lambda_re, ssm_lambda_im,
           ssm_b_re, ssm_b_im, ssm_c_re, ssm_c_im, ssm_d, ssm_log_dt, ssm_glu_w, ssm_glu_b, ssm_out_gain,
           dn_conv_w, dn_a_log, dn_dt_bias, dn_norm_gain, attn_out_gain, rel_bias):
    p = dict(ssm_lambda_re=ssm_lambda_re, ssm_lambda_im=ssm_lambda_im, ssm_b_re=ssm_b_re, ssm_b_im=ssm_b_im,
             ssm_c_re=ssm_c_re, ssm_c_im=ssm_c_im, ssm_d=ssm_d, ssm_log_dt=ssm_log_dt, ssm_glu_w=ssm_glu_w,
             ssm_glu_b=ssm_glu_b, ssm_out_gain=ssm_out_gain, dn_conv_w=dn_conv_w, dn_a_log=dn_a_log,
             dn_dt_bias=dn_dt_bias, dn_norm_gain=dn_norm_gain)
    wg = ffn_w_gate.astype(BF16)
    wu = ffn_w_up.astype(BF16)
    wd = ffn_w_down.astype(BF16)
    w_in_p = jnp.pad(w_in, ((0, 0), (0, 0), (0, N_IN_PAD - N_IN_COLS))).astype(BF16)
    w_out_b = w_out.astype(BF16)
    bucket_table = _attn_bucket_table()
    gains = norm_gains.reshape(DEPTH, 6, 1, D_MODEL)
    x = x.reshape(N_TOK, D_MODEL)
    for l in range(DEPTH):
        x = _ffn(x, gains[l, 0], gains[l, 1], wg[l, 0], wu[l, 0], wd[l, 0])
        proj = _inproj(x, gains[l, 2], w_in_p[l])
        y_ssm = _s5_mixer(proj, p, l)
        y_dn = _dn_mixer(proj, p, l)
        o_at = _attn(proj, rel_bias, bucket_table)
        x = _outproj(x, y_ssm, y_dn, o_at, attn_out_gain[l].reshape(1, ATTN_WIDTH), gains[l, 3], w_out_b[l])
        x = _ffn(x, gains[l, 4], gains[l, 5], wg[l, 1], wu[l, 1], wd[l, 1])
    return x.reshape(BATCH, SEQ, D_MODEL)
```

```python
import functools
import math

import jax
import jax.numpy as jnp
import numpy as np
from jax import lax
from jax.experimental import pallas as pl
from jax.experimental.pallas import tpu as pltpu

D_MODEL = 2048
BATCH = 2
SEQ = 4096
DEPTH = 4
SSM_GROUPS = 32
SSM_CH = 16
SSM_STATE = 64
SSM_WIDTH = SSM_GROUPS * SSM_CH
DN_HEADS = 6
DN_HEAD_DIM = 128
DN_WIDTH = DN_HEADS * DN_HEAD_DIM
DN_CONV = 4
DN_CHUNK = 64
ATTN_HEADS = 6
ATTN_HEAD_DIM = 128
ATTN_WIDTH = ATTN_HEADS * ATTN_HEAD_DIM
DILATED_PAIRS = ((128, 1), (512, 4), (2048, 16))
ATTN_BLOCK = 128
N_BUCKETS = 32
REL_MAX_DIST = 2048
D_MIX = SSM_WIDTH + DN_WIDTH + ATTN_WIDTH
IN_SPLITS = (SSM_WIDTH, ATTN_WIDTH, ATTN_WIDTH, ATTN_WIDTH, 3 * DN_WIDTH, DN_WIDTH, DN_HEADS, DN_HEADS)
N_IN_COLS = sum(IN_SPLITS)
D_FF = 5632
NORM_EPS = 1e-6
NEG_INF = -1e30

LANES = 128
N_IN_PAD = 6144
V7X_VMEM_BYTES = 64 * 1024 * 1024
VMEM_LIMIT = V7X_VMEM_BYTES - 4 * 1024 * 1024
N_TOK = BATCH * SEQ

BF16 = jnp.bfloat16
F32 = jnp.float32


def _rms(x, gain):
    return x * lax.rsqrt(jnp.mean(x * x, axis=-1, keepdims=True) + NORM_EPS) * gain


FFN_TM = 1024
FFN_TF = 256


def _ffn_kernel(x_ref, gpre_ref, gpost_ref, wg_ref, wu_ref, wd_ref, o_ref, h_scr):
    f = pl.program_id(1)

    @pl.when(f == 0)
    def _():
        h_scr[...] = _rms(x_ref[...], gpre_ref[...]).astype(BF16)
        o_ref[...] = jnp.zeros_like(o_ref)

    h = h_scr[...]
    gate = jnp.dot(h, wg_ref[...].astype(BF16), preferred_element_type=F32)
    up = jnp.dot(h, wu_ref[...].astype(BF16), preferred_element_type=F32)
    act = (gate * jax.nn.sigmoid(gate) * up).astype(BF16)
    o_ref[...] += jnp.dot(act, wd_ref[...].astype(BF16), preferred_element_type=F32)

    @pl.when(f == pl.num_programs(1) - 1)
    def _():
        o_ref[...] = x_ref[...] + 0.5 * _rms(o_ref[...], gpost_ref[...])


def _ffn(x, g_pre, g_post, wg, wu, wd):
    return pl.pallas_call(
        _ffn_kernel,
        grid=(N_TOK // FFN_TM, D_FF // FFN_TF),
        in_specs=[
            pl.BlockSpec((FFN_TM, D_MODEL), lambda i, f: (i, 0)),
            pl.BlockSpec((1, D_MODEL), lambda i, f: (0, 0)),
            pl.BlockSpec((1, D_MODEL), lambda i, f: (0, 0)),
            pl.BlockSpec((D_MODEL, FFN_TF), lambda i, f: (0, f)),
            pl.BlockSpec((D_MODEL, FFN_TF), lambda i, f: (0, f)),
            pl.BlockSpec((FFN_TF, D_MODEL), lambda i, f: (f, 0)),
        ],
        out_specs=pl.BlockSpec((FFN_TM, D_MODEL), lambda i, f: (i, 0)),
        out_shape=jax.ShapeDtypeStruct((N_TOK, D_MODEL), F32),
        scratch_shapes=[pltpu.VMEM((FFN_TM, D_MODEL), BF16)],
        compiler_params=pltpu.CompilerParams(
            dimension_semantics=("parallel", "arbitrary"), vmem_limit_bytes=VMEM_LIMIT),
        name="ffn",
    )(x, g_pre, g_post, wg, wu, wd)


INP_TM = 1024
INP_TN = 1536


def _inproj_kernel(x_ref, g_ref, w_ref, o_ref, h_scr):
    @pl.when(pl.program_id(1) == 0)
    def _():
        h_scr[...] = _rms(x_ref[...], g_ref[...]).astype(BF16)

    o_ref[...] = jnp.dot(h_scr[...], w_ref[...], preferred_element_type=F32)


def _inproj(x, g, w):
    return pl.pallas_call(
        _inproj_kernel,
        grid=(N_TOK // INP_TM, N_IN_PAD // INP_TN),
        in_specs=[
            pl.BlockSpec((INP_TM, D_MODEL), lambda i, n: (i, 0)),
            pl.BlockSpec((1, D_MODEL), lambda i, n: (0, 0)),
            pl.BlockSpec((D_MODEL, INP_TN), lambda i, n: (0, n)),
        ],
        out_specs=pl.BlockSpec((INP_TM, INP_TN), lambda i, n: (i, n)),
        out_shape=jax.ShapeDtypeStruct((N_TOK, N_IN_PAD), F32),
        scratch_shapes=[pltpu.VMEM((INP_TM, D_MODEL), BF16)],
        compiler_params=pltpu.CompilerParams(
            dimension_semantics=("parallel", "arbitrary"), vmem_limit_bytes=VMEM_LIMIT),
        name="inproj",
    )(x, g, w)


OUT_TM = 256


def _outproj_kernel(x_ref, ys_ref, yd_ref, oa_ref, ga_ref, gpost_ref, w_ref, o_ref):
    ya = _rms(oa_ref[...], ga_ref[...]).astype(BF16)
    mix = jnp.dot(ys_ref[...].astype(BF16), w_ref[0:SSM_WIDTH, :], preferred_element_type=F32)
    mix += jnp.dot(yd_ref[...].astype(BF16), w_ref[SSM_WIDTH:SSM_WIDTH + DN_WIDTH, :],
                   preferred_element_type=F32)
    mix += jnp.dot(ya, w_ref[SSM_WIDTH + DN_WIDTH:D_MIX, :], preferred_element_type=F32)
    o_ref[...] = x_ref[...] + _rms(mix, gpost_ref[...])


def _outproj(x, y_ssm, y_dn, o_at, g_attn, g_post, w):
    row = lambda i: (i, 0)
    fixed = lambda i: (0, 0)
    return pl.pallas_call(
        _outproj_kernel,
        grid=(N_TOK // OUT_TM,),
        in_specs=[
            pl.BlockSpec((OUT_TM, D_MODEL), row),
            pl.BlockSpec((OUT_TM, SSM_WIDTH), row),
            pl.BlockSpec((OUT_TM, DN_WIDTH), row),
            pl.BlockSpec((OUT_TM, ATTN_WIDTH), row),
            pl.BlockSpec((1, ATTN_WIDTH), fixed),
            pl.BlockSpec((1, D_MODEL), fixed),
            pl.BlockSpec((D_MIX, D_MODEL), fixed),
        ],
        out_specs=pl.BlockSpec((OUT_TM, D_MODEL), row),
        out_shape=jax.ShapeDtypeStruct((N_TOK, D_MODEL), F32),
        compiler_params=pltpu.CompilerParams(
            dimension_semantics=("parallel",), vmem_limit_bytes=VMEM_LIMIT),
        name="outproj",
    )(x, y_ssm, y_dn, o_at, g_attn, g_post, w)


S5_SEG = 8
S5_SEGLEN = SEQ // S5_SEG
S5_KB = 64
S5_ROWS = S5_KB * S5_SEG
S5_PITCH = S5_KB + 8
S5_PROWS = S5_PITCH * S5_SEG
S5_NS = SSM_GROUPS * SSM_STATE
S5_SLABS = S5_NS // LANES
S5_Q = 4
S5_QS = S5_NS // S5_Q
S5_QL = S5_QS // LANES
S5_QC = SSM_WIDTH // S5_Q
S5_LOG2_SEGLEN = 9
assert 1 << S5_LOG2_SEGLEN == S5_SEGLEN


def _s5_kernel(u_ref, wb_ref, are_ref, aim_ref, cre_ref, cim_ref, d_ref, gw_ref, gb_ref, go_ref, o_ref,
               bu_scr, st_scr, carry_scr, y_scr):
    p = pl.program_id(1)
    j = pl.program_id(2)
    u = u_ref[0].reshape(S5_ROWS, SSM_WIDTH)
    ub = u.astype(BF16)

    @pl.when(jnp.logical_and(p == 0, j == 0))
    def _():
        bu_scr[...] = jnp.zeros_like(bu_scr)
        st_scr[...] = jnp.zeros_like(st_scr)

    @pl.when(jnp.logical_and(p == 1, j == 0))
    def _():
        st_scr[...] = carry_scr[...]

    for q in range(S5_Q):
        r = jnp.dot(ub[:, q * S5_QC:(q + 1) * S5_QC], wb_ref[q], preferred_element_type=F32)
        for half in range(2):
            for c in range(S5_QL):
                slab = half * S5_SLABS + q * S5_QL + c
                lanes = slice((half * S5_QL + c) * LANES, (half * S5_QL + c + 1) * LANES)
                for i in range(S5_SEG):
                    bu_scr[slab, i * S5_PITCH:i * S5_PITCH + S5_KB, :] = r[i * S5_KB:(i + 1) * S5_KB, lanes]

    def scan(store):
        for q in range(S5_Q):
            slabs = [q * S5_QL + c for c in range(S5_QL)]
            lanes = [slice(s * LANES, (s + 1) * LANES) for s in slabs]
            ilanes = [slice(S5_NS + s * LANES, S5_NS + (s + 1) * LANES) for s in slabs]
            ar = [jnp.broadcast_to(are_ref[:, ln], (S5_SEG, LANES)) for ln in lanes]
            ai = [jnp.broadcast_to(aim_ref[:, ln], (S5_SEG, LANES)) for ln in lanes]

            def body(k, carry):
                sr, si = carry
                rows = pl.ds(k, S5_SEG, stride=S5_PITCH)
                nr = tuple(ar[c] * sr[c] - ai[c] * si[c] + bu_scr[slabs[c], rows, :] for c in range(S5_QL))
                ni = tuple(ar[c] * si[c] + ai[c] * sr[c] + bu_scr[slabs[c] + S5_SLABS, rows, :]
                           for c in range(S5_QL))
                if store:
                    for c in range(S5_QL):
                        bu_scr[slabs[c], rows, :] = nr[c]
                        bu_scr[slabs[c] + S5_SLABS, rows, :] = ni[c]
                return nr, ni

            init = (tuple(st_scr[:, ln] for ln in lanes), tuple(st_scr[:, ln] for ln in ilanes))
            sr, si = lax.fori_loop(0, S5_KB, body, init, unroll=8)
            for c in range(S5_QL):
                st_scr[:, lanes[c]] = sr[c]
                st_scr[:, ilanes[c]] = si[c]

    @pl.when(p == 0)
    def _():
        scan(False)

        @pl.when(j == pl.num_programs(2) - 1)
        def _():
            lr, li = are_ref[...], aim_ref[...]
            for _ in range(S5_LOG2_SEGLEN):
                lr, li = lr * lr - li * li, 2.0 * lr * li
            cr = jnp.zeros((1, S5_NS), F32)
            ci = jnp.zeros((1, S5_NS), F32)
            carry_scr[0:1, :] = jnp.zeros((1, 2 * S5_NS), F32)
            for i in range(1, S5_SEG):
                er = st_scr[i - 1:i, 0:S5_NS]
                ei = st_scr[i - 1:i, S5_NS:2 * S5_NS]
                cr, ci = er + lr * cr - li * ci, ei + lr * ci + li * cr
                carry_scr[i:i + 1, 0:S5_NS] = cr
                carry_scr[i:i + 1, S5_NS:2 * S5_NS] = ci

    @pl.when(p == 1)
    def _():
        scan(True)
        for q in range(S5_Q):
            sre = jnp.concatenate([bu_scr[q * S5_QL + c] for c in range(S5_QL)], axis=1).astype(BF16)
            sim = jnp.concatenate([bu_scr[S5_SLABS + q * S5_QL + c] for c in range(S5_QL)], axis=1).astype(BF16)
            y_scr[:, q * S5_QC:(q + 1) * S5_QC] = (
                jnp.dot(sre, cre_ref[q], preferred_element_type=F32)
                + jnp.dot(sim, cim_ref[q], preferred_element_type=F32))
        cs = jnp.concatenate([y_scr[i * S5_PITCH:i * S5_PITCH + S5_KB, :] for i in range(S5_SEG)], axis=0)
        y = jax.nn.gelu(cs + d_ref[...] * u)
        z = jnp.dot(y.astype(BF16), gw_ref[...], preferred_element_type=F32) + gb_ref[...]
        o_ref[0] = _rms(y * jax.nn.sigmoid(z), go_ref[...]).reshape(S5_SEG, S5_KB, SSM_WIDTH)


def _s5_params(lam_re, lam_im, b_re, b_im, c_re, c_im, log_dt):
    lam = lax.complex(lam_re, lam_im)
    lam_bar = jnp.exp(lam * jnp.exp(log_dt)[:, None])
    b_bar = ((lam_bar - 1.0) / lam)[..., None] * lax.complex(b_re, b_im)
    gq = SSM_GROUPS // S5_Q
    eye = jnp.eye(gq, dtype=F32)

    def in_map(t):
        t = t.reshape(S5_Q, gq, SSM_STATE, SSM_CH)
        return jnp.einsum('qgpc,gh->qgchp', t, eye).reshape(S5_Q, S5_QC, S5_QS)

    def out_map(t):
        t = t.reshape(S5_Q, gq, SSM_CH, SSM_STATE)
        return jnp.einsum('qgcp,gh->qgphc', t, eye).reshape(S5_Q, S5_QS, S5_QC)

    wb = jnp.concatenate([in_map(b_bar.real), in_map(b_bar.imag)], axis=-1).astype(BF16)
    return (wb, lam_bar.real.reshape(1, S5_NS), lam_bar.imag.reshape(1, S5_NS),
            out_map(c_re).astype(BF16), out_map(-c_im).astype(BF16))


def _s5(proj4, wb, a_re, a_im, cre, cim, d_skip, glu_w, glu_b, out_gain):
    nblk = S5_SEGLEN // S5_KB
    fix2 = lambda b, p, j: (0, 0)
    fix3 = lambda b, p, j: (0, 0, 0)
    return pl.pallas_call(
        _s5_kernel,
        grid=(BATCH, 2, nblk),
        in_specs=[
            pl.BlockSpec((1, S5_SEG, S5_KB, SSM_WIDTH), lambda b, p, j: (b, 0, j, 0)),
            pl.BlockSpec((S5_Q, S5_QC, 2 * S5_QS), fix3),
            pl.BlockSpec((1, S5_NS), fix2),
            pl.BlockSpec((1, S5_NS), fix2),
            pl.BlockSpec((S5_Q, S5_QS, S5_QC), fix3),
            pl.BlockSpec((S5_Q, S5_QS, S5_QC), fix3),
            pl.BlockSpec((1, SSM_WIDTH), fix2),
            pl.BlockSpec((SSM_WIDTH, SSM_WIDTH), fix2),
            pl.BlockSpec((1, SSM_WIDTH), fix2),
            pl.BlockSpec((1, SSM_WIDTH), fix2),
        ],
        out_specs=pl.BlockSpec((1, S5_SEG, S5_KB, SSM_WIDTH), lambda b, p, j: (b, 0, j * p, 0)),
        out_shape=jax.ShapeDtypeStruct((BATCH, S5_SEG, S5_SEGLEN, SSM_WIDTH), F32),
        scratch_shapes=[
            pltpu.VMEM((2 * S5_SLABS, S5_PROWS, LANES), F32),
            pltpu.VMEM((S5_SEG, 2 * S5_NS), F32),
            pltpu.VMEM((S5_SEG, 2 * S5_NS), F32),
            pltpu.VMEM((S5_PROWS, SSM_WIDTH), F32),
        ],
        compiler_params=pltpu.CompilerParams(
            dimension_semantics=("parallel", "arbitrary", "arbitrary"), vmem_limit_bytes=VMEM_LIMIT),
        name="s5",
    )(proj4, wb, a_re, a_im, cre, cim, d_skip, glu_w, glu_b, out_gain)


def _s5_mixer(proj, p, l):
    wb, a_re, a_im, cre, cim = _s5_params(p['ssm_lambda_re'][l], p['ssm_lambda_im'][l], p['ssm_b_re'][l],
                                          p['ssm_b_im'][l], p['ssm_c_re'][l], p['ssm_c_im'][l],
                                          p['ssm_log_dt'][l])
    y = _s5(proj.reshape(BATCH, S5_SEG, S5_SEGLEN, N_IN_PAD), wb, a_re, a_im, cre, cim,
            p['ssm_d'][l].reshape(1, SSM_WIDTH), p['ssm_glu_w'][l].astype(BF16),
            p['ssm_glu_b'][l].reshape(1, SSM_WIDTH), p['ssm_out_gain'][l].reshape(1, SSM_WIDTH))
    return y.reshape(N_TOK, SSM_WIDTH)


AT_BLK = ATTN_BLOCK
AT_NBRANCH = len(DILATED_PAIRS)
AT_BLOCKS = SEQ // AT_BLK
AT_GROUP = 4
AT_COMBINE_ROWS = 256
assert all(w // d == AT_BLK for w, d in DILATED_PAIRS)
AT_Q_COL0 = SSM_WIDTH // LANES
AT_K_COL0 = AT_Q_COL0 + ATTN_HEADS
AT_V_COL0 = AT_K_COL0 + ATTN_HEADS


def _attn_kernel(rb_ref, bkt_ref, q_ref, k_ref, v_ref, o_ref, bias_scr, o_scr, lse_scr):
    h = pl.program_id(1)
    for g in range(AT_NBRANCH):
        bkt = bkt_ref[g]
        bias = jnp.zeros((AT_BLK, 2 * AT_BLK), F32)
        for b in range(N_BUCKETS):
            bias = jnp.where(bkt == b, rb_ref[b, h], bias)
        bias_scr[g] = jnp.where(bkt < 0, NEG_INF, bias)

    lane = lax.broadcasted_iota(jnp.int32, (AT_BLK, 2 * AT_BLK), 1)
    scale = ATTN_HEAD_DIM ** -0.5
    for g, (_, dil) in enumerate(DILATED_PAIRS):
        nb = AT_BLOCKS // dil

        def body(it, carry, g=g, dil=dil, nb=nb):
            ts = [it * AT_GROUP + i for i in range(AT_GROUP)]
            ns = [t % nb for t in ts]
            starts = [t // nb + n * (AT_BLK * dil) for t, n in zip(ts, ns)]
            pstarts = [t // nb + jnp.maximum(n - 1, 0) * (AT_BLK * dil) for t, n in zip(ts, ns)]
            if dil == 1:
                rows = [pl.ds(pl.multiple_of(s, AT_BLK), AT_BLK) for s in starts]
                prows = [pl.ds(pl.multiple_of(s, AT_BLK), AT_BLK) for s in pstarts]
            else:
                rows = [pl.ds(s, AT_BLK, stride=dil) for s in starts]
                prows = [pl.ds(s, AT_BLK, stride=dil) for s in pstarts]
            q = [(q_ref[r, :] * scale).astype(BF16) for r in rows]
            kc = [jnp.concatenate([k_ref[pr, :], k_ref[r, :]], axis=0).astype(BF16) for pr, r in zip(prows, rows)]
            vc = [jnp.concatenate([v_ref[pr, :], v_ref[r, :]], axis=0).astype(BF16) for pr, r in zip(prows, rows)]
            s = [_dot_nt(qi, ki) + bias_scr[g] for qi, ki in zip(q, kc)]
            s = [jnp.where(jnp.logical_or(n > 0, lane >= AT_BLK), si, NEG_INF) for n, si in zip(ns, s)]
            m = [jnp.max(si, axis=1, keepdims=True) for si in s]
            p = [jnp.exp(si - mi) for si, mi in zip(s, m)]
            l = [jnp.sum(pi, axis=1, keepdims=True) for pi in p]
            o = [jnp.dot(pi.astype(BF16), vi, preferred_element_type=F32) / li for pi, vi, li in zip(p, vc, l)]
            for r, oi, mi, li in zip(rows, o, m, l):
                o_scr[g, r, :] = oi
                lse_scr[g, r, :] = jnp.broadcast_to(mi + jnp.log(li), (AT_BLK, LANES))
            return carry

        lax.fori_loop(0, AT_BLOCKS // AT_GROUP, body, 0)

    def combine(i, carry):
        rows = pl.ds(pl.multiple_of(i * AT_COMBINE_ROWS, AT_COMBINE_ROWS), AT_COMBINE_ROWS)
        l0, l1, l2 = lse_scr[0, rows, :], lse_scr[1, rows, :], lse_scr[2, rows, :]
        m = jnp.maximum(jnp.maximum(l0, l1), l2)
        w0, w1, w2 = jnp.exp(l0 - m), jnp.exp(l1 - m), jnp.exp(l2 - m)
        num = w0 * o_scr[0, rows, :] + w1 * o_scr[1, rows, :] + w2 * o_scr[2, rows, :]
        o_ref[rows, :] = num / (w0 + w1 + w2)
        return carry

    lax.fori_loop(0, SEQ // AT_COMBINE_ROWS, combine, 0)


def _attn_bucket_table():
    qi = jnp.arange(AT_BLK)[:, None]
    kj = jnp.arange(2 * AT_BLK)[None, :]
    rel = AT_BLK + qi - kj
    max_exact = N_BUCKETS // 2
    tables = []
    for window, dil in DILATED_PAIRS:
        dist = jnp.maximum(rel, 0) * dil
        d = jnp.maximum(dist, 1).astype(F32)
        large = max_exact + jnp.log(d / max_exact) / math.log(REL_MAX_DIST / max_exact) * (N_BUCKETS - max_exact)
        large = jnp.minimum(large.astype(jnp.int32), N_BUCKETS - 1)
        bucket = jnp.where(dist < max_exact, dist, large)
        tables.append(jnp.where((rel >= 0) & (rel <= window // dil), bucket, -1))
    return jnp.stack(tables).astype(jnp.int32)


def _attn(proj, rel_bias, bucket_table):
    col = lambda c0: pl.BlockSpec((SEQ, LANES), lambda b, h: (b, c0 + h))
    return pl.pallas_call(
        _attn_kernel,
        grid=(BATCH, ATTN_HEADS),
        in_specs=[
            pl.BlockSpec(memory_space=pltpu.SMEM),
            pl.BlockSpec((AT_NBRANCH, AT_BLK, 2 * AT_BLK), lambda b, h: (0, 0, 0)),
            col(AT_Q_COL0), col(AT_K_COL0), col(AT_V_COL0),
        ],
        out_specs=pl.BlockSpec((SEQ, LANES), lambda b, h: (b, h)),
        out_shape=jax.ShapeDtypeStruct((N_TOK, ATTN_WIDTH), F32),
        scratch_shapes=[
            pltpu.VMEM((AT_NBRANCH, AT_BLK, 2 * AT_BLK), F32),
            pltpu.VMEM((AT_NBRANCH, SEQ, LANES), F32),
            pltpu.VMEM((AT_NBRANCH, SEQ, LANES), F32),
        ],
        compiler_params=pltpu.CompilerParams(
            dimension_semantics=("parallel", "parallel"), vmem_limit_bytes=VMEM_LIMIT),
        name="dilated_attn",
    )(rel_bias, bucket_table, proj, proj, proj)


DN_TILE = 2 * DN_CHUNK
DN_NTILE = SEQ // DN_TILE
DN_BH = BATCH * DN_HEADS
DN_PAD = 8
DN_GROUP = 4
DN_INV_LEVELS = 6
assert 2 ** DN_INV_LEVELS == DN_CHUNK
DN_QKV_COL0 = (SSM_WIDTH + 3 * ATTN_WIDTH) // LANES
DN_Z_COL0 = DN_QKV_COL0 + 3 * DN_HEADS
DN_AB_COL = DN_Z_COL0 + DN_HEADS
HIGHEST = lax.Precision.HIGHEST


def _dot_nt(a, b):
    return lax.dot_general(a, b, (((1,), (1,)), ((), ())), preferred_element_type=F32)


def _split3(x):
    hi = x.astype(BF16)
    r = x - hi.astype(F32)
    mid = r.astype(BF16)
    return hi, mid, (r - mid.astype(F32)).astype(BF16)


def _dot_exact_lhs(x, m16):
    return sum(jnp.dot(piece, m16, preferred_element_type=F32) for piece in _split3(x))


def _dot_exact_rhs(m16, x):
    return sum(jnp.dot(m16, piece, preferred_element_type=F32) for piece in _split3(x))


def _dn_prep_kernel(alog_ref, dtb_ref, q_ref, k_ref, v_ref, z_ref, ab_ref, wq_ref, wk_ref, wv_ref,
                    u_o, wq_o, akt_o, dec_o, sz_o, qp, kp, vp):
    h = pl.program_id(1)
    for src, dst in ((q_ref, qp), (k_ref, kp), (v_ref, vp)):
        dst[0:DN_PAD, :] = jnp.zeros((DN_PAD, LANES), F32)
        dst[DN_PAD:DN_PAD + SEQ, :] = src[...]

    neg_a = -jnp.exp(jnp.full((1, LANES), alog_ref[h], F32))
    dt_bias = dtb_ref[h]
    row = lax.broadcasted_iota(jnp.int32, (DN_TILE, DN_TILE), 0)
    col = lax.broadcasted_iota(jnp.int32, (DN_TILE, DN_TILE), 1)
    same = (row // DN_CHUNK) == (col // DN_CHUNK)
    causal = jnp.logical_and(same, row >= col)
    cumsum_mat = causal.astype(BF16)
    pick = jnp.concatenate([row == h, row == h + DN_HEADS], axis=1).astype(BF16)
    eye = (row == col).astype(F32)
    pair_masks = [
        jnp.logical_and(jnp.logical_and((row // (2 * s)) == (col // (2 * s)), (row // s) % 2 == 1),
                        (col // s) % 2 == 0)
        for s in (2 ** i for i in range(DN_INV_LEVELS))]
    first_chunk_col = col < DN_CHUNK
    scale = DN_HEAD_DIM ** -0.5

    def conv_silu(pad_ref, w_ref, base):
        win = pad_ref[pl.ds(base, DN_TILE + DN_PAD), :]
        acc = None
        for j in range(DN_CONV):
            sh = DN_PAD - (DN_CONV - 1) + j
            term = w_ref[j:j + 1, :] * win[sh:sh + DN_TILE, :]
            acc = term if acc is None else acc + term
        return acc * jax.nn.sigmoid(acc)

    def l2n(x):
        return x * lax.rsqrt(jnp.sum(x * x, axis=1, keepdims=True) + NORM_EPS)

    def mm16(a, b):
        return jnp.dot(a.astype(BF16), b.astype(BF16), preferred_element_type=F32)

    def body(it, carry):
        tiles = [it * DN_GROUP + i for i in range(DN_GROUP)]
        bases = [pl.multiple_of(t * DN_TILE, DN_TILE) for t in tiles]
        q = [l2n(conv_silu(qp, wq_ref, b)) * scale for b in bases]
        k = [l2n(conv_silu(kp, wk_ref, b)) for b in bases]
        v = [conv_silu(vp, wv_ref, b) for b in bases]
        ab_rep = [_dot_exact_lhs(ab_ref[pl.ds(b, DN_TILE), :], pick) for b in bases]
        beta = [jax.nn.sigmoid(r[:, DN_TILE:]) for r in ab_rep]
        x = [r[:, :DN_TILE] + dt_bias for r in ab_rep]
        g = [neg_a * (jnp.maximum(xi, 0.0) + jnp.log1p(jnp.exp(-jnp.abs(xi)))) for xi in x]
        gc = [_dot_exact_rhs(cumsum_mat, gi) for gi in g]
        decay = [jnp.exp(jnp.where(causal, c - c.T, NEG_INF)) for c in gc]
        kb = [ki * bi for ki, bi in zip(k, beta)]
        k16 = [ki.astype(BF16) for ki in k]
        a_mat = [_dot_nt(kbi.astype(BF16), ki) * di for kbi, ki, di in zip(kb, k16, decay)]
        t_inv = [eye - jnp.where(pair_masks[0], am, 0.0) for am in a_mat]
        for mask in pair_masks[1:]:
            low = [jnp.where(mask, am, 0.0).astype(BF16) for am in a_mat]
            t_inv = [ti - mm16(mm16(ti, lo), ti) for ti, lo in zip(t_inv, low)]
        egc = [jnp.exp(c) for c in gc]
        uw = [mm16(ti, jnp.concatenate([vi * bi, kbi * ei], axis=1))
              for ti, vi, bi, kbi, ei in zip(t_inv, v, beta, kb, egc)]
        at = [(_dot_nt(qi.astype(BF16), ki) * di).astype(BF16) for qi, ki, di in zip(q, k16, decay)]
        qd = [(qi * ei).astype(BF16) for qi, ei in zip(q, egc)]
        c = DN_CHUNK
        for i, t in enumerate(tiles):
            rows = pl.ds(bases[i], DN_TILE)
            u_o[0, rows, :] = uw[i][:, :DN_HEAD_DIM]
            w = uw[i][:, DN_HEAD_DIM:].astype(BF16)
            wq_o[0, pl.ds(pl.multiple_of(t * (2 * DN_TILE), 2 * DN_TILE), 2 * DN_TILE), :] = jnp.concatenate(
                [w[:c], qd[i][:c], w[c:], qd[i][c:]], axis=0)
            gc_first, gc_second = gc[i][c - 1:c, :], gc[i][DN_TILE - 1:DN_TILE, :]
            gc_last = jnp.where(row < c, gc_first, gc_second)
            kt = (k[i] * jnp.exp(gc_last - gc[i])).T
            akt_o[0, pl.ds(pl.multiple_of(t * (3 * DN_TILE), DN_TILE), 3 * DN_TILE), :] = jnp.concatenate(
                [at[i][:c], jnp.where(first_chunk_col, kt, 0.0).astype(BF16),
                 at[i][c:], jnp.where(first_chunk_col, 0.0, kt).astype(BF16)], axis=0)
            dec_o[0, t] = jnp.concatenate(
                [jnp.exp(gc_first), jnp.exp(gc_second), jnp.zeros((DN_PAD - 2, LANES), F32)], axis=0)
            z = z_ref[rows, :]
            sz_o[0, rows, :] = (z * jax.nn.sigmoid(z)).astype(BF16)
        return carry

    lax.fori_loop(0, DN_NTILE // DN_GROUP, body, 0)


def _dn_prep(proj, conv_w, a_log, dt_bias):
    col = lambda c0: pl.BlockSpec((SEQ, LANES), lambda b, h: (b, c0 + h))
    cw = lambda c0: pl.BlockSpec((DN_CONV, LANES), lambda b, h: (0, c0 + h))
    bh_rows = lambda n: pl.BlockSpec((1, n, LANES), lambda b, h: (b * DN_HEADS + h, 0, 0))
    rows_shape = lambda n, dt: jax.ShapeDtypeStruct((DN_BH, n, LANES), dt)
    return pl.pallas_call(
        _dn_prep_kernel,
        grid=(BATCH, DN_HEADS),
        in_specs=[
            pl.BlockSpec(memory_space=pltpu.SMEM), pl.BlockSpec(memory_space=pltpu.SMEM),
            col(DN_QKV_COL0), col(DN_QKV_COL0 + DN_HEADS), col(DN_QKV_COL0 + 2 * DN_HEADS), col(DN_Z_COL0),
            pl.BlockSpec((SEQ, LANES), lambda b, h: (b, DN_AB_COL)),
            cw(0), cw(DN_HEADS), cw(2 * DN_HEADS),
        ],
        out_specs=[
            bh_rows(SEQ), bh_rows(2 * SEQ), bh_rows(3 * SEQ),
            pl.BlockSpec((1, DN_NTILE, DN_PAD, LANES), lambda b, h: (b * DN_HEADS + h, 0, 0, 0)),
            bh_rows(SEQ),
        ],
        out_shape=[
            rows_shape(SEQ, F32), rows_shape(2 * SEQ, BF16), rows_shape(3 * SEQ, BF16),
            jax.ShapeDtypeStruct((DN_BH, DN_NTILE, DN_PAD, LANES), F32),
            rows_shape(SEQ, BF16),
        ],
        scratch_shapes=[pltpu.VMEM((SEQ + DN_PAD, LANES), F32)] * 3,
        compiler_params=pltpu.CompilerParams(
            dimension_semantics=("parallel", "parallel"), vmem_limit_bytes=VMEM_LIMIT),
        name="dn_prep",
    )(a_log, dt_bias, proj, proj, proj, proj, proj, conv_w, conv_w, conv_w)


def _dn_scan_kernel(u_ref, wq_ref, akt_ref, dec_ref, sz_ref, gain_ref, o_ref, s_scr):
    @pl.when(pl.program_id(0) == 0)
    def _():
        s_scr[...] = jnp.zeros_like(s_scr)

    gain = gain_ref[...]
    c = DN_CHUNK
    chains = range(DN_BH)
    state = [s_scr[bh] for bh in chains]
    v_first = None
    for j in range(2):
        rows = slice(j * c, (j + 1) * c)
        s16 = [s.astype(BF16) for s in state]
        ws_qs = [jnp.dot(wq_ref[bh, j * DN_TILE:(j + 1) * DN_TILE, :], s16[bh], preferred_element_type=F32)
                 for bh in chains]
        v_new = [u_ref[bh, rows, :] - ws_qs[bh][:c] for bh in chains]
        if j == 0:
            v_pair = [jnp.concatenate([v, jnp.zeros_like(v)], axis=0).astype(BF16) for v in v_new]
            v_first = v_new
        else:
            v_pair = [jnp.concatenate([v0, v], axis=0).astype(BF16) for v0, v in zip(v_first, v_new)]
        ov_kv = [jnp.dot(akt_ref[bh, j * 3 * c:(j + 1) * 3 * c, :], v_pair[bh], preferred_element_type=F32)
                 for bh in chains]
        state = [state[bh] * dec_ref[bh, 0, j:j + 1, :] + ov_kv[bh][c:] for bh in chains]
        for bh in chains:
            b, h = divmod(bh, DN_HEADS)
            o = ws_qs[bh][c:] + ov_kv[bh][:c]
            o_ref[b, rows, h * DN_HEAD_DIM:(h + 1) * DN_HEAD_DIM] = (
                _rms(o, gain) * sz_ref[bh, rows, :].astype(F32))
    for bh in chains:
        s_scr[bh] = state[bh]


def _dn_scan(u, wq, akt, dec, sz, gain):
    rows = lambda n: pl.BlockSpec((DN_BH, n * DN_TILE, LANES), lambda t: (0, t, 0))
    return pl.pallas_call(
        _dn_scan_kernel,
        grid=(DN_NTILE,),
        in_specs=[rows(1), rows(2), rows(3),
                  pl.BlockSpec((DN_BH, 1, DN_PAD, LANES), lambda t: (0, t, 0, 0)), rows(1),
                  pl.BlockSpec((1, LANES), lambda t: (0, 0))],
        out_specs=pl.BlockSpec((BATCH, DN_TILE, DN_WIDTH), lambda t: (0, t, 0)),
        out_shape=jax.ShapeDtypeStruct((BATCH, SEQ, DN_WIDTH), F32),
        scratch_shapes=[pltpu.VMEM((DN_BH, DN_HEAD_DIM, DN_HEAD_DIM), F32)],
        compiler_params=pltpu.CompilerParams(
            dimension_semantics=("arbitrary",), vmem_limit_bytes=VMEM_LIMIT),
        name="dn_scan",
    )(u, wq, akt, dec, sz, gain)


def _dn_mixer(proj, p, l):
    outs = _dn_prep(proj, p['dn_conv_w'][l], p['dn_a_log'][l], p['dn_dt_bias'][l])
    y = _dn_scan(*outs, p['dn_norm_gain'][l].reshape(1, DN_HEAD_DIM))
    return y.reshape(N_TOK, DN_WIDTH)


def kernel(x, norm_gains, ffn_w_gate, ffn_w_up, ffn_w_down, w_in, w_out, ssm_lambda_re, ssm_lambda_im,
           ssm_b_re, ssm_b_im, ssm_c_re, ssm_c_im, ssm_d, ssm_log_dt, ssm_glu_w, ssm_glu_b, ssm_out_gain,
           dn_conv_w, dn_a_log, dn_dt_bias, dn_norm_gain, attn_out_gain, rel_bias):
    p = dict(ssm_lambda_re=ssm_lambda_re, ssm_lambda_im=ssm_lambda_im, ssm_b_re=ssm_b_re, ssm_b_im=ssm_b_im,
             ssm_c_re=ssm_c_re, ssm_c_im=ssm_c_im, ssm_d=ssm_d, ssm_log_dt=ssm_log_dt, ssm_glu_w=ssm_glu_w,
             ssm_glu_b=ssm_glu_b, ssm_out_gain=ssm_out_gain, dn_conv_w=dn_conv_w, dn_a_log=dn_a_log,
             dn_dt_bias=dn_dt_bias, dn_norm_gain=dn_norm_gain)
    wg, wu, wd = ffn_w_gate, ffn_w_up, ffn_w_down
    w_in_p = jnp.pad(w_in, ((0, 0), (0, 0), (0, N_IN_PAD - N_IN_COLS))).astype(BF16)
    w_out_b = w_out.astype(BF16)
    bucket_table = _attn_bucket_table()
    gains = norm_gains.reshape(DEPTH, 6, 1, D_MODEL)
    x = x.reshape(N_TOK, D_MODEL)
    for l in range(DEPTH):
        x = _ffn(x, gains[l, 0], gains[l, 1], wg[l, 0], wu[l, 0], wd[l, 0])
        proj = _inproj(x, gains[l, 2], w_in_p[l])
        y_ssm = _s5_mixer(proj, p, l)
        y_dn = _dn_mixer(proj, p, l)
        o_at = _attn(proj, rel_bias, bucket_table)
        x = _outproj(x, y_ssm, y_dn, o_at, attn_out_gain[l].reshape(1, ATTN_WIDTH), gains[l, 3], w_out_b[l])
        x = _ffn(x, gains[l, 4], gains[l, 5], wg[l, 1], wu[l, 1], wd[l, 1])
    return x.reshape(BATCH, SEQ, D_MODEL)
```

```python
import functools
import math

import jax
import jax.numpy as jnp
import numpy as np
from jax import lax
from jax.experimental import pallas as pl
from jax.experimental.pallas import tpu as pltpu

D_MODEL = 2048
BATCH = 2
SEQ = 4096
DEPTH = 4
SSM_GROUPS = 32
SSM_CH = 16
SSM_STATE = 64
SSM_WIDTH = SSM_GROUPS * SSM_CH
DN_HEADS = 6
DN_HEAD_DIM = 128
DN_WIDTH = DN_HEADS * DN_HEAD_DIM
DN_CONV = 4
DN_CHUNK = 64
ATTN_HEADS = 6
ATTN_HEAD_DIM = 128
ATTN_WIDTH = ATTN_HEADS * ATTN_HEAD_DIM
DILATED_PAIRS = ((128, 1), (512, 4), (2048, 16))
ATTN_BLOCK = 128
N_BUCKETS = 32
REL_MAX_DIST = 2048
D_MIX = SSM_WIDTH + DN_WIDTH + ATTN_WIDTH
IN_SPLITS = (SSM_WIDTH, ATTN_WIDTH, ATTN_WIDTH, ATTN_WIDTH, 3 * DN_WIDTH, DN_WIDTH, DN_HEADS, DN_HEADS)
N_IN_COLS = sum(IN_SPLITS)
D_FF = 5632
NORM_EPS = 1e-6
NEG_INF = -1e30

LANES = 128
N_IN_PAD = 6144
V7X_VMEM_BYTES = 64 * 1024 * 1024
VMEM_LIMIT = V7X_VMEM_BYTES - 4 * 1024 * 1024
N_TOK = BATCH * SEQ

BF16 = jnp.bfloat16
F32 = jnp.float32


def _rms(x, gain):
    return x * lax.rsqrt(jnp.mean(x * x, axis=-1, keepdims=True) + NORM_EPS) * gain


FFN_TM = 1024
FFN_TF = 256


def _ffn_kernel(x_ref, gpre_ref, gpost_ref, wg_ref, wu_ref, wd_ref, o_ref, h_scr):
    f = pl.program_id(1)

    @pl.when(f == 0)
    def _():
        h_scr[...] = _rms(x_ref[...], gpre_ref[...]).astype(BF16)
        o_ref[...] = jnp.zeros_like(o_ref)

    h = h_scr[...]
    gate = jnp.dot(h, wg_ref[...].astype(BF16), preferred_element_type=F32)
    up = jnp.dot(h, wu_ref[...].astype(BF16), preferred_element_type=F32)
    act = (gate * jax.nn.sigmoid(gate) * up).astype(BF16)
    o_ref[...] += jnp.dot(act, wd_ref[...].astype(BF16), preferred_element_type=F32)

    @pl.when(f == pl.num_programs(1) - 1)
    def _():
        o_ref[...] = x_ref[...] + 0.5 * _rms(o_ref[...], gpost_ref[...])


def _ffn(x, g_pre, g_post, wg, wu, wd, layer, half):
    return pl.pallas_call(
        _ffn_kernel,
        grid=(N_TOK // FFN_TM, D_FF // FFN_TF),
        in_specs=[
            pl.BlockSpec((FFN_TM, D_MODEL), lambda i, f: (i, 0)),
            pl.BlockSpec((1, D_MODEL), lambda i, f: (0, 0)),
            pl.BlockSpec((1, D_MODEL), lambda i, f: (0, 0)),
            pl.BlockSpec((None, None, D_MODEL, FFN_TF), lambda i, f: (layer, half, 0, f)),
            pl.BlockSpec((None, None, D_MODEL, FFN_TF), lambda i, f: (layer, half, 0, f)),
            pl.BlockSpec((None, None, FFN_TF, D_MODEL), lambda i, f: (layer, half, f, 0)),
        ],
        out_specs=pl.BlockSpec((FFN_TM, D_MODEL), lambda i, f: (i, 0)),
        out_shape=jax.ShapeDtypeStruct((N_TOK, D_MODEL), F32),
        scratch_shapes=[pltpu.VMEM((FFN_TM, D_MODEL), BF16)],
        compiler_params=pltpu.CompilerParams(
            dimension_semantics=("parallel", "arbitrary"), vmem_limit_bytes=VMEM_LIMIT),
        name="ffn",
    )(x, g_pre, g_post, wg, wu, wd)


INP_TM = 1024
INP_TN = 1536


def _inproj_kernel(x_ref, g_ref, w_ref, o_ref, h_scr):
    @pl.when(pl.program_id(1) == 0)
    def _():
        h_scr[...] = _rms(x_ref[...], g_ref[...]).astype(BF16)

    o_ref[...] = jnp.dot(h_scr[...], w_ref[...], preferred_element_type=F32)


def _inproj(x, g, w, layer):
    return pl.pallas_call(
        _inproj_kernel,
        grid=(N_TOK // INP_TM, N_IN_PAD // INP_TN),
        in_specs=[
            pl.BlockSpec((INP_TM, D_MODEL), lambda i, n: (i, 0)),
            pl.BlockSpec((1, D_MODEL), lambda i, n: (0, 0)),
            pl.BlockSpec((None, D_MODEL, INP_TN), lambda i, n: (layer, 0, n)),
        ],
        out_specs=pl.BlockSpec((INP_TM, INP_TN), lambda i, n: (i, n)),
        out_shape=jax.ShapeDtypeStruct((N_TOK, N_IN_PAD), F32),
        scratch_shapes=[pltpu.VMEM((INP_TM, D_MODEL), BF16)],
        compiler_params=pltpu.CompilerParams(
            dimension_semantics=("parallel", "arbitrary"), vmem_limit_bytes=VMEM_LIMIT),
        name="inproj",
    )(x, g, w)


OUT_TM = 256


def _outproj_kernel(x_ref, ys_ref, yd_ref, oa_ref, ga_ref, gpost_ref, w_ref, o_ref):
    ya = _rms(oa_ref[...], ga_ref[...]).astype(BF16)
    mix = jnp.dot(ys_ref[...].astype(BF16), w_ref[0:SSM_WIDTH, :], preferred_element_type=F32)
    mix += jnp.dot(yd_ref[...].astype(BF16), w_ref[SSM_WIDTH:SSM_WIDTH + DN_WIDTH, :],
                   preferred_element_type=F32)
    mix += jnp.dot(ya, w_ref[SSM_WIDTH + DN_WIDTH:D_MIX, :], preferred_element_type=F32)
    o_ref[...] = x_ref[...] + _rms(mix, gpost_ref[...])


def _outproj(x, y_ssm, y_dn, o_at, g_attn, g_post, w, layer):
    row = lambda i: (i, 0)
    fixed = lambda i: (0, 0)
    return pl.pallas_call(
        _outproj_kernel,
        grid=(N_TOK // OUT_TM,),
        in_specs=[
            pl.BlockSpec((OUT_TM, D_MODEL), row),
            pl.BlockSpec((OUT_TM, SSM_WIDTH), row),
            pl.BlockSpec((OUT_TM, DN_WIDTH), row),
            pl.BlockSpec((OUT_TM, ATTN_WIDTH), row),
            pl.BlockSpec((1, ATTN_WIDTH), fixed),
            pl.BlockSpec((1, D_MODEL), fixed),
            pl.BlockSpec((None, D_MIX, D_MODEL), lambda i: (layer, 0, 0)),
        ],
        out_specs=pl.BlockSpec((OUT_TM, D_MODEL), row),
        out_shape=jax.ShapeDtypeStruct((N_TOK, D_MODEL), F32),
        compiler_params=pltpu.CompilerParams(
            dimension_semantics=("parallel",), vmem_limit_bytes=VMEM_LIMIT),
        name="outproj",
    )(x, y_ssm, y_dn, o_at, g_attn, g_post, w)


S5_SEG = 8
S5_SEGLEN = SEQ // S5_SEG
S5_KB = 64
S5_ROWS = S5_KB * S5_SEG
S5_PITCH = S5_KB + 8
S5_PROWS = S5_PITCH * S5_SEG
S5_NS = SSM_GROUPS * SSM_STATE
S5_SLABS = S5_NS // LANES
S5_Q = 4
S5_QS = S5_NS // S5_Q
S5_QL = S5_QS // LANES
S5_QC = SSM_WIDTH // S5_Q
S5_LOG2_SEGLEN = 9
assert 1 << S5_LOG2_SEGLEN == S5_SEGLEN


def _s5_kernel(u_ref, wb_ref, are_ref, aim_ref, cre_ref, cim_ref, d_ref, gw_ref, gb_ref, go_ref, o_ref,
               bu_scr, st_scr, carry_scr, y_scr):
    p = pl.program_id(1)
    j = pl.program_id(2)
    u = u_ref[0].reshape(S5_ROWS, SSM_WIDTH)
    ub = u.astype(BF16)

    @pl.when(jnp.logical_and(p == 0, j == 0))
    def _():
        bu_scr[...] = jnp.zeros_like(bu_scr)
        st_scr[...] = jnp.zeros_like(st_scr)

    @pl.when(jnp.logical_and(p == 1, j == 0))
    def _():
        st_scr[...] = carry_scr[...]

    for q in range(S5_Q):
        r = jnp.dot(ub[:, q * S5_QC:(q + 1) * S5_QC], wb_ref[q], preferred_element_type=F32)
        for half in range(2):
            for c in range(S5_QL):
                slab = half * S5_SLABS + q * S5_QL + c
                lanes = slice((half * S5_QL + c) * LANES, (half * S5_QL + c + 1) * LANES)
                for i in range(S5_SEG):
                    bu_scr[slab, i * S5_PITCH:i * S5_PITCH + S5_KB, :] = r[i * S5_KB:(i + 1) * S5_KB, lanes]

    def scan(store):
        for q in range(S5_Q):
            slabs = [q * S5_QL + c for c in range(S5_QL)]
            lanes = [slice(s * LANES, (s + 1) * LANES) for s in slabs]
            ilanes = [slice(S5_NS + s * LANES, S5_NS + (s + 1) * LANES) for s in slabs]
            ar = [jnp.broadcast_to(are_ref[:, ln], (S5_SEG, LANES)) for ln in lanes]
            ai = [jnp.broadcast_to(aim_ref[:, ln], (S5_SEG, LANES)) for ln in lanes]

            def body(k, carry):
                sr, si = carry
                rows = pl.ds(k, S5_SEG, stride=S5_PITCH)
                nr = tuple(ar[c] * sr[c] - ai[c] * si[c] + bu_scr[slabs[c], rows, :] for c in range(S5_QL))
                ni = tuple(ar[c] * si[c] + ai[c] * sr[c] + bu_scr[slabs[c] + S5_SLABS, rows, :]
                           for c in range(S5_QL))
                if store:
                    for c in range(S5_QL):
                        bu_scr[slabs[c], rows, :] = nr[c]
                        bu_scr[slabs[c] + S5_SLABS, rows, :] = ni[c]
                return nr, ni

            init = (tuple(st_scr[:, ln] for ln in lanes), tuple(st_scr[:, ln] for ln in ilanes))
            sr, si = lax.fori_loop(0, S5_KB, body, init, unroll=8)
            for c in range(S5_QL):
                st_scr[:, lanes[c]] = sr[c]
                st_scr[:, ilanes[c]] = si[c]

    @pl.when(p == 0)
    def _():
        scan(False)

        @pl.when(j == pl.num_programs(2) - 1)
        def _():
            lr, li = are_ref[...], aim_ref[...]
            for _ in range(S5_LOG2_SEGLEN):
                lr, li = lr * lr - li * li, 2.0 * lr * li
            cr = jnp.zeros((1, S5_NS), F32)
            ci = jnp.zeros((1, S5_NS), F32)
            carry_scr[0:1, :] = jnp.zeros((1, 2 * S5_NS), F32)
            for i in range(1, S5_SEG):
                er = st_scr[i - 1:i, 0:S5_NS]
                ei = st_scr[i - 1:i, S5_NS:2 * S5_NS]
                cr, ci = er + lr * cr - li * ci, ei + lr * ci + li * cr
                carry_scr[i:i + 1, 0:S5_NS] = cr
                carry_scr[i:i + 1, S5_NS:2 * S5_NS] = ci

    @pl.when(p == 1)
    def _():
        scan(True)
        for q in range(S5_Q):
            sre = jnp.concatenate([bu_scr[q * S5_QL + c] for c in range(S5_QL)], axis=1).astype(BF16)
            sim = jnp.concatenate([bu_scr[S5_SLABS + q * S5_QL + c] for c in range(S5_QL)], axis=1).astype(BF16)
            y_scr[:, q * S5_QC:(q + 1) * S5_QC] = (
                jnp.dot(sre, cre_ref[q], preferred_element_type=F32)
                + jnp.dot(sim, cim_ref[q], preferred_element_type=F32))
        cs = jnp.concatenate([y_scr[i * S5_PITCH:i * S5_PITCH + S5_KB, :] for i in range(S5_SEG)], axis=0)
        y = jax.nn.gelu(cs + d_ref[...] * u)
        z = jnp.dot(y.astype(BF16), gw_ref[...], preferred_element_type=F32) + gb_ref[...]
        o_ref[0] = _rms(y * jax.nn.sigmoid(z), go_ref[...]).reshape(S5_SEG, S5_KB, SSM_WIDTH)


def _s5_params(lam_re, lam_im, b_re, b_im, c_re, c_im, log_dt):
    lam = lax.complex(lam_re, lam_im)
    lam_bar = jnp.exp(lam * jnp.exp(log_dt)[:, None])
    b_bar = ((lam_bar - 1.0) / lam)[..., None] * lax.complex(b_re, b_im)
    gq = SSM_GROUPS // S5_Q
    eye = jnp.eye(gq, dtype=F32)

    def in_map(t):
        t = t.reshape(S5_Q, gq, SSM_STATE, SSM_CH)
        return jnp.einsum('qgpc,gh->qgchp', t, eye).reshape(S5_Q, S5_QC, S5_QS)

    def out_map(t):
        t = t.reshape(S5_Q, gq, SSM_CH, SSM_STATE)
        return jnp.einsum('qgcp,gh->qgphc', t, eye).reshape(S5_Q, S5_QS, S5_QC)

    wb = jnp.concatenate([in_map(b_bar.real), in_map(b_bar.imag)], axis=-1).astype(BF16)
    return (wb, lam_bar.real.reshape(1, S5_NS), lam_bar.imag.reshape(1, S5_NS),
            out_map(c_re).astype(BF16), out_map(-c_im).astype(BF16))


def _s5(proj4, wb, a_re, a_im, cre, cim, d_skip, glu_w, glu_b, out_gain):
    nblk = S5_SEGLEN // S5_KB
    fix2 = lambda b, p, j: (0, 0)
    fix3 = lambda b, p, j: (0, 0, 0)
    return pl.pallas_call(
        _s5_kernel,
        grid=(BATCH, 2, nblk),
        in_specs=[
            pl.BlockSpec((1, S5_SEG, S5_KB, SSM_WIDTH), lambda b, p, j: (b, 0, j, 0)),
            pl.BlockSpec((S5_Q, S5_QC, 2 * S5_QS), fix3),
            pl.BlockSpec((1, S5_NS), fix2),
            pl.BlockSpec((1, S5_NS), fix2),
            pl.BlockSpec((S5_Q, S5_QS, S5_QC), fix3),
            pl.BlockSpec((S5_Q, S5_QS, S5_QC), fix3),
            pl.BlockSpec((1, SSM_WIDTH), fix2),
            pl.BlockSpec((SSM_WIDTH, SSM_WIDTH), fix2),
            pl.BlockSpec((1, SSM_WIDTH), fix2),
            pl.BlockSpec((1, SSM_WIDTH), fix2),
        ],
        out_specs=pl.BlockSpec((1, S5_SEG, S5_KB, SSM_WIDTH), lambda b, p, j: (b, 0, j * p, 0)),
        out_shape=jax.ShapeDtypeStruct((BATCH, S5_SEG, S5_SEGLEN, SSM_WIDTH), F32),
        scratch_shapes=[
            pltpu.VMEM((2 * S5_SLABS, S5_PROWS, LANES), F32),
            pltpu.VMEM((S5_SEG, 2 * S5_NS), F32),
            pltpu.VMEM((S5_SEG, 2 * S5_NS), F32),
            pltpu.VMEM((S5_PROWS, SSM_WIDTH), F32),
        ],
        compiler_params=pltpu.CompilerParams(
            dimension_semantics=("parallel", "arbitrary", "arbitrary"), vmem_limit_bytes=VMEM_LIMIT),
        name="s5",
    )(proj4, wb, a_re, a_im, cre, cim, d_skip, glu_w, glu_b, out_gain)


def _s5_mixer(proj, p, l):
    wb, a_re, a_im, cre, cim = _s5_params(p['ssm_lambda_re'][l], p['ssm_lambda_im'][l], p['ssm_b_re'][l],
                                          p['ssm_b_im'][l], p['ssm_c_re'][l], p['ssm_c_im'][l],
                                          p['ssm_log_dt'][l])
    y = _s5(proj.reshape(BATCH, S5_SEG, S5_SEGLEN, N_IN_PAD), wb, a_re, a_im, cre, cim,
            p['ssm_d'][l].reshape(1, SSM_WIDTH), p['ssm_glu_w'][l].astype(BF16),
            p['ssm_glu_b'][l].reshape(1, SSM_WIDTH), p['ssm_out_gain'][l].reshape(1, SSM_WIDTH))
    return y.reshape(N_TOK, SSM_WIDTH)


AT_BLK = ATTN_BLOCK
AT_NBRANCH = len(DILATED_PAIRS)
AT_BLOCKS = SEQ // AT_BLK
AT_GROUP = 8
AT_COMBINE_ROWS = 256
assert all(w // d == AT_BLK for w, d in DILATED_PAIRS)
assert all((AT_BLOCKS // d) % AT_GROUP == 0 or AT_GROUP % (AT_BLOCKS // d) == 0 for _, d in DILATED_PAIRS)
AT_Q_COL0 = SSM_WIDTH // LANES
AT_K_COL0 = AT_Q_COL0 + ATTN_HEADS
AT_V_COL0 = AT_K_COL0 + ATTN_HEADS


def _attn_kernel(rb_ref, bkt_ref, q_ref, k_ref, v_ref, o_ref, bias_scr, o_scr, lse_scr):
    h = pl.program_id(0)

    @pl.when(pl.program_id(1) == 0)
    def _():
        for g in range(AT_NBRANCH):
            bkt = bkt_ref[g]
            bias = jnp.zeros((AT_BLK, 2 * AT_BLK), F32)
            for b in range(N_BUCKETS):
                bias = jnp.where(bkt == b, rb_ref[b, h], bias)
            bias_scr[g] = jnp.where(bkt < 0, NEG_INF, bias)

    lane = lax.broadcasted_iota(jnp.int32, (AT_BLK, 2 * AT_BLK), 1)
    scale = ATTN_HEAD_DIM ** -0.5
    for g, (_, dil) in enumerate(DILATED_PAIRS):
        nb = AT_BLOCKS // dil

        def body(it, carry, g=g, dil=dil, nb=nb):
            def block_rows(start):
                if dil == 1:
                    return pl.ds(pl.multiple_of(start, AT_BLK), AT_BLK)
                return pl.ds(start, AT_BLK, stride=dil)

            group = range(AT_GROUP)
            ts = [it * AT_GROUP + i for i in group]
            if nb <= AT_GROUP:
                ns = [i % nb for i in group]
            else:
                ns = [None if i == 0 else i for i in group]
            n0 = ts[0] % nb
            rows = [block_rows(t // nb + (t % nb) * (AT_BLK * dil)) for t in ts]
            q = [(q_ref[r, :] * scale).astype(BF16) for r in rows]
            k_cur = [k_ref[r, :].astype(BF16) for r in rows]
            v_cur = [v_ref[r, :].astype(BF16) for r in rows]
            if ns[0] is None:
                prow = block_rows(ts[0] // nb + jnp.maximum(n0 - 1, 0) * (AT_BLK * dil))
                k_lead, v_lead = k_ref[prow, :].astype(BF16), v_ref[prow, :].astype(BF16)
            s, vc = [], []
            for i in group:
                if ns[i] == 0:
                    s.append(_dot_nt(q[i], k_cur[i]) + bias_scr[g, :, AT_BLK:])
                    vc.append(v_cur[i])
                    continue
                k_prev, v_prev = (k_lead, v_lead) if ns[i] is None else (k_cur[i - 1], v_cur[i - 1])
                si = _dot_nt(q[i], jnp.concatenate([k_prev, k_cur[i]], axis=0)) + bias_scr[g]
                if ns[i] is None:
                    si = jnp.where(jnp.logical_or(n0 > 0, lane >= AT_BLK), si, NEG_INF)
                s.append(si)
                vc.append(jnp.concatenate([v_prev, v_cur[i]], axis=0))
            m = [jnp.max(si, axis=1, keepdims=True) for si in s]
            p = [jnp.exp(si - mi) for si, mi in zip(s, m)]
            l = [jnp.sum(pi, axis=1, keepdims=True) for pi in p]
            o = [jnp.dot(pi.astype(BF16), vi, preferred_element_type=F32) / li for pi, vi, li in zip(p, vc, l)]
            for r, oi, mi, li in zip(rows, o, m, l):
                o_scr[g, r, :] = oi
                lse_scr[g, r, :] = jnp.broadcast_to(mi + jnp.log(li), (AT_BLK, LANES))
            return carry

        lax.fori_loop(0, AT_BLOCKS // AT_GROUP, body, 0)

    def combine(i, carry):
        rows = pl.ds(pl.multiple_of(i * AT_COMBINE_ROWS, AT_COMBINE_ROWS), AT_COMBINE_ROWS)
        l0, l1, l2 = lse_scr[0, rows, :], lse_scr[1, rows, :], lse_scr[2, rows, :]
        m = jnp.maximum(jnp.maximum(l0, l1), l2)
        w0, w1, w2 = jnp.exp(l0 - m), jnp.exp(l1 - m), jnp.exp(l2 - m)
        num = w0 * o_scr[0, rows, :] + w1 * o_scr[1, rows, :] + w2 * o_scr[2, rows, :]
        o_ref[rows, :] = num / (w0 + w1 + w2)
        return carry

    lax.fori_loop(0, SEQ // AT_COMBINE_ROWS, combine, 0)


def _attn_bucket_table():
    qi = jnp.arange(AT_BLK)[:, None]
    kj = jnp.arange(2 * AT_BLK)[None, :]
    rel = AT_BLK + qi - kj
    max_exact = N_BUCKETS // 2
    tables = []
    for window, dil in DILATED_PAIRS:
        dist = jnp.maximum(rel, 0) * dil
        d = jnp.maximum(dist, 1).astype(F32)
        large = max_exact + jnp.log(d / max_exact) / math.log(REL_MAX_DIST / max_exact) * (N_BUCKETS - max_exact)
        large = jnp.minimum(large.astype(jnp.int32), N_BUCKETS - 1)
        bucket = jnp.where(dist < max_exact, dist, large)
        tables.append(jnp.where((rel >= 0) & (rel <= window // dil), bucket, -1))
    return jnp.stack(tables).astype(jnp.int32)


def _attn(proj, rel_bias, bucket_table):
    col = lambda c0: pl.BlockSpec((SEQ, LANES), lambda h, b: (b, c0 + h))
    return pl.pallas_call(
        _attn_kernel,
        grid=(ATTN_HEADS, BATCH),
        in_specs=[
            pl.BlockSpec(memory_space=pltpu.SMEM),
            pl.BlockSpec((AT_NBRANCH, AT_BLK, 2 * AT_BLK), lambda h, b: (0, 0, 0)),
            col(AT_Q_COL0), col(AT_K_COL0), col(AT_V_COL0),
        ],
        out_specs=pl.BlockSpec((SEQ, LANES), lambda h, b: (b, h)),
        out_shape=jax.ShapeDtypeStruct((N_TOK, ATTN_WIDTH), F32),
        scratch_shapes=[
            pltpu.VMEM((AT_NBRANCH, AT_BLK, 2 * AT_BLK), F32),
            pltpu.VMEM((AT_NBRANCH, SEQ, LANES), F32),
            pltpu.VMEM((AT_NBRANCH, SEQ, LANES), F32),
        ],
        compiler_params=pltpu.CompilerParams(
            dimension_semantics=("parallel", "arbitrary"), vmem_limit_bytes=VMEM_LIMIT),
        name="dilated_attn",
    )(rel_bias, bucket_table, proj, proj, proj)


DN_TILE = 2 * DN_CHUNK
DN_NTILE = SEQ // DN_TILE
DN_BH = BATCH * DN_HEADS
DN_PAD = 8
DN_GROUP = 4
DN_INV_LEVELS = 6
assert 2 ** DN_INV_LEVELS == DN_CHUNK
DN_QKV_COL0 = (SSM_WIDTH + 3 * ATTN_WIDTH) // LANES
DN_Z_COL0 = DN_QKV_COL0 + 3 * DN_HEADS
DN_AB_COL = DN_Z_COL0 + DN_HEADS
HIGHEST = lax.Precision.HIGHEST


def _dot_nt(a, b):
    return lax.dot_general(a, b, (((1,), (1,)), ((), ())), preferred_element_type=F32)


def _split3(x):
    hi = x.astype(BF16)
    r = x - hi.astype(F32)
    mid = r.astype(BF16)
    return hi, mid, (r - mid.astype(F32)).astype(BF16)


def _dot_exact_lhs(x, m16):
    return sum(jnp.dot(piece, m16, preferred_element_type=F32) for piece in _split3(x))


def _dot_exact_rhs(m16, x):
    return sum(jnp.dot(m16, piece, preferred_element_type=F32) for piece in _split3(x))


def _dn_prep_kernel(alog_ref, dtb_ref, q_ref, k_ref, v_ref, z_ref, ab_ref, wq_ref, wk_ref, wv_ref,
                    u_o, wq_o, akt_o, dec_o, sz_o, qp, kp, vp):
    h = pl.program_id(1)
    for src, dst in ((q_ref, qp), (k_ref, kp), (v_ref, vp)):
        dst[0:DN_PAD, :] = jnp.zeros((DN_PAD, LANES), F32)
        dst[DN_PAD:DN_PAD + SEQ, :] = src[...]

    neg_a = -jnp.exp(jnp.full((1, LANES), alog_ref[h], F32))
    dt_bias = dtb_ref[h]
    row = lax.broadcasted_iota(jnp.int32, (DN_TILE, DN_TILE), 0)
    col = lax.broadcasted_iota(jnp.int32, (DN_TILE, DN_TILE), 1)
    same = (row // DN_CHUNK) == (col // DN_CHUNK)
    causal = jnp.logical_and(same, row >= col)
    cumsum_mat = causal.astype(BF16)
    pick = jnp.concatenate([row == h, row == h + DN_HEADS], axis=1).astype(BF16)
    eye = (row == col).astype(F32)
    pair_masks = [
        jnp.logical_and(jnp.logical_and((row // (2 * s)) == (col // (2 * s)), (row // s) % 2 == 1),
                        (col // s) % 2 == 0)
        for s in (2 ** i for i in range(DN_INV_LEVELS))]
    first_chunk_col = col < DN_CHUNK
    scale = DN_HEAD_DIM ** -0.5

    def conv_silu(pad_ref, w_ref, base):
        win = pad_ref[pl.ds(base, DN_TILE + DN_PAD), :]
        acc = None
        for j in range(DN_CONV):
            sh = DN_PAD - (DN_CONV - 1) + j
            term = w_ref[j:j + 1, :] * win[sh:sh + DN_TILE, :]
            acc = term if acc is None else acc + term
        return acc * jax.nn.sigmoid(acc)

    def l2n(x):
        return x * lax.rsqrt(jnp.sum(x * x, axis=1, keepdims=True) + NORM_EPS)

    def mm16(a, b):
        return jnp.dot(a.astype(BF16), b.astype(BF16), preferred_element_type=F32)

    def body(it, carry):
        tiles = [it * DN_GROUP + i for i in range(DN_GROUP)]
        bases = [pl.multiple_of(t * DN_TILE, DN_TILE) for t in tiles]
        q = [l2n(conv_silu(qp, wq_ref, b)) * scale for b in bases]
        k = [l2n(conv_silu(kp, wk_ref, b)) for b in bases]
        v = [conv_silu(vp, wv_ref, b) for b in bases]
        ab_rep = [_dot_exact_lhs(ab_ref[pl.ds(b, DN_TILE), :], pick) for b in bases]
        beta = [jax.nn.sigmoid(r[:, DN_TILE:]) for r in ab_rep]
        x = [r[:, :DN_TILE] + dt_bias for r in ab_rep]
        g = [neg_a * (jnp.maximum(xi, 0.0) + jnp.log1p(jnp.exp(-jnp.abs(xi)))) for xi in x]
        gc = [_dot_exact_rhs(cumsum_mat, gi) for gi in g]
        decay = [jnp.exp(jnp.where(causal, c - c.T, NEG_INF)) for c in gc]
        kb = [ki * bi for ki, bi in zip(k, beta)]
        k16 = [ki.astype(BF16) for ki in k]
        a_mat = [_dot_nt(kbi.astype(BF16), ki) * di for kbi, ki, di in zip(kb, k16, decay)]
        t_inv = [eye - jnp.where(pair_masks[0], am, 0.0) for am in a_mat]
        for mask in pair_masks[1:]:
            low = [jnp.where(mask, am, 0.0).astype(BF16) for am in a_mat]
            t_inv = [ti - mm16(mm16(ti, lo), ti) for ti, lo in zip(t_inv, low)]
        egc = [jnp.exp(c) for c in gc]
        uw = [mm16(ti, jnp.concatenate([vi * bi, kbi * ei], axis=1))
              for ti, vi, bi, kbi, ei in zip(t_inv, v, beta, kb, egc)]
        at = [(_dot_nt(qi.astype(BF16), ki) * di).astype(BF16) for qi, ki, di in zip(q, k16, decay)]
        qd = [(qi * ei).astype(BF16) for qi, ei in zip(q, egc)]
        c = DN_CHUNK
        for i, t in enumerate(tiles):
            rows = pl.ds(bases[i], DN_TILE)
            u_o[0, rows, :] = uw[i][:, :DN_HEAD_DIM]
            w = uw[i][:, DN_HEAD_DIM:].astype(BF16)
            wq_o[0, pl.ds(pl.multiple_of(t * (2 * DN_TILE), 2 * DN_TILE), 2 * DN_TILE), :] = jnp.concatenate(
                [w[:c], qd[i][:c], w[c:], qd[i][c:]], axis=0)
            gc_first, gc_second = gc[i][c - 1:c, :], gc[i][DN_TILE - 1:DN_TILE, :]
            gc_last = jnp.where(row < c, gc_first, gc_second)
            kt = (k[i] * jnp.exp(gc_last - gc[i])).T
            akt_o[0, pl.ds(pl.multiple_of(t * (3 * DN_TILE), DN_TILE), 3 * DN_TILE), :] = jnp.concatenate(
                [at[i][:c], jnp.where(first_chunk_col, kt, 0.0).astype(BF16),
                 at[i][c:], jnp.where(first_chunk_col, 0.0, kt).astype(BF16)], axis=0)
            dec_o[0, t] = jnp.concatenate(
                [jnp.exp(gc_first), jnp.exp(gc_second), jnp.zeros((DN_PAD - 2, LANES), F32)], axis=0)
            z = z_ref[rows, :]
            sz_o[0, rows, :] = (z * jax.nn.sigmoid(z)).astype(BF16)
        return carry

    lax.fori_loop(0, DN_NTILE // DN_GROUP, body, 0)


def _dn_prep(proj, conv_w, a_log, dt_bias):
    col = lambda c0: pl.BlockSpec((SEQ, LANES), lambda b, h: (b, c0 + h))
    cw = lambda c0: pl.BlockSpec((DN_CONV, LANES), lambda b, h: (0, c0 + h))
    bh_rows = lambda n: pl.BlockSpec((1, n, LANES), lambda b, h: (b * DN_HEADS + h, 0, 0))
    rows_shape = lambda n, dt: jax.ShapeDtypeStruct((DN_BH, n, LANES), dt)
    return pl.pallas_call(
        _dn_prep_kernel,
        grid=(BATCH, DN_HEADS),
        in_specs=[
            pl.BlockSpec(memory_space=pltpu.SMEM), pl.BlockSpec(memory_space=pltpu.SMEM),
            col(DN_QKV_COL0), col(DN_QKV_COL0 + DN_HEADS), col(DN_QKV_COL0 + 2 * DN_HEADS), col(DN_Z_COL0),
            pl.BlockSpec((SEQ, LANES), lambda b, h: (b, DN_AB_COL)),
            cw(0), cw(DN_HEADS), cw(2 * DN_HEADS),
        ],
        out_specs=[
            bh_rows(SEQ), bh_rows(2 * SEQ), bh_rows(3 * SEQ),
            pl.BlockSpec((1, DN_NTILE, DN_PAD, LANES), lambda b, h: (b * DN_HEADS + h, 0, 0, 0)),
            bh_rows(SEQ),
        ],
        out_shape=[
            rows_shape(SEQ, F32), rows_shape(2 * SEQ, BF16), rows_shape(3 * SEQ, BF16),
            jax.ShapeDtypeStruct((DN_BH, DN_NTILE, DN_PAD, LANES), F32),
            rows_shape(SEQ, BF16),
        ],
        scratch_shapes=[pltpu.VMEM((SEQ + DN_PAD, LANES), F32)] * 3,
        compiler_params=pltpu.CompilerParams(
            dimension_semantics=("parallel", "parallel"), vmem_limit_bytes=VMEM_LIMIT),
        name="dn_prep",
    )(a_log, dt_bias, proj, proj, proj, proj, proj, conv_w, conv_w, conv_w)


def _dn_scan_kernel(u_ref, wq_ref, akt_ref, dec_ref, sz_ref, gain_ref, o_ref, s_scr):
    @pl.when(pl.program_id(0) == 0)
    def _():
        s_scr[...] = jnp.zeros_like(s_scr)

    gain = gain_ref[...]
    c = DN_CHUNK
    chains = range(DN_BH)
    state = [s_scr[bh] for bh in chains]
    v_first = None
    for j in range(2):
        rows = slice(j * c, (j + 1) * c)
        s16 = [s.astype(BF16) for s in state]
        ws_qs = [jnp.dot(wq_ref[bh, j * DN_TILE:(j + 1) * DN_TILE, :], s16[bh], preferred_element_type=F32)
                 for bh in chains]
        v_new = [u_ref[bh, rows, :] - ws_qs[bh][:c] for bh in chains]
        if j == 0:
            v_pair = [jnp.concatenate([v, jnp.zeros_like(v)], axis=0).astype(BF16) for v in v_new]
            v_first = v_new
        else:
            v_pair = [jnp.concatenate([v0, v], axis=0).astype(BF16) for v0, v in zip(v_first, v_new)]
        ov_kv = [jnp.dot(akt_ref[bh, j * 3 * c:(j + 1) * 3 * c, :], v_pair[bh], preferred_element_type=F32)
                 for bh in chains]
        state = [state[bh] * dec_ref[bh, 0, j:j + 1, :] + ov_kv[bh][c:] for bh in chains]
        for bh in chains:
            b, h = divmod(bh, DN_HEADS)
            o = ws_qs[bh][c:] + ov_kv[bh][:c]
            o_ref[b, rows, h * DN_HEAD_DIM:(h + 1) * DN_HEAD_DIM] = (
                _rms(o, gain) * sz_ref[bh, rows, :].astype(F32))
    for bh in chains:
        s_scr[bh] = state[bh]


def _dn_scan(u, wq, akt, dec, sz, gain):
    rows = lambda n: pl.BlockSpec((DN_BH, n * DN_TILE, LANES), lambda t: (0, t, 0))
    return pl.pallas_call(
        _dn_scan_kernel,
        grid=(DN_NTILE,),
        in_specs=[rows(1), rows(2), rows(3),
                  pl.BlockSpec((DN_BH, 1, DN_PAD, LANES), lambda t: (0, t, 0, 0)), rows(1),
                  pl.BlockSpec((1, LANES), lambda t: (0, 0))],
        out_specs=pl.BlockSpec((BATCH, DN_TILE, DN_WIDTH), lambda t: (0, t, 0)),
        out_shape=jax.ShapeDtypeStruct((BATCH, SEQ, DN_WIDTH), F32),
        scratch_shapes=[pltpu.VMEM((DN_BH, DN_HEAD_DIM, DN_HEAD_DIM), F32)],
        compiler_params=pltpu.CompilerParams(
            dimension_semantics=("arbitrary",), vmem_limit_bytes=VMEM_LIMIT),
        name="dn_scan",
    )(u, wq, akt, dec, sz, gain)


def _dn_mixer(proj, p, l):
    outs = _dn_prep(proj, p['dn_conv_w'][l], p['dn_a_log'][l], p['dn_dt_bias'][l])
    y = _dn_scan(*outs, p['dn_norm_gain'][l].reshape(1, DN_HEAD_DIM))
    return y.reshape(N_TOK, DN_WIDTH)


def kernel(x, norm_gains, ffn_w_gate, ffn_w_up, ffn_w_down, w_in, w_out, ssm_lambda_re, ssm_lambda_im,
           ssm_b_re, ssm_b_im, ssm_c_re, ssm_c_im, ssm_d, ssm_log_dt, ssm_glu_w, ssm_glu_b, ssm_out_gain,
           dn_conv_w, dn_a_log, dn_dt_bias, dn_norm_gain, attn_out_gain, rel_bias):
    p = dict(ssm_lambda_re=ssm_lambda_re, ssm_lambda_im=ssm_lambda_im, ssm_b_re=ssm_b_re, ssm_b_im=ssm_b_im,
             ssm_c_re=ssm_c_re, ssm_c_im=ssm_c_im, ssm_d=ssm_d, ssm_log_dt=ssm_log_dt, ssm_glu_w=ssm_glu_w,
             ssm_glu_b=ssm_glu_b, ssm_out_gain=ssm_out_gain, dn_conv_w=dn_conv_w, dn_a_log=dn_a_log,
             dn_dt_bias=dn_dt_bias, dn_norm_gain=dn_norm_gain)
    wg, wu, wd = ffn_w_gate, ffn_w_up, ffn_w_down
    w_in_p = jnp.pad(w_in, ((0, 0), (0, 0), (0, N_IN_PAD - N_IN_COLS))).astype(BF16)
    w_out_b = w_out.astype(BF16)
    bucket_table = _attn_bucket_table()
    gains = norm_gains.reshape(DEPTH, 6, 1, D_MODEL)
    x = x.reshape(N_TOK, D_MODEL)
    for l in range(DEPTH):
        x = _ffn(x, gains[l, 0], gains[l, 1], wg, wu, wd, l, 0)
        proj = _inproj(x, gains[l, 2], w_in_p, l)
        y_ssm = _s5_mixer(proj, p, l)
        y_dn = _dn_mixer(proj, p, l)
        o_at = _attn(proj, rel_bias, bucket_table)
        x = _outproj(x, y_ssm, y_dn, o_at, attn_out_gain[l].reshape(1, ATTN_WIDTH), gains[l, 3], w_out_b, l)
        x = _ffn(x, gains[l, 4], gains[l, 5], wg, wu, wd, l, 1)
    return x.reshape(BATCH, SEQ, D_MODEL)
```

```python
import functools
import math

import jax
import jax.numpy as jnp
import numpy as np
from jax import lax
from jax.experimental import pallas as pl
from jax.experimental.pallas import tpu as pltpu

D_MODEL = 2048
BATCH = 2
SEQ = 4096
DEPTH = 4
SSM_GROUPS = 32
SSM_CH = 16
SSM_STATE = 64
SSM_WIDTH = SSM_GROUPS * SSM_CH
DN_HEADS = 6
DN_HEAD_DIM = 128
DN_WIDTH = DN_HEADS * DN_HEAD_DIM
DN_CONV = 4
DN_CHUNK = 64
ATTN_HEADS = 6
ATTN_HEAD_DIM = 128
ATTN_WIDTH = ATTN_HEADS * ATTN_HEAD_DIM
DILATED_PAIRS = ((128, 1), (512, 4), (2048, 16))
ATTN_BLOCK = 128
N_BUCKETS = 32
REL_MAX_DIST = 2048
D_MIX = SSM_WIDTH + DN_WIDTH + ATTN_WIDTH
IN_SPLITS = (SSM_WIDTH, ATTN_WIDTH, ATTN_WIDTH, ATTN_WIDTH, 3 * DN_WIDTH, DN_WIDTH, DN_HEADS, DN_HEADS)
N_IN_COLS = sum(IN_SPLITS)
D_FF = 5632
NORM_EPS = 1e-6
NEG_INF = -1e30

LANES = 128
N_IN_PAD = 6144
V7X_VMEM_BYTES = 64 * 1024 * 1024
VMEM_LIMIT = V7X_VMEM_BYTES - 4 * 1024 * 1024
N_TOK = BATCH * SEQ

BF16 = jnp.bfloat16
F32 = jnp.float32


def _rms(x, gain):
    return x * lax.rsqrt(jnp.mean(x * x, axis=-1, keepdims=True) + NORM_EPS) * gain


FFN_TM = 1024
FFN_TF = 256


def _ffn_kernel(x_ref, gpre_ref, gpost_ref, wg_ref, wu_ref, wd_ref, o_ref, h_scr):
    f = pl.program_id(1)

    @pl.when(f == 0)
    def _():
        h_scr[...] = _rms(x_ref[...], gpre_ref[...]).astype(BF16)
        o_ref[...] = jnp.zeros_like(o_ref)

    h = h_scr[...]
    gate = jnp.dot(h, wg_ref[...].astype(BF16), preferred_element_type=F32)
    up = jnp.dot(h, wu_ref[...].astype(BF16), preferred_element_type=F32)
    act = (gate * jax.nn.sigmoid(gate) * up).astype(BF16)
    o_ref[...] += jnp.dot(act, wd_ref[...].astype(BF16), preferred_element_type=F32)

    @pl.when(f == pl.num_programs(1) - 1)
    def _():
        o_ref[...] = x_ref[...] + _rms(o_ref[...], 0.5 * gpost_ref[...])


def _ffn(x, g_pre, g_post, wg, wu, wd, layer, half):
    return pl.pallas_call(
        _ffn_kernel,
        grid=(N_TOK // FFN_TM, D_FF // FFN_TF),
        in_specs=[
            pl.BlockSpec((FFN_TM, D_MODEL), lambda i, f: (i, 0)),
            pl.BlockSpec((1, D_MODEL), lambda i, f: (0, 0)),
            pl.BlockSpec((1, D_MODEL), lambda i, f: (0, 0)),
            pl.BlockSpec((None, None, D_MODEL, FFN_TF), lambda i, f: (layer, half, 0, f)),
            pl.BlockSpec((None, None, D_MODEL, FFN_TF), lambda i, f: (layer, half, 0, f)),
            pl.BlockSpec((None, None, FFN_TF, D_MODEL), lambda i, f: (layer, half, f, 0)),
        ],
        out_specs=pl.BlockSpec((FFN_TM, D_MODEL), lambda i, f: (i, 0)),
        out_shape=jax.ShapeDtypeStruct((N_TOK, D_MODEL), F32),
        scratch_shapes=[pltpu.VMEM((FFN_TM, D_MODEL), BF16)],
        compiler_params=pltpu.CompilerParams(
            dimension_semantics=("parallel", "arbitrary"), vmem_limit_bytes=VMEM_LIMIT),
        name="ffn",
    )(x, g_pre, g_post, wg, wu, wd)


INP_TM = 1024
INP_TN = 768


def _inproj_kernel(x_ref, g_ref, w_ref, o_ref, h_scr):
    @pl.when(pl.program_id(1) == 0)
    def _():
        h_scr[...] = _rms(x_ref[...], g_ref[...]).astype(BF16)

    o_ref[...] = jnp.dot(h_scr[...], w_ref[...].astype(BF16), preferred_element_type=F32)


def _inproj(x, g, w, layer):
    return pl.pallas_call(
        _inproj_kernel,
        grid=(N_TOK // INP_TM, N_IN_PAD // INP_TN),
        in_specs=[
            pl.BlockSpec((INP_TM, D_MODEL), lambda i, n: (i, 0)),
            pl.BlockSpec((1, D_MODEL), lambda i, n: (0, 0)),
            pl.BlockSpec((None, D_MODEL, INP_TN), lambda i, n: (layer, 0, n)),
        ],
        out_specs=pl.BlockSpec((INP_TM, INP_TN), lambda i, n: (i, n)),
        out_shape=jax.ShapeDtypeStruct((N_TOK, N_IN_PAD), F32),
        scratch_shapes=[pltpu.VMEM((INP_TM, D_MODEL), BF16)],
        compiler_params=pltpu.CompilerParams(
            dimension_semantics=("parallel", "arbitrary"), vmem_limit_bytes=VMEM_LIMIT),
        name="inproj",
    )(x, g, w)


OUT_TM = 256


def _outproj_kernel(x_ref, ys_ref, yd_ref, oa_ref, ga_ref, gpost_ref, w_ref, o_ref, w16):
    @pl.when(pl.program_id(0) == 0)
    def _():
        w16[...] = w_ref[...].astype(BF16)

    ya = _rms(oa_ref[...], ga_ref[...]).astype(BF16)
    mix = jnp.dot(ys_ref[...].astype(BF16), w16[0:SSM_WIDTH, :], preferred_element_type=F32)
    mix += jnp.dot(yd_ref[...].astype(BF16), w16[SSM_WIDTH:SSM_WIDTH + DN_WIDTH, :],
                   preferred_element_type=F32)
    mix += jnp.dot(ya, w16[SSM_WIDTH + DN_WIDTH:D_MIX, :], preferred_element_type=F32)
    o_ref[...] = x_ref[...] + _rms(mix, gpost_ref[...])


def _outproj(x, y_ssm, y_dn, o_at, g_attn, g_post, w, layer):
    row = lambda i: (i, 0)
    fixed = lambda i: (0, 0)
    return pl.pallas_call(
        _outproj_kernel,
        grid=(N_TOK // OUT_TM,),
        in_specs=[
            pl.BlockSpec((OUT_TM, D_MODEL), row),
            pl.BlockSpec((OUT_TM, SSM_WIDTH), row),
            pl.BlockSpec((OUT_TM, DN_WIDTH), row),
            pl.BlockSpec((OUT_TM, ATTN_WIDTH), row),
            pl.BlockSpec((1, ATTN_WIDTH), fixed),
            pl.BlockSpec((1, D_MODEL), fixed),
            pl.BlockSpec((None, D_MIX, D_MODEL), lambda i: (layer, 0, 0)),
        ],
        out_specs=pl.BlockSpec((OUT_TM, D_MODEL), row),
        out_shape=jax.ShapeDtypeStruct((N_TOK, D_MODEL), F32),
        scratch_shapes=[pltpu.VMEM((D_MIX, D_MODEL), BF16)],
        compiler_params=pltpu.CompilerParams(
            dimension_semantics=("arbitrary",), vmem_limit_bytes=VMEM_LIMIT),
        name="outproj",
    )(x, y_ssm, y_dn, o_at, g_attn, g_post, w)


S5_SEG = 8
S5_SEGLEN = SEQ // S5_SEG
S5_KB = 64
S5_ROWS = S5_KB * S5_SEG
S5_PITCH = S5_KB + 8
S5_PROWS = S5_PITCH * S5_SEG
S5_NS = SSM_GROUPS * SSM_STATE
S5_SLABS = S5_NS // LANES
S5_Q = 4
S5_QS = S5_NS // S5_Q
S5_QL = S5_QS // LANES
S5_QC = SSM_WIDTH // S5_Q
S5_LOG2_SEGLEN = 9
assert 1 << S5_LOG2_SEGLEN == S5_SEGLEN


def _s5_kernel(u_ref, wb_ref, are_ref, aim_ref, cre_ref, cim_ref, d_ref, gw_ref, gb_ref, go_ref, o_ref,
               bu_scr, st_scr, carry_scr, y_scr):
    p = pl.program_id(1)
    j = pl.program_id(2)
    u = u_ref[0].reshape(S5_ROWS, SSM_WIDTH)
    ub = u.astype(BF16)

    @pl.when(jnp.logical_and(p == 0, j == 0))
    def _():
        bu_scr[...] = jnp.zeros_like(bu_scr)
        st_scr[...] = jnp.zeros_like(st_scr)

    @pl.when(jnp.logical_and(p == 1, j == 0))
    def _():
        st_scr[...] = carry_scr[...]

    for q in range(S5_Q):
        r = jnp.dot(ub[:, q * S5_QC:(q + 1) * S5_QC], wb_ref[q], preferred_element_type=F32)
        for half in range(2):
            for c in range(S5_QL):
                slab = half * S5_SLABS + q * S5_QL + c
                lanes = slice((half * S5_QL + c) * LANES, (half * S5_QL + c + 1) * LANES)
                for i in range(S5_SEG):
                    bu_scr[slab, i * S5_PITCH:i * S5_PITCH + S5_KB, :] = r[i * S5_KB:(i + 1) * S5_KB, lanes]

    def scan(store):
        for q in range(S5_Q):
            slabs = [q * S5_QL + c for c in range(S5_QL)]
            lanes = [slice(s * LANES, (s + 1) * LANES) for s in slabs]
            ilanes = [slice(S5_NS + s * LANES, S5_NS + (s + 1) * LANES) for s in slabs]
            ar = [jnp.broadcast_to(are_ref[:, ln], (S5_SEG, LANES)) for ln in lanes]
            ai = [jnp.broadcast_to(aim_ref[:, ln], (S5_SEG, LANES)) for ln in lanes]

            def body(k, carry):
                sr, si = carry
                rows = pl.ds(k, S5_SEG, stride=S5_PITCH)
                nr = tuple(ar[c] * sr[c] - ai[c] * si[c] + bu_scr[slabs[c], rows, :] for c in range(S5_QL))
                ni = tuple(ar[c] * si[c] + ai[c] * sr[c] + bu_scr[slabs[c] + S5_SLABS, rows, :]
                           for c in range(S5_QL))
                if store:
                    for c in range(S5_QL):
                        bu_scr[slabs[c], rows, :] = nr[c]
                        bu_scr[slabs[c] + S5_SLABS, rows, :] = ni[c]
                return nr, ni

            init = (tuple(st_scr[:, ln] for ln in lanes), tuple(st_scr[:, ln] for ln in ilanes))
            sr, si = lax.fori_loop(0, S5_KB, body, init, unroll=8)
            for c in range(S5_QL):
                st_scr[:, lanes[c]] = sr[c]
                st_scr[:, ilanes[c]] = si[c]

    @pl.when(p == 0)
    def _():
        scan(False)

        @pl.when(j == pl.num_programs(2) - 1)
        def _():
            lr, li = are_ref[...], aim_ref[...]
            for _ in range(S5_LOG2_SEGLEN):
                lr, li = lr * lr - li * li, 2.0 * lr * li
            cr = jnp.zeros((1, S5_NS), F32)
            ci = jnp.zeros((1, S5_NS), F32)
            carry_scr[0:1, :] = jnp.zeros((1, 2 * S5_NS), F32)
            for i in range(1, S5_SEG):
                er = st_scr[i - 1:i, 0:S5_NS]
                ei = st_scr[i - 1:i, S5_NS:2 * S5_NS]
                cr, ci = er + lr * cr - li * ci, ei + lr * ci + li * cr
                carry_scr[i:i + 1, 0:S5_NS] = cr
                carry_scr[i:i + 1, S5_NS:2 * S5_NS] = ci

    @pl.when(p == 1)
    def _():
        scan(True)
        for q in range(S5_Q):
            sre = jnp.concatenate([bu_scr[q * S5_QL + c] for c in range(S5_QL)], axis=1).astype(BF16)
            sim = jnp.concatenate([bu_scr[S5_SLABS + q * S5_QL + c] for c in range(S5_QL)], axis=1).astype(BF16)
            y_scr[:, q * S5_QC:(q + 1) * S5_QC] = (
                jnp.dot(sre, cre_ref[q], preferred_element_type=F32)
                + jnp.dot(sim, cim_ref[q], preferred_element_type=F32))
        cs = jnp.concatenate([y_scr[i * S5_PITCH:i * S5_PITCH + S5_KB, :] for i in range(S5_SEG)], axis=0)
        y = jax.nn.gelu(cs + d_ref[...] * u)
        z = jnp.dot(y.astype(BF16), gw_ref[...], preferred_element_type=F32) + gb_ref[...]
        o_ref[0] = _rms(y * jax.nn.sigmoid(z), go_ref[...]).reshape(S5_SEG, S5_KB, SSM_WIDTH)


def _s5_params(lam_re, lam_im, b_re, b_im, c_re, c_im, log_dt):
    lam = lax.complex(lam_re, lam_im)
    lam_bar = jnp.exp(lam * jnp.exp(log_dt)[:, None])
    b_bar = ((lam_bar - 1.0) / lam)[..., None] * lax.complex(b_re, b_im)
    gq = SSM_GROUPS // S5_Q
    eye = jnp.eye(gq, dtype=F32)

    def in_map(t):
        t = t.reshape(S5_Q, gq, SSM_STATE, SSM_CH)
        return jnp.einsum('qgpc,gh->qgchp', t, eye).reshape(S5_Q, S5_QC, S5_QS)

    def out_map(t):
        t = t.reshape(S5_Q, gq, SSM_CH, SSM_STATE)
        return jnp.einsum('qgcp,gh->qgphc', t, eye).reshape(S5_Q, S5_QS, S5_QC)

    wb = jnp.concatenate([in_map(b_bar.real), in_map(b_bar.imag)], axis=-1).astype(BF16)
    return (wb, lam_bar.real.reshape(1, S5_NS), lam_bar.imag.reshape(1, S5_NS),
            out_map(c_re).astype(BF16), out_map(-c_im).astype(BF16))


def _s5(proj4, wb, a_re, a_im, cre, cim, d_skip, glu_w, glu_b, out_gain):
    nblk = S5_SEGLEN // S5_KB
    fix2 = lambda b, p, j: (0, 0)
    fix3 = lambda b, p, j: (0, 0, 0)
    return pl.pallas_call(
        _s5_kernel,
        grid=(BATCH, 2, nblk),
        in_specs=[
            pl.BlockSpec((1, S5_SEG, S5_KB, SSM_WIDTH), lambda b, p, j: (b, 0, j, 0)),
            pl.BlockSpec((S5_Q, S5_QC, 2 * S5_QS), fix3),
            pl.BlockSpec((1, S5_NS), fix2),
            pl.BlockSpec((1, S5_NS), fix2),
            pl.BlockSpec((S5_Q, S5_QS, S5_QC), fix3),
            pl.BlockSpec((S5_Q, S5_QS, S5_QC), fix3),
            pl.BlockSpec((1, SSM_WIDTH), fix2),
            pl.BlockSpec((SSM_WIDTH, SSM_WIDTH), fix2),
            pl.BlockSpec((1, SSM_WIDTH), fix2),
            pl.BlockSpec((1, SSM_WIDTH), fix2),
        ],
        out_specs=pl.BlockSpec((1, S5_SEG, S5_KB, SSM_WIDTH), lambda b, p, j: (b, 0, j * p, 0)),
        out_shape=jax.ShapeDtypeStruct((BATCH, S5_SEG, S5_SEGLEN, SSM_WIDTH), F32),
        scratch_shapes=[
            pltpu.VMEM((2 * S5_SLABS, S5_PROWS, LANES), F32),
            pltpu.VMEM((S5_SEG, 2 * S5_NS), F32),
            pltpu.VMEM((S5_SEG, 2 * S5_NS), F32),
            pltpu.VMEM((S5_PROWS, SSM_WIDTH), F32),
        ],
        compiler_params=pltpu.CompilerParams(
            dimension_semantics=("parallel", "arbitrary", "arbitrary"), vmem_limit_bytes=VMEM_LIMIT),
        name="s5",
    )(proj4, wb, a_re, a_im, cre, cim, d_skip, glu_w, glu_b, out_gain)


def _s5_mixer(proj, p, l):
    wb, a_re, a_im, cre, cim = _s5_params(p['ssm_lambda_re'][l], p['ssm_lambda_im'][l], p['ssm_b_re'][l],
                                          p['ssm_b_im'][l], p['ssm_c_re'][l], p['ssm_c_im'][l],
                                          p['ssm_log_dt'][l])
    y = _s5(proj.reshape(BATCH, S5_SEG, S5_SEGLEN, N_IN_PAD), wb, a_re, a_im, cre, cim,
            p['ssm_d'][l].reshape(1, SSM_WIDTH), p['ssm_glu_w'][l].astype(BF16),
            p['ssm_glu_b'][l].reshape(1, SSM_WIDTH), p['ssm_out_gain'][l].reshape(1, SSM_WIDTH))
    return y.reshape(N_TOK, SSM_WIDTH)


AT_BLK = ATTN_BLOCK
AT_NBRANCH = len(DILATED_PAIRS)
AT_BLOCKS = SEQ // AT_BLK
AT_GROUP = 8
AT_COMBINE_ROWS = 256
assert all(w // d == AT_BLK for w, d in DILATED_PAIRS)
assert all((AT_BLOCKS // d) % AT_GROUP == 0 or AT_GROUP % (AT_BLOCKS // d) == 0 for _, d in DILATED_PAIRS)
AT_Q_COL0 = SSM_WIDTH // LANES
AT_K_COL0 = AT_Q_COL0 + ATTN_HEADS
AT_V_COL0 = AT_K_COL0 + ATTN_HEADS


def _attn_kernel(rb_ref, bkt_ref, q_ref, k_ref, v_ref, o_ref, bias_scr, o_scr, lse_scr):
    h = pl.program_id(0)

    @pl.when(pl.program_id(1) == 0)
    def _():
        for g in range(AT_NBRANCH):
            bkt = bkt_ref[g]
            bias = jnp.zeros((AT_BLK, 2 * AT_BLK), F32)
            for b in range(N_BUCKETS):
                bias = jnp.where(bkt == b, rb_ref[b, h], bias)
            bias_scr[g] = jnp.where(bkt < 0, NEG_INF, bias)

    lane = lax.broadcasted_iota(jnp.int32, (AT_BLK, 2 * AT_BLK), 1)
    scale = ATTN_HEAD_DIM ** -0.5
    for g, (_, dil) in enumerate(DILATED_PAIRS):
        nb = AT_BLOCKS // dil

        def body(it, carry, g=g, dil=dil, nb=nb):
            def block_rows(start):
                if dil == 1:
                    return pl.ds(pl.multiple_of(start, AT_BLK), AT_BLK)
                return pl.ds(start, AT_BLK, stride=dil)

            group = range(AT_GROUP)
            ts = [it * AT_GROUP + i for i in group]
            if nb <= AT_GROUP:
                ns = [i % nb for i in group]
            else:
                ns = [None if i == 0 else i for i in group]
            n0 = ts[0] % nb
            rows = [block_rows(t // nb + (t % nb) * (AT_BLK * dil)) for t in ts]
            q = [(q_ref[r, :] * scale).astype(BF16) for r in rows]
            k_cur = [k_ref[r, :].astype(BF16) for r in rows]
            v_cur = [v_ref[r, :].astype(BF16) for r in rows]
            if ns[0] is None:
                prow = block_rows(ts[0] // nb + jnp.maximum(n0 - 1, 0) * (AT_BLK * dil))
                k_lead, v_lead = k_ref[prow, :].astype(BF16), v_ref[prow, :].astype(BF16)
            s, vc = [], []
            for i in group:
                if ns[i] == 0:
                    s.append(_dot_nt(q[i], k_cur[i]) + bias_scr[g, :, AT_BLK:])
                    vc.append(v_cur[i])
                    continue
                k_prev, v_prev = (k_lead, v_lead) if ns[i] is None else (k_cur[i - 1], v_cur[i - 1])
                si = _dot_nt(q[i], jnp.concatenate([k_prev, k_cur[i]], axis=0)) + bias_scr[g]
                if ns[i] is None:
                    si = jnp.where(jnp.logical_or(n0 > 0, lane >= AT_BLK), si, NEG_INF)
                s.append(si)
                vc.append(jnp.concatenate([v_prev, v_cur[i]], axis=0))
            m = [jnp.max(si, axis=1, keepdims=True) for si in s]
            p = [jnp.exp(si - mi) for si, mi in zip(s, m)]
            l = [jnp.sum(pi, axis=1, keepdims=True) for pi in p]
            o = [jnp.dot(pi.astype(BF16), vi, preferred_element_type=F32) / li for pi, vi, li in zip(p, vc, l)]
            for r, oi, mi, li in zip(rows, o, m, l):
                o_scr[g, r, :] = oi
                lse_scr[g, r, :] = jnp.broadcast_to(mi + jnp.log(li), (AT_BLK, LANES))
            return carry

        lax.fori_loop(0, AT_BLOCKS // AT_GROUP, body, 0)

    def combine(i, carry):
        rows = pl.ds(pl.multiple_of(i * AT_COMBINE_ROWS, AT_COMBINE_ROWS), AT_COMBINE_ROWS)
        l0, l1, l2 = lse_scr[0, rows, :], lse_scr[1, rows, :], lse_scr[2, rows, :]
        m = jnp.maximum(jnp.maximum(l0, l1), l2)
        w0, w1, w2 = jnp.exp(l0 - m), jnp.exp(l1 - m), jnp.exp(l2 - m)
        num = w0 * o_scr[0, rows, :] + w1 * o_scr[1, rows, :] + w2 * o_scr[2, rows, :]
        o_ref[rows, :] = num / (w0 + w1 + w2)
        return carry

    lax.fori_loop(0, SEQ // AT_COMBINE_ROWS, combine, 0)


def _attn_bucket_table():
    qi = jnp.arange(AT_BLK)[:, None]
    kj = jnp.arange(2 * AT_BLK)[None, :]
    rel = AT_BLK + qi - kj
    max_exact = N_BUCKETS // 2
    tables = []
    for window, dil in DILATED_PAIRS:
        dist = jnp.maximum(rel, 0) * dil
        d = jnp.maximum(dist, 1).astype(F32)
        large = max_exact + jnp.log(d / max_exact) / math.log(REL_MAX_DIST / max_exact) * (N_BUCKETS - max_exact)
        large = jnp.minimum(large.astype(jnp.int32), N_BUCKETS - 1)
        bucket = jnp.where(dist < max_exact, dist, large)
        tables.append(jnp.where((rel >= 0) & (rel <= window // dil), bucket, -1))
    return jnp.stack(tables).astype(jnp.int32)


def _attn(proj, rel_bias, bucket_table):
    col = lambda c0: pl.BlockSpec((SEQ, LANES), lambda h, b: (b, c0 + h))
    return pl.pallas_call(
        _attn_kernel,
        grid=(ATTN_HEADS, BATCH),
        in_specs=[
            pl.BlockSpec(memory_space=pltpu.SMEM),
            pl.BlockSpec((AT_NBRANCH, AT_BLK, 2 * AT_BLK), lambda h, b: (0, 0, 0)),
            col(AT_Q_COL0), col(AT_K_COL0), col(AT_V_COL0),
        ],
        out_specs=pl.BlockSpec((SEQ, LANES), lambda h, b: (b, h)),
        out_shape=jax.ShapeDtypeStruct((N_TOK, ATTN_WIDTH), F32),
        scratch_shapes=[
            pltpu.VMEM((AT_NBRANCH, AT_BLK, 2 * AT_BLK), F32),
            pltpu.VMEM((AT_NBRANCH, SEQ, LANES), F32),
            pltpu.VMEM((AT_NBRANCH, SEQ, LANES), F32),
        ],
        compiler_params=pltpu.CompilerParams(
            dimension_semantics=("parallel", "arbitrary"), vmem_limit_bytes=VMEM_LIMIT),
        name="dilated_attn",
    )(rel_bias, bucket_table, proj, proj, proj)


DN_TILE = 2 * DN_CHUNK
DN_NTILE = SEQ // DN_TILE
DN_BH = BATCH * DN_HEADS
DN_PAD = 8
DN_GROUP = 8
DN_INV_LEVELS = 6
assert 2 ** DN_INV_LEVELS == DN_CHUNK
DN_QKV_COL0 = (SSM_WIDTH + 3 * ATTN_WIDTH) // LANES
DN_Z_COL0 = DN_QKV_COL0 + 3 * DN_HEADS
DN_AB_COL = DN_Z_COL0 + DN_HEADS
HIGHEST = lax.Precision.HIGHEST


def _dot_nt(a, b):
    return lax.dot_general(a, b, (((1,), (1,)), ((), ())), preferred_element_type=F32)


def _split3(x):
    hi = x.astype(BF16)
    r = x - hi.astype(F32)
    mid = r.astype(BF16)
    return hi, mid, (r - mid.astype(F32)).astype(BF16)


def _dot_exact_lhs(x, m16):
    return sum(jnp.dot(piece, m16, preferred_element_type=F32) for piece in _split3(x))


def _dot_exact_rhs(m16, x):
    return sum(jnp.dot(m16, piece, preferred_element_type=F32) for piece in _split3(x))


def _dn_prep_kernel(alog_ref, dtb_ref, q_ref, k_ref, v_ref, z_ref, ab_ref, wq_ref, wk_ref, wv_ref,
                    u_o, wq_o, akt_o, dec_o, sz_o, qp, kp, vp):
    h = pl.program_id(1)
    for src, dst in ((q_ref, qp), (k_ref, kp), (v_ref, vp)):
        dst[0:DN_PAD, :] = jnp.zeros((DN_PAD, LANES), F32)
        dst[DN_PAD:DN_PAD + SEQ, :] = src[...]

    neg_a = -jnp.exp(jnp.full((1, LANES), alog_ref[h], F32))
    dt_bias = dtb_ref[h]
    row = lax.broadcasted_iota(jnp.int32, (DN_TILE, DN_TILE), 0)
    col = lax.broadcasted_iota(jnp.int32, (DN_TILE, DN_TILE), 1)
    same = (row // DN_CHUNK) == (col // DN_CHUNK)
    causal = jnp.logical_and(same, row >= col)
    cumsum_mat = causal.astype(BF16)
    pick = jnp.concatenate([row == h, row == h + DN_HEADS], axis=1).astype(BF16)
    eye = (row == col).astype(F32)
    pair_masks = [
        jnp.logical_and(jnp.logical_and((row // (2 * s)) == (col // (2 * s)), (row // s) % 2 == 1),
                        (col // s) % 2 == 0)
        for s in (2 ** i for i in range(DN_INV_LEVELS))]
    first_chunk_col = col < DN_CHUNK
    gate_lane = col < 2 * DN_HEADS
    scale = DN_HEAD_DIM ** -0.5

    def conv_silu(pad_ref, w_ref, base):
        acc = None
        for j in range(DN_CONV):
            sh = DN_PAD - (DN_CONV - 1) + j
            term = w_ref[j:j + 1, :] * pad_ref[pl.ds(base + sh, DN_TILE), :]
            acc = term if acc is None else acc + term
        return acc * jax.nn.sigmoid(acc)

    def l2n(x):
        return x * lax.rsqrt(jnp.sum(x * x, axis=1, keepdims=True) + NORM_EPS)

    def mm16(a, b):
        return jnp.dot(a.astype(BF16), b.astype(BF16), preferred_element_type=F32)

    def body(it, carry):
        tiles = [it * DN_GROUP + i for i in range(DN_GROUP)]
        bases = [pl.multiple_of(t * DN_TILE, DN_TILE) for t in tiles]
        q = [l2n(conv_silu(qp, wq_ref, b)) * scale for b in bases]
        k = [l2n(conv_silu(kp, wk_ref, b)) for b in bases]
        v = [conv_silu(vp, wv_ref, b) for b in bases]
        ab_rep = [_dot_exact_lhs(jnp.where(gate_lane, ab_ref[pl.ds(b, DN_TILE), :], 0.0), pick) for b in bases]
        beta = [jax.nn.sigmoid(r[:, DN_TILE:]) for r in ab_rep]
        x = [r[:, :DN_TILE] + dt_bias for r in ab_rep]
        g = [neg_a * (jnp.maximum(xi, 0.0) + jnp.log1p(jnp.exp(-jnp.abs(xi)))) for xi in x]
        gc = [_dot_exact_rhs(cumsum_mat, gi) for gi in g]
        decay = [jnp.exp(jnp.where(causal, c - c.T, NEG_INF)) for c in gc]
        kb = [ki * bi for ki, bi in zip(k, beta)]
        k16 = [ki.astype(BF16) for ki in k]
        a_mat = [_dot_nt(kbi.astype(BF16), ki) * di for kbi, ki, di in zip(kb, k16, decay)]
        t_inv = [eye - jnp.where(pair_masks[0], am, 0.0) for am in a_mat]
        for mask in pair_masks[1:]:
            low = [jnp.where(mask, am, 0.0).astype(BF16) for am in a_mat]
            t_inv = [ti - mm16(mm16(ti, lo), ti) for ti, lo in zip(t_inv, low)]
        egc = [jnp.exp(c) for c in gc]
        uw = [mm16(ti, jnp.concatenate([vi * bi, kbi * ei], axis=1))
              for ti, vi, bi, kbi, ei in zip(t_inv, v, beta, kb, egc)]
        at = [(_dot_nt(qi.astype(BF16), ki) * di).astype(BF16) for qi, ki, di in zip(q, k16, decay)]
        qd = [(qi * ei).astype(BF16) for qi, ei in zip(q, egc)]
        c = DN_CHUNK
        for i, t in enumerate(tiles):
            rows = pl.ds(bases[i], DN_TILE)
            u_o[0, rows, :] = uw[i][:, :DN_HEAD_DIM]
            w = uw[i][:, DN_HEAD_DIM:].astype(BF16)
            wq_o[0, pl.ds(pl.multiple_of(t * (2 * DN_TILE), 2 * DN_TILE), 2 * DN_TILE), :] = jnp.concatenate(
                [w[:c], qd[i][:c], w[c:], qd[i][c:]], axis=0)
            gc_first, gc_second = gc[i][c - 1:c, :], gc[i][DN_TILE - 1:DN_TILE, :]
            gc_last = jnp.where(row < c, gc_first, gc_second)
            kt = (k[i] * jnp.exp(gc_last - gc[i])).T
            akt_o[0, pl.ds(pl.multiple_of(t * (3 * DN_TILE), DN_TILE), 3 * DN_TILE), :] = jnp.concatenate(
                [at[i][:c], jnp.where(first_chunk_col, kt, 0.0).astype(BF16),
                 at[i][c:], jnp.where(first_chunk_col, 0.0, kt).astype(BF16)], axis=0)
            dec_o[0, t] = jnp.concatenate(
                [jnp.exp(gc_first), jnp.exp(gc_second), jnp.zeros((DN_PAD - 2, LANES), F32)], axis=0)
            z = z_ref[rows, :]
            sz_o[0, rows, :] = (z * jax.nn.sigmoid(z)).astype(BF16)
        return carry

    lax.fori_loop(0, DN_NTILE // DN_GROUP, body, 0)


def _dn_prep(proj, conv_w, a_log, dt_bias):
    col = lambda c0: pl.BlockSpec((SEQ, LANES), lambda b, h: (b, c0 + h))
    cw = lambda c0: pl.BlockSpec((DN_CONV, LANES), lambda b, h: (0, c0 + h))
    bh_rows = lambda n: pl.BlockSpec((1, n, LANES), lambda b, h: (b * DN_HEADS + h, 0, 0))
    rows_shape = lambda n, dt: jax.ShapeDtypeStruct((DN_BH, n, LANES), dt)
    return pl.pallas_call(
        _dn_prep_kernel,
        grid=(BATCH, DN_HEADS),
        in_specs=[
            pl.BlockSpec(memory_space=pltpu.SMEM), pl.BlockSpec(memory_space=pltpu.SMEM),
            col(DN_QKV_COL0), col(DN_QKV_COL0 + DN_HEADS), col(DN_QKV_COL0 + 2 * DN_HEADS), col(DN_Z_COL0),
            pl.BlockSpec((SEQ, LANES), lambda b, h: (b, DN_AB_COL)),
            cw(0), cw(DN_HEADS), cw(2 * DN_HEADS),
        ],
        out_specs=[
            bh_rows(SEQ), bh_rows(2 * SEQ), bh_rows(3 * SEQ),
            pl.BlockSpec((1, DN_NTILE, DN_PAD, LANES), lambda b, h: (b * DN_HEADS + h, 0, 0, 0)),
            bh_rows(SEQ),
        ],
        out_shape=[
            rows_shape(SEQ, F32), rows_shape(2 * SEQ, BF16), rows_shape(3 * SEQ, BF16),
            jax.ShapeDtypeStruct((DN_BH, DN_NTILE, DN_PAD, LANES), F32),
            rows_shape(SEQ, BF16),
        ],
        scratch_shapes=[pltpu.VMEM((SEQ + DN_PAD, LANES), F32)] * 3,
        compiler_params=pltpu.CompilerParams(
            dimension_semantics=("parallel", "parallel"), vmem_limit_bytes=VMEM_LIMIT),
        name="dn_prep",
    )(a_log, dt_bias, proj, proj, proj, proj, proj, conv_w, conv_w, conv_w)


def _dn_scan_kernel(u_ref, wq_ref, akt_ref, dec_ref, sz_ref, gain_ref, o_ref, s_scr):
    @pl.when(pl.program_id(0) == 0)
    def _():
        s_scr[...] = jnp.zeros_like(s_scr)

    gain = gain_ref[...]
    c = DN_CHUNK
    chains = range(DN_BH)
    state = [s_scr[bh] for bh in chains]
    v_first = None
    for j in range(2):
        rows = slice(j * c, (j + 1) * c)
        s16 = [s.astype(BF16) for s in state]
        ws_qs = [jnp.dot(wq_ref[bh, j * DN_TILE:(j + 1) * DN_TILE, :], s16[bh], preferred_element_type=F32)
                 for bh in chains]
        v_new = [u_ref[bh, rows, :] - ws_qs[bh][:c] for bh in chains]
        if j == 0:
            v_pair = [jnp.concatenate([v, jnp.zeros_like(v)], axis=0).astype(BF16) for v in v_new]
            v_first = v_new
        else:
            v_pair = [jnp.concatenate([v0, v], axis=0).astype(BF16) for v0, v in zip(v_first, v_new)]
        ov_kv = [jnp.dot(akt_ref[bh, j * 3 * c:(j + 1) * 3 * c, :], v_pair[bh], preferred_element_type=F32)
                 for bh in chains]
        state = [state[bh] * dec_ref[bh, 0, j:j + 1, :] + ov_kv[bh][c:] for bh in chains]
        for bh in chains:
            b, h = divmod(bh, DN_HEADS)
            o = ws_qs[bh][c:] + ov_kv[bh][:c]
            o_ref[b, rows, h * DN_HEAD_DIM:(h + 1) * DN_HEAD_DIM] = (
                _rms(o, gain) * sz_ref[bh, rows, :].astype(F32))
    for bh in chains:
        s_scr[bh] = state[bh]


def _dn_scan(u, wq, akt, dec, sz, gain):
    rows = lambda n: pl.BlockSpec((DN_BH, n * DN_TILE, LANES), lambda t: (0, t, 0))
    return pl.pallas_call(
        _dn_scan_kernel,
        grid=(DN_NTILE,),
        in_specs=[rows(1), rows(2), rows(3),
                  pl.BlockSpec((DN_BH, 1, DN_PAD, LANES), lambda t: (0, t, 0, 0)), rows(1),
                  pl.BlockSpec((1, LANES), lambda t: (0, 0))],
        out_specs=pl.BlockSpec((BATCH, DN_TILE, DN_WIDTH), lambda t: (0, t, 0)),
        out_shape=jax.ShapeDtypeStruct((BATCH, SEQ, DN_WIDTH), F32),
        scratch_shapes=[pltpu.VMEM((DN_BH, DN_HEAD_DIM, DN_HEAD_DIM), F32)],
        compiler_params=pltpu.CompilerParams(
            dimension_semantics=("arbitrary",), vmem_limit_bytes=VMEM_LIMIT),
        name="dn_scan",
    )(u, wq, akt, dec, sz, gain)


def _dn_mixer(proj, p, l):
    outs = _dn_prep(proj, p['dn_conv_w'][l], p['dn_a_log'][l], p['dn_dt_bias'][l])
    y = _dn_scan(*outs, p['dn_norm_gain'][l].reshape(1, DN_HEAD_DIM))
    return y.reshape(N_TOK, DN_WIDTH)


def kernel(x, norm_gains, ffn_w_gate, ffn_w_up, ffn_w_down, w_in, w_out, ssm_lambda_re, ssm_lambda_im,
           ssm_b_re, ssm_b_im, ssm_c_re, ssm_c_im, ssm_d, ssm_log_dt, ssm_glu_w, ssm_glu_b, ssm_out_gain,
           dn_conv_w, dn_a_log, dn_dt_bias, dn_norm_gain, attn_out_gain, rel_bias):
    p = dict(ssm_lambda_re=ssm_lambda_re, ssm_lambda_im=ssm_lambda_im, ssm_b_re=ssm_b_re, ssm_b_im=ssm_b_im,
             ssm_c_re=ssm_c_re, ssm_c_im=ssm_c_im, ssm_d=ssm_d, ssm_log_dt=ssm_log_dt, ssm_glu_w=ssm_glu_w,
             ssm_glu_b=ssm_glu_b, ssm_out_gain=ssm_out_gain, dn_conv_w=dn_conv_w, dn_a_log=dn_a_log,
             dn_dt_bias=dn_dt_bias, dn_norm_gain=dn_norm_gain)
    wg, wu, wd = ffn_w_gate, ffn_w_up, ffn_w_down
    w_in_p, w_out_b = w_in, w_out
    bucket_table = _attn_bucket_table()
    gains = norm_gains.reshape(DEPTH, 6, 1, D_MODEL)
    x = x.reshape(N_TOK, D_MODEL)
    for l in range(DEPTH):
        x = _ffn(x, gains[l, 0], gains[l, 1], wg, wu, wd, l, 0)
        proj = _inproj(x, gains[l, 2], w_in_p, l)
        y_ssm = _s5_mixer(proj, p, l)
        y_dn = _dn_mixer(proj, p, l)
        o_at = _attn(proj, rel_bias, bucket_table)
        x = _outproj(x, y_ssm, y_dn, o_at, attn_out_gain[l].reshape(1, ATTN_WIDTH), gains[l, 3], w_out_b, l)
        x = _ffn(x, gains[l, 4], gains[l, 5], wg, wu, wd, l, 1)
    return x.reshape(BATCH, SEQ, D_MODEL)
```

```python
import functools
import math

import jax
import jax.numpy as jnp
import numpy as np
from jax import lax
from jax.experimental import pallas as pl
from jax.experimental.pallas import tpu as pltpu

D_MODEL = 2048
BATCH = 2
SEQ = 4096
DEPTH = 4
SSM_GROUPS = 32
SSM_CH = 16
SSM_STATE = 64
SSM_WIDTH = SSM_GROUPS * SSM_CH
DN_HEADS = 6
DN_HEAD_DIM = 128
DN_WIDTH = DN_HEADS * DN_HEAD_DIM
DN_CONV = 4
DN_CHUNK = 64
ATTN_HEADS = 6
ATTN_HEAD_DIM = 128
ATTN_WIDTH = ATTN_HEADS * ATTN_HEAD_DIM
DILATED_PAIRS = ((128, 1), (512, 4), (2048, 16))
ATTN_BLOCK = 128
N_BUCKETS = 32
REL_MAX_DIST = 2048
D_MIX = SSM_WIDTH + DN_WIDTH + ATTN_WIDTH
IN_SPLITS = (SSM_WIDTH, ATTN_WIDTH, ATTN_WIDTH, ATTN_WIDTH, 3 * DN_WIDTH, DN_WIDTH, DN_HEADS, DN_HEADS)
N_IN_COLS = sum(IN_SPLITS)
D_FF = 5632
NORM_EPS = 1e-6
NEG_INF = -1e30

LANES = 128
N_IN_PAD = 6144
V7X_VMEM_BYTES = 64 * 1024 * 1024
VMEM_LIMIT = V7X_VMEM_BYTES - 4 * 1024 * 1024
N_TOK = BATCH * SEQ

BF16 = jnp.bfloat16
F32 = jnp.float32


def _rms(x, gain):
    return x * lax.rsqrt(jnp.mean(x * x, axis=-1, keepdims=True) + NORM_EPS) * gain


FFN_TM = 1024
FFN_TF = 256


def _ffn_kernel(x_ref, gpre_ref, gpost_ref, wg_ref, wu_ref, wd_ref, o_ref, h_scr):
    f = pl.program_id(1)
    last = pl.num_programs(1) - 1

    def down_proj(h):
        gate = jnp.dot(h, wg_ref[...].astype(BF16), preferred_element_type=F32)
        up = jnp.dot(h, wu_ref[...].astype(BF16), preferred_element_type=F32)
        act = (gate * jax.nn.sigmoid(gate) * up).astype(BF16)
        return jnp.dot(act, wd_ref[...].astype(BF16), preferred_element_type=F32)

    @pl.when(f == 0)
    def _():
        h = _rms(x_ref[...], gpre_ref[...]).astype(BF16)
        h_scr[...] = h
        o_ref[...] = down_proj(h)

    @pl.when(jnp.logical_and(f > 0, f < last))
    def _():
        o_ref[...] += down_proj(h_scr[...])

    @pl.when(f == last)
    def _():
        y = o_ref[...] + down_proj(h_scr[...])
        o_ref[...] = x_ref[...] + _rms(y, 0.5 * gpost_ref[...])


def _ffn(x, g_pre, g_post, wg, wu, wd, layer, half):
    return pl.pallas_call(
        _ffn_kernel,
        grid=(N_TOK // FFN_TM, D_FF // FFN_TF),
        in_specs=[
            pl.BlockSpec((FFN_TM, D_MODEL), lambda i, f: (i, 0)),
            pl.BlockSpec((1, D_MODEL), lambda i, f: (0, 0)),
            pl.BlockSpec((1, D_MODEL), lambda i, f: (0, 0)),
            pl.BlockSpec((None, None, D_MODEL, FFN_TF), lambda i, f: (layer, half, 0, f)),
            pl.BlockSpec((None, None, D_MODEL, FFN_TF), lambda i, f: (layer, half, 0, f)),
            pl.BlockSpec((None, None, FFN_TF, D_MODEL), lambda i, f: (layer, half, f, 0)),
        ],
        out_specs=pl.BlockSpec((FFN_TM, D_MODEL), lambda i, f: (i, 0)),
        out_shape=jax.ShapeDtypeStruct((N_TOK, D_MODEL), F32),
        scratch_shapes=[pltpu.VMEM((FFN_TM, D_MODEL), BF16)],
        compiler_params=pltpu.CompilerParams(
            dimension_semantics=("parallel", "arbitrary"), vmem_limit_bytes=VMEM_LIMIT),
        name="ffn",
    )(x, g_pre, g_post, wg, wu, wd)


INP_TM = 1024
INP_TN = 1536


def _inproj_kernel(x_ref, g_ref, w_ref, o_ref, h_scr):
    @pl.when(pl.program_id(1) == 0)
    def _():
        h = _rms(x_ref[...], g_ref[...]).astype(BF16)
        h_scr[...] = h
        o_ref[...] = jnp.dot(h, w_ref[...], preferred_element_type=F32)

    @pl.when(pl.program_id(1) > 0)
    def _():
        o_ref[...] = jnp.dot(h_scr[...], w_ref[...], preferred_element_type=F32)


def _inproj(x, g, w, layer):
    return pl.pallas_call(
        _inproj_kernel,
        grid=(N_TOK // INP_TM, N_IN_PAD // INP_TN),
        in_specs=[
            pl.BlockSpec((INP_TM, D_MODEL), lambda i, n: (i, 0)),
            pl.BlockSpec((1, D_MODEL), lambda i, n: (0, 0)),
            pl.BlockSpec((None, D_MODEL, INP_TN), lambda i, n: (layer, 0, n)),
        ],
        out_specs=pl.BlockSpec((INP_TM, INP_TN), lambda i, n: (i, n)),
        out_shape=jax.ShapeDtypeStruct((N_TOK, N_IN_PAD), F32),
        scratch_shapes=[pltpu.VMEM((INP_TM, D_MODEL), BF16)],
        compiler_params=pltpu.CompilerParams(
            dimension_semantics=("parallel", "arbitrary"), vmem_limit_bytes=VMEM_LIMIT),
        name="inproj",
    )(x, g, w)


OUT_TM = 512
OUT_SUB = 256


def _outproj_kernel(x_ref, ys_ref, yd_ref, oa_ref, ga_ref, gpost_ref, w_ref, o_ref, w16):
    @pl.when(pl.program_id(0) == 0)
    def _():
        w16[...] = w_ref[...].astype(BF16)

    halves = [slice(i * OUT_SUB, (i + 1) * OUT_SUB) for i in range(OUT_TM // OUT_SUB)]
    ya = [_rms(oa_ref[r, :], ga_ref[...]).astype(BF16) for r in halves]
    mix = [jnp.dot(ys_ref[r, :].astype(BF16), w16[0:SSM_WIDTH, :], preferred_element_type=F32) for r in halves]
    mix = [m + jnp.dot(yd_ref[r, :].astype(BF16), w16[SSM_WIDTH:SSM_WIDTH + DN_WIDTH, :],
                       preferred_element_type=F32) for m, r in zip(mix, halves)]
    mix = [m + jnp.dot(a, w16[SSM_WIDTH + DN_WIDTH:D_MIX, :], preferred_element_type=F32)
           for m, a in zip(mix, ya)]
    for r, m in zip(halves, mix):
        o_ref[r, :] = x_ref[r, :] + _rms(m, gpost_ref[...])


def _outproj(x, y_ssm, y_dn, o_at, g_attn, g_post, w, layer):
    row = lambda i: (i, 0)
    fixed = lambda i: (0, 0)
    return pl.pallas_call(
        _outproj_kernel,
        grid=(N_TOK // OUT_TM,),
        in_specs=[
            pl.BlockSpec((OUT_TM, D_MODEL), row),
            pl.BlockSpec((OUT_TM, SSM_WIDTH), row),
            pl.BlockSpec((OUT_TM, DN_WIDTH), row),
            pl.BlockSpec((OUT_TM, ATTN_WIDTH), row),
            pl.BlockSpec((1, ATTN_WIDTH), fixed),
            pl.BlockSpec((1, D_MODEL), fixed),
            pl.BlockSpec((None, D_MIX, D_MODEL), lambda i: (layer, 0, 0), pipeline_mode=pl.Buffered(1)),
        ],
        out_specs=pl.BlockSpec((OUT_TM, D_MODEL), row),
        out_shape=jax.ShapeDtypeStruct((N_TOK, D_MODEL), F32),
        scratch_shapes=[pltpu.VMEM((D_MIX, D_MODEL), BF16)],
        compiler_params=pltpu.CompilerParams(
            dimension_semantics=("arbitrary",), vmem_limit_bytes=VMEM_LIMIT),
        name="outproj",
    )(x, y_ssm, y_dn, o_at, g_attn, g_post, w)


S5_SEG = 8
S5_SEGLEN = SEQ // S5_SEG
S5_KB = 64
S5_ROWS = S5_KB * S5_SEG
S5_PITCH = S5_KB + 8
S5_PROWS = S5_PITCH * S5_SEG
S5_NS = SSM_GROUPS * SSM_STATE
S5_SLABS = S5_NS // LANES
S5_Q = 4
S5_QS = S5_NS // S5_Q
S5_QL = S5_QS // LANES
S5_QC = SSM_WIDTH // S5_Q
S5_LOG2_SEGLEN = 9
assert 1 << S5_LOG2_SEGLEN == S5_SEGLEN


def _s5_kernel(u_ref, wb_ref, are_ref, aim_ref, cre_ref, cim_ref, d_ref, gw_ref, gb_ref, go_ref, o_ref,
               bu_scr, st_scr, carry_scr, y_scr):
    p = pl.program_id(1)
    j = pl.program_id(2)
    u = u_ref[0].reshape(S5_ROWS, SSM_WIDTH)
    ub = u.astype(BF16)

    @pl.when(jnp.logical_and(p == 0, j == 0))
    def _():
        bu_scr[...] = jnp.zeros_like(bu_scr)
        st_scr[...] = jnp.zeros_like(st_scr)

    @pl.when(jnp.logical_and(p == 1, j == 0))
    def _():
        st_scr[...] = carry_scr[...]

    for q in range(S5_Q):
        r = jnp.dot(ub[:, q * S5_QC:(q + 1) * S5_QC], wb_ref[q], preferred_element_type=F32)
        for half in range(2):
            for c in range(S5_QL):
                slab = half * S5_SLABS + q * S5_QL + c
                lanes = slice((half * S5_QL + c) * LANES, (half * S5_QL + c + 1) * LANES)
                for i in range(S5_SEG):
                    bu_scr[slab, i * S5_PITCH:i * S5_PITCH + S5_KB, :] = r[i * S5_KB:(i + 1) * S5_KB, lanes]

    def scan(store):
        for q in range(S5_Q):
            slabs = [q * S5_QL + c for c in range(S5_QL)]
            lanes = [slice(s * LANES, (s + 1) * LANES) for s in slabs]
            ilanes = [slice(S5_NS + s * LANES, S5_NS + (s + 1) * LANES) for s in slabs]
            ar = [jnp.broadcast_to(are_ref[:, ln], (S5_SEG, LANES)) for ln in lanes]
            ai = [jnp.broadcast_to(aim_ref[:, ln], (S5_SEG, LANES)) for ln in lanes]

            def body(k, carry):
                sr, si = carry
                rows = pl.ds(k, S5_SEG, stride=S5_PITCH)
                nr = tuple(ar[c] * sr[c] - ai[c] * si[c] + bu_scr[slabs[c], rows, :] for c in range(S5_QL))
                ni = tuple(ar[c] * si[c] + ai[c] * sr[c] + bu_scr[slabs[c] + S5_SLABS, rows, :]
                           for c in range(S5_QL))
                if store:
                    for c in range(S5_QL):
                        bu_scr[slabs[c], rows, :] = nr[c]
                        bu_scr[slabs[c] + S5_SLABS, rows, :] = ni[c]
                return nr, ni

            init = (tuple(st_scr[:, ln] for ln in lanes), tuple(st_scr[:, ln] for ln in ilanes))
            sr, si = lax.fori_loop(0, S5_KB, body, init, unroll=8)
            for c in range(S5_QL):
                st_scr[:, lanes[c]] = sr[c]
                st_scr[:, ilanes[c]] = si[c]

    @pl.when(p == 0)
    def _():
        scan(False)

        @pl.when(j == pl.num_programs(2) - 1)
        def _():
            lr, li = are_ref[...], aim_ref[...]
            for _ in range(S5_LOG2_SEGLEN):
                lr, li = lr * lr - li * li, 2.0 * lr * li
            cr = jnp.zeros((1, S5_NS), F32)
            ci = jnp.zeros((1, S5_NS), F32)
            carry_scr[0:1, :] = jnp.zeros((1, 2 * S5_NS), F32)
            for i in range(1, S5_SEG):
                er = st_scr[i - 1:i, 0:S5_NS]
                ei = st_scr[i - 1:i, S5_NS:2 * S5_NS]
                cr, ci = er + lr * cr - li * ci, ei + lr * ci + li * cr
                carry_scr[i:i + 1, 0:S5_NS] = cr
                carry_scr[i:i + 1, S5_NS:2 * S5_NS] = ci

    @pl.when(p == 1)
    def _():
        scan(True)
        for q in range(S5_Q):
            sre = jnp.concatenate([bu_scr[q * S5_QL + c] for c in range(S5_QL)], axis=1).astype(BF16)
            sim = jnp.concatenate([bu_scr[S5_SLABS + q * S5_QL + c] for c in range(S5_QL)], axis=1).astype(BF16)
            y_scr[:, q * S5_QC:(q + 1) * S5_QC] = (
                jnp.dot(sre, cre_ref[q], preferred_element_type=F32)
                + jnp.dot(sim, cim_ref[q], preferred_element_type=F32))
        cs = jnp.concatenate([y_scr[i * S5_PITCH:i * S5_PITCH + S5_KB, :] for i in range(S5_SEG)], axis=0)
        y = jax.nn.gelu(cs + d_ref[...] * u)
        z = jnp.dot(y.astype(BF16), gw_ref[...], preferred_element_type=F32) + gb_ref[...]
        o_ref[0] = _rms(y * jax.nn.sigmoid(z), go_ref[...]).reshape(S5_SEG, S5_KB, SSM_WIDTH)


def _s5_params(lam_re, lam_im, b_re, b_im, c_re, c_im, log_dt):
    lam = lax.complex(lam_re, lam_im)
    lam_bar = jnp.exp(lam * jnp.exp(log_dt)[:, None])
    b_bar = ((lam_bar - 1.0) / lam)[..., None] * lax.complex(b_re, b_im)
    gq = SSM_GROUPS // S5_Q
    eye = jnp.eye(gq, dtype=F32)

    def in_map(t):
        t = t.reshape(S5_Q, gq, SSM_STATE, SSM_CH)
        return jnp.einsum('qgpc,gh->qgchp', t, eye).reshape(S5_Q, S5_QC, S5_QS)

    def out_map(t):
        t = t.reshape(S5_Q, gq, SSM_CH, SSM_STATE)
        return jnp.einsum('qgcp,gh->qgphc', t, eye).reshape(S5_Q, S5_QS, S5_QC)

    wb = jnp.concatenate([in_map(b_bar.real), in_map(b_bar.imag)], axis=-1).astype(BF16)
    return (wb, lam_bar.real.reshape(1, S5_NS), lam_bar.imag.reshape(1, S5_NS),
            out_map(c_re).astype(BF16), out_map(-c_im).astype(BF16))


def _s5(proj4, wb, a_re, a_im, cre, cim, d_skip, glu_w, glu_b, out_gain):
    nblk = S5_SEGLEN // S5_KB
    fix2 = lambda b, p, j: (0, 0)
    fix3 = lambda b, p, j: (0, 0, 0)
    return pl.pallas_call(
        _s5_kernel,
        grid=(BATCH, 2, nblk),
        in_specs=[
            pl.BlockSpec((1, S5_SEG, S5_KB, SSM_WIDTH), lambda b, p, j: (b, 0, j, 0)),
            pl.BlockSpec((S5_Q, S5_QC, 2 * S5_QS), fix3),
            pl.BlockSpec((1, S5_NS), fix2),
            pl.BlockSpec((1, S5_NS), fix2),
            pl.BlockSpec((S5_Q, S5_QS, S5_QC), fix3),
            pl.BlockSpec((S5_Q, S5_QS, S5_QC), fix3),
            pl.BlockSpec((1, SSM_WIDTH), fix2),
            pl.BlockSpec((SSM_WIDTH, SSM_WIDTH), fix2),
            pl.BlockSpec((1, SSM_WIDTH), fix2),
            pl.BlockSpec((1, SSM_WIDTH), fix2),
        ],
        out_specs=pl.BlockSpec((1, S5_SEG, S5_KB, SSM_WIDTH), lambda b, p, j: (b, 0, j * p, 0)),
        out_shape=jax.ShapeDtypeStruct((BATCH, S5_SEG, S5_SEGLEN, SSM_WIDTH), F32),
        scratch_shapes=[
            pltpu.VMEM((2 * S5_SLABS, S5_PROWS, LANES), F32),
            pltpu.VMEM((S5_SEG, 2 * S5_NS), F32),
            pltpu.VMEM((S5_SEG, 2 * S5_NS), F32),
            pltpu.VMEM((S5_PROWS, SSM_WIDTH), F32),
        ],
        compiler_params=pltpu.CompilerParams(
            dimension_semantics=("parallel", "arbitrary", "arbitrary"), vmem_limit_bytes=VMEM_LIMIT),
        name="s5",
    )(proj4, wb, a_re, a_im, cre, cim, d_skip, glu_w, glu_b, out_gain)


def _s5_mixer(proj, p, l):
    wb, a_re, a_im, cre, cim = _s5_params(p['ssm_lambda_re'][l], p['ssm_lambda_im'][l], p['ssm_b_re'][l],
                                          p['ssm_b_im'][l], p['ssm_c_re'][l], p['ssm_c_im'][l],
                                          p['ssm_log_dt'][l])
    y = _s5(proj.reshape(BATCH, S5_SEG, S5_SEGLEN, N_IN_PAD), wb, a_re, a_im, cre, cim,
            p['ssm_d'][l].reshape(1, SSM_WIDTH), p['ssm_glu_w'][l].astype(BF16),
            p['ssm_glu_b'][l].reshape(1, SSM_WIDTH), p['ssm_out_gain'][l].reshape(1, SSM_WIDTH))
    return y.reshape(N_TOK, SSM_WIDTH)


AT_BLK = ATTN_BLOCK
AT_NBRANCH = len(DILATED_PAIRS)
AT_BLOCKS = SEQ // AT_BLK
AT_GROUP = 8
AT_COMBINE_ROWS = 256
assert all(w // d == AT_BLK for w, d in DILATED_PAIRS)
assert all((AT_BLOCKS // d) % AT_GROUP == 0 or AT_GROUP % (AT_BLOCKS // d) == 0 for _, d in DILATED_PAIRS)
AT_Q_COL0 = SSM_WIDTH // LANES
AT_K_COL0 = AT_Q_COL0 + ATTN_HEADS
AT_V_COL0 = AT_K_COL0 + ATTN_HEADS


def _attn_kernel(rb_ref, bkt_ref, q_ref, k_ref, v_ref, o_ref, bias_scr, o_scr, lse_scr):
    h = pl.program_id(0)

    @pl.when(pl.program_id(1) == 0)
    def _():
        for g in range(AT_NBRANCH):
            bkt = bkt_ref[g]
            bias = jnp.zeros((AT_BLK, 2 * AT_BLK), F32)
            for b in range(N_BUCKETS):
                bias = jnp.where(bkt == b, rb_ref[b, h], bias)
            bias_scr[g] = jnp.where(bkt < 0, NEG_INF, bias)

    lane = lax.broadcasted_iota(jnp.int32, (AT_BLK, 2 * AT_BLK), 1)
    scale = ATTN_HEAD_DIM ** -0.5
    for g, (_, dil) in enumerate(DILATED_PAIRS):
        nb = AT_BLOCKS // dil

        def body(it, carry, g=g, dil=dil, nb=nb):
            def block_rows(start):
                if dil == 1:
                    return pl.ds(pl.multiple_of(start, AT_BLK), AT_BLK)
                return pl.ds(start, AT_BLK, stride=dil)

            group = range(AT_GROUP)
            ts = [it * AT_GROUP + i for i in group]
            if nb <= AT_GROUP:
                ns = [i % nb for i in group]
            else:
                ns = [None if i == 0 else i for i in group]
            n0 = ts[0] % nb
            rows = [block_rows(t // nb + (t % nb) * (AT_BLK * dil)) for t in ts]
            q = [(q_ref[r, :] * scale).astype(BF16) for r in rows]
            k_cur = [k_ref[r, :].astype(BF16) for r in rows]
            v_cur = [v_ref[r, :].astype(BF16) for r in rows]
            if ns[0] is None:
                prow = block_rows(ts[0] // nb + jnp.maximum(n0 - 1, 0) * (AT_BLK * dil))
                k_lead, v_lead = k_ref[prow, :].astype(BF16), v_ref[prow, :].astype(BF16)
            s, vc = [], []
            for i in group:
                if ns[i] == 0:
                    s.append(_dot_nt(q[i], k_cur[i]) + bias_scr[g, :, AT_BLK:])
                    vc.append(v_cur[i])
                    continue
                k_prev, v_prev = (k_lead, v_lead) if ns[i] is None else (k_cur[i - 1], v_cur[i - 1])
                si = _dot_nt(q[i], jnp.concatenate([k_prev, k_cur[i]], axis=0)) + bias_scr[g]
                if ns[i] is None:
                    si = jnp.where(jnp.logical_or(n0 > 0, lane >= AT_BLK), si, NEG_INF)
                s.append(si)
                vc.append(jnp.concatenate([v_prev, v_cur[i]], axis=0))
            m = [jnp.max(si, axis=1, keepdims=True) for si in s]
            p = [jnp.exp(si - mi) for si, mi in zip(s, m)]
            l = [jnp.sum(pi, axis=1, keepdims=True) for pi in p]
            o = [jnp.dot(pi.astype(BF16), vi, preferred_element_type=F32) / li for pi, vi, li in zip(p, vc, l)]
            for r, oi, mi, li in zip(rows, o, m, l):
                o_scr[g, r, :] = oi
                lse_scr[g, r, :] = jnp.broadcast_to(mi + jnp.log(li), (AT_BLK, LANES))
            return carry

        lax.fori_loop(0, AT_BLOCKS // AT_GROUP, body, 0)

    def combine(i, carry):
        rows = pl.ds(pl.multiple_of(i * AT_COMBINE_ROWS, AT_COMBINE_ROWS), AT_COMBINE_ROWS)
        l0, l1, l2 = lse_scr[0, rows, :], lse_scr[1, rows, :], lse_scr[2, rows, :]
        m = jnp.maximum(jnp.maximum(l0, l1), l2)
        w0, w1, w2 = jnp.exp(l0 - m), jnp.exp(l1 - m), jnp.exp(l2 - m)
        num = w0 * o_scr[0, rows, :] + w1 * o_scr[1, rows, :] + w2 * o_scr[2, rows, :]
        o_ref[rows, :] = num / (w0 + w1 + w2)
        return carry

    lax.fori_loop(0, SEQ // AT_COMBINE_ROWS, combine, 0)


def _attn_bucket_table():
    qi = jnp.arange(AT_BLK)[:, None]
    kj = jnp.arange(2 * AT_BLK)[None, :]
    rel = AT_BLK + qi - kj
    max_exact = N_BUCKETS // 2
    tables = []
    for window, dil in DILATED_PAIRS:
        dist = jnp.maximum(rel, 0) * dil
        d = jnp.maximum(dist, 1).astype(F32)
        large = max_exact + jnp.log(d / max_exact) / math.log(REL_MAX_DIST / max_exact) * (N_BUCKETS - max_exact)
        large = jnp.minimum(large.astype(jnp.int32), N_BUCKETS - 1)
        bucket = jnp.where(dist < max_exact, dist, large)
        tables.append(jnp.where((rel >= 0) & (rel <= window // dil), bucket, -1))
    return jnp.stack(tables).astype(jnp.int32)


def _attn(proj, rel_bias, bucket_table):
    col = lambda c0: pl.BlockSpec((SEQ, LANES), lambda h, b: (b, c0 + h))
    return pl.pallas_call(
        _attn_kernel,
        grid=(ATTN_HEADS, BATCH),
        in_specs=[
            pl.BlockSpec(memory_space=pltpu.SMEM),
            pl.BlockSpec((AT_NBRANCH, AT_BLK, 2 * AT_BLK), lambda h, b: (0, 0, 0)),
            col(AT_Q_COL0), col(AT_K_COL0), col(AT_V_COL0),
        ],
        out_specs=pl.BlockSpec((SEQ, LANES), lambda h, b: (b, h)),
        out_shape=jax.ShapeDtypeStruct((N_TOK, ATTN_WIDTH), F32),
        scratch_shapes=[
            pltpu.VMEM((AT_NBRANCH, AT_BLK, 2 * AT_BLK), F32),
            pltpu.VMEM((AT_NBRANCH, SEQ, LANES), F32),
            pltpu.VMEM((AT_NBRANCH, SEQ, LANES), F32),
        ],
        compiler_params=pltpu.CompilerParams(
            dimension_semantics=("parallel", "arbitrary"), vmem_limit_bytes=VMEM_LIMIT),
        name="dilated_attn",
    )(rel_bias, bucket_table, proj, proj, proj)


DN_TILE = 2 * DN_CHUNK
DN_NTILE = SEQ // DN_TILE
DN_BH = BATCH * DN_HEADS
DN_PAD = 8
DN_GROUP = 8
DN_INV_LEVELS = 6
assert 2 ** DN_INV_LEVELS == DN_CHUNK
DN_QKV_COL0 = (SSM_WIDTH + 3 * ATTN_WIDTH) // LANES
DN_Z_COL0 = DN_QKV_COL0 + 3 * DN_HEADS
DN_AB_COL = DN_Z_COL0 + DN_HEADS
HIGHEST = lax.Precision.HIGHEST


def _dot_nt(a, b):
    return lax.dot_general(a, b, (((1,), (1,)), ((), ())), preferred_element_type=F32)


def _split3(x):
    hi = x.astype(BF16)
    r = x - hi.astype(F32)
    mid = r.astype(BF16)
    return hi, mid, (r - mid.astype(F32)).astype(BF16)


def _dot_exact_lhs(x, m16):
    return sum(jnp.dot(piece, m16, preferred_element_type=F32) for piece in _split3(x))


def _dot_exact_rhs(m16, x):
    return sum(jnp.dot(m16, piece, preferred_element_type=F32) for piece in _split3(x))


def _dn_prep_kernel(alog_ref, dtb_ref, q_ref, k_ref, v_ref, z_ref, ab_ref, wq_ref, wk_ref, wv_ref,
                    u_o, wq_o, akt_o, dec_o, sz_o, qp, kp, vp):
    h = pl.program_id(1)
    for src, dst in ((q_ref, qp), (k_ref, kp), (v_ref, vp)):
        dst[0:DN_PAD, :] = jnp.zeros((DN_PAD, LANES), F32)
        dst[DN_PAD:DN_PAD + SEQ, :] = src[...]

    neg_a = -jnp.exp(jnp.full((1, LANES), alog_ref[h], F32))
    dt_bias = dtb_ref[h]
    row = lax.broadcasted_iota(jnp.int32, (DN_TILE, DN_TILE), 0)
    col = lax.broadcasted_iota(jnp.int32, (DN_TILE, DN_TILE), 1)
    same = (row // DN_CHUNK) == (col // DN_CHUNK)
    causal = jnp.logical_and(same, row >= col)
    cumsum_mat = causal.astype(BF16)
    pick = jnp.concatenate([row == h, row == h + DN_HEADS], axis=1).astype(BF16)
    eye = (row == col).astype(F32)
    pair_masks = [
        jnp.logical_and(jnp.logical_and((row // (2 * s)) == (col // (2 * s)), (row // s) % 2 == 1),
                        (col // s) % 2 == 0)
        for s in (2 ** i for i in range(DN_INV_LEVELS))]
    first_chunk_col = col < DN_CHUNK
    scale = DN_HEAD_DIM ** -0.5

    def conv_silu(pad_ref, w_ref, base):
        acc = None
        for j in range(DN_CONV):
            sh = DN_PAD - (DN_CONV - 1) + j
            term = w_ref[j:j + 1, :] * pad_ref[pl.ds(base + sh, DN_TILE), :]
            acc = term if acc is None else acc + term
        return acc * jax.nn.sigmoid(acc)

    def l2n(x):
        return x * lax.rsqrt(jnp.sum(x * x, axis=1, keepdims=True) + NORM_EPS)

    def mm16(a, b):
        return jnp.dot(a.astype(BF16), b.astype(BF16), preferred_element_type=F32)

    def body(it, carry):
        tiles = [it * DN_GROUP + i for i in range(DN_GROUP)]
        bases = [pl.multiple_of(t * DN_TILE, DN_TILE) for t in tiles]
        q = [l2n(conv_silu(qp, wq_ref, b)) * scale for b in bases]
        k = [l2n(conv_silu(kp, wk_ref, b)) for b in bases]
        v = [conv_silu(vp, wv_ref, b) for b in bases]
        ab_rep = [_dot_exact_lhs(ab_ref[pl.ds(b, DN_TILE), :], pick) for b in bases]
        beta = [jax.nn.sigmoid(r[:, DN_TILE:]) for r in ab_rep]
        x = [r[:, :DN_TILE] + dt_bias for r in ab_rep]
        g = [neg_a * (jnp.maximum(xi, 0.0) + jnp.log1p(jnp.exp(-jnp.abs(xi)))) for xi in x]
        gc = [_dot_exact_rhs(cumsum_mat, gi) for gi in g]
        decay = [jnp.exp(jnp.where(causal, c - c.T, NEG_INF)) for c in gc]
        kb = [ki * bi for ki, bi in zip(k, beta)]
        k16 = [ki.astype(BF16) for ki in k]
        a_mat = [_dot_nt(kbi.astype(BF16), ki) * di for kbi, ki, di in zip(kb, k16, decay)]
        t_inv = [eye - jnp.where(pair_masks[0], am, 0.0) for am in a_mat]
        for mask in pair_masks[1:]:
            low = [jnp.where(mask, am, 0.0).astype(BF16) for am in a_mat]
            t_inv = [ti - mm16(mm16(ti, lo), ti) for ti, lo in zip(t_inv, low)]
        egc = [jnp.exp(c) for c in gc]
        uw = [mm16(ti, jnp.concatenate([vi * bi, kbi * ei], axis=1))
              for ti, vi, bi, kbi, ei in zip(t_inv, v, beta, kb, egc)]
        at = [(_dot_nt(qi.astype(BF16), ki) * di).astype(BF16) for qi, ki, di in zip(q, k16, decay)]
        qd = [(qi * ei).astype(BF16) for qi, ei in zip(q, egc)]
        c = DN_CHUNK
        for i, t in enumerate(tiles):
            rows = pl.ds(bases[i], DN_TILE)
            u_o[0, rows, :] = uw[i][:, :DN_HEAD_DIM]
            w = uw[i][:, DN_HEAD_DIM:].astype(BF16)
            wq_o[0, pl.ds(pl.multiple_of(t * (2 * DN_TILE), 2 * DN_TILE), 2 * DN_TILE), :] = jnp.concatenate(
                [w[:c], qd[i][:c], w[c:], qd[i][c:]], axis=0)
            gc_first, gc_second = gc[i][c - 1:c, :], gc[i][DN_TILE - 1:DN_TILE, :]
            gc_last = jnp.where(row < c, gc_first, gc_second)
            kt = (k[i] * jnp.exp(gc_last - gc[i])).T
            akt_o[0, pl.ds(pl.multiple_of(t * (3 * DN_TILE), DN_TILE), 3 * DN_TILE), :] = jnp.concatenate(
                [at[i][:c], jnp.where(first_chunk_col, kt, 0.0).astype(BF16),
                 at[i][c:], jnp.where(first_chunk_col, 0.0, kt).astype(BF16)], axis=0)
            dec_o[0, t] = jnp.concatenate(
                [jnp.exp(gc_first), jnp.exp(gc_second), jnp.zeros((DN_PAD - 2, LANES), F32)], axis=0)
            z = z_ref[rows, :]
            sz_o[0, rows, :] = (z * jax.nn.sigmoid(z)).astype(BF16)
        return carry

    lax.fori_loop(0, DN_NTILE // DN_GROUP, body, 0)


def _dn_prep(proj, conv_w, a_log, dt_bias):
    col = lambda c0: pl.BlockSpec((SEQ, LANES), lambda b, h: (b, c0 + h))
    cw = lambda c0: pl.BlockSpec((DN_CONV, LANES), lambda b, h: (0, c0 + h))
    bh_rows = lambda n: pl.BlockSpec((1, n, LANES), lambda b, h: (b * DN_HEADS + h, 0, 0))
    rows_shape = lambda n, dt: jax.ShapeDtypeStruct((DN_BH, n, LANES), dt)
    return pl.pallas_call(
        _dn_prep_kernel,
        grid=(BATCH, DN_HEADS),
        in_specs=[
            pl.BlockSpec(memory_space=pltpu.SMEM), pl.BlockSpec(memory_space=pltpu.SMEM),
            col(DN_QKV_COL0), col(DN_QKV_COL0 + DN_HEADS), col(DN_QKV_COL0 + 2 * DN_HEADS), col(DN_Z_COL0),
            pl.BlockSpec((SEQ, LANES), lambda b, h: (b, DN_AB_COL)),
            cw(0), cw(DN_HEADS), cw(2 * DN_HEADS),
        ],
        out_specs=[
            bh_rows(SEQ), bh_rows(2 * SEQ), bh_rows(3 * SEQ),
            pl.BlockSpec((1, DN_NTILE, DN_PAD, LANES), lambda b, h: (b * DN_HEADS + h, 0, 0, 0)),
            bh_rows(SEQ),
        ],
        out_shape=[
            rows_shape(SEQ, F32), rows_shape(2 * SEQ, BF16), rows_shape(3 * SEQ, BF16),
            jax.ShapeDtypeStruct((DN_BH, DN_NTILE, DN_PAD, LANES), F32),
            rows_shape(SEQ, BF16),
        ],
        scratch_shapes=[pltpu.VMEM((SEQ + DN_PAD, LANES), F32)] * 3,
        compiler_params=pltpu.CompilerParams(
            dimension_semantics=("parallel", "parallel"), vmem_limit_bytes=VMEM_LIMIT),
        name="dn_prep",
    )(a_log, dt_bias, proj, proj, proj, proj, proj, conv_w, conv_w, conv_w)


def _dn_scan_kernel(u_ref, wq_ref, akt_ref, dec_ref, sz_ref, gain_ref, o_ref, s_scr):
    @pl.when(pl.program_id(0) == 0)
    def _():
        s_scr[...] = jnp.zeros_like(s_scr)

    gain = gain_ref[...]
    c = DN_CHUNK
    chains = range(DN_BH)
    state = [s_scr[bh] for bh in chains]
    v_first = None
    for j in range(2):
        rows = slice(j * c, (j + 1) * c)
        s16 = [s.astype(BF16) for s in state]
        ws_qs = [jnp.dot(wq_ref[bh, j * DN_TILE:(j + 1) * DN_TILE, :], s16[bh], preferred_element_type=F32)
                 for bh in chains]
        v_new = [u_ref[bh, rows, :] - ws_qs[bh][:c] for bh in chains]
        if j == 0:
            v_pair = [jnp.concatenate([v, jnp.zeros_like(v)], axis=0).astype(BF16) for v in v_new]
            v_first = v_new
        else:
            v_pair = [jnp.concatenate([v0, v], axis=0).astype(BF16) for v0, v in zip(v_first, v_new)]
        ov_kv = [jnp.dot(akt_ref[bh, j * 3 * c:(j + 1) * 3 * c, :], v_pair[bh], preferred_element_type=F32)
                 for bh in chains]
        state = [state[bh] * dec_ref[bh, 0, j:j + 1, :] + ov_kv[bh][c:] for bh in chains]
        for bh in chains:
            b, h = divmod(bh, DN_HEADS)
            o = ws_qs[bh][c:] + ov_kv[bh][:c]
            o_ref[b, rows, h * DN_HEAD_DIM:(h + 1) * DN_HEAD_DIM] = (
                _rms(o, gain) * sz_ref[bh, rows, :].astype(F32))
    for bh in chains:
        s_scr[bh] = state[bh]


def _dn_scan(u, wq, akt, dec, sz, gain):
    rows = lambda n: pl.BlockSpec((DN_BH, n * DN_TILE, LANES), lambda t: (0, t, 0))
    return pl.pallas_call(
        _dn_scan_kernel,
        grid=(DN_NTILE,),
        in_specs=[rows(1), rows(2), rows(3),
                  pl.BlockSpec((DN_BH, 1, DN_PAD, LANES), lambda t: (0, t, 0, 0)), rows(1),
                  pl.BlockSpec((1, LANES), lambda t: (0, 0))],
        out_specs=pl.BlockSpec((BATCH, DN_TILE, DN_WIDTH), lambda t: (0, t, 0)),
        out_shape=jax.ShapeDtypeStruct((BATCH, SEQ, DN_WIDTH), F32),
        scratch_shapes=[pltpu.VMEM((DN_BH, DN_HEAD_DIM, DN_HEAD_DIM), F32)],
        compiler_params=pltpu.CompilerParams(
            dimension_semantics=("arbitrary",), vmem_limit_bytes=VMEM_LIMIT),
        name="dn_scan",
    )(u, wq, akt, dec, sz, gain)


def _dn_mixer(proj, p, l):
    outs = _dn_prep(proj, p['dn_conv_w'][l], p['dn_a_log'][l], p['dn_dt_bias'][l])
    y = _dn_scan(*outs, p['dn_norm_gain'][l].reshape(1, DN_HEAD_DIM))
    return y.reshape(N_TOK, DN_WIDTH)


def kernel(x, norm_gains, ffn_w_gate, ffn_w_up, ffn_w_down, w_in, w_out, ssm_lambda_re, ssm_lambda_im,
           ssm_b_re, ssm_b_im, ssm_c_re, ssm_c_im, ssm_d, ssm_log_dt, ssm_glu_w, ssm_glu_b, ssm_out_gain,
           dn_conv_w, dn_a_log, dn_dt_bias, dn_norm_gain, attn_out_gain, rel_bias):
    p = dict(ssm_lambda_re=ssm_lambda_re, ssm_lambda_im=ssm_lambda_im, ssm_b_re=ssm_b_re, ssm_b_im=ssm_b_im,
             ssm_c_re=ssm_c_re, ssm_c_im=ssm_c_im, ssm_d=ssm_d, ssm_log_dt=ssm_log_dt, ssm_glu_w=ssm_glu_w,
             ssm_glu_b=ssm_glu_b, ssm_out_gain=ssm_out_gain, dn_conv_w=dn_conv_w, dn_a_log=dn_a_log,
             dn_dt_bias=dn_dt_bias, dn_norm_gain=dn_norm_gain)
    wg, wu, wd = ffn_w_gate, ffn_w_up, ffn_w_down
    w_in_p = jnp.pad(w_in, ((0, 0), (0, 0), (0, N_IN_PAD - N_IN_COLS))).astype(BF16)
    w_out_b = w_out
    bucket_table = _attn_bucket_table()
    gains = norm_gains.reshape(DEPTH, 6, 1, D_MODEL)
    x = x.reshape(N_TOK, D_MODEL)
    for l in range(DEPTH):
        x = _ffn(x, gains[l, 0], gains[l, 1], wg, wu, wd, l, 0)
        proj = _inproj(x, gains[l, 2], w_in_p, l)
        y_ssm = _s5_mixer(proj, p, l)
        y_dn = _dn_mixer(proj, p, l)
        o_at = _attn(proj, rel_bias, bucket_table)
        x = _outproj(x, y_ssm, y_dn, o_at, attn_out_gain[l].reshape(1, ATTN_WIDTH), gains[l, 3], w_out_b, l)
        x = _ffn(x, gains[l, 4], gains[l, 5], wg, wu, wd, l, 1)
    return x.reshape(BATCH, SEQ, D_MODEL)
```

```python
import functools
import math

import jax
import jax.numpy as jnp
import numpy as np
from jax import lax
from jax.experimental import pallas as pl
from jax.experimental.pallas import tpu as pltpu

D_MODEL = 2048
BATCH = 2
SEQ = 4096
DEPTH = 4
SSM_GROUPS = 32
SSM_CH = 16
SSM_STATE = 64
SSM_WIDTH = SSM_GROUPS * SSM_CH
DN_HEADS = 6
DN_HEAD_DIM = 128
DN_WIDTH = DN_HEADS * DN_HEAD_DIM
DN_CONV = 4
DN_CHUNK = 64
ATTN_HEADS = 6
ATTN_HEAD_DIM = 128
ATTN_WIDTH = ATTN_HEADS * ATTN_HEAD_DIM
DILATED_PAIRS = ((128, 1), (512, 4), (2048, 16))
ATTN_BLOCK = 128
N_BUCKETS = 32
REL_MAX_DIST = 2048
D_MIX = SSM_WIDTH + DN_WIDTH + ATTN_WIDTH
IN_SPLITS = (SSM_WIDTH, ATTN_WIDTH, ATTN_WIDTH, ATTN_WIDTH, 3 * DN_WIDTH, DN_WIDTH, DN_HEADS, DN_HEADS)
N_IN_COLS = sum(IN_SPLITS)
D_FF = 5632
NORM_EPS = 1e-6
NEG_INF = -1e30

LANES = 128
N_IN_PAD = 6144
V7X_VMEM_BYTES = 64 * 1024 * 1024
VMEM_LIMIT = V7X_VMEM_BYTES - 4 * 1024 * 1024
N_TOK = BATCH * SEQ

BF16 = jnp.bfloat16
F32 = jnp.float32


def _rms(x, gain):
    return x * lax.rsqrt(jnp.mean(x * x, axis=-1, keepdims=True) + NORM_EPS) * gain


FFN_TM = 1024
FFN_TF = 256


def _ffn_kernel(x_ref, gpre_ref, gpost_ref, wg_ref, wu_ref, wd_ref, o_ref, h_scr):
    f = pl.program_id(1)
    last = pl.num_programs(1) - 1

    def down_proj(h):
        gate = jnp.dot(h, wg_ref[...].astype(BF16), preferred_element_type=F32)
        up = jnp.dot(h, wu_ref[...].astype(BF16), preferred_element_type=F32)
        act = (gate * jax.nn.sigmoid(gate) * up).astype(BF16)
        return jnp.dot(act, wd_ref[...].astype(BF16), preferred_element_type=F32)

    @pl.when(f == 0)
    def _():
        h = _rms(x_ref[...], gpre_ref[...]).astype(BF16)
        h_scr[...] = h
        o_ref[...] = down_proj(h)

    @pl.when(jnp.logical_and(f > 0, f < last))
    def _():
        o_ref[...] += down_proj(h_scr[...])

    @pl.when(f == last)
    def _():
        y = o_ref[...] + down_proj(h_scr[...])
        o_ref[...] = x_ref[...] + _rms(y, 0.5 * gpost_ref[...])


def _ffn(x, g_pre, g_post, wg, wu, wd, layer, half):
    return pl.pallas_call(
        _ffn_kernel,
        grid=(N_TOK // FFN_TM, D_FF // FFN_TF),
        in_specs=[
            pl.BlockSpec((FFN_TM, D_MODEL), lambda i, f: (i, 0)),
            pl.BlockSpec((1, D_MODEL), lambda i, f: (0, 0)),
            pl.BlockSpec((1, D_MODEL), lambda i, f: (0, 0)),
            pl.BlockSpec((None, None, D_MODEL, FFN_TF), lambda i, f: (layer, half, 0, f)),
            pl.BlockSpec((None, None, D_MODEL, FFN_TF), lambda i, f: (layer, half, 0, f)),
            pl.BlockSpec((None, None, FFN_TF, D_MODEL), lambda i, f: (layer, half, f, 0)),
        ],
        out_specs=pl.BlockSpec((FFN_TM, D_MODEL), lambda i, f: (i, 0)),
        out_shape=jax.ShapeDtypeStruct((N_TOK, D_MODEL), F32),
        scratch_shapes=[pltpu.VMEM((FFN_TM, D_MODEL), BF16)],
        compiler_params=pltpu.CompilerParams(
            dimension_semantics=("parallel", "arbitrary"), vmem_limit_bytes=VMEM_LIMIT),
        name="ffn",
    )(x, g_pre, g_post, wg, wu, wd)


INP_TM = 1024
INP_TN = 1536


def _inproj_kernel(x_ref, g_ref, w_ref, o_ref, h_scr):
    @pl.when(pl.program_id(1) == 0)
    def _():
        h = _rms(x_ref[...], g_ref[...]).astype(BF16)
        h_scr[...] = h
        o_ref[...] = jnp.dot(h, w_ref[...], preferred_element_type=F32)

    @pl.when(pl.program_id(1) > 0)
    def _():
        o_ref[...] = jnp.dot(h_scr[...], w_ref[...], preferred_element_type=F32)


def _inproj(x, g, w, layer):
    return pl.pallas_call(
        _inproj_kernel,
        grid=(N_TOK // INP_TM, N_IN_PAD // INP_TN),
        in_specs=[
            pl.BlockSpec((INP_TM, D_MODEL), lambda i, n: (i, 0)),
            pl.BlockSpec((1, D_MODEL), lambda i, n: (0, 0)),
            pl.BlockSpec((None, D_MODEL, INP_TN), lambda i, n: (layer, 0, n)),
        ],
        out_specs=pl.BlockSpec((INP_TM, INP_TN), lambda i, n: (i, n)),
        out_shape=jax.ShapeDtypeStruct((N_TOK, N_IN_PAD), F32),
        scratch_shapes=[pltpu.VMEM((INP_TM, D_MODEL), BF16)],
        compiler_params=pltpu.CompilerParams(
            dimension_semantics=("parallel", "arbitrary"), vmem_limit_bytes=VMEM_LIMIT),
        name="inproj",
    )(x, g, w)


OUT_TM = 512
OUT_SUB = 256


def _outproj_kernel(x_ref, ys_ref, yd_ref, oa_ref, ga_ref, gpost_ref, w_ref, o_ref, w16):
    @pl.when(pl.program_id(0) == 0)
    def _():
        w16[...] = w_ref[...].astype(BF16)

    halves = [slice(i * OUT_SUB, (i + 1) * OUT_SUB) for i in range(OUT_TM // OUT_SUB)]
    ya = [_rms(oa_ref[r, :], ga_ref[...]).astype(BF16) for r in halves]
    mix = [jnp.dot(ys_ref[r, :].astype(BF16), w16[0:SSM_WIDTH, :], preferred_element_type=F32) for r in halves]
    mix = [m + jnp.dot(yd_ref[r, :].astype(BF16), w16[SSM_WIDTH:SSM_WIDTH + DN_WIDTH, :],
                       preferred_element_type=F32) for m, r in zip(mix, halves)]
    mix = [m + jnp.dot(a, w16[SSM_WIDTH + DN_WIDTH:D_MIX, :], preferred_element_type=F32)
           for m, a in zip(mix, ya)]
    for r, m in zip(halves, mix):
        o_ref[r, :] = x_ref[r, :] + _rms(m, gpost_ref[...])


def _outproj(x, y_ssm, y_dn, o_at, g_attn, g_post, w, layer):
    row = lambda i: (i, 0)
    fixed = lambda i: (0, 0)
    return pl.pallas_call(
        _outproj_kernel,
        grid=(N_TOK // OUT_TM,),
        in_specs=[
            pl.BlockSpec((OUT_TM, D_MODEL), row),
            pl.BlockSpec((OUT_TM, SSM_WIDTH), row),
            pl.BlockSpec((OUT_TM, DN_WIDTH), row),
            pl.BlockSpec((OUT_TM, ATTN_WIDTH), row),
            pl.BlockSpec((1, ATTN_WIDTH), fixed),
            pl.BlockSpec((1, D_MODEL), fixed),
            pl.BlockSpec((None, D_MIX, D_MODEL), lambda i: (layer, 0, 0), pipeline_mode=pl.Buffered(1)),
        ],
        out_specs=pl.BlockSpec((OUT_TM, D_MODEL), row),
        out_shape=jax.ShapeDtypeStruct((N_TOK, D_MODEL), F32),
        scratch_shapes=[pltpu.VMEM((D_MIX, D_MODEL), BF16)],
        compiler_params=pltpu.CompilerParams(
            dimension_semantics=("arbitrary",), vmem_limit_bytes=VMEM_LIMIT),
        name="outproj",
    )(x, y_ssm, y_dn, o_at, g_attn, g_post, w)


S5_SEG = 8
S5_SEGLEN = SEQ // S5_SEG
S5_KB = 64
S5_ROWS = S5_KB * S5_SEG
S5_PITCH = S5_KB + 8
S5_PROWS = S5_PITCH * S5_SEG
S5_NS = SSM_GROUPS * SSM_STATE
S5_SLABS = S5_NS // LANES
S5_Q = 4
S5_QS = S5_NS // S5_Q
S5_QL = S5_QS // LANES
S5_QC = SSM_WIDTH // S5_Q
S5_LOG2_SEGLEN = 9
assert 1 << S5_LOG2_SEGLEN == S5_SEGLEN


def _s5_kernel(u_ref, wb_ref, are_ref, aim_ref, cre_ref, cim_ref, d_ref, gw_ref, gb_ref, go_ref, o_ref,
               bu_scr, st_scr, carry_scr, y_scr):
    p = pl.program_id(1)
    j = pl.program_id(2)
    u = u_ref[0].reshape(S5_ROWS, SSM_WIDTH)
    ub = u.astype(BF16)

    @pl.when(jnp.logical_and(p == 0, j == 0))
    def _():
        bu_scr[...] = jnp.zeros_like(bu_scr)
        st_scr[...] = jnp.zeros_like(st_scr)

    @pl.when(jnp.logical_and(p == 1, j == 0))
    def _():
        st_scr[...] = carry_scr[...]

    for q in range(S5_Q):
        r = jnp.dot(ub[:, q * S5_QC:(q + 1) * S5_QC], wb_ref[q], preferred_element_type=F32)
        for half in range(2):
            for c in range(S5_QL):
                slab = half * S5_SLABS + q * S5_QL + c
                lanes = slice((half * S5_QL + c) * LANES, (half * S5_QL + c + 1) * LANES)
                for i in range(S5_SEG):
                    bu_scr[slab, i * S5_PITCH:i * S5_PITCH + S5_KB, :] = r[i * S5_KB:(i + 1) * S5_KB, lanes]

    def scan(store):
        for q in range(S5_Q):
            slabs = [q * S5_QL + c for c in range(S5_QL)]
            lanes = [slice(s * LANES, (s + 1) * LANES) for s in slabs]
            ilanes = [slice(S5_NS + s * LANES, S5_NS + (s + 1) * LANES) for s in slabs]
            ar = [jnp.broadcast_to(are_ref[:, ln], (S5_SEG, LANES)) for ln in lanes]
            ai = [jnp.broadcast_to(aim_ref[:, ln], (S5_SEG, LANES)) for ln in lanes]

            def body(k, carry):
                sr, si = carry
                rows = pl.ds(k, S5_SEG, stride=S5_PITCH)
                nr = tuple(ar[c] * sr[c] - ai[c] * si[c] + bu_scr[slabs[c], rows, :] for c in range(S5_QL))
                ni = tuple(ar[c] * si[c] + ai[c] * sr[c] + bu_scr[slabs[c] + S5_SLABS, rows, :]
                           for c in range(S5_QL))
                if store:
                    for c in range(S5_QL):
                        bu_scr[slabs[c], rows, :] = nr[c]
                        bu_scr[slabs[c] + S5_SLABS, rows, :] = ni[c]
                return nr, ni

            init = (tuple(st_scr[:, ln] for ln in lanes), tuple(st_scr[:, ln] for ln in ilanes))
            sr, si = lax.fori_loop(0, S5_KB, body, init, unroll=8)
            for c in range(S5_QL):
                st_scr[:, lanes[c]] = sr[c]
                st_scr[:, ilanes[c]] = si[c]

    @pl.when(p == 0)
    def _():
        scan(False)

        @pl.when(j == pl.num_programs(2) - 1)
        def _():
            lr, li = are_ref[...], aim_ref[...]
            for _ in range(S5_LOG2_SEGLEN):
                lr, li = lr * lr - li * li, 2.0 * lr * li
            cr = jnp.zeros((1, S5_NS), F32)
            ci = jnp.zeros((1, S5_NS), F32)
            carry_scr[0:1, :] = jnp.zeros((1, 2 * S5_NS), F32)
            for i in range(1, S5_SEG):
                er = st_scr[i - 1:i, 0:S5_NS]
                ei = st_scr[i - 1:i, S5_NS:2 * S5_NS]
                cr, ci = er + lr * cr - li * ci, ei + lr * ci + li * cr
                carry_scr[i:i + 1, 0:S5_NS] = cr
                carry_scr[i:i + 1, S5_NS:2 * S5_NS] = ci

    @pl.when(p == 1)
    def _():
        scan(True)
        for q in range(S5_Q):
            sre = jnp.concatenate([bu_scr[q * S5_QL + c] for c in range(S5_QL)], axis=1).astype(BF16)
            sim = jnp.concatenate([bu_scr[S5_SLABS + q * S5_QL + c] for c in range(S5_QL)], axis=1).astype(BF16)
            y_scr[:, q * S5_QC:(q + 1) * S5_QC] = (
                jnp.dot(sre, cre_ref[q], preferred_element_type=F32)
                + jnp.dot(sim, cim_ref[q], preferred_element_type=F32))
        cs = jnp.concatenate([y_scr[i * S5_PITCH:i * S5_PITCH + S5_KB, :] for i in range(S5_SEG)], axis=0)
        y = jax.nn.gelu(cs + d_ref[...] * u)
        z = jnp.dot(y.astype(BF16), gw_ref[...], preferred_element_type=F32) + gb_ref[...]
        o_ref[0] = _rms(y * jax.nn.sigmoid(z), go_ref[...]).reshape(S5_SEG, S5_KB, SSM_WIDTH)


def _s5_params(lam_re, lam_im, b_re, b_im, c_re, c_im, log_dt):
    lam = lax.complex(lam_re, lam_im)
    lam_bar = jnp.exp(lam * jnp.exp(log_dt)[:, None])
    b_bar = ((lam_bar - 1.0) / lam)[..., None] * lax.complex(b_re, b_im)
    gq = SSM_GROUPS // S5_Q
    eye = jnp.eye(gq, dtype=F32)

    def in_map(t):
        t = t.reshape(S5_Q, gq, SSM_STATE, SSM_CH)
        return jnp.einsum('qgpc,gh->qgchp', t, eye).reshape(S5_Q, S5_QC, S5_QS)

    def out_map(t):
        t = t.reshape(S5_Q, gq, SSM_CH, SSM_STATE)
        return jnp.einsum('qgcp,gh->qgphc', t, eye).reshape(S5_Q, S5_QS, S5_QC)

    wb = jnp.concatenate([in_map(b_bar.real), in_map(b_bar.imag)], axis=-1).astype(BF16)
    return (wb, lam_bar.real.reshape(1, S5_NS), lam_bar.imag.reshape(1, S5_NS),
            out_map(c_re).astype(BF16), out_map(-c_im).astype(BF16))


def _s5(proj4, wb, a_re, a_im, cre, cim, d_skip, glu_w, glu_b, out_gain):
    nblk = S5_SEGLEN // S5_KB
    fix2 = lambda b, p, j: (0, 0)
    fix3 = lambda b, p, j: (0, 0, 0)
    return pl.pallas_call(
        _s5_kernel,
        grid=(BATCH, 2, nblk),
        in_specs=[
            pl.BlockSpec((1, S5_SEG, S5_KB, SSM_WIDTH), lambda b, p, j: (b, 0, j, 0)),
            pl.BlockSpec((S5_Q, S5_QC, 2 * S5_QS), fix3),
            pl.BlockSpec((1, S5_NS), fix2),
            pl.BlockSpec((1, S5_NS), fix2),
            pl.BlockSpec((S5_Q, S5_QS, S5_QC), fix3),
            pl.BlockSpec((S5_Q, S5_QS, S5_QC), fix3),
            pl.BlockSpec((1, SSM_WIDTH), fix2),
            pl.BlockSpec((SSM_WIDTH, SSM_WIDTH), fix2),
            pl.BlockSpec((1, SSM_WIDTH), fix2),
            pl.BlockSpec((1, SSM_WIDTH), fix2),
        ],
        out_specs=pl.BlockSpec((1, S5_SEG, S5_KB, SSM_WIDTH), lambda b, p, j: (b, 0, j * p, 0)),
        out_shape=jax.ShapeDtypeStruct((BATCH, S5_SEG, S5_SEGLEN, SSM_WIDTH), F32),
        scratch_shapes=[
            pltpu.VMEM((2 * S5_SLABS, S5_PROWS, LANES), F32),
            pltpu.VMEM((S5_SEG, 2 * S5_NS), F32),
            pltpu.VMEM((S5_SEG, 2 * S5_NS), F32),
            pltpu.VMEM((S5_PROWS, SSM_WIDTH), F32),
        ],
        compiler_params=pltpu.CompilerParams(
            dimension_semantics=("parallel", "arbitrary", "arbitrary"), vmem_limit_bytes=VMEM_LIMIT),
        name="s5",
    )(proj4, wb, a_re, a_im, cre, cim, d_skip, glu_w, glu_b, out_gain)


def _s5_mixer(proj, p, l):
    wb, a_re, a_im, cre, cim = _s5_params(p['ssm_lambda_re'][l], p['ssm_lambda_im'][l], p['ssm_b_re'][l],
                                          p['ssm_b_im'][l], p['ssm_c_re'][l], p['ssm_c_im'][l],
                                          p['ssm_log_dt'][l])
    y = _s5(proj.reshape(BATCH, S5_SEG, S5_SEGLEN, N_IN_PAD), wb, a_re, a_im, cre, cim,
            p['ssm_d'][l].reshape(1, SSM_WIDTH), p['ssm_glu_w'][l].astype(BF16),
            p['ssm_glu_b'][l].reshape(1, SSM_WIDTH), p['ssm_out_gain'][l].reshape(1, SSM_WIDTH))
    return y.reshape(N_TOK, SSM_WIDTH)


AT_BLK = ATTN_BLOCK
AT_NBRANCH = len(DILATED_PAIRS)
AT_BLOCKS = SEQ // AT_BLK
AT_GROUP = 8
AT_ORDER = tuple(range(1, AT_NBRANCH)) + (0,)
assert DILATED_PAIRS[0][1] == 1 and all(d > 1 for _, d in DILATED_PAIRS[1:])
assert all(w // d == AT_BLK for w, d in DILATED_PAIRS)
assert all((AT_BLOCKS // d) % AT_GROUP == 0 or AT_GROUP % (AT_BLOCKS // d) == 0 for _, d in DILATED_PAIRS)
AT_Q_COL0 = SSM_WIDTH // LANES
AT_K_COL0 = AT_Q_COL0 + ATTN_HEADS
AT_V_COL0 = AT_K_COL0 + ATTN_HEADS


def _attn_kernel(rb_ref, bkt_ref, q_ref, k_ref, v_ref, o_ref, bias_scr, o_scr, lse_scr):
    h = pl.program_id(0)

    @pl.when(pl.program_id(1) == 0)
    def _():
        for g in range(AT_NBRANCH):
            bkt = bkt_ref[g]
            bias = jnp.zeros((AT_BLK, 2 * AT_BLK), F32)
            for b in range(N_BUCKETS):
                bias = jnp.where(bkt == b, rb_ref[b, h], bias)
            bias_scr[g] = jnp.where(bkt < 0, NEG_INF, bias)

    lane = lax.broadcasted_iota(jnp.int32, (AT_BLK, 2 * AT_BLK), 1)
    scale = ATTN_HEAD_DIM ** -0.5
    for g in AT_ORDER:
        dil = DILATED_PAIRS[g][1]
        nb = AT_BLOCKS // dil

        def body(it, carry, g=g, dil=dil, nb=nb):
            def block_rows(start):
                if dil == 1:
                    return pl.ds(pl.multiple_of(start, AT_BLK), AT_BLK)
                return pl.ds(start, AT_BLK, stride=dil)

            group = range(AT_GROUP)
            ts = [it * AT_GROUP + i for i in group]
            if nb <= AT_GROUP:
                ns = [i % nb for i in group]
            else:
                ns = [None if i == 0 else i for i in group]
            n0 = ts[0] % nb
            rows = [block_rows(t // nb + (t % nb) * (AT_BLK * dil)) for t in ts]
            q = [(q_ref[r, :] * scale).astype(BF16) for r in rows]
            k_cur = [k_ref[r, :].astype(BF16) for r in rows]
            v_cur = [v_ref[r, :].astype(BF16) for r in rows]
            if ns[0] is None:
                prow = block_rows(ts[0] // nb + jnp.maximum(n0 - 1, 0) * (AT_BLK * dil))
                k_lead, v_lead = k_ref[prow, :].astype(BF16), v_ref[prow, :].astype(BF16)
            s, vc = [], []
            for i in group:
                if ns[i] == 0:
                    s.append(_dot_nt(q[i], k_cur[i]) + bias_scr[g, :, AT_BLK:])
                    vc.append(v_cur[i])
                    continue
                k_prev, v_prev = (k_lead, v_lead) if ns[i] is None else (k_cur[i - 1], v_cur[i - 1])
                si = _dot_nt(q[i], jnp.concatenate([k_prev, k_cur[i]], axis=0)) + bias_scr[g]
                if ns[i] is None:
                    si = jnp.where(jnp.logical_or(n0 > 0, lane >= AT_BLK), si, NEG_INF)
                s.append(si)
                vc.append(jnp.concatenate([v_prev, v_cur[i]], axis=0))
            m = [jnp.max(si, axis=1, keepdims=True) for si in s]
            p = [jnp.exp(si - mi) for si, mi in zip(s, m)]
            l = [jnp.sum(pi, axis=1, keepdims=True) for pi in p]
            o = [jnp.dot(pi.astype(BF16), vi, preferred_element_type=F32) / li for pi, vi, li in zip(p, vc, l)]
            lse = [jnp.broadcast_to(mi + jnp.log(li), (AT_BLK, LANES)) for mi, li in zip(m, l)]
            if dil > 1:
                for r, oi, li in zip(rows, o, lse):
                    o_scr[g - 1, r, :] = oi
                    lse_scr[g - 1, r, :] = li
                return carry
            for r, oi, li in zip(rows, o, lse):
                parked = [(o_scr[j, r, :], lse_scr[j, r, :]) for j in range(AT_NBRANCH - 1)]
                top = functools.reduce(jnp.maximum, [lj for _, lj in parked], li)
                wi = jnp.exp(li - top)
                num, den = wi * oi, wi
                for oj, lj in parked:
                    wj = jnp.exp(lj - top)
                    num, den = num + wj * oj, den + wj
                o_ref[r, :] = num / den
            return carry

        lax.fori_loop(0, AT_BLOCKS // AT_GROUP, body, 0)


def _attn_bucket_table():
    qi = jnp.arange(AT_BLK)[:, None]
    kj = jnp.arange(2 * AT_BLK)[None, :]
    rel = AT_BLK + qi - kj
    max_exact = N_BUCKETS // 2
    tables = []
    for window, dil in DILATED_PAIRS:
        dist = jnp.maximum(rel, 0) * dil
        d = jnp.maximum(dist, 1).astype(F32)
        large = max_exact + jnp.log(d / max_exact) / math.log(REL_MAX_DIST / max_exact) * (N_BUCKETS - max_exact)
        large = jnp.minimum(large.astype(jnp.int32), N_BUCKETS - 1)
        bucket = jnp.where(dist < max_exact, dist, large)
        tables.append(jnp.where((rel >= 0) & (rel <= window // dil), bucket, -1))
    return jnp.stack(tables).astype(jnp.int32)


def _attn(proj, rel_bias, bucket_table):
    col = lambda c0: pl.BlockSpec((SEQ, LANES), lambda h, b: (b, c0 + h))
    return pl.pallas_call(
        _attn_kernel,
        grid=(ATTN_HEADS, BATCH),
        in_specs=[
            pl.BlockSpec(memory_space=pltpu.SMEM),
            pl.BlockSpec((AT_NBRANCH, AT_BLK, 2 * AT_BLK), lambda h, b: (0, 0, 0)),
            col(AT_Q_COL0), col(AT_K_COL0), col(AT_V_COL0),
        ],
        out_specs=pl.BlockSpec((SEQ, LANES), lambda h, b: (b, h)),
        out_shape=jax.ShapeDtypeStruct((N_TOK, ATTN_WIDTH), F32),
        scratch_shapes=[
            pltpu.VMEM((AT_NBRANCH, AT_BLK, 2 * AT_BLK), F32),
            pltpu.VMEM((AT_NBRANCH - 1, SEQ, LANES), F32),
            pltpu.VMEM((AT_NBRANCH - 1, SEQ, LANES), F32),
        ],
        compiler_params=pltpu.CompilerParams(
            dimension_semantics=("parallel", "arbitrary"), vmem_limit_bytes=VMEM_LIMIT),
        name="dilated_attn",
    )(rel_bias, bucket_table, proj, proj, proj)


DN_TILE = 2 * DN_CHUNK
DN_NTILE = SEQ // DN_TILE
DN_BH = BATCH * DN_HEADS
DN_PAD = 8
DN_GROUP = 8
DN_AKT_ROWS = DN_CHUNK + DN_HEAD_DIM
DN_INV_LEVELS = 6
assert 2 ** DN_INV_LEVELS == DN_CHUNK
DN_QKV_COL0 = (SSM_WIDTH + 3 * ATTN_WIDTH) // LANES
DN_Z_COL0 = DN_QKV_COL0 + 3 * DN_HEADS
DN_AB_COL = DN_Z_COL0 + DN_HEADS
HIGHEST = lax.Precision.HIGHEST


def _dot_nt(a, b):
    return lax.dot_general(a, b, (((1,), (1,)), ((), ())), preferred_element_type=F32)


def _split3(x):
    hi = x.astype(BF16)
    r = x - hi.astype(F32)
    mid = r.astype(BF16)
    return hi, mid, (r - mid.astype(F32)).astype(BF16)


def _dot_exact_lhs(x, m16):
    return sum(jnp.dot(piece, m16, preferred_element_type=F32) for piece in _split3(x))


def _dot_exact_rhs(m16, x):
    return sum(jnp.dot(m16, piece, preferred_element_type=F32) for piece in _split3(x))


def _dn_tile_masks():
    row = lax.broadcasted_iota(jnp.int32, (DN_TILE, DN_TILE), 0)
    col = lax.broadcasted_iota(jnp.int32, (DN_TILE, DN_TILE), 1)
    causal = jnp.logical_and((row // DN_CHUNK) == (col // DN_CHUNK), row >= col)
    return row, col, causal


DN_GATE_ROWS = 1024


def _dn_gates_kernel(ab_ref, alog_ref, dtb_ref, o_ref):
    _, col, causal = _dn_tile_masks()
    cumsum_mat = causal.astype(BF16)
    neg_a = -jnp.exp(alog_ref[...])
    for t in range(DN_GATE_ROWS // DN_TILE):
        rows = slice(t * DN_TILE, (t + 1) * DN_TILE)
        ab = ab_ref[rows, :]
        x = ab + dtb_ref[...]
        g = neg_a * (jnp.maximum(x, 0.0) + jnp.log1p(jnp.exp(-jnp.abs(x))))
        gc = _dot_exact_rhs(cumsum_mat, g)
        o_ref[rows, :] = jnp.where(col < DN_HEADS, gc, jax.nn.sigmoid(ab))


def _dn_gates(proj, a_log, dt_bias):
    lanes = lambda v: jnp.pad(v, (0, LANES - DN_HEADS)).reshape(1, LANES)
    return pl.pallas_call(
        _dn_gates_kernel,
        grid=(N_TOK // DN_GATE_ROWS,),
        in_specs=[
            pl.BlockSpec((DN_GATE_ROWS, LANES), lambda i: (i, DN_AB_COL)),
            pl.BlockSpec((1, LANES), lambda i: (0, 0)),
            pl.BlockSpec((1, LANES), lambda i: (0, 0)),
        ],
        out_specs=pl.BlockSpec((DN_GATE_ROWS, LANES), lambda i: (i, 0)),
        out_shape=jax.ShapeDtypeStruct((N_TOK, LANES), F32),
        compiler_params=pltpu.CompilerParams(dimension_semantics=("parallel",), vmem_limit_bytes=VMEM_LIMIT),
        name="dn_gates",
    )(proj, lanes(a_log), lanes(dt_bias))


def _dn_prep_kernel(q_ref, k_ref, v_ref, z_ref, gate_ref, wq_ref, wk_ref, wv_ref,
                    u_o, wq_o, akt_o, dec_o, sz_o, qp, kp, vp):
    h = pl.program_id(1)
    for src, dst in ((q_ref, qp), (k_ref, kp), (v_ref, vp)):
        dst[0:DN_PAD, :] = jnp.zeros((DN_PAD, LANES), F32)
        dst[DN_PAD:DN_PAD + SEQ, :] = src[...]

    row, col, causal = _dn_tile_masks()
    pick = jnp.concatenate([row == h, row == h + DN_HEADS], axis=1).astype(BF16)
    eye = (row == col).astype(F32)
    pair_masks = [
        jnp.logical_and(jnp.logical_and((row // (2 * s)) == (col // (2 * s)), (row // s) % 2 == 1),
                        (col // s) % 2 == 0)
        for s in (2 ** i for i in range(DN_INV_LEVELS))]
    scale = DN_HEAD_DIM ** -0.5

    def conv_silu(pad_ref, w_ref, base):
        acc = None
        for j in range(DN_CONV):
            sh = DN_PAD - (DN_CONV - 1) + j
            term = w_ref[j:j + 1, :] * pad_ref[pl.ds(base + sh, DN_TILE), :]
            acc = term if acc is None else acc + term
        return acc * jax.nn.sigmoid(acc)

    def l2n(x):
        return x * lax.rsqrt(jnp.sum(x * x, axis=1, keepdims=True) + NORM_EPS)

    def mm16(a, b):
        return jnp.dot(a.astype(BF16), b.astype(BF16), preferred_element_type=F32)

    def body(it, carry):
        tiles = [it * DN_GROUP + i for i in range(DN_GROUP)]
        bases = [pl.multiple_of(t * DN_TILE, DN_TILE) for t in tiles]
        q = [l2n(conv_silu(qp, wq_ref, b)) * scale for b in bases]
        k = [l2n(conv_silu(kp, wk_ref, b)) for b in bases]
        v = [conv_silu(vp, wv_ref, b) for b in bases]
        gate_rep = [_dot_exact_lhs(gate_ref[pl.ds(b, DN_TILE), :], pick) for b in bases]
        gc = [r[:, :DN_TILE] for r in gate_rep]
        beta = [r[:, DN_TILE:] for r in gate_rep]
        decay = [jnp.exp(jnp.where(causal, c - c.T, NEG_INF)) for c in gc]
        kb = [ki * bi for ki, bi in zip(k, beta)]
        k16 = [ki.astype(BF16) for ki in k]
        a_mat = [_dot_nt(kbi.astype(BF16), ki) * di for kbi, ki, di in zip(kb, k16, decay)]
        t_inv = [eye - jnp.where(pair_masks[0], am, 0.0) for am in a_mat]
        for mask in pair_masks[1:]:
            low = [jnp.where(mask, am, 0.0).astype(BF16) for am in a_mat]
            t_inv = [ti - mm16(mm16(ti, lo), ti) for ti, lo in zip(t_inv, low)]
        egc = [jnp.exp(c) for c in gc]
        uw = [mm16(ti, jnp.concatenate([vi * bi, kbi * ei], axis=1))
              for ti, vi, bi, kbi, ei in zip(t_inv, v, beta, kb, egc)]
        at = [(_dot_nt(qi.astype(BF16), ki) * di).astype(BF16) for qi, ki, di in zip(q, k16, decay)]
        qd = [(qi * ei).astype(BF16) for qi, ei in zip(q, egc)]
        c = DN_CHUNK
        for i, t in enumerate(tiles):
            rows = pl.ds(bases[i], DN_TILE)
            u_o[0, rows, :] = uw[i][:, :DN_HEAD_DIM]
            w = uw[i][:, DN_HEAD_DIM:].astype(BF16)
            wq_o[0, pl.ds(pl.multiple_of(t * (2 * DN_TILE), 2 * DN_TILE), 2 * DN_TILE), :] = jnp.concatenate(
                [w[:c], qd[i][:c], w[c:], qd[i][c:]], axis=0)
            gc_first, gc_second = gc[i][c - 1:c, :], gc[i][DN_TILE - 1:DN_TILE, :]
            gc_last = jnp.where(row < c, gc_first, gc_second)
            kt = (k[i] * jnp.exp(gc_last - gc[i])).T
            akt_o[0, pl.ds(pl.multiple_of(t * DN_AKT_ROWS, DN_CHUNK), DN_AKT_ROWS), :] = jnp.concatenate(
                [at[i][:c] + at[i][c:], kt.astype(BF16)], axis=0)
            dec_o[0, t] = jnp.concatenate(
                [jnp.exp(gc_first), jnp.exp(gc_second), jnp.zeros((DN_PAD - 2, LANES), F32)], axis=0)
            z = z_ref[rows, :]
            sz_o[0, rows, :] = (z * jax.nn.sigmoid(z)).astype(BF16)
        return carry

    lax.fori_loop(0, DN_NTILE // DN_GROUP, body, 0)


def _dn_prep(proj, gates, conv_w):
    col = lambda c0: pl.BlockSpec((SEQ, LANES), lambda b, h: (b, c0 + h))
    cw = lambda c0: pl.BlockSpec((DN_CONV, LANES), lambda b, h: (0, c0 + h))
    bh_rows = lambda n: pl.BlockSpec((1, n, LANES), lambda b, h: (b * DN_HEADS + h, 0, 0))
    rows_shape = lambda n, dt: jax.ShapeDtypeStruct((DN_BH, n, LANES), dt)
    return pl.pallas_call(
        _dn_prep_kernel,
        grid=(BATCH, DN_HEADS),
        in_specs=[
            col(DN_QKV_COL0), col(DN_QKV_COL0 + DN_HEADS), col(DN_QKV_COL0 + 2 * DN_HEADS), col(DN_Z_COL0),
            pl.BlockSpec((SEQ, LANES), lambda b, h: (b, 0)),
            cw(0), cw(DN_HEADS), cw(2 * DN_HEADS),
        ],
        out_specs=[
            bh_rows(SEQ), bh_rows(2 * SEQ), bh_rows(DN_NTILE * DN_AKT_ROWS),
            pl.BlockSpec((1, DN_NTILE, DN_PAD, LANES), lambda b, h: (b * DN_HEADS + h, 0, 0, 0)),
            bh_rows(SEQ),
        ],
        out_shape=[
            rows_shape(SEQ, F32), rows_shape(2 * SEQ, BF16), rows_shape(DN_NTILE * DN_AKT_ROWS, BF16),
            jax.ShapeDtypeStruct((DN_BH, DN_NTILE, DN_PAD, LANES), F32),
            rows_shape(SEQ, BF16),
        ],
        scratch_shapes=[pltpu.VMEM((SEQ + DN_PAD, LANES), F32)] * 3,
        compiler_params=pltpu.CompilerParams(
            dimension_semantics=("parallel", "parallel"), vmem_limit_bytes=VMEM_LIMIT),
        name="dn_prep",
    )(proj, proj, proj, proj, gates, conv_w, conv_w, conv_w)


def _dn_scan_kernel(u_ref, wq_ref, akt_ref, dec_ref, sz_ref, gain_ref, o_ref, s_scr):
    @pl.when(pl.program_id(0) == 0)
    def _():
        s_scr[...] = jnp.zeros_like(s_scr)

    gain = gain_ref[...]
    c = DN_CHUNK
    chains = range(DN_BH)
    second_chunk_lane = lax.broadcasted_iota(jnp.int32, (DN_AKT_ROWS, LANES), 1) >= c
    state = [s_scr[bh] for bh in chains]
    v_first = None
    for j in range(2):
        rows = slice(j * c, (j + 1) * c)
        s16 = [s.astype(BF16) for s in state]
        ws_qs = [jnp.dot(wq_ref[bh, j * DN_TILE:(j + 1) * DN_TILE, :], s16[bh], preferred_element_type=F32)
                 for bh in chains]
        v_new = [u_ref[bh, rows, :] - ws_qs[bh][:c] for bh in chains]
        if j == 0:
            v_pair = [jnp.concatenate([v, jnp.zeros_like(v)], axis=0).astype(BF16) for v in v_new]
            v_first = v_new
            akt = [akt_ref[bh] for bh in chains]
        else:
            v_pair = [jnp.concatenate([v0, v], axis=0).astype(BF16) for v0, v in zip(v_first, v_new)]
            akt = [jnp.where(second_chunk_lane, akt_ref[bh], jnp.zeros((DN_AKT_ROWS, LANES), BF16))
                   for bh in chains]
        ov_kv = [jnp.dot(akt[bh], v_pair[bh], preferred_element_type=F32) for bh in chains]
        state = [state[bh] * dec_ref[bh, 0, j:j + 1, :] + ov_kv[bh][c:] for bh in chains]
        for bh in chains:
            b, h = divmod(bh, DN_HEADS)
            o = ws_qs[bh][c:] + ov_kv[bh][:c]
            o_ref[b, rows, h * DN_HEAD_DIM:(h + 1) * DN_HEAD_DIM] = (
                _rms(o, gain) * sz_ref[bh, rows, :].astype(F32))
    for bh in chains:
        s_scr[bh] = state[bh]


def _dn_scan(u, wq, akt, dec, sz, gain):
    rows = lambda n: pl.BlockSpec((DN_BH, n, LANES), lambda t: (0, t, 0))
    return pl.pallas_call(
        _dn_scan_kernel,
        grid=(DN_NTILE,),
        in_specs=[rows(DN_TILE), rows(2 * DN_TILE), rows(DN_AKT_ROWS),
                  pl.BlockSpec((DN_BH, 1, DN_PAD, LANES), lambda t: (0, t, 0, 0)), rows(DN_TILE),
                  pl.BlockSpec((1, LANES), lambda t: (0, 0))],
        out_specs=pl.BlockSpec((BATCH, DN_TILE, DN_WIDTH), lambda t: (0, t, 0)),
        out_shape=jax.ShapeDtypeStruct((BATCH, SEQ, DN_WIDTH), F32),
        scratch_shapes=[pltpu.VMEM((DN_BH, DN_HEAD_DIM, DN_HEAD_DIM), F32)],
        compiler_params=pltpu.CompilerParams(
            dimension_semantics=("arbitrary",), vmem_limit_bytes=VMEM_LIMIT),
        name="dn_scan",
    )(u, wq, akt, dec, sz, gain)


def _dn_mixer(proj, p, l):
    gates = _dn_gates(proj, p['dn_a_log'][l], p['dn_dt_bias'][l])
    outs = _dn_prep(proj, gates, p['dn_conv_w'][l])
    y = _dn_scan(*outs, p['dn_norm_gain'][l].reshape(1, DN_HEAD_DIM))
    return y.reshape(N_TOK, DN_WIDTH)


def kernel(x, norm_gains, ffn_w_gate, ffn_w_up, ffn_w_down, w_in, w_out, ssm_lambda_re, ssm_lambda_im,
           ssm_b_re, ssm_b_im, ssm_c_re, ssm_c_im, ssm_d, ssm_log_dt, ssm_glu_w, ssm_glu_b, ssm_out_gain,
           dn_conv_w, dn_a_log, dn_dt_bias, dn_norm_gain, attn_out_gain, rel_bias):
    p = dict(ssm_lambda_re=ssm_lambda_re, ssm_lambda_im=ssm_lambda_im, ssm_b_re=ssm_b_re, ssm_b_im=ssm_b_im,
             ssm_c_re=ssm_c_re, ssm_c_im=ssm_c_im, ssm_d=ssm_d, ssm_log_dt=ssm_log_dt, ssm_glu_w=ssm_glu_w,
             ssm_glu_b=ssm_glu_b, ssm_out_gain=ssm_out_gain, dn_conv_w=dn_conv_w, dn_a_log=dn_a_log,
             dn_dt_bias=dn_dt_bias, dn_norm_gain=dn_norm_gain)
    wg, wu, wd = ffn_w_gate, ffn_w_up, ffn_w_down
    w_in_p = jnp.pad(w_in, ((0, 0), (0, 0), (0, N_IN_PAD - N_IN_COLS))).astype(BF16)
    w_out_b = w_out
    bucket_table = _attn_bucket_table()
    gains = norm_gains.reshape(DEPTH, 6, 1, D_MODEL)
    x = x.reshape(N_TOK, D_MODEL)
    for l in range(DEPTH):
        x = _ffn(x, gains[l, 0], gains[l, 1], wg, wu, wd, l, 0)
        proj = _inproj(x, gains[l, 2], w_in_p, l)
        y_ssm = _s5_mixer(proj, p, l)
        y_dn = _dn_mixer(proj, p, l)
        o_at = _attn(proj, rel_bias, bucket_table)
        x = _outproj(x, y_ssm, y_dn, o_at, attn_out_gain[l].reshape(1, ATTN_WIDTH), gains[l, 3], w_out_b, l)
        x = _ffn(x, gains[l, 4], gains[l, 5], wg, wu, wd, l, 1)
    return x.reshape(BATCH, SEQ, D_MODEL)
```

```python
import functools
import math

import jax
import jax.numpy as jnp
import numpy as np
from jax import lax
from jax.experimental import pallas as pl
from jax.experimental.pallas import tpu as pltpu

D_MODEL = 2048
BATCH = 2
SEQ = 4096
DEPTH = 4
SSM_GROUPS = 32
SSM_CH = 16
SSM_STATE = 64
SSM_WIDTH = SSM_GROUPS * SSM_CH
DN_HEADS = 6
DN_HEAD_DIM = 128
DN_WIDTH = DN_HEADS * DN_HEAD_DIM
DN_CONV = 4
DN_CHUNK = 64
ATTN_HEADS = 6
ATTN_HEAD_DIM = 128
ATTN_WIDTH = ATTN_HEADS * ATTN_HEAD_DIM
DILATED_PAIRS = ((128, 1), (512, 4), (2048, 16))
ATTN_BLOCK = 128
N_BUCKETS = 32
REL_MAX_DIST = 2048
D_MIX = SSM_WIDTH + DN_WIDTH + ATTN_WIDTH
IN_SPLITS = (SSM_WIDTH, ATTN_WIDTH, ATTN_WIDTH, ATTN_WIDTH, 3 * DN_WIDTH, DN_WIDTH, DN_HEADS, DN_HEADS)
N_IN_COLS = sum(IN_SPLITS)
D_FF = 5632
NORM_EPS = 1e-6
NEG_INF = -1e30

LANES = 128
N_IN_PAD = 6144
V7X_VMEM_BYTES = 64 * 1024 * 1024
VMEM_LIMIT = V7X_VMEM_BYTES - 4 * 1024 * 1024
N_TOK = BATCH * SEQ

BF16 = jnp.bfloat16
F32 = jnp.float32


def _rms(x, gain):
    return x * lax.rsqrt(jnp.mean(x * x, axis=-1, keepdims=True) + NORM_EPS) * gain


FFN_TM = 1024
FFN_TF = 256


def _ffn_kernel(x_ref, gpre_ref, gpost_ref, wg_ref, wu_ref, wd_ref, o_ref, h_scr):
    f = pl.program_id(1)
    last = pl.num_programs(1) - 1

    def down_proj(h):
        gate = jnp.dot(h, wg_ref[...].astype(BF16), preferred_element_type=F32)
        up = jnp.dot(h, wu_ref[...].astype(BF16), preferred_element_type=F32)
        act = (gate * jax.nn.sigmoid(gate) * up).astype(BF16)
        return jnp.dot(act, wd_ref[...].astype(BF16), preferred_element_type=F32)

    @pl.when(f == 0)
    def _():
        h = _rms(x_ref[...], gpre_ref[...]).astype(BF16)
        h_scr[...] = h
        o_ref[...] = down_proj(h)

    @pl.when(jnp.logical_and(f > 0, f < last))
    def _():
        o_ref[...] += down_proj(h_scr[...])

    @pl.when(f == last)
    def _():
        y = o_ref[...] + down_proj(h_scr[...])
        o_ref[...] = x_ref[...] + _rms(y, 0.5 * gpost_ref[...])


def _ffn(x, gains, wg, wu, wd, layer, half):
    return pl.pallas_call(
        _ffn_kernel,
        grid=(N_TOK // FFN_TM, D_FF // FFN_TF),
        in_specs=[
            pl.BlockSpec((FFN_TM, D_MODEL), lambda i, f: (i, 0)),
            pl.BlockSpec((None, None, 1, D_MODEL), lambda i, f: (layer, 4 * half, 0, 0)),
            pl.BlockSpec((None, None, 1, D_MODEL), lambda i, f: (layer, 4 * half + 1, 0, 0)),
            pl.BlockSpec((None, None, D_MODEL, FFN_TF), lambda i, f: (layer, half, 0, f)),
            pl.BlockSpec((None, None, D_MODEL, FFN_TF), lambda i, f: (layer, half, 0, f)),
            pl.BlockSpec((None, None, FFN_TF, D_MODEL), lambda i, f: (layer, half, f, 0)),
        ],
        out_specs=pl.BlockSpec((FFN_TM, D_MODEL), lambda i, f: (i, 0)),
        out_shape=jax.ShapeDtypeStruct((N_TOK, D_MODEL), F32),
        scratch_shapes=[pltpu.VMEM((FFN_TM, D_MODEL), BF16)],
        compiler_params=pltpu.CompilerParams(
            dimension_semantics=("parallel", "arbitrary"), vmem_limit_bytes=VMEM_LIMIT),
        name="ffn",
    )(x, gains, gains, wg, wu, wd)


INP_TM = 1024
INP_TN = 1536


def _inproj_kernel(x_ref, g_ref, w_ref, o_ref, h_scr):
    @pl.when(pl.program_id(1) == 0)
    def _():
        h = _rms(x_ref[...], g_ref[...]).astype(BF16)
        h_scr[...] = h
        o_ref[...] = jnp.dot(h, w_ref[...], preferred_element_type=F32)

    @pl.when(pl.program_id(1) > 0)
    def _():
        o_ref[...] = jnp.dot(h_scr[...], w_ref[...], preferred_element_type=F32)


def _inproj(x, gains, w, layer):
    return pl.pallas_call(
        _inproj_kernel,
        grid=(N_TOK // INP_TM, N_IN_PAD // INP_TN),
        in_specs=[
            pl.BlockSpec((INP_TM, D_MODEL), lambda i, n: (i, 0)),
            pl.BlockSpec((None, None, 1, D_MODEL), lambda i, n: (layer, 2, 0, 0)),
            pl.BlockSpec((None, D_MODEL, INP_TN), lambda i, n: (layer, 0, n)),
        ],
        out_specs=pl.BlockSpec((INP_TM, INP_TN), lambda i, n: (i, n)),
        out_shape=jax.ShapeDtypeStruct((N_TOK, N_IN_PAD), F32),
        scratch_shapes=[pltpu.VMEM((INP_TM, D_MODEL), BF16)],
        compiler_params=pltpu.CompilerParams(
            dimension_semantics=("parallel", "arbitrary"), vmem_limit_bytes=VMEM_LIMIT),
        name="inproj",
    )(x, gains, w)


OUT_TM = 512
OUT_SUB = 256


def _outproj_kernel(x_ref, ys_ref, yd_ref, oa_ref, ga_ref, gpost_ref, w_ref, o_ref, w16):
    @pl.when(pl.program_id(0) == 0)
    def _():
        w16[...] = w_ref[...].astype(BF16)

    halves = [slice(i * OUT_SUB, (i + 1) * OUT_SUB) for i in range(OUT_TM // OUT_SUB)]
    ya = [_rms(oa_ref[r, :], ga_ref[...]).astype(BF16) for r in halves]
    mix = [jnp.dot(ys_ref[r, :].astype(BF16), w16[0:SSM_WIDTH, :], preferred_element_type=F32) for r in halves]
    mix = [m + jnp.dot(yd_ref[r, :].astype(BF16), w16[SSM_WIDTH:SSM_WIDTH + DN_WIDTH, :],
                       preferred_element_type=F32) for m, r in zip(mix, halves)]
    mix = [m + jnp.dot(a, w16[SSM_WIDTH + DN_WIDTH:D_MIX, :], preferred_element_type=F32)
           for m, a in zip(mix, ya)]
    for r, m in zip(halves, mix):
        o_ref[r, :] = x_ref[r, :] + _rms(m, gpost_ref[...])


def _outproj(x, y_ssm, y_dn, o_at, g_attn, gains, w, layer):
    row = lambda i: (i, 0)
    fixed = lambda i: (0, 0)
    return pl.pallas_call(
        _outproj_kernel,
        grid=(N_TOK // OUT_TM,),
        in_specs=[
            pl.BlockSpec((OUT_TM, D_MODEL), row),
            pl.BlockSpec((OUT_TM, SSM_WIDTH), row),
            pl.BlockSpec((OUT_TM, DN_WIDTH), row),
            pl.BlockSpec((OUT_TM, ATTN_WIDTH), row),
            pl.BlockSpec((1, ATTN_WIDTH), fixed),
            pl.BlockSpec((None, None, 1, D_MODEL), lambda i: (layer, 3, 0, 0)),
            pl.BlockSpec((None, D_MIX, D_MODEL), lambda i: (layer, 0, 0), pipeline_mode=pl.Buffered(1)),
        ],
        out_specs=pl.BlockSpec((OUT_TM, D_MODEL), row),
        out_shape=jax.ShapeDtypeStruct((N_TOK, D_MODEL), F32),
        scratch_shapes=[pltpu.VMEM((D_MIX, D_MODEL), BF16)],
        compiler_params=pltpu.CompilerParams(
            dimension_semantics=("arbitrary",), vmem_limit_bytes=VMEM_LIMIT),
        name="outproj",
    )(x, y_ssm, y_dn, o_at, g_attn, gains, w)


S5_SEG = 8
S5_SEGLEN = SEQ // S5_SEG
S5_KB = 64
S5_ROWS = S5_KB * S5_SEG
S5_PITCH = S5_KB + 8
S5_PROWS = S5_PITCH * S5_SEG
S5_NS = SSM_GROUPS * SSM_STATE
S5_SLABS = S5_NS // LANES
S5_Q = 4
S5_QS = S5_NS // S5_Q
S5_QL = S5_QS // LANES
S5_QC = SSM_WIDTH // S5_Q
S5_LOG2_SEGLEN = 9
assert 1 << S5_LOG2_SEGLEN == S5_SEGLEN


def _s5_kernel(u_ref, wb_ref, are_ref, aim_ref, cre_ref, cim_ref, d_ref, gw_ref, gb_ref, go_ref, o_ref,
               bu_scr, st_scr, carry_scr, y_scr):
    p = pl.program_id(1)
    j = pl.program_id(2)
    u = u_ref[0].reshape(S5_ROWS, SSM_WIDTH)
    ub = u.astype(BF16)

    @pl.when(jnp.logical_and(p == 0, j == 0))
    def _():
        bu_scr[...] = jnp.zeros_like(bu_scr)
        st_scr[...] = jnp.zeros_like(st_scr)

    @pl.when(jnp.logical_and(p == 1, j == 0))
    def _():
        st_scr[...] = carry_scr[...]

    for q in range(S5_Q):
        r = jnp.dot(ub[:, q * S5_QC:(q + 1) * S5_QC], wb_ref[q], preferred_element_type=F32)
        for half in range(2):
            for c in range(S5_QL):
                slab = half * S5_SLABS + q * S5_QL + c
                lanes = slice((half * S5_QL + c) * LANES, (half * S5_QL + c + 1) * LANES)
                for i in range(S5_SEG):
                    bu_scr[slab, i * S5_PITCH:i * S5_PITCH + S5_KB, :] = r[i * S5_KB:(i + 1) * S5_KB, lanes]

    def scan(store):
        for q in range(S5_Q):
            slabs = [q * S5_QL + c for c in range(S5_QL)]
            lanes = [slice(s * LANES, (s + 1) * LANES) for s in slabs]
            ilanes = [slice(S5_NS + s * LANES, S5_NS + (s + 1) * LANES) for s in slabs]
            ar = [jnp.broadcast_to(are_ref[:, ln], (S5_SEG, LANES)) for ln in lanes]
            ai = [jnp.broadcast_to(aim_ref[:, ln], (S5_SEG, LANES)) for ln in lanes]

            def body(k, carry):
                sr, si = carry
                rows = pl.ds(k, S5_SEG, stride=S5_PITCH)
                nr = tuple(ar[c] * sr[c] - ai[c] * si[c] + bu_scr[slabs[c], rows, :] for c in range(S5_QL))
                ni = tuple(ar[c] * si[c] + ai[c] * sr[c] + bu_scr[slabs[c] + S5_SLABS, rows, :]
                           for c in range(S5_QL))
                if store:
                    for c in range(S5_QL):
                        bu_scr[slabs[c], rows, :] = nr[c]
                        bu_scr[slabs[c] + S5_SLABS, rows, :] = ni[c]
                return nr, ni

            init = (tuple(st_scr[:, ln] for ln in lanes), tuple(st_scr[:, ln] for ln in ilanes))
            sr, si = lax.fori_loop(0, S5_KB, body, init, unroll=8)
            for c in range(S5_QL):
                st_scr[:, lanes[c]] = sr[c]
                st_scr[:, ilanes[c]] = si[c]

    @pl.when(p == 0)
    def _():
        scan(False)

        @pl.when(j == pl.num_programs(2) - 1)
        def _():
            lr, li = are_ref[...], aim_ref[...]
            for _ in range(S5_LOG2_SEGLEN):
                lr, li = lr * lr - li * li, 2.0 * lr * li
            cr = jnp.zeros((1, S5_NS), F32)
            ci = jnp.zeros((1, S5_NS), F32)
            carry_scr[0:1, :] = jnp.zeros((1, 2 * S5_NS), F32)
            for i in range(1, S5_SEG):
                er = st_scr[i - 1:i, 0:S5_NS]
                ei = st_scr[i - 1:i, S5_NS:2 * S5_NS]
                cr, ci = er + lr * cr - li * ci, ei + lr * ci + li * cr
                carry_scr[i:i + 1, 0:S5_NS] = cr
                carry_scr[i:i + 1, S5_NS:2 * S5_NS] = ci

    @pl.when(p == 1)
    def _():
        scan(True)
        for q in range(S5_Q):
            sre = jnp.concatenate([bu_scr[q * S5_QL + c] for c in range(S5_QL)], axis=1).astype(BF16)
            sim = jnp.concatenate([bu_scr[S5_SLABS + q * S5_QL + c] for c in range(S5_QL)], axis=1).astype(BF16)
            y_scr[:, q * S5_QC:(q + 1) * S5_QC] = (
                jnp.dot(sre, cre_ref[q], preferred_element_type=F32)
                + jnp.dot(sim, cim_ref[q], preferred_element_type=F32))
        cs = jnp.concatenate([y_scr[i * S5_PITCH:i * S5_PITCH + S5_KB, :] for i in range(S5_SEG)], axis=0)
        y = jax.nn.gelu(cs + d_ref[...] * u)
        z = jnp.dot(y.astype(BF16), gw_ref[...], preferred_element_type=F32) + gb_ref[...]
        o_ref[0] = _rms(y * jax.nn.sigmoid(z), go_ref[...]).reshape(S5_SEG, S5_KB, SSM_WIDTH)


def _s5_params(lam_re, lam_im, b_re, b_im, c_re, c_im, log_dt):
    lam = lax.complex(lam_re, lam_im)
    lam_bar = jnp.exp(lam * jnp.exp(log_dt)[:, None])
    b_bar = ((lam_bar - 1.0) / lam)[..., None] * lax.complex(b_re, b_im)
    gq = SSM_GROUPS // S5_Q
    eye = jnp.eye(gq, dtype=F32)

    def in_map(t):
        t = t.reshape(S5_Q, gq, SSM_STATE, SSM_CH)
        return jnp.einsum('qgpc,gh->qgchp', t, eye).reshape(S5_Q, S5_QC, S5_QS)

    def out_map(t):
        t = t.reshape(S5_Q, gq, SSM_CH, SSM_STATE)
        return jnp.einsum('qgcp,gh->qgphc', t, eye).reshape(S5_Q, S5_QS, S5_QC)

    wb = jnp.concatenate([in_map(b_bar.real), in_map(b_bar.imag)], axis=-1).astype(BF16)
    return (wb, lam_bar.real.reshape(1, S5_NS), lam_bar.imag.reshape(1, S5_NS),
            out_map(c_re).astype(BF16), out_map(-c_im).astype(BF16))


def _s5(proj4, wb, a_re, a_im, cre, cim, d_skip, glu_w, glu_b, out_gain):
    nblk = S5_SEGLEN // S5_KB
    fix2 = lambda b, p, j: (0, 0)
    fix3 = lambda b, p, j: (0, 0, 0)
    return pl.pallas_call(
        _s5_kernel,
        grid=(BATCH, 2, nblk),
        in_specs=[
            pl.BlockSpec((1, S5_SEG, S5_KB, SSM_WIDTH), lambda b, p, j: (b, 0, j, 0)),
            pl.BlockSpec((S5_Q, S5_QC, 2 * S5_QS), fix3),
            pl.BlockSpec((1, S5_NS), fix2),
            pl.BlockSpec((1, S5_NS), fix2),
            pl.BlockSpec((S5_Q, S5_QS, S5_QC), fix3),
            pl.BlockSpec((S5_Q, S5_QS, S5_QC), fix3),
            pl.BlockSpec((1, SSM_WIDTH), fix2),
            pl.BlockSpec((SSM_WIDTH, SSM_WIDTH), fix2),
            pl.BlockSpec((1, SSM_WIDTH), fix2),
            pl.BlockSpec((1, SSM_WIDTH), fix2),
        ],
        out_specs=pl.BlockSpec((1, S5_SEG, S5_KB, SSM_WIDTH), lambda b, p, j: (b, 0, j * p, 0)),
        out_shape=jax.ShapeDtypeStruct((BATCH, S5_SEG, S5_SEGLEN, SSM_WIDTH), F32),
        scratch_shapes=[
            pltpu.VMEM((2 * S5_SLABS, S5_PROWS, LANES), F32),
            pltpu.VMEM((S5_SEG, 2 * S5_NS), F32),
            pltpu.VMEM((S5_SEG, 2 * S5_NS), F32),
            pltpu.VMEM((S5_PROWS, SSM_WIDTH), F32),
        ],
        compiler_params=pltpu.CompilerParams(
            dimension_semantics=("parallel", "arbitrary", "arbitrary"), vmem_limit_bytes=VMEM_LIMIT),
        name="s5",
    )(proj4, wb, a_re, a_im, cre, cim, d_skip, glu_w, glu_b, out_gain)


def _s5_mixer(proj, p, l):
    wb, a_re, a_im, cre, cim = _s5_params(p['ssm_lambda_re'][l], p['ssm_lambda_im'][l], p['ssm_b_re'][l],
                                          p['ssm_b_im'][l], p['ssm_c_re'][l], p['ssm_c_im'][l],
                                          p['ssm_log_dt'][l])
    y = _s5(proj.reshape(BATCH, S5_SEG, S5_SEGLEN, N_IN_PAD), wb, a_re, a_im, cre, cim,
            p['ssm_d'][l].reshape(1, SSM_WIDTH), p['ssm_glu_w'][l].astype(BF16),
            p['ssm_glu_b'][l].reshape(1, SSM_WIDTH), p['ssm_out_gain'][l].reshape(1, SSM_WIDTH))
    return y.reshape(N_TOK, SSM_WIDTH)


AT_BLK = ATTN_BLOCK
AT_NBRANCH = len(DILATED_PAIRS)
AT_BLOCKS = SEQ // AT_BLK
AT_GROUP = 8
AT_ORDER = tuple(range(1, AT_NBRANCH)) + (0,)
AT_PAD_GROUP = 16
AT_PAD_PITCH = 24
AT_PAD_ROWS = SEQ // AT_PAD_GROUP * AT_PAD_PITCH
assert [d for _, d in DILATED_PAIRS].count(AT_PAD_GROUP) == 1 and DILATED_PAIRS[-1][1] == AT_PAD_GROUP
assert DILATED_PAIRS[0][1] == 1 and all(d > 1 for _, d in DILATED_PAIRS[1:])
assert all(w // d == AT_BLK for w, d in DILATED_PAIRS)
assert all((AT_BLOCKS // d) % AT_GROUP == 0 or AT_GROUP % (AT_BLOCKS // d) == 0 for _, d in DILATED_PAIRS)
AT_Q_COL0 = SSM_WIDTH // LANES
AT_K_COL0 = AT_Q_COL0 + ATTN_HEADS
AT_V_COL0 = AT_K_COL0 + ATTN_HEADS


def _at_padded_row(s):
    return (s // AT_PAD_GROUP) * AT_PAD_PITCH + s % AT_PAD_GROUP


def _attn_kernel(rb_ref, bkt_ref, q_ref, k_ref, v_ref, o_ref, bias_scr, o_scr, lse_scr,
                 qkv_pad, o_pad, lse_pad):
    h = pl.program_id(0)

    def to_padded(i, carry):
        src_rows = pl.ds(pl.multiple_of(i * AT_PAD_GROUP, AT_PAD_GROUP), AT_PAD_GROUP)
        dst_rows = pl.ds(pl.multiple_of(i * AT_PAD_PITCH, 8), AT_PAD_GROUP)
        for j, ref in enumerate((q_ref, k_ref, v_ref)):
            qkv_pad[j, dst_rows, :] = ref[src_rows, :]
        return carry

    lax.fori_loop(0, SEQ // AT_PAD_GROUP, to_padded, 0, unroll=8)

    @pl.when(pl.program_id(1) == 0)
    def _():
        for g in range(AT_NBRANCH):
            bkt = bkt_ref[g]
            bias = jnp.zeros((AT_BLK, 2 * AT_BLK), F32)
            for b in range(N_BUCKETS):
                bias = jnp.where(bkt == b, rb_ref[b, h], bias)
            bias_scr[g] = jnp.where(bkt < 0, NEG_INF, bias)

    lane = lax.broadcasted_iota(jnp.int32, (AT_BLK, 2 * AT_BLK), 1)
    scale = ATTN_HEAD_DIM ** -0.5
    for g in AT_ORDER:
        dil = DILATED_PAIRS[g][1]
        nb = AT_BLOCKS // dil

        def body(it, carry, g=g, dil=dil, nb=nb):
            padded = dil == AT_PAD_GROUP
            if padded:
                qs, ks, vs = qkv_pad.at[0], qkv_pad.at[1], qkv_pad.at[2]
            else:
                qs, ks, vs = q_ref, k_ref, v_ref

            def block_rows(start):
                if dil == 1:
                    return pl.ds(pl.multiple_of(start, AT_BLK), AT_BLK)
                if padded:
                    return pl.ds(_at_padded_row(start), AT_BLK, stride=AT_PAD_PITCH)
                return pl.ds(start, AT_BLK, stride=dil)

            group = range(AT_GROUP)
            ts = [it * AT_GROUP + i for i in group]
            if nb <= AT_GROUP:
                ns = [i % nb for i in group]
            else:
                ns = [None if i == 0 else i for i in group]
            n0 = ts[0] % nb
            rows = [block_rows(t // nb + (t % nb) * (AT_BLK * dil)) for t in ts]
            q = [(qs[r, :] * scale).astype(BF16) for r in rows]
            k_cur = [ks[r, :].astype(BF16) for r in rows]
            v_cur = [vs[r, :].astype(BF16) for r in rows]
            if ns[0] is None:
                prow = block_rows(ts[0] // nb + jnp.maximum(n0 - 1, 0) * (AT_BLK * dil))
                k_lead, v_lead = ks[prow, :].astype(BF16), vs[prow, :].astype(BF16)
            s, vc = [], []
            for i in group:
                if ns[i] == 0:
                    s.append(_dot_nt(q[i], k_cur[i]) + bias_scr[g, :, AT_BLK:])
                    vc.append(v_cur[i])
                    continue
                k_prev, v_prev = (k_lead, v_lead) if ns[i] is None else (k_cur[i - 1], v_cur[i - 1])
                si = _dot_nt(q[i], jnp.concatenate([k_prev, k_cur[i]], axis=0)) + bias_scr[g]
                if ns[i] is None:
                    si = jnp.where(jnp.logical_or(n0 > 0, lane >= AT_BLK), si, NEG_INF)
                s.append(si)
                vc.append(jnp.concatenate([v_prev, v_cur[i]], axis=0))
            m = [jnp.max(si, axis=1, keepdims=True) for si in s]
            p = [jnp.exp(si - mi) for si, mi in zip(s, m)]
            l = [jnp.sum(pi, axis=1, keepdims=True) for pi in p]
            o = [jnp.dot(pi.astype(BF16), vi, preferred_element_type=F32) / li for pi, vi, li in zip(p, vc, l)]
            lse = [jnp.broadcast_to(mi + jnp.log(li), (AT_BLK, LANES)) for mi, li in zip(m, l)]
            if dil > 1:
                for r, oi, li in zip(rows, o, lse):
                    if padded:
                        o_pad[r, :], lse_pad[r, :] = oi, li
                    else:
                        o_scr[g - 1, r, :], lse_scr[g - 1, r, :] = oi, li
                return carry

            def unpad(ref, start):
                base = (start // AT_PAD_GROUP) * AT_PAD_PITCH
                return jnp.concatenate(
                    [ref[pl.ds(pl.multiple_of(base + j * AT_PAD_PITCH, 8), AT_PAD_GROUP), :]
                     for j in range(AT_BLK // AT_PAD_GROUP)], axis=0)

            for t, r, oi, li in zip(ts, rows, o, lse):
                parked = [(unpad(o_pad, t * AT_BLK), unpad(lse_pad, t * AT_BLK))
                          if DILATED_PAIRS[j][1] == AT_PAD_GROUP else (o_scr[j - 1, r, :], lse_scr[j - 1, r, :])
                          for j in range(1, AT_NBRANCH)]
                top = functools.reduce(jnp.maximum, [lj for _, lj in parked], li)
                wi = jnp.exp(li - top)
                num, den = wi * oi, wi
                for oj, lj in parked:
                    wj = jnp.exp(lj - top)
                    num, den = num + wj * oj, den + wj
                o_ref[r, :] = num / den
            return carry

        lax.fori_loop(0, AT_BLOCKS // AT_GROUP, body, 0)


def _attn_bucket_table():
    qi = jnp.arange(AT_BLK)[:, None]
    kj = jnp.arange(2 * AT_BLK)[None, :]
    rel = AT_BLK + qi - kj
    max_exact = N_BUCKETS // 2
    tables = []
    for window, dil in DILATED_PAIRS:
        dist = jnp.maximum(rel, 0) * dil
        d = jnp.maximum(dist, 1).astype(F32)
        large = max_exact + jnp.log(d / max_exact) / math.log(REL_MAX_DIST / max_exact) * (N_BUCKETS - max_exact)
        large = jnp.minimum(large.astype(jnp.int32), N_BUCKETS - 1)
        bucket = jnp.where(dist < max_exact, dist, large)
        tables.append(jnp.where((rel >= 0) & (rel <= window // dil), bucket, -1))
    return jnp.stack(tables).astype(jnp.int32)


def _attn(proj, rel_bias, bucket_table):
    col = lambda c0: pl.BlockSpec((SEQ, LANES), lambda h, b: (b, c0 + h))
    return pl.pallas_call(
        _attn_kernel,
        grid=(ATTN_HEADS, BATCH),
        in_specs=[
            pl.BlockSpec(memory_space=pltpu.SMEM),
            pl.BlockSpec((AT_NBRANCH, AT_BLK, 2 * AT_BLK), lambda h, b: (0, 0, 0)),
            col(AT_Q_COL0), col(AT_K_COL0), col(AT_V_COL0),
        ],
        out_specs=pl.BlockSpec((SEQ, LANES), lambda h, b: (b, h)),
        out_shape=jax.ShapeDtypeStruct((N_TOK, ATTN_WIDTH), F32),
        scratch_shapes=[
            pltpu.VMEM((AT_NBRANCH, AT_BLK, 2 * AT_BLK), F32),
            pltpu.VMEM((AT_NBRANCH - 2, SEQ, LANES), F32),
            pltpu.VMEM((AT_NBRANCH - 2, SEQ, LANES), F32),
            pltpu.VMEM((3, AT_PAD_ROWS, LANES), F32),
            pltpu.VMEM((AT_PAD_ROWS, LANES), F32),
            pltpu.VMEM((AT_PAD_ROWS, LANES), F32),
        ],
        compiler_params=pltpu.CompilerParams(
            dimension_semantics=("parallel", "arbitrary"), vmem_limit_bytes=VMEM_LIMIT),
        name="dilated_attn",
    )(rel_bias, bucket_table, proj, proj, proj)


DN_TILE = 2 * DN_CHUNK
DN_NTILE = SEQ // DN_TILE
DN_BH = BATCH * DN_HEADS
DN_PAD = 8
DN_GROUP = 8
DN_AKT_ROWS = DN_CHUNK + DN_HEAD_DIM
DN_INV_LEVELS = 6
assert 2 ** DN_INV_LEVELS == DN_CHUNK
DN_QKV_COL0 = (SSM_WIDTH + 3 * ATTN_WIDTH) // LANES
DN_Z_COL0 = DN_QKV_COL0 + 3 * DN_HEADS
DN_AB_COL = DN_Z_COL0 + DN_HEADS
HIGHEST = lax.Precision.HIGHEST


def _dot_nt(a, b):
    return lax.dot_general(a, b, (((1,), (1,)), ((), ())), preferred_element_type=F32)


def _split3(x):
    hi = x.astype(BF16)
    r = x - hi.astype(F32)
    mid = r.astype(BF16)
    return hi, mid, (r - mid.astype(F32)).astype(BF16)


def _dot_exact_lhs(x, m16):
    return sum(jnp.dot(piece, m16, preferred_element_type=F32) for piece in _split3(x))


def _dot_exact_rhs(m16, x):
    return sum(jnp.dot(m16, piece, preferred_element_type=F32) for piece in _split3(x))


def _dn_tile_masks():
    row = lax.broadcasted_iota(jnp.int32, (DN_TILE, DN_TILE), 0)
    col = lax.broadcasted_iota(jnp.int32, (DN_TILE, DN_TILE), 1)
    causal = jnp.logical_and((row // DN_CHUNK) == (col // DN_CHUNK), row >= col)
    return row, col, causal


DN_GATE_ROWS = 1024


def _dn_gates_kernel(ab_ref, alog_ref, dtb_ref, o_ref):
    _, col, causal = _dn_tile_masks()
    cumsum_mat = causal.astype(BF16)
    neg_a = -jnp.exp(alog_ref[...])
    for t in range(DN_GATE_ROWS // DN_TILE):
        rows = slice(t * DN_TILE, (t + 1) * DN_TILE)
        ab = ab_ref[rows, :]
        x = ab + dtb_ref[...]
        g = neg_a * (jnp.maximum(x, 0.0) + jnp.log1p(jnp.exp(-jnp.abs(x))))
        gc = _dot_exact_rhs(cumsum_mat, g)
        o_ref[rows, :] = jnp.where(col < DN_HEADS, gc, jax.nn.sigmoid(ab))


def _dn_gates(proj, a_log, dt_bias):
    lanes = lambda v: jnp.pad(v, (0, LANES - DN_HEADS)).reshape(1, LANES)
    return pl.pallas_call(
        _dn_gates_kernel,
        grid=(N_TOK // DN_GATE_ROWS,),
        in_specs=[
            pl.BlockSpec((DN_GATE_ROWS, LANES), lambda i: (i, DN_AB_COL)),
            pl.BlockSpec((1, LANES), lambda i: (0, 0)),
            pl.BlockSpec((1, LANES), lambda i: (0, 0)),
        ],
        out_specs=pl.BlockSpec((DN_GATE_ROWS, LANES), lambda i: (i, 0)),
        out_shape=jax.ShapeDtypeStruct((N_TOK, LANES), F32),
        compiler_params=pltpu.CompilerParams(dimension_semantics=("parallel",), vmem_limit_bytes=VMEM_LIMIT),
        name="dn_gates",
    )(proj, lanes(a_log), lanes(dt_bias))


def _dn_prep_kernel(q_ref, k_ref, v_ref, z_ref, gate_ref, wq_ref, wk_ref, wv_ref,
                    u_o, wq_o, akt_o, dec_o, sz_o, qp, kp, vp):
    h = pl.program_id(1)
    for src, dst in ((q_ref, qp), (k_ref, kp), (v_ref, vp)):
        dst[0:DN_PAD, :] = jnp.zeros((DN_PAD, LANES), F32)
        dst[DN_PAD:DN_PAD + SEQ, :] = src[...]

    row, col, causal = _dn_tile_masks()
    pick = jnp.concatenate([row == h, row == h + DN_HEADS], axis=1).astype(BF16)
    eye = (row == col).astype(F32)
    pair_masks = [
        jnp.logical_and(jnp.logical_and((row // (2 * s)) == (col // (2 * s)), (row // s) % 2 == 1),
                        (col // s) % 2 == 0)
        for s in (2 ** i for i in range(DN_INV_LEVELS))]
    scale = DN_HEAD_DIM ** -0.5

    def conv_silu(pad_ref, w_ref, base):
        acc = None
        for j in range(DN_CONV):
            sh = DN_PAD - (DN_CONV - 1) + j
            term = w_ref[j:j + 1, :] * pad_ref[pl.ds(base + sh, DN_TILE), :]
            acc = term if acc is None else acc + term
        return acc * jax.nn.sigmoid(acc)

    def l2n(x):
        return x * lax.rsqrt(jnp.sum(x * x, axis=1, keepdims=True) + NORM_EPS)

    def mm16(a, b):
        return jnp.dot(a.astype(BF16), b.astype(BF16), preferred_element_type=F32)

    def body(it, carry):
        tiles = [it * DN_GROUP + i for i in range(DN_GROUP)]
        bases = [pl.multiple_of(t * DN_TILE, DN_TILE) for t in tiles]
        q = [l2n(conv_silu(qp, wq_ref, b)) * scale for b in bases]
        k = [l2n(conv_silu(kp, wk_ref, b)) for b in bases]
        v = [conv_silu(vp, wv_ref, b) for b in bases]
        gate_rep = [_dot_exact_lhs(gate_ref[pl.ds(b, DN_TILE), :], pick) for b in bases]
        gc = [r[:, :DN_TILE] for r in gate_rep]
        beta = [r[:, DN_TILE:] for r in gate_rep]
        decay = [jnp.exp(jnp.where(causal, c - c.T, NEG_INF)) for c in gc]
        kb = [ki * bi for ki, bi in zip(k, beta)]
        k16 = [ki.astype(BF16) for ki in k]
        a_mat = [_dot_nt(kbi.astype(BF16), ki) * di for kbi, ki, di in zip(kb, k16, decay)]
        t_inv = [eye - jnp.where(pair_masks[0], am, 0.0) for am in a_mat]
        for mask in pair_masks[1:]:
            low = [jnp.where(mask, am, 0.0).astype(BF16) for am in a_mat]
            t_inv = [ti - mm16(mm16(ti, lo), ti) for ti, lo in zip(t_inv, low)]
        egc = [jnp.exp(c) for c in gc]
        uw = [mm16(ti, jnp.concatenate([vi * bi, kbi * ei], axis=1))
              for ti, vi, bi, kbi, ei in zip(t_inv, v, beta, kb, egc)]
        at = [(_dot_nt(qi.astype(BF16), ki) * di).astype(BF16) for qi, ki, di in zip(q, k16, decay)]
        qd = [(qi * ei).astype(BF16) for qi, ei in zip(q, egc)]
        c = DN_CHUNK
        for i, t in enumerate(tiles):
            rows = pl.ds(bases[i], DN_TILE)
            u_o[t] = uw[i][:, :DN_HEAD_DIM]
            w = uw[i][:, DN_HEAD_DIM:].astype(BF16)
            wq_o[t] = jnp.concatenate([w[:c], qd[i][:c], w[c:], qd[i][c:]], axis=0)
            gc_first, gc_second = gc[i][c - 1:c, :], gc[i][DN_TILE - 1:DN_TILE, :]
            gc_last = jnp.where(row < c, gc_first, gc_second)
            kt = (k[i] * jnp.exp(gc_last - gc[i])).T
            akt_o[t] = jnp.concatenate([at[i][:c] + at[i][c:], kt.astype(BF16)], axis=0)
            dec_o[t] = jnp.concatenate(
                [jnp.exp(gc_first), jnp.exp(gc_second), jnp.zeros((DN_PAD - 2, LANES), F32)], axis=0)
            z = z_ref[rows, :]
            sz_o[t] = (z * jax.nn.sigmoid(z)).astype(BF16)
        return carry

    lax.fori_loop(0, DN_NTILE // DN_GROUP, body, 0)


def _dn_prep(proj, gates, conv_w):
    col = lambda c0: pl.BlockSpec((SEQ, LANES), lambda b, h: (b, c0 + h))
    cw = lambda c0: pl.BlockSpec((DN_CONV, LANES), lambda b, h: (0, c0 + h))
    bh_rows = lambda n: pl.BlockSpec((DN_NTILE, None, n, LANES), lambda b, h: (0, b * DN_HEADS + h, 0, 0))
    rows_shape = lambda n, dt: jax.ShapeDtypeStruct((DN_NTILE, DN_BH, n, LANES), dt)
    return pl.pallas_call(
        _dn_prep_kernel,
        grid=(BATCH, DN_HEADS),
        in_specs=[
            col(DN_QKV_COL0), col(DN_QKV_COL0 + DN_HEADS), col(DN_QKV_COL0 + 2 * DN_HEADS), col(DN_Z_COL0),
            pl.BlockSpec((SEQ, LANES), lambda b, h: (b, 0)),
            cw(0), cw(DN_HEADS), cw(2 * DN_HEADS),
        ],
        out_specs=[bh_rows(DN_TILE), bh_rows(2 * DN_TILE), bh_rows(DN_AKT_ROWS), bh_rows(DN_PAD), bh_rows(DN_TILE)],
        out_shape=[
            rows_shape(DN_TILE, F32), rows_shape(2 * DN_TILE, BF16), rows_shape(DN_AKT_ROWS, BF16),
            rows_shape(DN_PAD, F32), rows_shape(DN_TILE, BF16),
        ],
        scratch_shapes=[pltpu.VMEM((SEQ + DN_PAD, LANES), F32)] * 3,
        compiler_params=pltpu.CompilerParams(
            dimension_semantics=("parallel", "parallel"), vmem_limit_bytes=VMEM_LIMIT),
        name="dn_prep",
    )(proj, proj, proj, proj, gates, conv_w, conv_w, conv_w)


def _dn_scan_kernel(u_ref, wq_ref, akt_ref, dec_ref, sz_ref, gain_ref, o_ref, s_scr):
    @pl.when(pl.program_id(0) == 0)
    def _():
        s_scr[...] = jnp.zeros_like(s_scr)

    gain = gain_ref[...]
    c = DN_CHUNK
    chains = range(DN_BH)
    second_chunk_lane = lax.broadcasted_iota(jnp.int32, (DN_AKT_ROWS, LANES), 1) >= c
    state = [s_scr[bh] for bh in chains]
    v_first = None
    for j in range(2):
        rows = slice(j * c, (j + 1) * c)
        s16 = [s.astype(BF16) for s in state]
        ws_qs = [jnp.dot(wq_ref[bh, j * DN_TILE:(j + 1) * DN_TILE, :], s16[bh], preferred_element_type=F32)
                 for bh in chains]
        v_new = [u_ref[bh, rows, :] - ws_qs[bh][:c] for bh in chains]
        if j == 0:
            v_pair = [jnp.concatenate([v, jnp.zeros_like(v)], axis=0).astype(BF16) for v in v_new]
            v_first = v_new
            akt = [akt_ref[bh] for bh in chains]
        else:
            v_pair = [jnp.concatenate([v0, v], axis=0).astype(BF16) for v0, v in zip(v_first, v_new)]
            akt = [jnp.where(second_chunk_lane, akt_ref[bh], jnp.zeros((DN_AKT_ROWS, LANES), BF16))
                   for bh in chains]
        ov_kv = [jnp.dot(akt[bh], v_pair[bh], preferred_element_type=F32) for bh in chains]
        state = [state[bh] * dec_ref[bh, j:j + 1, :] + ov_kv[bh][c:] for bh in chains]
        for bh in chains:
            b, h = divmod(bh, DN_HEADS)
            o = ws_qs[bh][c:] + ov_kv[bh][:c]
            o_ref[b, rows, h * DN_HEAD_DIM:(h + 1) * DN_HEAD_DIM] = (
                _rms(o, gain) * sz_ref[bh, rows, :].astype(F32))
    for bh in chains:
        s_scr[bh] = state[bh]


def _dn_scan(u, wq, akt, dec, sz, gain):
    rows = lambda n: pl.BlockSpec((None, DN_BH, n, LANES), lambda t: (t, 0, 0, 0))
    return pl.pallas_call(
        _dn_scan_kernel,
        grid=(DN_NTILE,),
        in_specs=[rows(DN_TILE), rows(2 * DN_TILE), rows(DN_AKT_ROWS), rows(DN_PAD), rows(DN_TILE),
                  pl.BlockSpec((1, LANES), lambda t: (0, 0))],
        out_specs=pl.BlockSpec((BATCH, DN_TILE, DN_WIDTH), lambda t: (0, t, 0)),
        out_shape=jax.ShapeDtypeStruct((BATCH, SEQ, DN_WIDTH), F32),
        scratch_shapes=[pltpu.VMEM((DN_BH, DN_HEAD_DIM, DN_HEAD_DIM), F32)],
        compiler_params=pltpu.CompilerParams(
            dimension_semantics=("arbitrary",), vmem_limit_bytes=VMEM_LIMIT),
        name="dn_scan",
    )(u, wq, akt, dec, sz, gain)


def _dn_mixer(proj, p, l):
    gates = _dn_gates(proj, p['dn_a_log'][l], p['dn_dt_bias'][l])
    outs = _dn_prep(proj, gates, p['dn_conv_w'][l])
    y = _dn_scan(*outs, p['dn_norm_gain'][l].reshape(1, DN_HEAD_DIM))
    return y.reshape(N_TOK, DN_WIDTH)


def kernel(x, norm_gains, ffn_w_gate, ffn_w_up, ffn_w_down, w_in, w_out, ssm_lambda_re, ssm_lambda_im,
           ssm_b_re, ssm_b_im, ssm_c_re, ssm_c_im, ssm_d, ssm_log_dt, ssm_glu_w, ssm_glu_b, ssm_out_gain,
           dn_conv_w, dn_a_log, dn_dt_bias, dn_norm_gain, attn_out_gain, rel_bias):
    p = dict(ssm_lambda_re=ssm_lambda_re, ssm_lambda_im=ssm_lambda_im, ssm_b_re=ssm_b_re, ssm_b_im=ssm_b_im,
             ssm_c_re=ssm_c_re, ssm_c_im=ssm_c_im, ssm_d=ssm_d, ssm_log_dt=ssm_log_dt, ssm_glu_w=ssm_glu_w,
             ssm_glu_b=ssm_glu_b, ssm_out_gain=ssm_out_gain, dn_conv_w=dn_conv_w, dn_a_log=dn_a_log,
             dn_dt_bias=dn_dt_bias, dn_norm_gain=dn_norm_gain)
    wg, wu, wd = ffn_w_gate, ffn_w_up, ffn_w_down
    w_in_p = jnp.pad(w_in, ((0, 0), (0, 0), (0, N_IN_PAD - N_IN_COLS))).astype(BF16)
    w_out_b = w_out
    bucket_table = _attn_bucket_table()
    gains = norm_gains.reshape(DEPTH, 6, 1, D_MODEL)
    x = x.reshape(N_TOK, D_MODEL)
    for l in range(DEPTH):
        x = _ffn(x, gains, wg, wu, wd, l, 0)
        proj = _inproj(x, gains, w_in_p, l)
        y_ssm = _s5_mixer(proj, p, l)
        y_dn = _dn_mixer(proj, p, l)
        o_at = _attn(proj, rel_bias, bucket_table)
        x = _outproj(x, y_ssm, y_dn, o_at, attn_out_gain[l].reshape(1, ATTN_WIDTH), gains, w_out_b, l)
        x = _ffn(x, gains, wg, wu, wd, l, 1)
    return x.reshape(BATCH, SEQ, D_MODEL)
```

```python
import functools
import math

import jax
import jax.numpy as jnp
import numpy as np
from jax import lax
from jax.experimental import pallas as pl
from jax.experimental.pallas import tpu as pltpu

D_MODEL = 2048
BATCH = 2
SEQ = 4096
DEPTH = 4
SSM_GROUPS = 32
SSM_CH = 16
SSM_STATE = 64
SSM_WIDTH = SSM_GROUPS * SSM_CH
DN_HEADS = 6
DN_HEAD_DIM = 128
DN_WIDTH = DN_HEADS * DN_HEAD_DIM
DN_CONV = 4
DN_CHUNK = 64
ATTN_HEADS = 6
ATTN_HEAD_DIM = 128
ATTN_WIDTH = ATTN_HEADS * ATTN_HEAD_DIM
DILATED_PAIRS = ((128, 1), (512, 4), (2048, 16))
ATTN_BLOCK = 128
N_BUCKETS = 32
REL_MAX_DIST = 2048
D_MIX = SSM_WIDTH + DN_WIDTH + ATTN_WIDTH
IN_SPLITS = (SSM_WIDTH, ATTN_WIDTH, ATTN_WIDTH, ATTN_WIDTH, 3 * DN_WIDTH, DN_WIDTH, DN_HEADS, DN_HEADS)
N_IN_COLS = sum(IN_SPLITS)
D_FF = 5632
NORM_EPS = 1e-6
NEG_INF = -1e30

LANES = 128
N_IN_PAD = 6144
V7X_VMEM_BYTES = 64 * 1024 * 1024
VMEM_LIMIT = V7X_VMEM_BYTES - 4 * 1024 * 1024
N_TOK = BATCH * SEQ

BF16 = jnp.bfloat16
F32 = jnp.float32


def _rms(x, gain):
    return x * lax.rsqrt(jnp.mean(x * x, axis=-1, keepdims=True) + NORM_EPS) * gain


FFN_TM = 1024
FFN_TF = 256


def _ffn_kernel(x_ref, gpre_ref, gpost_ref, wg_ref, wu_ref, wd_ref, o_ref, h_scr):
    f = pl.program_id(1)
    last = pl.num_programs(1) - 1

    def down_proj(h):
        gate = jnp.dot(h, wg_ref[...].astype(BF16), preferred_element_type=F32)
        up = jnp.dot(h, wu_ref[...].astype(BF16), preferred_element_type=F32)
        act = (gate * jax.nn.sigmoid(gate) * up).astype(BF16)
        return jnp.dot(act, wd_ref[...].astype(BF16), preferred_element_type=F32)

    @pl.when(f == 0)
    def _():
        h = _rms(x_ref[...], gpre_ref[...]).astype(BF16)
        h_scr[...] = h
        o_ref[...] = down_proj(h)

    @pl.when(jnp.logical_and(f > 0, f < last))
    def _():
        o_ref[...] += down_proj(h_scr[...])

    @pl.when(f == last)
    def _():
        y = o_ref[...] + down_proj(h_scr[...])
        o_ref[...] = x_ref[...] + _rms(y, 0.5 * gpost_ref[...])


def _ffn(x, gains, wg, wu, wd, layer, half):
    return pl.pallas_call(
        _ffn_kernel,
        grid=(N_TOK // FFN_TM, D_FF // FFN_TF),
        in_specs=[
            pl.BlockSpec((FFN_TM, D_MODEL), lambda i, f: (i, 0)),
            pl.BlockSpec((None, None, 1, D_MODEL), lambda i, f: (layer, 4 * half, 0, 0)),
            pl.BlockSpec((None, None, 1, D_MODEL), lambda i, f: (layer, 4 * half + 1, 0, 0)),
            pl.BlockSpec((None, None, D_MODEL, FFN_TF), lambda i, f: (layer, half, 0, f)),
            pl.BlockSpec((None, None, D_MODEL, FFN_TF), lambda i, f: (layer, half, 0, f)),
            pl.BlockSpec((None, None, FFN_TF, D_MODEL), lambda i, f: (layer, half, f, 0)),
        ],
        out_specs=pl.BlockSpec((FFN_TM, D_MODEL), lambda i, f: (i, 0)),
        out_shape=jax.ShapeDtypeStruct((N_TOK, D_MODEL), F32),
        scratch_shapes=[pltpu.VMEM((FFN_TM, D_MODEL), BF16)],
        compiler_params=pltpu.CompilerParams(
            dimension_semantics=("parallel", "arbitrary"), vmem_limit_bytes=VMEM_LIMIT),
        name="ffn",
    )(x, gains, gains, wg, wu, wd)


INP_TM = 1024
INP_TN = 1536


def _inproj_kernel(x_ref, g_ref, w_ref, o_ref, h_scr):
    @pl.when(pl.program_id(1) == 0)
    def _():
        h = _rms(x_ref[...], g_ref[...]).astype(BF16)
        h_scr[...] = h
        o_ref[...] = jnp.dot(h, w_ref[...], preferred_element_type=F32)

    @pl.when(pl.program_id(1) > 0)
    def _():
        o_ref[...] = jnp.dot(h_scr[...], w_ref[...], preferred_element_type=F32)


def _inproj(x, gains, w, layer):
    return pl.pallas_call(
        _inproj_kernel,
        grid=(N_TOK // INP_TM, N_IN_PAD // INP_TN),
        in_specs=[
            pl.BlockSpec((INP_TM, D_MODEL), lambda i, n: (i, 0)),
            pl.BlockSpec((None, None, 1, D_MODEL), lambda i, n: (layer, 2, 0, 0)),
            pl.BlockSpec((None, D_MODEL, INP_TN), lambda i, n: (layer, 0, n)),
        ],
        out_specs=pl.BlockSpec((INP_TM, INP_TN), lambda i, n: (i, n)),
        out_shape=jax.ShapeDtypeStruct((N_TOK, N_IN_PAD), F32),
        scratch_shapes=[pltpu.VMEM((INP_TM, D_MODEL), BF16)],
        compiler_params=pltpu.CompilerParams(
            dimension_semantics=("parallel", "arbitrary"), vmem_limit_bytes=VMEM_LIMIT),
        name="inproj",
    )(x, gains, w)


OUT_TM = 512
OUT_SUB = 256


def _outproj_kernel(x_ref, ys_ref, yd_ref, oa_ref, ga_ref, gpost_ref, w_ref, o_ref, w16):
    @pl.when(pl.program_id(0) == 0)
    def _():
        w16[...] = w_ref[...].astype(BF16)

    halves = [slice(i * OUT_SUB, (i + 1) * OUT_SUB) for i in range(OUT_TM // OUT_SUB)]
    ya = [_rms(oa_ref[r, :], ga_ref[...]).astype(BF16) for r in halves]
    mix = [jnp.dot(ys_ref[r, :].astype(BF16), w16[0:SSM_WIDTH, :], preferred_element_type=F32) for r in halves]
    mix = [m + jnp.dot(yd_ref[r, :].astype(BF16), w16[SSM_WIDTH:SSM_WIDTH + DN_WIDTH, :],
                       preferred_element_type=F32) for m, r in zip(mix, halves)]
    mix = [m + jnp.dot(a, w16[SSM_WIDTH + DN_WIDTH:D_MIX, :], preferred_element_type=F32)
           for m, a in zip(mix, ya)]
    for r, m in zip(halves, mix):
        o_ref[r, :] = x_ref[r, :] + _rms(m, gpost_ref[...])


def _outproj(x, y_ssm, y_dn, o_at, g_attn, gains, w, layer):
    row = lambda i: (i, 0)
    fixed = lambda i: (0, 0)
    return pl.pallas_call(
        _outproj_kernel,
        grid=(N_TOK // OUT_TM,),
        in_specs=[
            pl.BlockSpec((OUT_TM, D_MODEL), row),
            pl.BlockSpec((OUT_TM, SSM_WIDTH), row),
            pl.BlockSpec((OUT_TM, DN_WIDTH), row),
            pl.BlockSpec((OUT_TM, ATTN_WIDTH), row),
            pl.BlockSpec((1, ATTN_WIDTH), fixed),
            pl.BlockSpec((None, None, 1, D_MODEL), lambda i: (layer, 3, 0, 0)),
            pl.BlockSpec((None, D_MIX, D_MODEL), lambda i: (layer, 0, 0), pipeline_mode=pl.Buffered(1)),
        ],
        out_specs=pl.BlockSpec((OUT_TM, D_MODEL), row),
        out_shape=jax.ShapeDtypeStruct((N_TOK, D_MODEL), F32),
        scratch_shapes=[pltpu.VMEM((D_MIX, D_MODEL), BF16)],
        compiler_params=pltpu.CompilerParams(
            dimension_semantics=("arbitrary",), vmem_limit_bytes=VMEM_LIMIT),
        name="outproj",
    )(x, y_ssm, y_dn, o_at, g_attn, gains, w)


S5_SEG = 8
S5_SEGLEN = SEQ // S5_SEG
S5_KB = 128
S5_ROWS = S5_KB * S5_SEG
S5_PITCH = S5_KB + 8
S5_PROWS = S5_PITCH * S5_SEG
S5_NS = SSM_GROUPS * SSM_STATE
S5_SLABS = S5_NS // LANES
S5_Q = 4
S5_QS = S5_NS // S5_Q
S5_QL = S5_QS // LANES
S5_QC = SSM_WIDTH // S5_Q
S5_LOG2_SEGLEN = 9
assert 1 << S5_LOG2_SEGLEN == S5_SEGLEN


def _s5_kernel(u_ref, wb_ref, are_ref, aim_ref, cre_ref, cim_ref, d_ref, gw_ref, gb_ref, go_ref, o_ref,
               bu_scr, st_scr, carry_scr, y_scr):
    p = pl.program_id(1)
    j = pl.program_id(2)
    u = u_ref[0].reshape(S5_ROWS, SSM_WIDTH)
    ub = u.astype(BF16)

    @pl.when(jnp.logical_and(p == 0, j == 0))
    def _():
        bu_scr[...] = jnp.zeros_like(bu_scr)
        st_scr[...] = jnp.zeros_like(st_scr)

    @pl.when(jnp.logical_and(p == 1, j == 0))
    def _():
        st_scr[...] = carry_scr[...]

    for q in range(S5_Q):
        r = jnp.dot(ub[:, q * S5_QC:(q + 1) * S5_QC], wb_ref[q], preferred_element_type=F32)
        for half in range(2):
            for c in range(S5_QL):
                slab = half * S5_SLABS + q * S5_QL + c
                lanes = slice((half * S5_QL + c) * LANES, (half * S5_QL + c + 1) * LANES)
                for i in range(S5_SEG):
                    bu_scr[slab, i * S5_PITCH:i * S5_PITCH + S5_KB, :] = r[i * S5_KB:(i + 1) * S5_KB, lanes]

    def scan(store):
        for q in range(S5_Q):
            slabs = [q * S5_QL + c for c in range(S5_QL)]
            lanes = [slice(s * LANES, (s + 1) * LANES) for s in slabs]
            ilanes = [slice(S5_NS + s * LANES, S5_NS + (s + 1) * LANES) for s in slabs]
            ar = [jnp.broadcast_to(are_ref[:, ln], (S5_SEG, LANES)) for ln in lanes]
            ai = [jnp.broadcast_to(aim_ref[:, ln], (S5_SEG, LANES)) for ln in lanes]

            def body(k, carry):
                sr, si = carry
                rows = pl.ds(k, S5_SEG, stride=S5_PITCH)
                nr = tuple(ar[c] * sr[c] - ai[c] * si[c] + bu_scr[slabs[c], rows, :] for c in range(S5_QL))
                ni = tuple(ar[c] * si[c] + ai[c] * sr[c] + bu_scr[slabs[c] + S5_SLABS, rows, :]
                           for c in range(S5_QL))
                if store:
                    for c in range(S5_QL):
                        bu_scr[slabs[c], rows, :] = nr[c]
                        bu_scr[slabs[c] + S5_SLABS, rows, :] = ni[c]
                return nr, ni

            init = (tuple(st_scr[:, ln] for ln in lanes), tuple(st_scr[:, ln] for ln in ilanes))
            sr, si = lax.fori_loop(0, S5_KB, body, init, unroll=8)
            for c in range(S5_QL):
                st_scr[:, lanes[c]] = sr[c]
                st_scr[:, ilanes[c]] = si[c]

    @pl.when(p == 0)
    def _():
        scan(False)

        @pl.when(j == pl.num_programs(2) - 1)
        def _():
            lr, li = are_ref[...], aim_ref[...]
            for _ in range(S5_LOG2_SEGLEN):
                lr, li = lr * lr - li * li, 2.0 * lr * li
            cr = jnp.zeros((1, S5_NS), F32)
            ci = jnp.zeros((1, S5_NS), F32)
            carry_scr[0:1, :] = jnp.zeros((1, 2 * S5_NS), F32)
            for i in range(1, S5_SEG):
                er = st_scr[i - 1:i, 0:S5_NS]
                ei = st_scr[i - 1:i, S5_NS:2 * S5_NS]
                cr, ci = er + lr * cr - li * ci, ei + lr * ci + li * cr
                carry_scr[i:i + 1, 0:S5_NS] = cr
                carry_scr[i:i + 1, S5_NS:2 * S5_NS] = ci

    @pl.when(p == 1)
    def _():
        scan(True)
        for q in range(S5_Q):
            sre = jnp.concatenate([bu_scr[q * S5_QL + c] for c in range(S5_QL)], axis=1).astype(BF16)
            sim = jnp.concatenate([bu_scr[S5_SLABS + q * S5_QL + c] for c in range(S5_QL)], axis=1).astype(BF16)
            y_scr[:, q * S5_QC:(q + 1) * S5_QC] = (
                jnp.dot(sre, cre_ref[q], preferred_element_type=F32)
                + jnp.dot(sim, cim_ref[q], preferred_element_type=F32))
        cs = jnp.concatenate([y_scr[i * S5_PITCH:i * S5_PITCH + S5_KB, :] for i in range(S5_SEG)], axis=0)
        y = jax.nn.gelu(cs + d_ref[...] * u)
        z = jnp.dot(y.astype(BF16), gw_ref[...], preferred_element_type=F32) + gb_ref[...]
        o_ref[0] = _rms(y * jax.nn.sigmoid(z), go_ref[...]).reshape(S5_SEG, S5_KB, SSM_WIDTH)


def _s5_params(lam_re, lam_im, b_re, b_im, c_re, c_im, log_dt):
    lam = lax.complex(lam_re, lam_im)
    lam_bar = jnp.exp(lam * jnp.exp(log_dt)[:, None])
    b_bar = ((lam_bar - 1.0) / lam)[..., None] * lax.complex(b_re, b_im)
    gq = SSM_GROUPS // S5_Q
    eye = jnp.eye(gq, dtype=F32)

    def in_map(t):
        t = t.reshape(S5_Q, gq, SSM_STATE, SSM_CH)
        return jnp.einsum('qgpc,gh->qgchp', t, eye).reshape(S5_Q, S5_QC, S5_QS)

    def out_map(t):
        t = t.reshape(S5_Q, gq, SSM_CH, SSM_STATE)
        return jnp.einsum('qgcp,gh->qgphc', t, eye).reshape(S5_Q, S5_QS, S5_QC)

    wb = jnp.concatenate([in_map(b_bar.real), in_map(b_bar.imag)], axis=-1).astype(BF16)
    return (wb, lam_bar.real.reshape(1, S5_NS), lam_bar.imag.reshape(1, S5_NS),
            out_map(c_re).astype(BF16), out_map(-c_im).astype(BF16))


def _s5(proj4, wb, a_re, a_im, cre, cim, d_skip, glu_w, glu_b, out_gain):
    nblk = S5_SEGLEN // S5_KB
    fix2 = lambda b, p, j: (0, 0)
    fix3 = lambda b, p, j: (0, 0, 0)
    return pl.pallas_call(
        _s5_kernel,
        grid=(BATCH, 2, nblk),
        in_specs=[
            pl.BlockSpec((1, S5_SEG, S5_KB, SSM_WIDTH), lambda b, p, j: (b, 0, j, 0)),
            pl.BlockSpec((S5_Q, S5_QC, 2 * S5_QS), fix3),
            pl.BlockSpec((1, S5_NS), fix2),
            pl.BlockSpec((1, S5_NS), fix2),
            pl.BlockSpec((S5_Q, S5_QS, S5_QC), fix3),
            pl.BlockSpec((S5_Q, S5_QS, S5_QC), fix3),
            pl.BlockSpec((1, SSM_WIDTH), fix2),
            pl.BlockSpec((SSM_WIDTH, SSM_WIDTH), fix2),
            pl.BlockSpec((1, SSM_WIDTH), fix2),
            pl.BlockSpec((1, SSM_WIDTH), fix2),
        ],
        out_specs=pl.BlockSpec((1, S5_SEG, S5_KB, SSM_WIDTH), lambda b, p, j: (b, 0, j * p, 0)),
        out_shape=jax.ShapeDtypeStruct((BATCH, S5_SEG, S5_SEGLEN, SSM_WIDTH), F32),
        scratch_shapes=[
            pltpu.VMEM((2 * S5_SLABS, S5_PROWS, LANES), F32),
            pltpu.VMEM((S5_SEG, 2 * S5_NS), F32),
            pltpu.VMEM((S5_SEG, 2 * S5_NS), F32),
            pltpu.VMEM((S5_PROWS, SSM_WIDTH), F32),
        ],
        compiler_params=pltpu.CompilerParams(
            dimension_semantics=("parallel", "arbitrary", "arbitrary"), vmem_limit_bytes=VMEM_LIMIT),
        name="s5",
    )(proj4, wb, a_re, a_im, cre, cim, d_skip, glu_w, glu_b, out_gain)


def _s5_mixer(proj, p, l):
    wb, a_re, a_im, cre, cim = _s5_params(p['ssm_lambda_re'][l], p['ssm_lambda_im'][l], p['ssm_b_re'][l],
                                          p['ssm_b_im'][l], p['ssm_c_re'][l], p['ssm_c_im'][l],
                                          p['ssm_log_dt'][l])
    y = _s5(proj.reshape(BATCH, S5_SEG, S5_SEGLEN, N_IN_PAD), wb, a_re, a_im, cre, cim,
            p['ssm_d'][l].reshape(1, SSM_WIDTH), p['ssm_glu_w'][l].astype(BF16),
            p['ssm_glu_b'][l].reshape(1, SSM_WIDTH), p['ssm_out_gain'][l].reshape(1, SSM_WIDTH))
    return y.reshape(N_TOK, SSM_WIDTH)


AT_BLK = ATTN_BLOCK
AT_NBRANCH = len(DILATED_PAIRS)
AT_BLOCKS = SEQ // AT_BLK
AT_GROUP = 8
AT_ORDER = tuple(range(1, AT_NBRANCH)) + (0,)
AT_PAD_GROUP = 16
AT_PAD_PITCH = 24
AT_PAD_ROWS = SEQ // AT_PAD_GROUP * AT_PAD_PITCH
assert [d for _, d in DILATED_PAIRS].count(AT_PAD_GROUP) == 1 and DILATED_PAIRS[-1][1] == AT_PAD_GROUP
assert DILATED_PAIRS[0][1] == 1 and all(d > 1 for _, d in DILATED_PAIRS[1:])
assert all(w // d == AT_BLK for w, d in DILATED_PAIRS)
assert all((AT_BLOCKS // d) % AT_GROUP == 0 or AT_GROUP % (AT_BLOCKS // d) == 0 for _, d in DILATED_PAIRS)
AT_Q_COL0 = SSM_WIDTH // LANES
AT_K_COL0 = AT_Q_COL0 + ATTN_HEADS
AT_V_COL0 = AT_K_COL0 + ATTN_HEADS


def _at_padded_row(s):
    return (s // AT_PAD_GROUP) * AT_PAD_PITCH + s % AT_PAD_GROUP


def _attn_kernel(rb_ref, bkt_ref, q_ref, k_ref, v_ref, o_ref, bias_scr, o_scr, lse_scr,
                 qkv_pad, o_pad, lse_pad):
    h = pl.program_id(0)

    def to_padded(i, carry):
        src_rows = pl.ds(pl.multiple_of(i * AT_PAD_GROUP, AT_PAD_GROUP), AT_PAD_GROUP)
        dst_rows = pl.ds(pl.multiple_of(i * AT_PAD_PITCH, 8), AT_PAD_GROUP)
        for j, ref in enumerate((q_ref, k_ref, v_ref)):
            qkv_pad[j, dst_rows, :] = ref[src_rows, :]
        return carry

    lax.fori_loop(0, SEQ // AT_PAD_GROUP, to_padded, 0, unroll=8)

    @pl.when(pl.program_id(1) == 0)
    def _():
        for g in range(AT_NBRANCH):
            bkt = bkt_ref[g]
            bias = jnp.zeros((AT_BLK, 2 * AT_BLK), F32)
            for b in range(N_BUCKETS):
                bias = jnp.where(bkt == b, rb_ref[b, h], bias)
            bias_scr[g] = jnp.where(bkt < 0, NEG_INF, bias)

    lane = lax.broadcasted_iota(jnp.int32, (AT_BLK, 2 * AT_BLK), 1)
    scale = ATTN_HEAD_DIM ** -0.5
    for g in AT_ORDER:
        dil = DILATED_PAIRS[g][1]
        nb = AT_BLOCKS // dil

        def body(it, carry, g=g, dil=dil, nb=nb):
            padded = dil == AT_PAD_GROUP
            if padded:
                qs, ks, vs = qkv_pad.at[0], qkv_pad.at[1], qkv_pad.at[2]
            else:
                qs, ks, vs = q_ref, k_ref, v_ref

            def block_rows(start):
                if dil == 1:
                    return pl.ds(pl.multiple_of(start, AT_BLK), AT_BLK)
                if padded:
                    return pl.ds(_at_padded_row(start), AT_BLK, stride=AT_PAD_PITCH)
                return pl.ds(start, AT_BLK, stride=dil)

            group = range(AT_GROUP)
            ts = [it * AT_GROUP + i for i in group]
            if nb <= AT_GROUP:
                ns = [i % nb for i in group]
            else:
                ns = [None if i == 0 else i for i in group]
            n0 = ts[0] % nb
            rows = [block_rows(t // nb + (t % nb) * (AT_BLK * dil)) for t in ts]
            q = [(qs[r, :] * scale).astype(BF16) for r in rows]
            k_cur = [ks[r, :].astype(BF16) for r in rows]
            v_cur = [vs[r, :].astype(BF16) for r in rows]
            if ns[0] is None:
                prow = block_rows(ts[0] // nb + jnp.maximum(n0 - 1, 0) * (AT_BLK * dil))
                k_lead, v_lead = ks[prow, :].astype(BF16), vs[prow, :].astype(BF16)
            s, vc = [], []
            for i in group:
                if ns[i] == 0:
                    s.append(_dot_nt(q[i], k_cur[i]) + bias_scr[g, :, AT_BLK:])
                    vc.append(v_cur[i])
                    continue
                k_prev, v_prev = (k_lead, v_lead) if ns[i] is None else (k_cur[i - 1], v_cur[i - 1])
                si = _dot_nt(q[i], jnp.concatenate([k_prev, k_cur[i]], axis=0)) + bias_scr[g]
                if ns[i] is None:
                    si = jnp.where(jnp.logical_or(n0 > 0, lane >= AT_BLK), si, NEG_INF)
                s.append(si)
                vc.append(jnp.concatenate([v_prev, v_cur[i]], axis=0))
            m = [jnp.max(si, axis=1, keepdims=True) for si in s]
            p = [jnp.exp(si - mi) for si, mi in zip(s, m)]
            l = [jnp.sum(pi, axis=1, keepdims=True) for pi in p]
            o = [jnp.dot(pi.astype(BF16), vi, preferred_element_type=F32) / li for pi, vi, li in zip(p, vc, l)]
            lse = [jnp.broadcast_to(mi + jnp.log(li), (AT_BLK, LANES)) for mi, li in zip(m, l)]
            if dil > 1:
                for r, oi, li in zip(rows, o, lse):
                    if padded:
                        o_pad[r, :], lse_pad[r, :] = oi, li
                    else:
                        o_scr[g - 1, r, :], lse_scr[g - 1, r, :] = oi, li
                return carry

            def unpad(ref, start):
                base = (start // AT_PAD_GROUP) * AT_PAD_PITCH
                return jnp.concatenate(
                    [ref[pl.ds(pl.multiple_of(base + j * AT_PAD_PITCH, 8), AT_PAD_GROUP), :]
                     for j in range(AT_BLK // AT_PAD_GROUP)], axis=0)

            for t, r, oi, li in zip(ts, rows, o, lse):
                parked = [(unpad(o_pad, t * AT_BLK), unpad(lse_pad, t * AT_BLK))
                          if DILATED_PAIRS[j][1] == AT_PAD_GROUP else (o_scr[j - 1, r, :], lse_scr[j - 1, r, :])
                          for j in range(1, AT_NBRANCH)]
                top = functools.reduce(jnp.maximum, [lj for _, lj in parked], li)
                wi = jnp.exp(li - top)
                num, den = wi * oi, wi
                for oj, lj in parked:
                    wj = jnp.exp(lj - top)
                    num, den = num + wj * oj, den + wj
                o_ref[r, :] = num / den
            return carry

        lax.fori_loop(0, AT_BLOCKS // AT_GROUP, body, 0)


def _attn_bucket_table():
    qi = jnp.arange(AT_BLK)[:, None]
    kj = jnp.arange(2 * AT_BLK)[None, :]
    rel = AT_BLK + qi - kj
    max_exact = N_BUCKETS // 2
    tables = []
    for window, dil in DILATED_PAIRS:
        dist = jnp.maximum(rel, 0) * dil
        d = jnp.maximum(dist, 1).astype(F32)
        large = max_exact + jnp.log(d / max_exact) / math.log(REL_MAX_DIST / max_exact) * (N_BUCKETS - max_exact)
        large = jnp.minimum(large.astype(jnp.int32), N_BUCKETS - 1)
        bucket = jnp.where(dist < max_exact, dist, large)
        tables.append(jnp.where((rel >= 0) & (rel <= window // dil), bucket, -1))
    return jnp.stack(tables).astype(jnp.int32)


def _attn(proj, rel_bias, bucket_table):
    col = lambda c0: pl.BlockSpec((SEQ, LANES), lambda h, b: (b, c0 + h))
    return pl.pallas_call(
        _attn_kernel,
        grid=(ATTN_HEADS, BATCH),
        in_specs=[
            pl.BlockSpec(memory_space=pltpu.SMEM),
            pl.BlockSpec((AT_NBRANCH, AT_BLK, 2 * AT_BLK), lambda h, b: (0, 0, 0)),
            col(AT_Q_COL0), col(AT_K_COL0), col(AT_V_COL0),
        ],
        out_specs=pl.BlockSpec((SEQ, LANES), lambda h, b: (b, h)),
        out_shape=jax.ShapeDtypeStruct((N_TOK, ATTN_WIDTH), F32),
        scratch_shapes=[
            pltpu.VMEM((AT_NBRANCH, AT_BLK, 2 * AT_BLK), F32),
            pltpu.VMEM((AT_NBRANCH - 2, SEQ, LANES), F32),
            pltpu.VMEM((AT_NBRANCH - 2, SEQ, LANES), F32),
            pltpu.VMEM((3, AT_PAD_ROWS, LANES), F32),
            pltpu.VMEM((AT_PAD_ROWS, LANES), F32),
            pltpu.VMEM((AT_PAD_ROWS, LANES), F32),
        ],
        compiler_params=pltpu.CompilerParams(
            dimension_semantics=("parallel", "arbitrary"), vmem_limit_bytes=VMEM_LIMIT),
        name="dilated_attn",
    )(rel_bias, bucket_table, proj, proj, proj)


DN_TILE = 2 * DN_CHUNK
DN_NTILE = SEQ // DN_TILE
DN_BH = BATCH * DN_HEADS
DN_PAD = 8
DN_GROUP = 8
DN_SCAN_TILES = 2
DN_AKT_ROWS = DN_CHUNK + DN_HEAD_DIM
DN_INV_LEVELS = 6
assert 2 ** DN_INV_LEVELS == DN_CHUNK
DN_QKV_COL0 = (SSM_WIDTH + 3 * ATTN_WIDTH) // LANES
DN_Z_COL0 = DN_QKV_COL0 + 3 * DN_HEADS
DN_AB_COL = DN_Z_COL0 + DN_HEADS
HIGHEST = lax.Precision.HIGHEST


def _dot_nt(a, b):
    return lax.dot_general(a, b, (((1,), (1,)), ((), ())), preferred_element_type=F32)


def _split3(x):
    hi = x.astype(BF16)
    r = x - hi.astype(F32)
    mid = r.astype(BF16)
    return hi, mid, (r - mid.astype(F32)).astype(BF16)


def _dot_exact_lhs(x, m16):
    return sum(jnp.dot(piece, m16, preferred_element_type=F32) for piece in _split3(x))


def _dot_exact_rhs(m16, x):
    return sum(jnp.dot(m16, piece, preferred_element_type=F32) for piece in _split3(x))


def _dn_tile_masks():
    row = lax.broadcasted_iota(jnp.int32, (DN_TILE, DN_TILE), 0)
    col = lax.broadcasted_iota(jnp.int32, (DN_TILE, DN_TILE), 1)
    causal = jnp.logical_and((row // DN_CHUNK) == (col // DN_CHUNK), row >= col)
    return row, col, causal


DN_GATE_ROWS = 1024


def _dn_gates_kernel(ab_ref, alog_ref, dtb_ref, o_ref):
    _, col, causal = _dn_tile_masks()
    cumsum_mat = causal.astype(BF16)
    neg_a = -jnp.exp(alog_ref[...])
    for t in range(DN_GATE_ROWS // DN_TILE):
        rows = slice(t * DN_TILE, (t + 1) * DN_TILE)
        ab = ab_ref[rows, :]
        x = ab + dtb_ref[...]
        g = neg_a * (jnp.maximum(x, 0.0) + jnp.log1p(jnp.exp(-jnp.abs(x))))
        gc = _dot_exact_rhs(cumsum_mat, g)
        o_ref[rows, :] = jnp.where(col < DN_HEADS, gc, jax.nn.sigmoid(ab))


def _dn_gates(proj, a_log, dt_bias):
    lanes = lambda v: jnp.pad(v, (0, LANES - DN_HEADS)).reshape(1, LANES)
    return pl.pallas_call(
        _dn_gates_kernel,
        grid=(N_TOK // DN_GATE_ROWS,),
        in_specs=[
            pl.BlockSpec((DN_GATE_ROWS, LANES), lambda i: (i, DN_AB_COL)),
            pl.BlockSpec((1, LANES), lambda i: (0, 0)),
            pl.BlockSpec((1, LANES), lambda i: (0, 0)),
        ],
        out_specs=pl.BlockSpec((DN_GATE_ROWS, LANES), lambda i: (i, 0)),
        out_shape=jax.ShapeDtypeStruct((N_TOK, LANES), F32),
        compiler_params=pltpu.CompilerParams(dimension_semantics=("parallel",), vmem_limit_bytes=VMEM_LIMIT),
        name="dn_gates",
    )(proj, lanes(a_log), lanes(dt_bias))


def _dn_prep_kernel(q_ref, k_ref, v_ref, z_ref, gate_ref, wq_ref, wk_ref, wv_ref,
                    u_o, wq_o, akt_o, dec_o, sz_o, qp, kp, vp):
    h = pl.program_id(1)
    for src, dst in ((q_ref, qp), (k_ref, kp), (v_ref, vp)):
        dst[0:DN_PAD, :] = jnp.zeros((DN_PAD, LANES), F32)
        dst[DN_PAD:DN_PAD + SEQ, :] = src[...]

    row, col, causal = _dn_tile_masks()
    pick = jnp.concatenate([row == h, row == h + DN_HEADS], axis=1).astype(BF16)
    eye = (row == col).astype(F32)
    pair_masks = [
        jnp.logical_and(jnp.logical_and((row // (2 * s)) == (col // (2 * s)), (row // s) % 2 == 1),
                        (col // s) % 2 == 0)
        for s in (2 ** i for i in range(DN_INV_LEVELS))]
    scale = DN_HEAD_DIM ** -0.5

    def conv_silu(pad_ref, w_ref, base):
        acc = None
        for j in range(DN_CONV):
            sh = DN_PAD - (DN_CONV - 1) + j
            term = w_ref[j:j + 1, :] * pad_ref[pl.ds(base + sh, DN_TILE), :]
            acc = term if acc is None else acc + term
        return acc * jax.nn.sigmoid(acc)

    def l2n(x):
        return x * lax.rsqrt(jnp.sum(x * x, axis=1, keepdims=True) + NORM_EPS)

    def mm16(a, b):
        return jnp.dot(a.astype(BF16), b.astype(BF16), preferred_element_type=F32)

    def body(it, carry):
        tiles = [it * DN_GROUP + i for i in range(DN_GROUP)]
        bases = [pl.multiple_of(t * DN_TILE, DN_TILE) for t in tiles]
        q = [l2n(conv_silu(qp, wq_ref, b)) * scale for b in bases]
        k = [l2n(conv_silu(kp, wk_ref, b)) for b in bases]
        v = [conv_silu(vp, wv_ref, b) for b in bases]
        gate_rep = [_dot_exact_lhs(gate_ref[pl.ds(b, DN_TILE), :], pick) for b in bases]
        gc = [r[:, :DN_TILE] for r in gate_rep]
        beta = [r[:, DN_TILE:] for r in gate_rep]
        decay = [jnp.exp(jnp.where(causal, c - c.T, NEG_INF)) for c in gc]
        kb = [ki * bi for ki, bi in zip(k, beta)]
        k16 = [ki.astype(BF16) for ki in k]
        a_mat = [_dot_nt(kbi.astype(BF16), ki) * di for kbi, ki, di in zip(kb, k16, decay)]
        t_inv = [eye - jnp.where(pair_masks[0], am, 0.0) for am in a_mat]
        for mask in pair_masks[1:]:
            low = [jnp.where(mask, am, 0.0).astype(BF16) for am in a_mat]
            t_inv = [ti - mm16(mm16(ti, lo), ti) for ti, lo in zip(t_inv, low)]
        egc = [jnp.exp(c) for c in gc]
        uw = [mm16(ti, jnp.concatenate([vi * bi, kbi * ei], axis=1))
              for ti, vi, bi, kbi, ei in zip(t_inv, v, beta, kb, egc)]
        at = [(_dot_nt(qi.astype(BF16), ki) * di).astype(BF16) for qi, ki, di in zip(q, k16, decay)]
        qd = [(qi * ei).astype(BF16) for qi, ei in zip(q, egc)]
        c = DN_CHUNK
        for i, t in enumerate(tiles):
            rows = pl.ds(bases[i], DN_TILE)
            u_o[t] = uw[i][:, :DN_HEAD_DIM]
            w = uw[i][:, DN_HEAD_DIM:].astype(BF16)
            wq_o[t] = jnp.concatenate([w[:c], qd[i][:c], w[c:], qd[i][c:]], axis=0)
            gc_first, gc_second = gc[i][c - 1:c, :], gc[i][DN_TILE - 1:DN_TILE, :]
            gc_last = jnp.where(row < c, gc_first, gc_second)
            kt = (k[i] * jnp.exp(gc_last - gc[i])).T
            akt_o[t] = jnp.concatenate([at[i][:c] + at[i][c:], kt.astype(BF16)], axis=0)
            dec_o[t] = jnp.concatenate(
                [jnp.exp(gc_first), jnp.exp(gc_second), jnp.zeros((DN_PAD - 2, LANES), F32)], axis=0)
            z = z_ref[rows, :]
            sz_o[t] = (z * jax.nn.sigmoid(z)).astype(BF16)
        return carry

    lax.fori_loop(0, DN_NTILE // DN_GROUP, body, 0)


def _dn_prep(proj, gates, conv_w):
    col = lambda c0: pl.BlockSpec((SEQ, LANES), lambda b, h: (b, c0 + h))
    cw = lambda c0: pl.BlockSpec((DN_CONV, LANES), lambda b, h: (0, c0 + h))
    bh_rows = lambda n: pl.BlockSpec((DN_NTILE, None, n, LANES), lambda b, h: (0, b * DN_HEADS + h, 0, 0))
    rows_shape = lambda n, dt: jax.ShapeDtypeStruct((DN_NTILE, DN_BH, n, LANES), dt)
    return pl.pallas_call(
        _dn_prep_kernel,
        grid=(BATCH, DN_HEADS),
        in_specs=[
            col(DN_QKV_COL0), col(DN_QKV_COL0 + DN_HEADS), col(DN_QKV_COL0 + 2 * DN_HEADS), col(DN_Z_COL0),
            pl.BlockSpec((SEQ, LANES), lambda b, h: (b, 0)),
            cw(0), cw(DN_HEADS), cw(2 * DN_HEADS),
        ],
        out_specs=[bh_rows(DN_TILE), bh_rows(2 * DN_TILE), bh_rows(DN_AKT_ROWS), bh_rows(DN_PAD), bh_rows(DN_TILE)],
        out_shape=[
            rows_shape(DN_TILE, F32), rows_shape(2 * DN_TILE, BF16), rows_shape(DN_AKT_ROWS, BF16),
            rows_shape(DN_PAD, F32), rows_shape(DN_TILE, BF16),
        ],
        scratch_shapes=[pltpu.VMEM((SEQ + DN_PAD, LANES), F32)] * 3,
        compiler_params=pltpu.CompilerParams(
            dimension_semantics=("parallel", "parallel"), vmem_limit_bytes=VMEM_LIMIT),
        name="dn_prep",
    )(proj, proj, proj, proj, gates, conv_w, conv_w, conv_w)


def _dn_scan_kernel(u_ref, wq_ref, akt_ref, dec_ref, sz_ref, gain_ref, o_ref, s_scr):
    @pl.when(pl.program_id(0) == 0)
    def _():
        s_scr[...] = jnp.zeros_like(s_scr)

    gain = gain_ref[...]
    c = DN_CHUNK
    chains = range(DN_BH)
    second_chunk_lane = lax.broadcasted_iota(jnp.int32, (DN_AKT_ROWS, LANES), 1) >= c
    state = [s_scr[bh] for bh in chains]
    v_first = None
    for tt, j in ((tt, j) for tt in range(DN_SCAN_TILES) for j in range(2)):
        rows = slice(j * c, (j + 1) * c)
        out_rows = slice(tt * DN_TILE + j * c, tt * DN_TILE + (j + 1) * c)
        s16 = [s.astype(BF16) for s in state]
        ws_qs = [jnp.dot(wq_ref[tt, bh, j * DN_TILE:(j + 1) * DN_TILE, :], s16[bh],
                         preferred_element_type=F32) for bh in chains]
        v_new = [u_ref[tt, bh, rows, :] - ws_qs[bh][:c] for bh in chains]
        if j == 0:
            v_pair = [jnp.concatenate([v, jnp.zeros_like(v)], axis=0).astype(BF16) for v in v_new]
            v_first = v_new
            akt = [akt_ref[tt, bh] for bh in chains]
        else:
            v_pair = [jnp.concatenate([v0, v], axis=0).astype(BF16) for v0, v in zip(v_first, v_new)]
            akt = [jnp.where(second_chunk_lane, akt_ref[tt, bh], jnp.zeros((DN_AKT_ROWS, LANES), BF16))
                   for bh in chains]
        ov_kv = [jnp.dot(akt[bh], v_pair[bh], preferred_element_type=F32) for bh in chains]
        state = [state[bh] * dec_ref[tt, bh, j:j + 1, :] + ov_kv[bh][c:] for bh in chains]
        for bh in chains:
            b, h = divmod(bh, DN_HEADS)
            o = ws_qs[bh][c:] + ov_kv[bh][:c]
            o_ref[b, out_rows, h * DN_HEAD_DIM:(h + 1) * DN_HEAD_DIM] = (
                _rms(o, gain) * sz_ref[tt, bh, rows, :].astype(F32))
    for bh in chains:
        s_scr[bh] = state[bh]


def _dn_scan(u, wq, akt, dec, sz, gain):
    rows = lambda n: pl.BlockSpec((DN_SCAN_TILES, DN_BH, n, LANES), lambda t: (t, 0, 0, 0))
    return pl.pallas_call(
        _dn_scan_kernel,
        grid=(DN_NTILE // DN_SCAN_TILES,),
        in_specs=[rows(DN_TILE), rows(2 * DN_TILE), rows(DN_AKT_ROWS), rows(DN_PAD), rows(DN_TILE),
                  pl.BlockSpec((1, LANES), lambda t: (0, 0))],
        out_specs=pl.BlockSpec((BATCH, DN_SCAN_TILES * DN_TILE, DN_WIDTH), lambda t: (0, t, 0)),
        out_shape=jax.ShapeDtypeStruct((BATCH, SEQ, DN_WIDTH), F32),
        scratch_shapes=[pltpu.VMEM((DN_BH, DN_HEAD_DIM, DN_HEAD_DIM), F32)],
        compiler_params=pltpu.CompilerParams(
            dimension_semantics=("arbitrary",), vmem_limit_bytes=VMEM_LIMIT),
        name="dn_scan",
    )(u, wq, akt, dec, sz, gain)


def _dn_mixer(proj, p, l):
    gates = _dn_gates(proj, p['dn_a_log'][l], p['dn_dt_bias'][l])
    outs = _dn_prep(proj, gates, p['dn_conv_w'][l])
    y = _dn_scan(*outs, p['dn_norm_gain'][l].reshape(1, DN_HEAD_DIM))
    return y.reshape(N_TOK, DN_WIDTH)


def kernel(x, norm_gains, ffn_w_gate, ffn_w_up, ffn_w_down, w_in, w_out, ssm_lambda_re, ssm_lambda_im,
           ssm_b_re, ssm_b_im, ssm_c_re, ssm_c_im, ssm_d, ssm_log_dt, ssm_glu_w, ssm_glu_b, ssm_out_gain,
           dn_conv_w, dn_a_log, dn_dt_bias, dn_norm_gain, attn_out_gain, rel_bias):
    p = dict(ssm_lambda_re=ssm_lambda_re, ssm_lambda_im=ssm_lambda_im, ssm_b_re=ssm_b_re, ssm_b_im=ssm_b_im,
             ssm_c_re=ssm_c_re, ssm_c_im=ssm_c_im, ssm_d=ssm_d, ssm_log_dt=ssm_log_dt, ssm_glu_w=ssm_glu_w,
             ssm_glu_b=ssm_glu_b, ssm_out_gain=ssm_out_gain, dn_conv_w=dn_conv_w, dn_a_log=dn_a_log,
             dn_dt_bias=dn_dt_bias, dn_norm_gain=dn_norm_gain)
    wg, wu, wd = ffn_w_gate, ffn_w_up, ffn_w_down
    w_in_p = jnp.pad(w_in.astype(BF16), ((0, 0), (0, 0), (0, N_IN_PAD - N_IN_COLS)))
    w_out_b = w_out
    bucket_table = _attn_bucket_table()
    gains = norm_gains.reshape(DEPTH, 6, 1, D_MODEL)
    x = x.reshape(N_TOK, D_MODEL)
    for l in range(DEPTH):
        x = _ffn(x, gains, wg, wu, wd, l, 0)
        proj = _inproj(x, gains, w_in_p, l)
        y_ssm = _s5_mixer(proj, p, l)
        y_dn = _dn_mixer(proj, p, l)
        o_at = _attn(proj, rel_bias, bucket_table)
        x = _outproj(x, y_ssm, y_dn, o_at, attn_out_gain[l].reshape(1, ATTN_WIDTH), gains, w_out_b, l)
        x = _ffn(x, gains, wg, wu, wd, l, 1)
    return x.reshape(BATCH, SEQ, D_MODEL)
```

```python
import functools
import math

import jax
import jax.numpy as jnp
import numpy as np
from jax import lax
from jax.experimental import pallas as pl
from jax.experimental.pallas import tpu as pltpu

D_MODEL = 2048
BATCH = 2
SEQ = 4096
DEPTH = 4
SSM_GROUPS = 32
SSM_CH = 16
SSM_STATE = 64
SSM_WIDTH = SSM_GROUPS * SSM_CH
DN_HEADS = 6
DN_HEAD_DIM = 128
DN_WIDTH = DN_HEADS * DN_HEAD_DIM
DN_CONV = 4
DN_CHUNK = 64
ATTN_HEADS = 6
ATTN_HEAD_DIM = 128
ATTN_WIDTH = ATTN_HEADS * ATTN_HEAD_DIM
DILATED_PAIRS = ((128, 1), (512, 4), (2048, 16))
ATTN_BLOCK = 128
N_BUCKETS = 32
REL_MAX_DIST = 2048
D_MIX = SSM_WIDTH + DN_WIDTH + ATTN_WIDTH
IN_SPLITS = (SSM_WIDTH, ATTN_WIDTH, ATTN_WIDTH, ATTN_WIDTH, 3 * DN_WIDTH, DN_WIDTH, DN_HEADS, DN_HEADS)
N_IN_COLS = sum(IN_SPLITS)
D_FF = 5632
NORM_EPS = 1e-6
NEG_INF = -1e30

LANES = 128
N_IN_PAD = 6144
V7X_VMEM_BYTES = 64 * 1024 * 1024
VMEM_LIMIT = V7X_VMEM_BYTES - 4 * 1024 * 1024
N_TOK = BATCH * SEQ

BF16 = jnp.bfloat16
F32 = jnp.float32


def _rms(x, gain):
    return x * lax.rsqrt(jnp.mean(x * x, axis=-1, keepdims=True) + NORM_EPS) * gain


def _silu(x):
    half = 0.5 * x
    return half + half * jnp.tanh(half)


FFN_TM = 1024
FFN_TF = 256


def _ffn_kernel(x_ref, gpre_ref, gpost_ref, wg_ref, wu_ref, wd_ref, o_ref, h_scr):
    f = pl.program_id(1)
    last = pl.num_programs(1) - 1

    def down_proj(h):
        gate = jnp.dot(h, wg_ref[...].astype(BF16), preferred_element_type=F32)
        up = jnp.dot(h, wu_ref[...].astype(BF16), preferred_element_type=F32)
        act = (gate * jax.nn.sigmoid(gate) * up).astype(BF16)
        return jnp.dot(act, wd_ref[...].astype(BF16), preferred_element_type=F32)

    @pl.when(f == 0)
    def _():
        h = _rms(x_ref[...], gpre_ref[...]).astype(BF16)
        h_scr[...] = h
        o_ref[...] = down_proj(h)

    @pl.when(jnp.logical_and(f > 0, f < last))
    def _():
        o_ref[...] += down_proj(h_scr[...])

    @pl.when(f == last)
    def _():
        y = o_ref[...] + down_proj(h_scr[...])
        o_ref[...] = x_ref[...] + _rms(y, 0.5 * gpost_ref[...])


def _ffn(x, gains, wg, wu, wd, layer, half):
    return pl.pallas_call(
        _ffn_kernel,
        grid=(N_TOK // FFN_TM, D_FF // FFN_TF),
        in_specs=[
            pl.BlockSpec((FFN_TM, D_MODEL), lambda i, f: (i, 0)),
            pl.BlockSpec((None, None, 1, D_MODEL), lambda i, f: (layer, 4 * half, 0, 0)),
            pl.BlockSpec((None, None, 1, D_MODEL), lambda i, f: (layer, 4 * half + 1, 0, 0)),
            pl.BlockSpec((None, None, D_MODEL, FFN_TF), lambda i, f: (layer, half, 0, f)),
            pl.BlockSpec((None, None, D_MODEL, FFN_TF), lambda i, f: (layer, half, 0, f)),
            pl.BlockSpec((None, None, FFN_TF, D_MODEL), lambda i, f: (layer, half, f, 0)),
        ],
        out_specs=pl.BlockSpec((FFN_TM, D_MODEL), lambda i, f: (i, 0)),
        out_shape=jax.ShapeDtypeStruct((N_TOK, D_MODEL), F32),
        scratch_shapes=[pltpu.VMEM((FFN_TM, D_MODEL), BF16)],
        compiler_params=pltpu.CompilerParams(
            dimension_semantics=("parallel", "arbitrary"), vmem_limit_bytes=VMEM_LIMIT),
        name="ffn",
    )(x, gains, gains, wg, wu, wd)


INP_TM = 1024
INP_TN = 1536


def _inproj_kernel(x_ref, g_ref, w_ref, o_ref, h_scr):
    @pl.when(pl.program_id(1) == 0)
    def _():
        h = _rms(x_ref[...], g_ref[...]).astype(BF16)
        h_scr[...] = h
        o_ref[...] = jnp.dot(h, w_ref[...], preferred_element_type=F32)

    @pl.when(pl.program_id(1) > 0)
    def _():
        o_ref[...] = jnp.dot(h_scr[...], w_ref[...], preferred_element_type=F32)


def _inproj(x, gains, w, layer):
    return pl.pallas_call(
        _inproj_kernel,
        grid=(N_TOK // INP_TM, N_IN_PAD // INP_TN),
        in_specs=[
            pl.BlockSpec((INP_TM, D_MODEL), lambda i, n: (i, 0)),
            pl.BlockSpec((None, None, 1, D_MODEL), lambda i, n: (layer, 2, 0, 0)),
            pl.BlockSpec((None, D_MODEL, INP_TN), lambda i, n: (layer, 0, n)),
        ],
        out_specs=pl.BlockSpec((INP_TM, INP_TN), lambda i, n: (i, n)),
        out_shape=jax.ShapeDtypeStruct((N_TOK, N_IN_PAD), F32),
        scratch_shapes=[pltpu.VMEM((INP_TM, D_MODEL), BF16)],
        compiler_params=pltpu.CompilerParams(
            dimension_semantics=("parallel", "arbitrary"), vmem_limit_bytes=VMEM_LIMIT),
        name="inproj",
    )(x, gains, w)


OUT_TM = 512
OUT_SUB = 256


def _outproj_kernel(x_ref, ys_ref, yd_ref, oa_ref, ga_ref, gpost_ref, w_ref, o_ref, w16):
    @pl.when(pl.program_id(0) == 0)
    def _():
        w16[...] = w_ref[...].astype(BF16)

    halves = [slice(i * OUT_SUB, (i + 1) * OUT_SUB) for i in range(OUT_TM // OUT_SUB)]
    ya = [_rms(oa_ref[r, :], ga_ref[...]).astype(BF16) for r in halves]
    mix = [jnp.dot(ys_ref[r, :].astype(BF16), w16[0:SSM_WIDTH, :], preferred_element_type=F32) for r in halves]
    mix = [m + jnp.dot(yd_ref[r, :].astype(BF16), w16[SSM_WIDTH:SSM_WIDTH + DN_WIDTH, :],
                       preferred_element_type=F32) for m, r in zip(mix, halves)]
    mix = [m + jnp.dot(a, w16[SSM_WIDTH + DN_WIDTH:D_MIX, :], preferred_element_type=F32)
           for m, a in zip(mix, ya)]
    for r, m in zip(halves, mix):
        o_ref[r, :] = x_ref[r, :] + _rms(m, gpost_ref[...])


def _outproj(x, y_ssm, y_dn, o_at, g_attn, gains, w, layer):
    row = lambda i: (i, 0)
    fixed = lambda i: (0, 0)
    return pl.pallas_call(
        _outproj_kernel,
        grid=(N_TOK // OUT_TM,),
        in_specs=[
            pl.BlockSpec((OUT_TM, D_MODEL), row),
            pl.BlockSpec((OUT_TM, SSM_WIDTH), row),
            pl.BlockSpec((OUT_TM, DN_WIDTH), row),
            pl.BlockSpec((OUT_TM, ATTN_WIDTH), row),
            pl.BlockSpec((1, ATTN_WIDTH), fixed),
            pl.BlockSpec((None, None, 1, D_MODEL), lambda i: (layer, 3, 0, 0)),
            pl.BlockSpec((None, D_MIX, D_MODEL), lambda i: (layer, 0, 0), pipeline_mode=pl.Buffered(1)),
        ],
        out_specs=pl.BlockSpec((OUT_TM, D_MODEL), row),
        out_shape=jax.ShapeDtypeStruct((N_TOK, D_MODEL), F32),
        scratch_shapes=[pltpu.VMEM((D_MIX, D_MODEL), BF16)],
        compiler_params=pltpu.CompilerParams(
            dimension_semantics=("arbitrary",), vmem_limit_bytes=VMEM_LIMIT),
        name="outproj",
    )(x, y_ssm, y_dn, o_at, g_attn, gains, w)


S5_SEG = 8
S5_SEGLEN = SEQ // S5_SEG
S5_KB = 128
S5_ROWS = S5_KB * S5_SEG
S5_PITCH = S5_KB + 8
S5_PROWS = S5_PITCH * S5_SEG
S5_NS = SSM_GROUPS * SSM_STATE
S5_SLABS = S5_NS // LANES
S5_Q = 4
S5_QS = S5_NS // S5_Q
S5_QL = S5_QS // LANES
S5_QC = SSM_WIDTH // S5_Q
S5_LOG2_SEGLEN = 9
assert 1 << S5_LOG2_SEGLEN == S5_SEGLEN


def _s5_kernel(u_ref, wb_ref, are_ref, aim_ref, cre_ref, cim_ref, d_ref, gw_ref, gb_ref, go_ref, o_ref,
               bu_scr, st_scr, carry_scr, y_scr):
    p = pl.program_id(1)
    j = pl.program_id(2)
    u = u_ref[0].reshape(S5_ROWS, SSM_WIDTH)
    ub = u.astype(BF16)

    @pl.when(jnp.logical_and(p == 0, j == 0))
    def _():
        bu_scr[...] = jnp.zeros_like(bu_scr)
        st_scr[...] = jnp.zeros_like(st_scr)

    @pl.when(jnp.logical_and(p == 1, j == 0))
    def _():
        st_scr[...] = carry_scr[...]

    for q in range(S5_Q):
        r = jnp.dot(ub[:, q * S5_QC:(q + 1) * S5_QC], wb_ref[q], preferred_element_type=F32)
        for half in range(2):
            for c in range(S5_QL):
                slab = half * S5_SLABS + q * S5_QL + c
                lanes = slice((half * S5_QL + c) * LANES, (half * S5_QL + c + 1) * LANES)
                for i in range(S5_SEG):
                    bu_scr[slab, i * S5_PITCH:i * S5_PITCH + S5_KB, :] = r[i * S5_KB:(i + 1) * S5_KB, lanes]

    def scan(store):
        for q in range(S5_Q):
            slabs = [q * S5_QL + c for c in range(S5_QL)]
            lanes = [slice(s * LANES, (s + 1) * LANES) for s in slabs]
            ilanes = [slice(S5_NS + s * LANES, S5_NS + (s + 1) * LANES) for s in slabs]
            ar = [jnp.broadcast_to(are_ref[:, ln], (S5_SEG, LANES)) for ln in lanes]
            ai = [jnp.broadcast_to(aim_ref[:, ln], (S5_SEG, LANES)) for ln in lanes]

            def body(k, carry):
                sr, si = carry
                rows = pl.ds(k, S5_SEG, stride=S5_PITCH)
                nr = tuple(ar[c] * sr[c] - ai[c] * si[c] + bu_scr[slabs[c], rows, :] for c in range(S5_QL))
                ni = tuple(ar[c] * si[c] + ai[c] * sr[c] + bu_scr[slabs[c] + S5_SLABS, rows, :]
                           for c in range(S5_QL))
                if store:
                    for c in range(S5_QL):
                        bu_scr[slabs[c], rows, :] = nr[c]
                        bu_scr[slabs[c] + S5_SLABS, rows, :] = ni[c]
                return nr, ni

            init = (tuple(st_scr[:, ln] for ln in lanes), tuple(st_scr[:, ln] for ln in ilanes))
            sr, si = lax.fori_loop(0, S5_KB, body, init, unroll=8)
            for c in range(S5_QL):
                st_scr[:, lanes[c]] = sr[c]
                st_scr[:, ilanes[c]] = si[c]

    @pl.when(p == 0)
    def _():
        scan(False)

        @pl.when(j == pl.num_programs(2) - 1)
        def _():
            lr, li = are_ref[...], aim_ref[...]
            for _ in range(S5_LOG2_SEGLEN):
                lr, li = lr * lr - li * li, 2.0 * lr * li
            cr = jnp.zeros((1, S5_NS), F32)
            ci = jnp.zeros((1, S5_NS), F32)
            carry_scr[0:1, :] = jnp.zeros((1, 2 * S5_NS), F32)
            for i in range(1, S5_SEG):
                er = st_scr[i - 1:i, 0:S5_NS]
                ei = st_scr[i - 1:i, S5_NS:2 * S5_NS]
                cr, ci = er + lr * cr - li * ci, ei + lr * ci + li * cr
                carry_scr[i:i + 1, 0:S5_NS] = cr
                carry_scr[i:i + 1, S5_NS:2 * S5_NS] = ci

    @pl.when(p == 1)
    def _():
        scan(True)
        for q in range(S5_Q):
            sre = jnp.concatenate([bu_scr[q * S5_QL + c] for c in range(S5_QL)], axis=1).astype(BF16)
            sim = jnp.concatenate([bu_scr[S5_SLABS + q * S5_QL + c] for c in range(S5_QL)], axis=1).astype(BF16)
            y_scr[:, q * S5_QC:(q + 1) * S5_QC] = (
                jnp.dot(sre, cre_ref[q], preferred_element_type=F32)
                + jnp.dot(sim, cim_ref[q], preferred_element_type=F32))
        cs = jnp.concatenate([y_scr[i * S5_PITCH:i * S5_PITCH + S5_KB, :] for i in range(S5_SEG)], axis=0)
        y = jax.nn.gelu(cs + d_ref[...] * u)
        z = jnp.dot(y.astype(BF16), gw_ref[...], preferred_element_type=F32) + gb_ref[...]
        o_ref[0] = _rms(y * jax.nn.sigmoid(z), go_ref[...]).reshape(S5_SEG, S5_KB, SSM_WIDTH)


def _s5_params(lam_re, lam_im, b_re, b_im, c_re, c_im, log_dt):
    lam = lax.complex(lam_re, lam_im)
    lam_bar = jnp.exp(lam * jnp.exp(log_dt)[:, None])
    b_bar = ((lam_bar - 1.0) / lam)[..., None] * lax.complex(b_re, b_im)
    gq = SSM_GROUPS // S5_Q
    eye = jnp.eye(gq, dtype=F32)

    def in_map(t):
        t = t.reshape(S5_Q, gq, SSM_STATE, SSM_CH)
        return jnp.einsum('qgpc,gh->qgchp', t, eye).reshape(S5_Q, S5_QC, S5_QS)

    def out_map(t):
        t = t.reshape(S5_Q, gq, SSM_CH, SSM_STATE)
        return jnp.einsum('qgcp,gh->qgphc', t, eye).reshape(S5_Q, S5_QS, S5_QC)

    wb = jnp.concatenate([in_map(b_bar.real), in_map(b_bar.imag)], axis=-1).astype(BF16)
    return (wb, lam_bar.real.reshape(1, S5_NS), lam_bar.imag.reshape(1, S5_NS),
            out_map(c_re).astype(BF16), out_map(-c_im).astype(BF16))


def _s5(proj4, wb, a_re, a_im, cre, cim, d_skip, glu_w, glu_b, out_gain):
    nblk = S5_SEGLEN // S5_KB
    fix2 = lambda b, p, j: (0, 0)
    fix3 = lambda b, p, j: (0, 0, 0)
    return pl.pallas_call(
        _s5_kernel,
        grid=(BATCH, 2, nblk),
        in_specs=[
            pl.BlockSpec((1, S5_SEG, S5_KB, SSM_WIDTH), lambda b, p, j: (b, 0, j, 0)),
            pl.BlockSpec((S5_Q, S5_QC, 2 * S5_QS), fix3),
            pl.BlockSpec((1, S5_NS), fix2),
            pl.BlockSpec((1, S5_NS), fix2),
            pl.BlockSpec((S5_Q, S5_QS, S5_QC), fix3),
            pl.BlockSpec((S5_Q, S5_QS, S5_QC), fix3),
            pl.BlockSpec((1, SSM_WIDTH), fix2),
            pl.BlockSpec((SSM_WIDTH, SSM_WIDTH), fix2),
            pl.BlockSpec((1, SSM_WIDTH), fix2),
            pl.BlockSpec((1, SSM_WIDTH), fix2),
        ],
        out_specs=pl.BlockSpec((1, S5_SEG, S5_KB, SSM_WIDTH), lambda b, p, j: (b, 0, j * p, 0)),
        out_shape=jax.ShapeDtypeStruct((BATCH, S5_SEG, S5_SEGLEN, SSM_WIDTH), F32),
        scratch_shapes=[
            pltpu.VMEM((2 * S5_SLABS, S5_PROWS, LANES), F32),
            pltpu.VMEM((S5_SEG, 2 * S5_NS), F32),
            pltpu.VMEM((S5_SEG, 2 * S5_NS), F32),
            pltpu.VMEM((S5_PROWS, SSM_WIDTH), F32),
        ],
        compiler_params=pltpu.CompilerParams(
            dimension_semantics=("parallel", "arbitrary", "arbitrary"), vmem_limit_bytes=VMEM_LIMIT),
        name="s5",
    )(proj4, wb, a_re, a_im, cre, cim, d_skip, glu_w, glu_b, out_gain)


def _s5_mixer(proj, p, l):
    wb, a_re, a_im, cre, cim = _s5_params(p['ssm_lambda_re'][l], p['ssm_lambda_im'][l], p['ssm_b_re'][l],
                                          p['ssm_b_im'][l], p['ssm_c_re'][l], p['ssm_c_im'][l],
                                          p['ssm_log_dt'][l])
    y = _s5(proj.reshape(BATCH, S5_SEG, S5_SEGLEN, N_IN_PAD), wb, a_re, a_im, cre, cim,
            p['ssm_d'][l].reshape(1, SSM_WIDTH), p['ssm_glu_w'][l].astype(BF16),
            p['ssm_glu_b'][l].reshape(1, SSM_WIDTH), p['ssm_out_gain'][l].reshape(1, SSM_WIDTH))
    return y.reshape(N_TOK, SSM_WIDTH)


AT_BLK = ATTN_BLOCK
AT_NBRANCH = len(DILATED_PAIRS)
AT_BLOCKS = SEQ // AT_BLK
AT_GROUP = 8
AT_ORDER = tuple(range(1, AT_NBRANCH)) + (0,)
AT_PAD_GROUP = 16
AT_PAD_PITCH = 24
AT_PAD_ROWS = SEQ // AT_PAD_GROUP * AT_PAD_PITCH
assert [d for _, d in DILATED_PAIRS].count(AT_PAD_GROUP) == 1 and DILATED_PAIRS[-1][1] == AT_PAD_GROUP
assert DILATED_PAIRS[0][1] == 1 and all(d > 1 for _, d in DILATED_PAIRS[1:])
assert all(w // d == AT_BLK for w, d in DILATED_PAIRS)
assert all((AT_BLOCKS // d) % AT_GROUP == 0 or AT_GROUP % (AT_BLOCKS // d) == 0 for _, d in DILATED_PAIRS)
AT_Q_COL0 = SSM_WIDTH // LANES
AT_K_COL0 = AT_Q_COL0 + ATTN_HEADS
AT_V_COL0 = AT_K_COL0 + ATTN_HEADS


def _at_padded_row(s):
    return (s // AT_PAD_GROUP) * AT_PAD_PITCH + s % AT_PAD_GROUP


def _attn_kernel(rb_ref, bkt_ref, q_ref, k_ref, v_ref, o_ref, bias_scr, o_scr, lse_scr,
                 qkv_pad, o_pad, lse_pad):
    h = pl.program_id(0)

    def to_padded(i, carry):
        src_rows = pl.ds(pl.multiple_of(i * AT_PAD_GROUP, AT_PAD_GROUP), AT_PAD_GROUP)
        dst_rows = pl.ds(pl.multiple_of(i * AT_PAD_PITCH, 8), AT_PAD_GROUP)
        for j, ref in enumerate((q_ref, k_ref, v_ref)):
            qkv_pad[j, dst_rows, :] = ref[src_rows, :]
        return carry

    lax.fori_loop(0, SEQ // AT_PAD_GROUP, to_padded, 0, unroll=8)

    @pl.when(pl.program_id(1) == 0)
    def _():
        for g in range(AT_NBRANCH):
            bkt = bkt_ref[g]
            bias = jnp.zeros((AT_BLK, 2 * AT_BLK), F32)
            for b in range(N_BUCKETS):
                bias = jnp.where(bkt == b, rb_ref[b, h], bias)
            bias_scr[g] = jnp.where(bkt < 0, NEG_INF, bias)

    lane = lax.broadcasted_iota(jnp.int32, (AT_BLK, 2 * AT_BLK), 1)
    scale = ATTN_HEAD_DIM ** -0.5
    for g in AT_ORDER:
        dil = DILATED_PAIRS[g][1]
        nb = AT_BLOCKS // dil

        def body(it, carry, g=g, dil=dil, nb=nb):
            padded = dil == AT_PAD_GROUP
            if padded:
                qs, ks, vs = qkv_pad.at[0], qkv_pad.at[1], qkv_pad.at[2]
            else:
                qs, ks, vs = q_ref, k_ref, v_ref

            def block_rows(start):
                if dil == 1:
                    return pl.ds(pl.multiple_of(start, AT_BLK), AT_BLK)
                if padded:
                    return pl.ds(_at_padded_row(start), AT_BLK, stride=AT_PAD_PITCH)
                return pl.ds(start, AT_BLK, stride=dil)

            group = range(AT_GROUP)
            ts = [it * AT_GROUP + i for i in group]
            if nb <= AT_GROUP:
                ns = [i % nb for i in group]
            else:
                ns = [None if i == 0 else i for i in group]
            n0 = ts[0] % nb
            rows = [block_rows(t // nb + (t % nb) * (AT_BLK * dil)) for t in ts]
            q = [(qs[r, :] * scale).astype(BF16) for r in rows]
            k_cur = [ks[r, :].astype(BF16) for r in rows]
            v_cur = [vs[r, :].astype(BF16) for r in rows]
            if ns[0] is None:
                prow = block_rows(ts[0] // nb + jnp.maximum(n0 - 1, 0) * (AT_BLK * dil))
                k_lead, v_lead = ks[prow, :].astype(BF16), vs[prow, :].astype(BF16)
            s, vc = [], []
            for i in group:
                if ns[i] == 0:
                    s.append(_dot_nt(q[i], k_cur[i]) + bias_scr[g, :, AT_BLK:])
                    vc.append(v_cur[i])
                    continue
                k_prev, v_prev = (k_lead, v_lead) if ns[i] is None else (k_cur[i - 1], v_cur[i - 1])
                si = _dot_nt(q[i], jnp.concatenate([k_prev, k_cur[i]], axis=0)) + bias_scr[g]
                if ns[i] is None:
                    si = jnp.where(jnp.logical_or(n0 > 0, lane >= AT_BLK), si, NEG_INF)
                s.append(si)
                vc.append(jnp.concatenate([v_prev, v_cur[i]], axis=0))
            m = [jnp.max(si, axis=1, keepdims=True) for si in s]
            p = [jnp.exp(si - mi) for si, mi in zip(s, m)]
            l = [jnp.sum(pi, axis=1, keepdims=True) for pi in p]
            o = [jnp.dot(pi.astype(BF16), vi, preferred_element_type=F32) / li for pi, vi, li in zip(p, vc, l)]
            lse = [jnp.broadcast_to(mi + jnp.log(li), (AT_BLK, LANES)) for mi, li in zip(m, l)]
            if dil > 1:
                for r, oi, li in zip(rows, o, lse):
                    if padded:
                        o_pad[r, :], lse_pad[r, :] = oi, li
                    else:
                        o_scr[g - 1, r, :], lse_scr[g - 1, r, :] = oi, li
                return carry

            def unpad(ref, start):
                base = (start // AT_PAD_GROUP) * AT_PAD_PITCH
                return jnp.concatenate(
                    [ref[pl.ds(pl.multiple_of(base + j * AT_PAD_PITCH, 8), AT_PAD_GROUP), :]
                     for j in range(AT_BLK // AT_PAD_GROUP)], axis=0)

            for t, r, oi, li in zip(ts, rows, o, lse):
                parked = [(unpad(o_pad, t * AT_BLK), unpad(lse_pad, t * AT_BLK))
                          if DILATED_PAIRS[j][1] == AT_PAD_GROUP else (o_scr[j - 1, r, :], lse_scr[j - 1, r, :])
                          for j in range(1, AT_NBRANCH)]
                top = functools.reduce(jnp.maximum, [lj for _, lj in parked], li)
                wi = jnp.exp(li - top)
                num, den = wi * oi, wi
                for oj, lj in parked:
                    wj = jnp.exp(lj - top)
                    num, den = num + wj * oj, den + wj
                o_ref[r, :] = num / den
            return carry

        lax.fori_loop(0, AT_BLOCKS // AT_GROUP, body, 0)


def _attn_bucket_table():
    qi = jnp.arange(AT_BLK)[:, None]
    kj = jnp.arange(2 * AT_BLK)[None, :]
    rel = AT_BLK + qi - kj
    max_exact = N_BUCKETS // 2
    tables = []
    for window, dil in DILATED_PAIRS:
        dist = jnp.maximum(rel, 0) * dil
        d = jnp.maximum(dist, 1).astype(F32)
        large = max_exact + jnp.log(d / max_exact) / math.log(REL_MAX_DIST / max_exact) * (N_BUCKETS - max_exact)
        large = jnp.minimum(large.astype(jnp.int32), N_BUCKETS - 1)
        bucket = jnp.where(dist < max_exact, dist, large)
        tables.append(jnp.where((rel >= 0) & (rel <= window // dil), bucket, -1))
    return jnp.stack(tables).astype(jnp.int32)


def _attn(proj, rel_bias, bucket_table):
    col = lambda c0: pl.BlockSpec((SEQ, LANES), lambda h, b: (b, c0 + h))
    return pl.pallas_call(
        _attn_kernel,
        grid=(ATTN_HEADS, BATCH),
        in_specs=[
            pl.BlockSpec(memory_space=pltpu.SMEM),
            pl.BlockSpec((AT_NBRANCH, AT_BLK, 2 * AT_BLK), lambda h, b: (0, 0, 0)),
            col(AT_Q_COL0), col(AT_K_COL0), col(AT_V_COL0),
        ],
        out_specs=pl.BlockSpec((SEQ, LANES), lambda h, b: (b, h)),
        out_shape=jax.ShapeDtypeStruct((N_TOK, ATTN_WIDTH), F32),
        scratch_shapes=[
            pltpu.VMEM((AT_NBRANCH, AT_BLK, 2 * AT_BLK), F32),
            pltpu.VMEM((AT_NBRANCH - 2, SEQ, LANES), F32),
            pltpu.VMEM((AT_NBRANCH - 2, SEQ, LANES), F32),
            pltpu.VMEM((3, AT_PAD_ROWS, LANES), F32),
            pltpu.VMEM((AT_PAD_ROWS, LANES), F32),
            pltpu.VMEM((AT_PAD_ROWS, LANES), F32),
        ],
        compiler_params=pltpu.CompilerParams(
            dimension_semantics=("parallel", "arbitrary"), vmem_limit_bytes=VMEM_LIMIT),
        name="dilated_attn",
    )(rel_bias, bucket_table, proj, proj, proj)


DN_TILE = 2 * DN_CHUNK
DN_NTILE = SEQ // DN_TILE
DN_BH = BATCH * DN_HEADS
DN_PAD = 8
DN_GROUP = 16
DN_SCAN_TILES = 4
DN_AKT_ROWS = DN_CHUNK + DN_HEAD_DIM
DN_INV_LEVELS = 6
assert 2 ** DN_INV_LEVELS == DN_CHUNK
DN_QKV_COL0 = (SSM_WIDTH + 3 * ATTN_WIDTH) // LANES
DN_Z_COL0 = DN_QKV_COL0 + 3 * DN_HEADS
DN_AB_COL = DN_Z_COL0 + DN_HEADS
HIGHEST = lax.Precision.HIGHEST


def _dot_nt(a, b):
    return lax.dot_general(a, b, (((1,), (1,)), ((), ())), preferred_element_type=F32)


def _split3(x):
    hi = x.astype(BF16)
    r = x - hi.astype(F32)
    mid = r.astype(BF16)
    return hi, mid, (r - mid.astype(F32)).astype(BF16)


def _dot_exact_lhs(x, m16):
    return sum(jnp.dot(piece, m16, preferred_element_type=F32) for piece in _split3(x))


def _dot_exact_rhs(m16, x):
    return sum(jnp.dot(m16, piece, preferred_element_type=F32) for piece in _split3(x))


def _dn_tile_masks():
    row = lax.broadcasted_iota(jnp.int32, (DN_TILE, DN_TILE), 0)
    col = lax.broadcasted_iota(jnp.int32, (DN_TILE, DN_TILE), 1)
    causal = jnp.logical_and((row // DN_CHUNK) == (col // DN_CHUNK), row >= col)
    return row, col, causal


DN_GATE_ROWS = 1024


def _dn_gates_kernel(ab_ref, alog_ref, dtb_ref, o_ref):
    _, col, causal = _dn_tile_masks()
    cumsum_mat = causal.astype(BF16)
    neg_a = -jnp.exp(alog_ref[...])
    for t in range(DN_GATE_ROWS // DN_TILE):
        rows = slice(t * DN_TILE, (t + 1) * DN_TILE)
        ab = jnp.where(col < 2 * DN_HEADS, ab_ref[rows, :], 0.0)
        x = ab + dtb_ref[...]
        g = neg_a * (jnp.maximum(x, 0.0) + jnp.log1p(jnp.exp(-jnp.abs(x))))
        gc = _dot_exact_rhs(cumsum_mat, g)
        o_ref[rows, :] = jnp.where(col < DN_HEADS, gc, jax.nn.sigmoid(ab))


def _dn_gates(proj, a_log, dt_bias):
    lanes = lambda v: jnp.pad(v, (0, LANES - DN_HEADS)).reshape(1, LANES)
    return pl.pallas_call(
        _dn_gates_kernel,
        grid=(N_TOK // DN_GATE_ROWS,),
        in_specs=[
            pl.BlockSpec((DN_GATE_ROWS, LANES), lambda i: (i, DN_AB_COL)),
            pl.BlockSpec((1, LANES), lambda i: (0, 0)),
            pl.BlockSpec((1, LANES), lambda i: (0, 0)),
        ],
        out_specs=pl.BlockSpec((DN_GATE_ROWS, LANES), lambda i: (i, 0)),
        out_shape=jax.ShapeDtypeStruct((N_TOK, LANES), F32),
        compiler_params=pltpu.CompilerParams(dimension_semantics=("parallel",), vmem_limit_bytes=VMEM_LIMIT),
        name="dn_gates",
    )(proj, lanes(a_log), lanes(dt_bias))


def _dn_prep_kernel(q_ref, k_ref, v_ref, z_ref, gate_ref, wq_ref, wk_ref, wv_ref,
                    u_o, wq_o, akt_o, dec_o, sz_o, qp, kp, vp):
    h = pl.program_id(1)
    for src, dst in ((q_ref, qp), (k_ref, kp), (v_ref, vp)):
        dst[0:DN_PAD, :] = jnp.zeros((DN_PAD, LANES), F32)
        dst[DN_PAD:DN_PAD + SEQ, :] = src[...]

    row, col, causal = _dn_tile_masks()
    pick = jnp.concatenate([row == h, row == h + DN_HEADS], axis=1).astype(BF16)
    eye = (row == col).astype(F32)
    pair_masks = [
        jnp.logical_and(jnp.logical_and((row // (2 * s)) == (col // (2 * s)), (row // s) % 2 == 1),
                        (col // s) % 2 == 0)
        for s in (2 ** i for i in range(DN_INV_LEVELS))]
    scale = DN_HEAD_DIM ** -0.5

    def conv_silu(pad_ref, w_ref, base):
        acc = None
        for j in range(DN_CONV):
            sh = DN_PAD - (DN_CONV - 1) + j
            term = w_ref[j:j + 1, :] * pad_ref[pl.ds(base + sh, DN_TILE), :]
            acc = term if acc is None else acc + term
        return _silu(acc)

    def l2n(x):
        return x * lax.rsqrt(jnp.sum(x * x, axis=1, keepdims=True) + NORM_EPS)

    def mm16(a, b):
        return jnp.dot(a.astype(BF16), b.astype(BF16), preferred_element_type=F32)

    def body(it, carry):
        tiles = [it * DN_GROUP + i for i in range(DN_GROUP)]
        bases = [pl.multiple_of(t * DN_TILE, DN_TILE) for t in tiles]
        q = [l2n(conv_silu(qp, wq_ref, b)) * scale for b in bases]
        k = [l2n(conv_silu(kp, wk_ref, b)) for b in bases]
        v = [conv_silu(vp, wv_ref, b) for b in bases]
        gate_rep = [_dot_exact_lhs(gate_ref[pl.ds(b, DN_TILE), :], pick) for b in bases]
        gc = [r[:, :DN_TILE] for r in gate_rep]
        beta = [r[:, DN_TILE:] for r in gate_rep]
        decay = [jnp.exp(jnp.where(causal, c - c.T, NEG_INF)) for c in gc]
        kb = [ki * bi for ki, bi in zip(k, beta)]
        k16 = [ki.astype(BF16) for ki in k]
        a_mat = [_dot_nt(kbi.astype(BF16), ki) * di for kbi, ki, di in zip(kb, k16, decay)]
        t_inv = [eye - jnp.where(pair_masks[0], am, 0.0) for am in a_mat]
        for mask in pair_masks[1:]:
            low = [jnp.where(mask, am, 0.0).astype(BF16) for am in a_mat]
            t_inv = [ti - mm16(mm16(ti, lo), ti) for ti, lo in zip(t_inv, low)]
        egc = [jnp.exp(c) for c in gc]
        uw = [mm16(ti, jnp.concatenate([vi * bi, kbi * ei], axis=1))
              for ti, vi, bi, kbi, ei in zip(t_inv, v, beta, kb, egc)]
        at = [(_dot_nt(qi.astype(BF16), ki) * di).astype(BF16) for qi, ki, di in zip(q, k16, decay)]
        qd = [(qi * ei).astype(BF16) for qi, ei in zip(q, egc)]
        c = DN_CHUNK
        for i, t in enumerate(tiles):
            rows = pl.ds(bases[i], DN_TILE)
            u_o[t] = uw[i][:, :DN_HEAD_DIM]
            w = uw[i][:, DN_HEAD_DIM:].astype(BF16)
            wq_o[t] = jnp.concatenate([w[:c], qd[i][:c], w[c:], qd[i][c:]], axis=0)
            gc_first, gc_second = gc[i][c - 1:c, :], gc[i][DN_TILE - 1:DN_TILE, :]
            gc_last = jnp.where(row < c, gc_first, gc_second)
            kt = (k[i] * jnp.exp(gc_last - gc[i])).T
            akt_o[t] = jnp.concatenate([at[i][:c] + at[i][c:], kt.astype(BF16)], axis=0)
            dec_o[t] = jnp.concatenate(
                [jnp.exp(gc_first), jnp.exp(gc_second), jnp.zeros((DN_PAD - 2, LANES), F32)], axis=0)
            z = z_ref[rows, :]
            sz_o[t] = _silu(z).astype(BF16)
        return carry

    lax.fori_loop(0, DN_NTILE // DN_GROUP, body, 0)


def _dn_prep(proj, gates, conv_w):
    col = lambda c0: pl.BlockSpec((SEQ, LANES), lambda b, h: (b, c0 + h))
    cw = lambda c0: pl.BlockSpec((DN_CONV, LANES), lambda b, h: (0, c0 + h))
    bh_rows = lambda n: pl.BlockSpec((DN_NTILE, None, n, LANES), lambda b, h: (0, b * DN_HEADS + h, 0, 0))
    rows_shape = lambda n, dt: jax.ShapeDtypeStruct((DN_NTILE, DN_BH, n, LANES), dt)
    return pl.pallas_call(
        _dn_prep_kernel,
        grid=(BATCH, DN_HEADS),
        in_specs=[
            col(DN_QKV_COL0), col(DN_QKV_COL0 + DN_HEADS), col(DN_QKV_COL0 + 2 * DN_HEADS), col(DN_Z_COL0),
            pl.BlockSpec((SEQ, LANES), lambda b, h: (b, 0)),
            cw(0), cw(DN_HEADS), cw(2 * DN_HEADS),
        ],
        out_specs=[bh_rows(DN_TILE), bh_rows(2 * DN_TILE), bh_rows(DN_AKT_ROWS), bh_rows(DN_PAD), bh_rows(DN_TILE)],
        out_shape=[
            rows_shape(DN_TILE, F32), rows_shape(2 * DN_TILE, BF16), rows_shape(DN_AKT_ROWS, BF16),
            rows_shape(DN_PAD, F32), rows_shape(DN_TILE, BF16),
        ],
        scratch_shapes=[pltpu.VMEM((SEQ + DN_PAD, LANES), F32)] * 3,
        compiler_params=pltpu.CompilerParams(
            dimension_semantics=("parallel", "parallel"), vmem_limit_bytes=VMEM_LIMIT),
        name="dn_prep",
    )(proj, proj, proj, proj, gates, conv_w, conv_w, conv_w)


def _dn_scan_kernel(u_ref, wq_ref, akt_ref, dec_ref, sz_ref, gain_ref, o_ref, s_scr):
    @pl.when(pl.program_id(0) == 0)
    def _():
        s_scr[...] = jnp.zeros_like(s_scr)

    gain = gain_ref[...]
    c = DN_CHUNK
    chains = range(DN_BH)
    second_chunk_lane = lax.broadcasted_iota(jnp.int32, (DN_AKT_ROWS, LANES), 1) >= c
    state = [s_scr[bh] for bh in chains]
    v_first = None
    for tt, j in ((tt, j) for tt in range(DN_SCAN_TILES) for j in range(2)):
        rows = slice(j * c, (j + 1) * c)
        out_rows = slice(tt * DN_TILE + j * c, tt * DN_TILE + (j + 1) * c)
        s16 = [s.astype(BF16) for s in state]
        ws_qs = [jnp.dot(wq_ref[tt, bh, j * DN_TILE:(j + 1) * DN_TILE, :], s16[bh],
                         preferred_element_type=F32) for bh in chains]
        v_new = [u_ref[tt, bh, rows, :] - ws_qs[bh][:c] for bh in chains]
        if j == 0:
            v_pair = [jnp.concatenate([v, jnp.zeros_like(v)], axis=0).astype(BF16) for v in v_new]
            v_first = v_new
            akt = [akt_ref[tt, bh] for bh in chains]
        else:
            v_pair = [jnp.concatenate([v0, v], axis=0).astype(BF16) for v0, v in zip(v_first, v_new)]
            akt = [jnp.where(second_chunk_lane, akt_ref[tt, bh], jnp.zeros((DN_AKT_ROWS, LANES), BF16))
                   for bh in chains]
        ov_kv = [jnp.dot(akt[bh], v_pair[bh], preferred_element_type=F32) for bh in chains]
        state = [state[bh] * dec_ref[tt, bh, j:j + 1, :] + ov_kv[bh][c:] for bh in chains]
        for bh in chains:
            b, h = divmod(bh, DN_HEADS)
            o = ws_qs[bh][c:] + ov_kv[bh][:c]
            o_ref[b, out_rows, h * DN_HEAD_DIM:(h + 1) * DN_HEAD_DIM] = (
                _rms(o, gain) * sz_ref[tt, bh, rows, :].astype(F32))
    for bh in chains:
        s_scr[bh] = state[bh]


def _dn_scan(u, wq, akt, dec, sz, gain):
    rows = lambda n: pl.BlockSpec((DN_SCAN_TILES, DN_BH, n, LANES), lambda t: (t, 0, 0, 0))
    return pl.pallas_call(
        _dn_scan_kernel,
        grid=(DN_NTILE // DN_SCAN_TILES,),
        in_specs=[rows(DN_TILE), rows(2 * DN_TILE), rows(DN_AKT_ROWS), rows(DN_PAD), rows(DN_TILE),
                  pl.BlockSpec((1, LANES), lambda t: (0, 0))],
        out_specs=pl.BlockSpec((BATCH, DN_SCAN_TILES * DN_TILE, DN_WIDTH), lambda t: (0, t, 0)),
        out_shape=jax.ShapeDtypeStruct((BATCH, SEQ, DN_WIDTH), F32),
        scratch_shapes=[pltpu.VMEM((DN_BH, DN_HEAD_DIM, DN_HEAD_DIM), F32)],
        compiler_params=pltpu.CompilerParams(
            dimension_semantics=("arbitrary",), vmem_limit_bytes=VMEM_LIMIT),
        name="dn_scan",
    )(u, wq, akt, dec, sz, gain)


def _dn_mixer(proj, p, l):
    gates = _dn_gates(proj, p['dn_a_log'][l], p['dn_dt_bias'][l])
    outs = _dn_prep(proj, gates, p['dn_conv_w'][l])
    y = _dn_scan(*outs, p['dn_norm_gain'][l].reshape(1, DN_HEAD_DIM))
    return y.reshape(N_TOK, DN_WIDTH)


def kernel(x, norm_gains, ffn_w_gate, ffn_w_up, ffn_w_down, w_in, w_out, ssm_lambda_re, ssm_lambda_im,
           ssm_b_re, ssm_b_im, ssm_c_re, ssm_c_im, ssm_d, ssm_log_dt, ssm_glu_w, ssm_glu_b, ssm_out_gain,
           dn_conv_w, dn_a_log, dn_dt_bias, dn_norm_gain, attn_out_gain, rel_bias):
    p = dict(ssm_lambda_re=ssm_lambda_re, ssm_lambda_im=ssm_lambda_im, ssm_b_re=ssm_b_re, ssm_b_im=ssm_b_im,
             ssm_c_re=ssm_c_re, ssm_c_im=ssm_c_im, ssm_d=ssm_d, ssm_log_dt=ssm_log_dt, ssm_glu_w=ssm_glu_w,
             ssm_glu_b=ssm_glu_b, ssm_out_gain=ssm_out_gain, dn_conv_w=dn_conv_w, dn_a_log=dn_a_log,
             dn_dt_bias=dn_dt_bias, dn_norm_gain=dn_norm_gain)
    wg, wu, wd = ffn_w_gate, ffn_w_up, ffn_w_down
    w_in_p = w_in.astype(BF16)
    w_out_b = w_out
    bucket_table = _attn_bucket_table()
    gains = norm_gains.reshape(DEPTH, 6, 1, D_MODEL)
    x = x.reshape(N_TOK, D_MODEL)
    for l in range(DEPTH):
        x = _ffn(x, gains, wg, wu, wd, l, 0)
        proj = _inproj(x, gains, w_in_p, l)
        y_ssm = _s5_mixer(proj, p, l)
        y_dn = _dn_mixer(proj, p, l)
        o_at = _attn(proj, rel_bias, bucket_table)
        x = _outproj(x, y_ssm, y_dn, o_at, attn_out_gain[l].reshape(1, ATTN_WIDTH), gains, w_out_b, l)
        x = _ffn(x, gains, wg, wu, wd, l, 1)
    return x.reshape(BATCH, SEQ, D_MODEL)
```

```python
import functools
import math

import jax
import jax.numpy as jnp
from jax import lax
from jax.experimental import pallas as pl
from jax.experimental.pallas import tpu as pltpu

D_MODEL = 2048
BATCH = 2
SEQ = 4096
DEPTH = 4
SSM_GROUPS = 32
SSM_CH = 16
SSM_STATE = 64
SSM_WIDTH = SSM_GROUPS * SSM_CH
DN_HEADS = 6
DN_HEAD_DIM = 128
DN_WIDTH = DN_HEADS * DN_HEAD_DIM
DN_CONV = 4
DN_CHUNK = 64
ATTN_HEADS = 6
ATTN_HEAD_DIM = 128
ATTN_WIDTH = ATTN_HEADS * ATTN_HEAD_DIM
DILATED_PAIRS = ((128, 1), (512, 4), (2048, 16))
ATTN_BLOCK = 128
N_BUCKETS = 32
REL_MAX_DIST = 2048
D_MIX = SSM_WIDTH + DN_WIDTH + ATTN_WIDTH
IN_SPLITS = (SSM_WIDTH, ATTN_WIDTH, ATTN_WIDTH, ATTN_WIDTH, 3 * DN_WIDTH, DN_WIDTH, DN_HEADS, DN_HEADS)
N_IN_COLS = sum(IN_SPLITS)
D_FF = 5632
NORM_EPS = 1e-6
NEG_INF = -1e30

LANES = 128
N_IN_PAD = 6144
V7X_VMEM_BYTES = 64 * 1024 * 1024
VMEM_LIMIT = V7X_VMEM_BYTES - 4 * 1024 * 1024
N_TOK = BATCH * SEQ

BF16 = jnp.bfloat16
F32 = jnp.float32


def _rms(x, gain):
    return x * lax.rsqrt(jnp.mean(x * x, axis=-1, keepdims=True) + NORM_EPS) * gain


def _sigmoid(x):
    return 0.5 + 0.5 * jnp.tanh(0.5 * x)


def _silu(x):
    half = 0.5 * x
    return half + half * jnp.tanh(half)


FFN_TM = 1024
FFN_TF = 256


def _ffn_kernel(x_ref, gpre_ref, gpost_ref, wg_ref, wu_ref, wd_ref, o_ref, h_scr):
    f = pl.program_id(1)
    last = pl.num_programs(1) - 1

    def down_proj(h):
        gate = jnp.dot(h, wg_ref[...].astype(BF16), preferred_element_type=F32)
        up = jnp.dot(h, wu_ref[...].astype(BF16), preferred_element_type=F32)
        act = (_silu(gate) * up).astype(BF16)
        return jnp.dot(act, wd_ref[...].astype(BF16), preferred_element_type=F32)

    @pl.when(f == 0)
    def _():
        h = _rms(x_ref[...], gpre_ref[...]).astype(BF16)
        h_scr[...] = h
        o_ref[...] = down_proj(h)

    @pl.when(jnp.logical_and(f > 0, f < last))
    def _():
        o_ref[...] += down_proj(h_scr[...])

    @pl.when(f == last)
    def _():
        y = o_ref[...] + down_proj(h_scr[...])
        o_ref[...] = x_ref[...] + _rms(y, 0.5 * gpost_ref[...])


def _ffn(x, gains, wg, wu, wd, layer, half):
    return pl.pallas_call(
        _ffn_kernel,
        grid=(N_TOK // FFN_TM, D_FF // FFN_TF),
        in_specs=[
            pl.BlockSpec((FFN_TM, D_MODEL), lambda i, f: (i, 0)),
            pl.BlockSpec((None, None, 1, D_MODEL), lambda i, f: (layer, 4 * half, 0, 0)),
            pl.BlockSpec((None, None, 1, D_MODEL), lambda i, f: (layer, 4 * half + 1, 0, 0)),
            pl.BlockSpec((None, None, D_MODEL, FFN_TF), lambda i, f: (layer, half, 0, f)),
            pl.BlockSpec((None, None, D_MODEL, FFN_TF), lambda i, f: (layer, half, 0, f)),
            pl.BlockSpec((None, None, FFN_TF, D_MODEL), lambda i, f: (layer, half, f, 0)),
        ],
        out_specs=pl.BlockSpec((FFN_TM, D_MODEL), lambda i, f: (i, 0)),
        out_shape=jax.ShapeDtypeStruct((N_TOK, D_MODEL), F32),
        scratch_shapes=[pltpu.VMEM((FFN_TM, D_MODEL), BF16)],
        compiler_params=pltpu.CompilerParams(
            dimension_semantics=("parallel", "arbitrary"), vmem_limit_bytes=VMEM_LIMIT),
        name="ffn",
    )(x, gains, gains, wg, wu, wd)


INP_TM = 1024
INP_TN = 1536


def _inproj_kernel(x_ref, g_ref, w_ref, o_ref, h_scr):
    @pl.when(pl.program_id(1) == 0)
    def _():
        h = _rms(x_ref[...], g_ref[...]).astype(BF16)
        h_scr[...] = h
        o_ref[...] = jnp.dot(h, w_ref[...], preferred_element_type=F32)

    @pl.when(pl.program_id(1) > 0)
    def _():
        o_ref[...] = jnp.dot(h_scr[...], w_ref[...], preferred_element_type=F32)


def _inproj(x, gains, w, layer):
    return pl.pallas_call(
        _inproj_kernel,
        grid=(N_TOK // INP_TM, N_IN_PAD // INP_TN),
        in_specs=[
            pl.BlockSpec((INP_TM, D_MODEL), lambda i, n: (i, 0)),
            pl.BlockSpec((None, None, 1, D_MODEL), lambda i, n: (layer, 2, 0, 0)),
            pl.BlockSpec((None, D_MODEL, INP_TN), lambda i, n: (layer, 0, n)),
        ],
        out_specs=pl.BlockSpec((INP_TM, INP_TN), lambda i, n: (i, n)),
        out_shape=jax.ShapeDtypeStruct((N_TOK, N_IN_PAD), F32),
        scratch_shapes=[pltpu.VMEM((INP_TM, D_MODEL), BF16)],
        compiler_params=pltpu.CompilerParams(
            dimension_semantics=("parallel", "arbitrary"), vmem_limit_bytes=VMEM_LIMIT),
        name="inproj",
    )(x, gains, w)


OUT_TM = 512
OUT_SUB = 256


def _outproj_kernel(x_ref, ys_ref, yd_ref, oa_ref, ga_ref, gpost_ref, w_ref, o_ref, w16):
    @pl.when(pl.program_id(0) == 0)
    def _():
        w16[...] = w_ref[...].astype(BF16)

    halves = [slice(i * OUT_SUB, (i + 1) * OUT_SUB) for i in range(OUT_TM // OUT_SUB)]
    ya = [_rms(oa_ref[r, :], ga_ref[...]).astype(BF16) for r in halves]
    mix = [jnp.dot(ys_ref[r, :].astype(BF16), w16[0:SSM_WIDTH, :], preferred_element_type=F32) for r in halves]
    mix = [m + jnp.dot(yd_ref[r, :].astype(BF16), w16[SSM_WIDTH:SSM_WIDTH + DN_WIDTH, :],
                       preferred_element_type=F32) for m, r in zip(mix, halves)]
    mix = [m + jnp.dot(a, w16[SSM_WIDTH + DN_WIDTH:D_MIX, :], preferred_element_type=F32)
           for m, a in zip(mix, ya)]
    for r, m in zip(halves, mix):
        o_ref[r, :] = x_ref[r, :] + _rms(m, gpost_ref[...])


def _outproj(x, y_ssm, y_dn, o_at, g_attn, gains, w, layer):
    row = lambda i: (i, 0)
    fixed = lambda i: (0, 0)
    return pl.pallas_call(
        _outproj_kernel,
        grid=(N_TOK // OUT_TM,),
        in_specs=[
            pl.BlockSpec((OUT_TM, D_MODEL), row),
            pl.BlockSpec((OUT_TM, SSM_WIDTH), row),
            pl.BlockSpec((OUT_TM, DN_WIDTH), row),
            pl.BlockSpec((OUT_TM, ATTN_WIDTH), row),
            pl.BlockSpec((1, ATTN_WIDTH), fixed),
            pl.BlockSpec((None, None, 1, D_MODEL), lambda i: (layer, 3, 0, 0)),
            pl.BlockSpec((None, D_MIX, D_MODEL), lambda i: (layer, 0, 0), pipeline_mode=pl.Buffered(1)),
        ],
        out_specs=pl.BlockSpec((OUT_TM, D_MODEL), row),
        out_shape=jax.ShapeDtypeStruct((N_TOK, D_MODEL), F32),
        scratch_shapes=[pltpu.VMEM((D_MIX, D_MODEL), BF16)],
        compiler_params=pltpu.CompilerParams(
            dimension_semantics=("arbitrary",), vmem_limit_bytes=VMEM_LIMIT),
        name="outproj",
    )(x, y_ssm, y_dn, o_at, g_attn, gains, w)


S5_SEG = 8
S5_SEGLEN = SEQ // S5_SEG
S5_KB = 128
S5_ROWS = S5_KB * S5_SEG
S5_PITCH = S5_KB + 8
S5_PROWS = S5_PITCH * S5_SEG
S5_NS = SSM_GROUPS * SSM_STATE
S5_SLABS = S5_NS // LANES
S5_Q = 4
S5_QS = S5_NS // S5_Q
S5_QL = S5_QS // LANES
S5_QC = SSM_WIDTH // S5_Q
S5_LOG2_SEGLEN = 9
assert 1 << S5_LOG2_SEGLEN == S5_SEGLEN


def _s5_kernel(u_ref, wb_ref, are_ref, aim_ref, cre_ref, cim_ref, d_ref, gw_ref, gb_ref, go_ref, o_ref,
               bu_scr, st_scr, carry_scr, y_scr):
    p = pl.program_id(1)
    j = pl.program_id(2)
    u = u_ref[0].reshape(S5_ROWS, SSM_WIDTH)
    ub = u.astype(BF16)

    @pl.when(jnp.logical_and(p == 0, j == 0))
    def _():
        bu_scr[...] = jnp.zeros_like(bu_scr)
        st_scr[...] = jnp.zeros_like(st_scr)

    @pl.when(jnp.logical_and(p == 1, j == 0))
    def _():
        st_scr[...] = carry_scr[...]

    for q in range(S5_Q):
        r = jnp.dot(ub[:, q * S5_QC:(q + 1) * S5_QC], wb_ref[q], preferred_element_type=F32)
        for half in range(2):
            for c in range(S5_QL):
                slab = half * S5_SLABS + q * S5_QL + c
                lanes = slice((half * S5_QL + c) * LANES, (half * S5_QL + c + 1) * LANES)
                for i in range(S5_SEG):
                    bu_scr[slab, i * S5_PITCH:i * S5_PITCH + S5_KB, :] = r[i * S5_KB:(i + 1) * S5_KB, lanes]

    def scan(store):
        for q in range(S5_Q):
            slabs = [q * S5_QL + c for c in range(S5_QL)]
            lanes = [slice(s * LANES, (s + 1) * LANES) for s in slabs]
            ilanes = [slice(S5_NS + s * LANES, S5_NS + (s + 1) * LANES) for s in slabs]
            ar = [jnp.broadcast_to(are_ref[:, ln], (S5_SEG, LANES)) for ln in lanes]
            ai = [jnp.broadcast_to(aim_ref[:, ln], (S5_SEG, LANES)) for ln in lanes]

            def body(k, carry):
                sr, si = carry
                rows = pl.ds(k, S5_SEG, stride=S5_PITCH)
                nr = tuple(ar[c] * sr[c] - ai[c] * si[c] + bu_scr[slabs[c], rows, :] for c in range(S5_QL))
                ni = tuple(ar[c] * si[c] + ai[c] * sr[c] + bu_scr[slabs[c] + S5_SLABS, rows, :]
                           for c in range(S5_QL))
                if store:
                    for c in range(S5_QL):
                        bu_scr[slabs[c], rows, :] = nr[c]
                        bu_scr[slabs[c] + S5_SLABS, rows, :] = ni[c]
                return nr, ni

            init = (tuple(st_scr[:, ln] for ln in lanes), tuple(st_scr[:, ln] for ln in ilanes))
            sr, si = lax.fori_loop(0, S5_KB, body, init, unroll=8)
            for c in range(S5_QL):
                st_scr[:, lanes[c]] = sr[c]
                st_scr[:, ilanes[c]] = si[c]

    @pl.when(p == 0)
    def _():
        scan(False)

        @pl.when(j == pl.num_programs(2) - 1)
        def _():
            lr, li = are_ref[...], aim_ref[...]
            for _ in range(S5_LOG2_SEGLEN):
                lr, li = lr * lr - li * li, 2.0 * lr * li
            cr = jnp.zeros((1, S5_NS), F32)
            ci = jnp.zeros((1, S5_NS), F32)
            carry_scr[0:1, :] = jnp.zeros((1, 2 * S5_NS), F32)
            for i in range(1, S5_SEG):
                er = st_scr[i - 1:i, 0:S5_NS]
                ei = st_scr[i - 1:i, S5_NS:2 * S5_NS]
                cr, ci = er + lr * cr - li * ci, ei + lr * ci + li * cr
                carry_scr[i:i + 1, 0:S5_NS] = cr
                carry_scr[i:i + 1, S5_NS:2 * S5_NS] = ci

    @pl.when(p == 1)
    def _():
        scan(True)
        for q in range(S5_Q):
            sre = jnp.concatenate([bu_scr[q * S5_QL + c] for c in range(S5_QL)], axis=1).astype(BF16)
            sim = jnp.concatenate([bu_scr[S5_SLABS + q * S5_QL + c] for c in range(S5_QL)], axis=1).astype(BF16)
            y_scr[:, q * S5_QC:(q + 1) * S5_QC] = (
                jnp.dot(sre, cre_ref[q], preferred_element_type=F32)
                + jnp.dot(sim, cim_ref[q], preferred_element_type=F32))
        cs = jnp.concatenate([y_scr[i * S5_PITCH:i * S5_PITCH + S5_KB, :] for i in range(S5_SEG)], axis=0)
        y = jax.nn.gelu(cs + d_ref[...] * u)
        z = jnp.dot(y.astype(BF16), gw_ref[...], preferred_element_type=F32) + gb_ref[...]
        o_ref[0] = _rms(y * _sigmoid(z), go_ref[...]).reshape(S5_SEG, S5_KB, SSM_WIDTH)


def _s5_params(lam_re, lam_im, b_re, b_im, c_re, c_im, log_dt):
    lam = lax.complex(lam_re, lam_im)
    lam_bar = jnp.exp(lam * jnp.exp(log_dt)[:, None])
    b_bar = ((lam_bar - 1.0) / lam)[..., None] * lax.complex(b_re, b_im)
    gq = SSM_GROUPS // S5_Q
    eye = jnp.eye(gq, dtype=F32)

    def in_map(t):
        t = t.reshape(S5_Q, gq, SSM_STATE, SSM_CH)
        return jnp.einsum('qgpc,gh->qgchp', t, eye).reshape(S5_Q, S5_QC, S5_QS)

    def out_map(t):
        t = t.reshape(S5_Q, gq, SSM_CH, SSM_STATE)
        return jnp.einsum('qgcp,gh->qgphc', t, eye).reshape(S5_Q, S5_QS, S5_QC)

    wb = jnp.concatenate([in_map(b_bar.real), in_map(b_bar.imag)], axis=-1).astype(BF16)
    return (wb, lam_bar.real.reshape(1, S5_NS), lam_bar.imag.reshape(1, S5_NS),
            out_map(c_re).astype(BF16), out_map(-c_im).astype(BF16))


def _s5(proj4, wb, a_re, a_im, cre, cim, d_skip, glu_w, glu_b, out_gain):
    nblk = S5_SEGLEN // S5_KB
    fix2 = lambda b, p, j: (0, 0)
    fix3 = lambda b, p, j: (0, 0, 0)
    return pl.pallas_call(
        _s5_kernel,
        grid=(BATCH, 2, nblk),
        in_specs=[
            pl.BlockSpec((1, S5_SEG, S5_KB, SSM_WIDTH), lambda b, p, j: (b, 0, j, 0)),
            pl.BlockSpec((S5_Q, S5_QC, 2 * S5_QS), fix3),
            pl.BlockSpec((1, S5_NS), fix2),
            pl.BlockSpec((1, S5_NS), fix2),
            pl.BlockSpec((S5_Q, S5_QS, S5_QC), fix3),
            pl.BlockSpec((S5_Q, S5_QS, S5_QC), fix3),
            pl.BlockSpec((1, SSM_WIDTH), fix2),
            pl.BlockSpec((SSM_WIDTH, SSM_WIDTH), fix2),
            pl.BlockSpec((1, SSM_WIDTH), fix2),
            pl.BlockSpec((1, SSM_WIDTH), fix2),
        ],
        out_specs=pl.BlockSpec((1, S5_SEG, S5_KB, SSM_WIDTH), lambda b, p, j: (b, 0, j * p, 0)),
        out_shape=jax.ShapeDtypeStruct((BATCH, S5_SEG, S5_SEGLEN, SSM_WIDTH), F32),
        scratch_shapes=[
            pltpu.VMEM((2 * S5_SLABS, S5_PROWS, LANES), F32),
            pltpu.VMEM((S5_SEG, 2 * S5_NS), F32),
            pltpu.VMEM((S5_SEG, 2 * S5_NS), F32),
            pltpu.VMEM((S5_PROWS, SSM_WIDTH), F32),
        ],
        compiler_params=pltpu.CompilerParams(
            dimension_semantics=("parallel", "arbitrary", "arbitrary"), vmem_limit_bytes=VMEM_LIMIT),
        name="s5",
    )(proj4, wb, a_re, a_im, cre, cim, d_skip, glu_w, glu_b, out_gain)


def _s5_mixer(proj, p, l):
    wb, a_re, a_im, cre, cim = _s5_params(p['ssm_lambda_re'][l], p['ssm_lambda_im'][l], p['ssm_b_re'][l],
                                          p['ssm_b_im'][l], p['ssm_c_re'][l], p['ssm_c_im'][l],
                                          p['ssm_log_dt'][l])
    y = _s5(proj.reshape(BATCH, S5_SEG, S5_SEGLEN, N_IN_PAD), wb, a_re, a_im, cre, cim,
            p['ssm_d'][l].reshape(1, SSM_WIDTH), p['ssm_glu_w'][l].astype(BF16),
            p['ssm_glu_b'][l].reshape(1, SSM_WIDTH), p['ssm_out_gain'][l].reshape(1, SSM_WIDTH))
    return y.reshape(N_TOK, SSM_WIDTH)


AT_BLK = ATTN_BLOCK
AT_NBRANCH = len(DILATED_PAIRS)
AT_BLOCKS = SEQ // AT_BLK
AT_GROUP = 8
AT_ORDER = tuple(range(1, AT_NBRANCH)) + (0,)
AT_PAD_GROUP = 16
AT_PAD_PITCH = 24
AT_PAD_ROWS = SEQ // AT_PAD_GROUP * AT_PAD_PITCH
assert [d for _, d in DILATED_PAIRS].count(AT_PAD_GROUP) == 1 and DILATED_PAIRS[-1][1] == AT_PAD_GROUP
assert DILATED_PAIRS[0][1] == 1 and all(d > 1 for _, d in DILATED_PAIRS[1:])
assert all(w // d == AT_BLK for w, d in DILATED_PAIRS)
assert all((AT_BLOCKS // d) % AT_GROUP == 0 or AT_GROUP % (AT_BLOCKS // d) == 0 for _, d in DILATED_PAIRS)
AT_Q_COL0 = SSM_WIDTH // LANES
AT_K_COL0 = AT_Q_COL0 + ATTN_HEADS
AT_V_COL0 = AT_K_COL0 + ATTN_HEADS


def _at_padded_row(s):
    return (s // AT_PAD_GROUP) * AT_PAD_PITCH + s % AT_PAD_GROUP


def _attn_kernel(rb_ref, bkt_ref, q_ref, k_ref, v_ref, o_ref, bias_scr, o_scr, lse_scr,
                 qkv_pad, o_pad, lse_pad):
    h = pl.program_id(0)

    def to_padded(i, carry):
        src_rows = pl.ds(pl.multiple_of(i * AT_PAD_GROUP, AT_PAD_GROUP), AT_PAD_GROUP)
        dst_rows = pl.ds(pl.multiple_of(i * AT_PAD_PITCH, 8), AT_PAD_GROUP)
        for j, ref in enumerate((q_ref, k_ref, v_ref)):
            qkv_pad[j, dst_rows, :] = ref[src_rows, :]
        return carry

    lax.fori_loop(0, SEQ // AT_PAD_GROUP, to_padded, 0, unroll=8)

    @pl.when(pl.program_id(1) == 0)
    def _():
        for g in range(AT_NBRANCH):
            bkt = bkt_ref[g]
            bias = jnp.zeros((AT_BLK, 2 * AT_BLK), F32)
            for b in range(N_BUCKETS):
                bias = jnp.where(bkt == b, rb_ref[b, h], bias)
            bias_scr[g] = jnp.where(bkt < 0, NEG_INF, bias)

    lane = lax.broadcasted_iota(jnp.int32, (AT_BLK, 2 * AT_BLK), 1)
    scale = ATTN_HEAD_DIM ** -0.5
    for g in AT_ORDER:
        dil = DILATED_PAIRS[g][1]
        nb = AT_BLOCKS // dil

        def body(it, carry, g=g, dil=dil, nb=nb):
            padded = dil == AT_PAD_GROUP
            if padded:
                qs, ks, vs = qkv_pad.at[0], qkv_pad.at[1], qkv_pad.at[2]
            else:
                qs, ks, vs = q_ref, k_ref, v_ref

            def block_rows(start):
                if dil == 1:
                    return pl.ds(pl.multiple_of(start, AT_BLK), AT_BLK)
                if padded:
                    return pl.ds(_at_padded_row(start), AT_BLK, stride=AT_PAD_PITCH)
                return pl.ds(start, AT_BLK, stride=dil)

            group = range(AT_GROUP)
            ts = [it * AT_GROUP + i for i in group]
            if nb <= AT_GROUP:
                ns = [i % nb for i in group]
            else:
                ns = [None if i == 0 else i for i in group]
            n0 = ts[0] % nb
            rows = [block_rows(t // nb + (t % nb) * (AT_BLK * dil)) for t in ts]
            q = [(qs[r, :] * scale).astype(BF16) for r in rows]
            k_cur = [ks[r, :].astype(BF16) for r in rows]
            v_cur = [vs[r, :].astype(BF16) for r in rows]
            if ns[0] is None:
                prow = block_rows(ts[0] // nb + jnp.maximum(n0 - 1, 0) * (AT_BLK * dil))
                k_lead, v_lead = ks[prow, :].astype(BF16), vs[prow, :].astype(BF16)
            s, vc = [], []
            for i in group:
                if ns[i] == 0:
                    s.append(_dot_nt(q[i], k_cur[i]) + bias_scr[g, :, AT_BLK:])
                    vc.append(v_cur[i])
                    continue
                k_prev, v_prev = (k_lead, v_lead) if ns[i] is None else (k_cur[i - 1], v_cur[i - 1])
                si = _dot_nt(q[i], jnp.concatenate([k_prev, k_cur[i]], axis=0)) + bias_scr[g]
                if ns[i] is None:
                    si = jnp.where(jnp.logical_or(n0 > 0, lane >= AT_BLK), si, NEG_INF)
                s.append(si)
                vc.append(jnp.concatenate([v_prev, v_cur[i]], axis=0))
            m = [jnp.max(si, axis=1, keepdims=True) for si in s]
            p = [jnp.exp(si - mi) for si, mi in zip(s, m)]
            l = [jnp.sum(pi, axis=1, keepdims=True) for pi in p]
            o = [jnp.dot(pi.astype(BF16), vi, preferred_element_type=F32) / li for pi, vi, li in zip(p, vc, l)]
            lse = [jnp.broadcast_to(mi + jnp.log(li), (AT_BLK, LANES)) for mi, li in zip(m, l)]
            if dil > 1:
                for r, oi, li in zip(rows, o, lse):
                    if padded:
                        o_pad[r, :], lse_pad[r, :] = oi, li
                    else:
                        o_scr[g - 1, r, :], lse_scr[g - 1, r, :] = oi, li
                return carry

            def unpad(ref, start):
                base = (start // AT_PAD_GROUP) * AT_PAD_PITCH
                return jnp.concatenate(
                    [ref[pl.ds(pl.multiple_of(base + j * AT_PAD_PITCH, 8), AT_PAD_GROUP), :]
                     for j in range(AT_BLK // AT_PAD_GROUP)], axis=0)

            for t, r, oi, li in zip(ts, rows, o, lse):
                parked = [(unpad(o_pad, t * AT_BLK), unpad(lse_pad, t * AT_BLK))
                          if DILATED_PAIRS[j][1] == AT_PAD_GROUP else (o_scr[j - 1, r, :], lse_scr[j - 1, r, :])
                          for j in range(1, AT_NBRANCH)]
                top = functools.reduce(jnp.maximum, [lj for _, lj in parked], li)
                wi = jnp.exp(li - top)
                num, den = wi * oi, wi
                for oj, lj in parked:
                    wj = jnp.exp(lj - top)
                    num, den = num + wj * oj, den + wj
                o_ref[r, :] = num / den
            return carry

        lax.fori_loop(0, AT_BLOCKS // AT_GROUP, body, 0)


def _attn_bucket_table():
    qi = jnp.arange(AT_BLK)[:, None]
    kj = jnp.arange(2 * AT_BLK)[None, :]
    rel = AT_BLK + qi - kj
    max_exact = N_BUCKETS // 2
    tables = []
    for window, dil in DILATED_PAIRS:
        dist = jnp.maximum(rel, 0) * dil
        d = jnp.maximum(dist, 1).astype(F32)
        large = max_exact + jnp.log(d / max_exact) / math.log(REL_MAX_DIST / max_exact) * (N_BUCKETS - max_exact)
        large = jnp.minimum(large.astype(jnp.int32), N_BUCKETS - 1)
        bucket = jnp.where(dist < max_exact, dist, large)
        tables.append(jnp.where((rel >= 0) & (rel <= window // dil), bucket, -1))
    return jnp.stack(tables).astype(jnp.int32)


def _attn(proj, rel_bias, bucket_table):
    col = lambda c0: pl.BlockSpec((SEQ, LANES), lambda h, b: (b, c0 + h))
    return pl.pallas_call(
        _attn_kernel,
        grid=(ATTN_HEADS, BATCH),
        in_specs=[
            pl.BlockSpec(memory_space=pltpu.SMEM),
            pl.BlockSpec((AT_NBRANCH, AT_BLK, 2 * AT_BLK), lambda h, b: (0, 0, 0)),
            col(AT_Q_COL0), col(AT_K_COL0), col(AT_V_COL0),
        ],
        out_specs=pl.BlockSpec((SEQ, LANES), lambda h, b: (b, h)),
        out_shape=jax.ShapeDtypeStruct((N_TOK, ATTN_WIDTH), F32),
        scratch_shapes=[
            pltpu.VMEM((AT_NBRANCH, AT_BLK, 2 * AT_BLK), F32),
            pltpu.VMEM((AT_NBRANCH - 2, SEQ, LANES), F32),
            pltpu.VMEM((AT_NBRANCH - 2, SEQ, LANES), F32),
            pltpu.VMEM((3, AT_PAD_ROWS, LANES), F32),
            pltpu.VMEM((AT_PAD_ROWS, LANES), F32),
            pltpu.VMEM((AT_PAD_ROWS, LANES), F32),
        ],
        compiler_params=pltpu.CompilerParams(
            dimension_semantics=("parallel", "arbitrary"), vmem_limit_bytes=VMEM_LIMIT),
        name="dilated_attn",
    )(rel_bias, bucket_table, proj, proj, proj)


DN_TILE = 2 * DN_CHUNK
DN_NTILE = SEQ // DN_TILE
DN_BH = BATCH * DN_HEADS
DN_PAD = 8
DN_GROUP = 16
DN_SCAN_TILES = 4
DN_AKT_ROWS = DN_CHUNK + DN_HEAD_DIM
DN_INV_LEVELS = 6
assert 2 ** DN_INV_LEVELS == DN_CHUNK
DN_QKV_COL0 = (SSM_WIDTH + 3 * ATTN_WIDTH) // LANES
DN_Z_COL0 = DN_QKV_COL0 + 3 * DN_HEADS
DN_AB_COL = DN_Z_COL0 + DN_HEADS
HIGHEST = lax.Precision.HIGHEST


def _dot_nt(a, b):
    return lax.dot_general(a, b, (((1,), (1,)), ((), ())), preferred_element_type=F32)


def _split3(x):
    hi = x.astype(BF16)
    r = x - hi.astype(F32)
    mid = r.astype(BF16)
    return hi, mid, (r - mid.astype(F32)).astype(BF16)


def _dot_exact_lhs(x, m16):
    return sum(jnp.dot(piece, m16, preferred_element_type=F32) for piece in _split3(x))


def _dot_exact_rhs(m16, x):
    return sum(jnp.dot(m16, piece, preferred_element_type=F32) for piece in _split3(x))


def _dn_tile_masks():
    row = lax.broadcasted_iota(jnp.int32, (DN_TILE, DN_TILE), 0)
    col = lax.broadcasted_iota(jnp.int32, (DN_TILE, DN_TILE), 1)
    causal = jnp.logical_and((row // DN_CHUNK) == (col // DN_CHUNK), row >= col)
    return row, col, causal


DN_GATE_ROWS = 1024


def _dn_gates_kernel(ab_ref, alog_ref, dtb_ref, o_ref):
    _, col, causal = _dn_tile_masks()
    cumsum_mat = causal.astype(BF16)
    neg_a = -jnp.exp(alog_ref[...])
    for t in range(DN_GATE_ROWS // DN_TILE):
        rows = slice(t * DN_TILE, (t + 1) * DN_TILE)
        ab = jnp.where(col < 2 * DN_HEADS, ab_ref[rows, :], 0.0)
        x = ab + dtb_ref[...]
        g = neg_a * (jnp.maximum(x, 0.0) + jnp.log1p(jnp.exp(-jnp.abs(x))))
        gc = _dot_exact_rhs(cumsum_mat, g)
        o_ref[rows, :] = jnp.where(col < DN_HEADS, gc, _sigmoid(ab))


def _dn_gates(proj, a_log, dt_bias):
    lanes = lambda v: jnp.pad(v, (0, LANES - DN_HEADS)).reshape(1, LANES)
    return pl.pallas_call(
        _dn_gates_kernel,
        grid=(N_TOK // DN_GATE_ROWS,),
        in_specs=[
            pl.BlockSpec((DN_GATE_ROWS, LANES), lambda i: (i, DN_AB_COL)),
            pl.BlockSpec((1, LANES), lambda i: (0, 0)),
            pl.BlockSpec((1, LANES), lambda i: (0, 0)),
        ],
        out_specs=pl.BlockSpec((DN_GATE_ROWS, LANES), lambda i: (i, 0)),
        out_shape=jax.ShapeDtypeStruct((N_TOK, LANES), F32),
        compiler_params=pltpu.CompilerParams(dimension_semantics=("parallel",), vmem_limit_bytes=VMEM_LIMIT),
        name="dn_gates",
    )(proj, lanes(a_log), lanes(dt_bias))


def _dn_prep_kernel(q_ref, k_ref, v_ref, z_ref, gate_ref, wq_ref, wk_ref, wv_ref,
                    u_o, wq_o, akt_o, dec_o, sz_o, qp, kp, vp):
    h = pl.program_id(1)
    for src, dst in ((q_ref, qp), (k_ref, kp), (v_ref, vp)):
        dst[0:DN_PAD, :] = jnp.zeros((DN_PAD, LANES), F32)
        dst[DN_PAD:DN_PAD + SEQ, :] = src[...]

    row, col, causal = _dn_tile_masks()
    pick = jnp.concatenate([row == h, row == h + DN_HEADS], axis=1).astype(BF16)
    eye = (row == col).astype(F32)
    pair_masks = [
        jnp.logical_and(jnp.logical_and((row // (2 * s)) == (col // (2 * s)), (row // s) % 2 == 1),
                        (col // s) % 2 == 0)
        for s in (2 ** i for i in range(DN_INV_LEVELS))]
    scale = DN_HEAD_DIM ** -0.5

    def conv_silu(pad_ref, w_ref, base):
        acc = None
        for j in range(DN_CONV):
            sh = DN_PAD - (DN_CONV - 1) + j
            term = w_ref[j:j + 1, :] * pad_ref[pl.ds(base + sh, DN_TILE), :]
            acc = term if acc is None else acc + term
        return _silu(acc)

    def l2n(x):
        return x * lax.rsqrt(jnp.sum(x * x, axis=1, keepdims=True) + NORM_EPS)

    def mm16(a, b):
        return jnp.dot(a.astype(BF16), b.astype(BF16), preferred_element_type=F32)

    def body(it, carry):
        tiles = [it * DN_GROUP + i for i in range(DN_GROUP)]
        bases = [pl.multiple_of(t * DN_TILE, DN_TILE) for t in tiles]
        q = [l2n(conv_silu(qp, wq_ref, b)) * scale for b in bases]
        k = [l2n(conv_silu(kp, wk_ref, b)) for b in bases]
        v = [conv_silu(vp, wv_ref, b) for b in bases]
        gate_rep = [_dot_exact_lhs(gate_ref[pl.ds(b, DN_TILE), :], pick) for b in bases]
        gc = [r[:, :DN_TILE] for r in gate_rep]
        beta = [r[:, DN_TILE:] for r in gate_rep]
        decay = [jnp.exp(jnp.where(causal, c - c.T, NEG_INF)) for c in gc]
        kb = [ki * bi for ki, bi in zip(k, beta)]
        k16 = [ki.astype(BF16) for ki in k]
        a_mat = [_dot_nt(kbi.astype(BF16), ki) * di for kbi, ki, di in zip(kb, k16, decay)]
        t_inv = [eye - jnp.where(pair_masks[0], am, 0.0) for am in a_mat]
        for mask in pair_masks[1:]:
            low = [jnp.where(mask, am, 0.0).astype(BF16) for am in a_mat]
            t_inv = [ti - mm16(mm16(ti, lo), ti) for ti, lo in zip(t_inv, low)]
        egc = [jnp.exp(c) for c in gc]
        uw = [mm16(ti, jnp.concatenate([vi * bi, kbi * ei], axis=1))
              for ti, vi, bi, kbi, ei in zip(t_inv, v, beta, kb, egc)]
        at = [(_dot_nt(qi.astype(BF16), ki) * di).astype(BF16) for qi, ki, di in zip(q, k16, decay)]
        qd = [(qi * ei).astype(BF16) for qi, ei in zip(q, egc)]
        c = DN_CHUNK
        for i, t in enumerate(tiles):
            rows = pl.ds(bases[i], DN_TILE)
            u_o[t] = uw[i][:, :DN_HEAD_DIM]
            w = uw[i][:, DN_HEAD_DIM:].astype(BF16)
            wq_o[t] = jnp.concatenate([w[:c], qd[i][:c], w[c:], qd[i][c:]], axis=0)
            gc_first, gc_second = gc[i][c - 1:c, :], gc[i][DN_TILE - 1:DN_TILE, :]
            gc_last = jnp.where(row < c, gc_first, gc_second)
            kt = (k[i] * jnp.exp(gc_last - gc[i])).T
            akt_o[t] = jnp.concatenate([at[i][:c] + at[i][c:], kt.astype(BF16)], axis=0)
            dec_o[t] = jnp.concatenate(
                [jnp.exp(gc_first), jnp.exp(gc_second), jnp.zeros((DN_PAD - 2, LANES), F32)], axis=0)
            z = z_ref[rows, :]
            sz_o[t] = _silu(z).astype(BF16)
        return carry

    lax.fori_loop(0, DN_NTILE // DN_GROUP, body, 0)


def _dn_prep(proj, gates, conv_w):
    col = lambda c0: pl.BlockSpec((SEQ, LANES), lambda b, h: (b, c0 + h))
    cw = lambda c0: pl.BlockSpec((DN_CONV, LANES), lambda b, h: (0, c0 + h))
    bh_rows = lambda n: pl.BlockSpec((DN_NTILE, None, n, LANES), lambda b, h: (0, b * DN_HEADS + h, 0, 0))
    rows_shape = lambda n, dt: jax.ShapeDtypeStruct((DN_NTILE, DN_BH, n, LANES), dt)
    return pl.pallas_call(
        _dn_prep_kernel,
        grid=(BATCH, DN_HEADS),
        in_specs=[
            col(DN_QKV_COL0), col(DN_QKV_COL0 + DN_HEADS), col(DN_QKV_COL0 + 2 * DN_HEADS), col(DN_Z_COL0),
            pl.BlockSpec((SEQ, LANES), lambda b, h: (b, 0)),
            cw(0), cw(DN_HEADS), cw(2 * DN_HEADS),
        ],
        out_specs=[bh_rows(DN_TILE), bh_rows(2 * DN_TILE), bh_rows(DN_AKT_ROWS), bh_rows(DN_PAD), bh_rows(DN_TILE)],
        out_shape=[
            rows_shape(DN_TILE, F32), rows_shape(2 * DN_TILE, BF16), rows_shape(DN_AKT_ROWS, BF16),
            rows_shape(DN_PAD, F32), rows_shape(DN_TILE, BF16),
        ],
        scratch_shapes=[pltpu.VMEM((SEQ + DN_PAD, LANES), F32)] * 3,
        compiler_params=pltpu.CompilerParams(
            dimension_semantics=("parallel", "parallel"), vmem_limit_bytes=VMEM_LIMIT),
        name="dn_prep",
    )(proj, proj, proj, proj, gates, conv_w, conv_w, conv_w)


def _dn_scan_kernel(u_ref, wq_ref, akt_ref, dec_ref, sz_ref, gain_ref, o_ref, s_scr):
    @pl.when(pl.program_id(0) == 0)
    def _():
        s_scr[...] = jnp.zeros_like(s_scr)

    gain = gain_ref[...]
    c = DN_CHUNK
    chains = range(DN_BH)
    second_chunk_lane = lax.broadcasted_iota(jnp.int32, (DN_AKT_ROWS, LANES), 1) >= c
    state = [s_scr[bh] for bh in chains]
    v_first = None
    for tt, j in ((tt, j) for tt in range(DN_SCAN_TILES) for j in range(2)):
        rows = slice(j * c, (j + 1) * c)
        out_rows = slice(tt * DN_TILE + j * c, tt * DN_TILE + (j + 1) * c)
        s16 = [s.astype(BF16) for s in state]
        ws_qs = [jnp.dot(wq_ref[tt, bh, j * DN_TILE:(j + 1) * DN_TILE, :], s16[bh],
                         preferred_element_type=F32) for bh in chains]
        v_new = [u_ref[tt, bh, rows, :] - ws_qs[bh][:c] for bh in chains]
        if j == 0:
            v_pair = [jnp.concatenate([v, jnp.zeros_like(v)], axis=0).astype(BF16) for v in v_new]
            v_first = v_new
            akt = [akt_ref[tt, bh] for bh in chains]
        else:
            v_pair = [jnp.concatenate([v0, v], axis=0).astype(BF16) for v0, v in zip(v_first, v_new)]
            akt = [jnp.where(second_chunk_lane, akt_ref[tt, bh], jnp.zeros((DN_AKT_ROWS, LANES), BF16))
                   for bh in chains]
        ov_kv = [jnp.dot(akt[bh], v_pair[bh], preferred_element_type=F32) for bh in chains]
        state = [state[bh] * dec_ref[tt, bh, j:j + 1, :] + ov_kv[bh][c:] for bh in chains]
        for bh in chains:
            b, h = divmod(bh, DN_HEADS)
            o = ws_qs[bh][c:] + ov_kv[bh][:c]
            o_ref[b, out_rows, h * DN_HEAD_DIM:(h + 1) * DN_HEAD_DIM] = (
                _rms(o, gain) * sz_ref[tt, bh, rows, :].astype(F32))
    for bh in chains:
        s_scr[bh] = state[bh]


def _dn_scan(u, wq, akt, dec, sz, gain):
    rows = lambda n: pl.BlockSpec((DN_SCAN_TILES, DN_BH, n, LANES), lambda t: (t, 0, 0, 0))
    return pl.pallas_call(
        _dn_scan_kernel,
        grid=(DN_NTILE // DN_SCAN_TILES,),
        in_specs=[rows(DN_TILE), rows(2 * DN_TILE), rows(DN_AKT_ROWS), rows(DN_PAD), rows(DN_TILE),
                  pl.BlockSpec((1, LANES), lambda t: (0, 0))],
        out_specs=pl.BlockSpec((BATCH, DN_SCAN_TILES * DN_TILE, DN_WIDTH), lambda t: (0, t, 0)),
        out_shape=jax.ShapeDtypeStruct((BATCH, SEQ, DN_WIDTH), F32),
        scratch_shapes=[pltpu.VMEM((DN_BH, DN_HEAD_DIM, DN_HEAD_DIM), F32)],
        compiler_params=pltpu.CompilerParams(
            dimension_semantics=("arbitrary",), vmem_limit_bytes=VMEM_LIMIT),
        name="dn_scan",
    )(u, wq, akt, dec, sz, gain)


def _dn_mixer(proj, p, l):
    gates = _dn_gates(proj, p['dn_a_log'][l], p['dn_dt_bias'][l])
    outs = _dn_prep(proj, gates, p['dn_conv_w'][l])
    y = _dn_scan(*outs, p['dn_norm_gain'][l].reshape(1, DN_HEAD_DIM))
    return y.reshape(N_TOK, DN_WIDTH)


def kernel(x, norm_gains, ffn_w_gate, ffn_w_up, ffn_w_down, w_in, w_out, ssm_lambda_re, ssm_lambda_im,
           ssm_b_re, ssm_b_im, ssm_c_re, ssm_c_im, ssm_d, ssm_log_dt, ssm_glu_w, ssm_glu_b, ssm_out_gain,
           dn_conv_w, dn_a_log, dn_dt_bias, dn_norm_gain, attn_out_gain, rel_bias):
    p = dict(ssm_lambda_re=ssm_lambda_re, ssm_lambda_im=ssm_lambda_im, ssm_b_re=ssm_b_re, ssm_b_im=ssm_b_im,
             ssm_c_re=ssm_c_re, ssm_c_im=ssm_c_im, ssm_d=ssm_d, ssm_log_dt=ssm_log_dt, ssm_glu_w=ssm_glu_w,
             ssm_glu_b=ssm_glu_b, ssm_out_gain=ssm_out_gain, dn_conv_w=dn_conv_w, dn_a_log=dn_a_log,
             dn_dt_bias=dn_dt_bias, dn_norm_gain=dn_norm_gain)
    wg, wu, wd = ffn_w_gate, ffn_w_up, ffn_w_down
    w_in_p = w_in.astype(BF16)
    w_out_b = w_out
    bucket_table = _attn_bucket_table()
    gains = norm_gains.reshape(DEPTH, 6, 1, D_MODEL)
    x = x.reshape(N_TOK, D_MODEL)
    for l in range(DEPTH):
        x = _ffn(x, gains, wg, wu, wd, l, 0)
        proj = _inproj(x, gains, w_in_p, l)
        y_ssm = _s5_mixer(proj, p, l)
        y_dn = _dn_mixer(proj, p, l)
        o_at = _attn(proj, rel_bias, bucket_table)
        x = _outproj(x, y_ssm, y_dn, o_at, attn_out_gain[l].reshape(1, ATTN_WIDTH), gains, w_out_b, l)
        x = _ffn(x, gains, wg, wu, wd, l, 1)
    return x.reshape(BATCH, SEQ, D_MODEL)
```

```python
import functools
import math

import jax
import jax.numpy as jnp
import numpy as np
from jax import lax
from jax.experimental import pallas as pl
from jax.experimental.pallas import tpu as pltpu

D_MODEL = 2048
BATCH = 2
SEQ = 4096
DEPTH = 4
SSM_GROUPS = 32
SSM_CH = 16
SSM_STATE = 64
SSM_WIDTH = SSM_GROUPS * SSM_CH
DN_HEADS = 6
DN_HEAD_DIM = 128
DN_WIDTH = DN_HEADS * DN_HEAD_DIM
DN_CONV = 4
DN_CHUNK = 64
ATTN_HEADS = 6
ATTN_HEAD_DIM = 128
ATTN_WIDTH = ATTN_HEADS * ATTN_HEAD_DIM
DILATED_PAIRS = ((128, 1), (512, 4), (2048, 16))
ATTN_BLOCK = 128
N_BUCKETS = 32
REL_MAX_DIST = 2048
D_MIX = SSM_WIDTH + DN_WIDTH + ATTN_WIDTH
IN_SPLITS = (SSM_WIDTH, ATTN_WIDTH, ATTN_WIDTH, ATTN_WIDTH, 3 * DN_WIDTH, DN_WIDTH, DN_HEADS, DN_HEADS)
N_IN_COLS = sum(IN_SPLITS)
D_FF = 5632
NORM_EPS = 1e-6
NEG_INF = -1e30

LANES = 128
N_IN_PAD = 6144
V7X_VMEM_BYTES = 64 * 1024 * 1024
VMEM_LIMIT = V7X_VMEM_BYTES - 4 * 1024 * 1024
N_TOK = BATCH * SEQ

BF16 = jnp.bfloat16
F32 = jnp.float32


def _rms(x, gain):
    return x * lax.rsqrt(jnp.mean(x * x, axis=-1, keepdims=True) + NORM_EPS) * gain


def _silu(x):
    half = 0.5 * x
    return half + half * jnp.tanh(half)


FFN_TM = 1024
FFN_TF = 256


def _ffn_kernel(x_ref, gpre_ref, gpost_ref, wg_ref, wu_ref, wd_ref, o_ref, h_scr):
    f = pl.program_id(1)
    last = pl.num_programs(1) - 1

    def down_proj(h):
        gate = jnp.dot(h, wg_ref[...].astype(BF16), preferred_element_type=F32)
        up = jnp.dot(h, wu_ref[...].astype(BF16), preferred_element_type=F32)
        act = (gate * jax.nn.sigmoid(gate) * up).astype(BF16)
        return jnp.dot(act, wd_ref[...].astype(BF16), preferred_element_type=F32)

    @pl.when(f == 0)
    def _():
        h = _rms(x_ref[...], gpre_ref[...]).astype(BF16)
        h_scr[...] = h
        o_ref[...] = down_proj(h)

    @pl.when(jnp.logical_and(f > 0, f < last))
    def _():
        o_ref[...] += down_proj(h_scr[...])

    @pl.when(f == last)
    def _():
        y = o_ref[...] + down_proj(h_scr[...])
        o_ref[...] = x_ref[...] + _rms(y, 0.5 * gpost_ref[...])


def _ffn(x, gains, wg, wu, wd, layer, half):
    return pl.pallas_call(
        _ffn_kernel,
        grid=(N_TOK // FFN_TM, D_FF // FFN_TF),
        in_specs=[
            pl.BlockSpec((FFN_TM, D_MODEL), lambda i, f: (i, 0)),
            pl.BlockSpec((None, None, 1, D_MODEL), lambda i, f: (layer, 4 * half, 0, 0)),
            pl.BlockSpec((None, None, 1, D_MODEL), lambda i, f: (layer, 4 * half + 1, 0, 0)),
            pl.BlockSpec((None, None, D_MODEL, FFN_TF), lambda i, f: (layer, half, 0, f)),
            pl.BlockSpec((None, None, D_MODEL, FFN_TF), lambda i, f: (layer, half, 0, f)),
            pl.BlockSpec((None, None, FFN_TF, D_MODEL), lambda i, f: (layer, half, f, 0)),
        ],
        out_specs=pl.BlockSpec((FFN_TM, D_MODEL), lambda i, f: (i, 0)),
        out_shape=jax.ShapeDtypeStruct((N_TOK, D_MODEL), F32),
        scratch_shapes=[pltpu.VMEM((FFN_TM, D_MODEL), BF16)],
        compiler_params=pltpu.CompilerParams(
            dimension_semantics=("parallel", "arbitrary"), vmem_limit_bytes=VMEM_LIMIT),
        name="ffn",
    )(x, gains, gains, wg, wu, wd)


INP_TM = 1024
INP_TN = 1536


def _inproj_kernel(x_ref, g_ref, w_ref, o_ref, h_scr):
    @pl.when(pl.program_id(1) == 0)
    def _():
        h = _rms(x_ref[...], g_ref[...]).astype(BF16)
        h_scr[...] = h
        o_ref[...] = jnp.dot(h, w_ref[...], preferred_element_type=F32)

    @pl.when(pl.program_id(1) > 0)
    def _():
        o_ref[...] = jnp.dot(h_scr[...], w_ref[...], preferred_element_type=F32)


def _inproj(x, gains, w, layer):
    return pl.pallas_call(
        _inproj_kernel,
        grid=(N_TOK // INP_TM, N_IN_PAD // INP_TN),
        in_specs=[
            pl.BlockSpec((INP_TM, D_MODEL), lambda i, n: (i, 0)),
            pl.BlockSpec((None, None, 1, D_MODEL), lambda i, n: (layer, 2, 0, 0)),
            pl.BlockSpec((None, D_MODEL, INP_TN), lambda i, n: (layer, 0, n)),
        ],
        out_specs=pl.BlockSpec((INP_TM, INP_TN), lambda i, n: (i, n)),
        out_shape=jax.ShapeDtypeStruct((N_TOK, N_IN_PAD), F32),
        scratch_shapes=[pltpu.VMEM((INP_TM, D_MODEL), BF16)],
        compiler_params=pltpu.CompilerParams(
            dimension_semantics=("parallel", "arbitrary"), vmem_limit_bytes=VMEM_LIMIT),
        name="inproj",
    )(x, gains, w)


OUT_TM = 512
OUT_SUB = 256


def _outproj_kernel(x_ref, ys_ref, yd_ref, oa_ref, ga_ref, gpost_ref, w_ref, o_ref, w16):
    @pl.when(pl.program_id(0) == 0)
    def _():
        w16[...] = w_ref[...].astype(BF16)

    halves = [slice(i * OUT_SUB, (i + 1) * OUT_SUB) for i in range(OUT_TM // OUT_SUB)]
    ya = [_rms(oa_ref[r, :], ga_ref[...]).astype(BF16) for r in halves]
    mix = [jnp.dot(ys_ref[r, :].astype(BF16), w16[0:SSM_WIDTH, :], preferred_element_type=F32) for r in halves]
    mix = [m + jnp.dot(yd_ref[r, :].astype(BF16), w16[SSM_WIDTH:SSM_WIDTH + DN_WIDTH, :],
                       preferred_element_type=F32) for m, r in zip(mix, halves)]
    mix = [m + jnp.dot(a, w16[SSM_WIDTH + DN_WIDTH:D_MIX, :], preferred_element_type=F32)
           for m, a in zip(mix, ya)]
    for r, m in zip(halves, mix):
        o_ref[r, :] = x_ref[r, :] + _rms(m, gpost_ref[...])


def _outproj(x, y_ssm, y_dn, o_at, g_attn, gains, w, layer):
    row = lambda i: (i, 0)
    fixed = lambda i: (0, 0)
    return pl.pallas_call(
        _outproj_kernel,
        grid=(N_TOK // OUT_TM,),
        in_specs=[
            pl.BlockSpec((OUT_TM, D_MODEL), row),
            pl.BlockSpec((OUT_TM, SSM_WIDTH), row),
            pl.BlockSpec((OUT_TM, DN_WIDTH), row),
            pl.BlockSpec((OUT_TM, ATTN_WIDTH), row),
            pl.BlockSpec((1, ATTN_WIDTH), fixed),
            pl.BlockSpec((None, None, 1, D_MODEL), lambda i: (layer, 3, 0, 0)),
            pl.BlockSpec((None, D_MIX, D_MODEL), lambda i: (layer, 0, 0), pipeline_mode=pl.Buffered(1)),
        ],
        out_specs=pl.BlockSpec((OUT_TM, D_MODEL), row),
        out_shape=jax.ShapeDtypeStruct((N_TOK, D_MODEL), F32),
        scratch_shapes=[pltpu.VMEM((D_MIX, D_MODEL), BF16)],
        compiler_params=pltpu.CompilerParams(
            dimension_semantics=("arbitrary",), vmem_limit_bytes=VMEM_LIMIT),
        name="outproj",
    )(x, y_ssm, y_dn, o_at, g_attn, gains, w)


S5_SEG = 8
S5_SEGLEN = SEQ // S5_SEG
S5_KB = 128
S5_ROWS = S5_KB * S5_SEG
S5_PITCH = S5_KB + 8
S5_PROWS = S5_PITCH * S5_SEG
S5_NS = SSM_GROUPS * SSM_STATE
S5_SLABS = S5_NS // LANES
S5_Q = 4
S5_QS = S5_NS // S5_Q
S5_QL = S5_QS // LANES
S5_QC = SSM_WIDTH // S5_Q
S5_LOG2_SEGLEN = 9
assert 1 << S5_LOG2_SEGLEN == S5_SEGLEN


def _s5_kernel(u_ref, wb_ref, are_ref, aim_ref, pre_ref, pim_ref, cre_ref, cim_ref, d_ref, gw_ref, gb_ref,
               go_ref, o_ref, bu_scr, st_scr, carry_scr, y_scr):
    p = pl.program_id(1)
    j = pl.program_id(2)
    u = u_ref[0].reshape(S5_ROWS, SSM_WIDTH)
    ub = u.astype(BF16)

    @pl.when(jnp.logical_and(p == 0, j == 0))
    def _():
        bu_scr[...] = jnp.zeros_like(bu_scr)
        st_scr[...] = jnp.zeros_like(st_scr)

    @pl.when(jnp.logical_and(p == 1, j == 0))
    def _():
        st_scr[...] = carry_scr[...]

    for q in range(S5_Q):
        r = jnp.dot(ub[:, q * S5_QC:(q + 1) * S5_QC], wb_ref[q], preferred_element_type=F32)

        @pl.when(p == 0)
        def _(q=q, r=r):
            re_cols = slice(q * S5_QS, (q + 1) * S5_QS)
            im_cols = slice(S5_NS + q * S5_QS, S5_NS + (q + 1) * S5_QS)
            pr, pi = pre_ref[:, re_cols], pim_ref[:, re_cols]
            sums_re, sums_im = [], []
            for i in range(S5_SEG):
                rr = r[i * S5_KB:(i + 1) * S5_KB, :S5_QS]
                ri = r[i * S5_KB:(i + 1) * S5_KB, S5_QS:]
                sums_re.append(jnp.sum(pr * rr - pi * ri, axis=0, keepdims=True))
                sums_im.append(jnp.sum(pr * ri + pi * rr, axis=0, keepdims=True))
            ar, ai = are_ref[:, re_cols], aim_ref[:, re_cols]
            kr = ar * pr[0:1, :] - ai * pi[0:1, :]
            ki = ar * pi[0:1, :] + ai * pr[0:1, :]
            sr, si = st_scr[:, re_cols], st_scr[:, im_cols]
            st_scr[:, re_cols] = kr * sr - ki * si + jnp.concatenate(sums_re, axis=0)
            st_scr[:, im_cols] = kr * si + ki * sr + jnp.concatenate(sums_im, axis=0)

        @pl.when(p == 1)
        def _(q=q, r=r):
            for half in range(2):
                for c in range(S5_QL):
                    slab = half * S5_SLABS + q * S5_QL + c
                    lanes = slice((half * S5_QL + c) * LANES, (half * S5_QL + c + 1) * LANES)
                    for i in range(S5_SEG):
                        bu_scr[slab, i * S5_PITCH:i * S5_PITCH + S5_KB, :] = r[i * S5_KB:(i + 1) * S5_KB, lanes]

    def scan():
        for q in range(S5_Q):
            slabs = [q * S5_QL + c for c in range(S5_QL)]
            lanes = [slice(s * LANES, (s + 1) * LANES) for s in slabs]
            ilanes = [slice(S5_NS + s * LANES, S5_NS + (s + 1) * LANES) for s in slabs]
            ar = [jnp.broadcast_to(are_ref[:, ln], (S5_SEG, LANES)) for ln in lanes]
            ai = [jnp.broadcast_to(aim_ref[:, ln], (S5_SEG, LANES)) for ln in lanes]

            def body(k, carry):
                sr, si = carry
                rows = pl.ds(k, S5_SEG, stride=S5_PITCH)
                nr = tuple(ar[c] * sr[c] - ai[c] * si[c] + bu_scr[slabs[c], rows, :] for c in range(S5_QL))
                ni = tuple(ar[c] * si[c] + ai[c] * sr[c] + bu_scr[slabs[c] + S5_SLABS, rows, :]
                           for c in range(S5_QL))
                for c in range(S5_QL):
                    bu_scr[slabs[c], rows, :] = nr[c]
                    bu_scr[slabs[c] + S5_SLABS, rows, :] = ni[c]
                return nr, ni

            init = (tuple(st_scr[:, ln] for ln in lanes), tuple(st_scr[:, ln] for ln in ilanes))
            sr, si = lax.fori_loop(0, S5_KB, body, init, unroll=8)
            for c in range(S5_QL):
                st_scr[:, lanes[c]] = sr[c]
                st_scr[:, ilanes[c]] = si[c]

    @pl.when(jnp.logical_and(p == 0, j == pl.num_programs(2) - 1))
    def _():
        lr, li = are_ref[...], aim_ref[...]
        for _ in range(S5_LOG2_SEGLEN):
            lr, li = lr * lr - li * li, 2.0 * lr * li
        cr = jnp.zeros((1, S5_NS), F32)
        ci = jnp.zeros((1, S5_NS), F32)
        carry_scr[0:1, :] = jnp.zeros((1, 2 * S5_NS), F32)
        for i in range(1, S5_SEG):
            er = st_scr[i - 1:i, 0:S5_NS]
            ei = st_scr[i - 1:i, S5_NS:2 * S5_NS]
            cr, ci = er + lr * cr - li * ci, ei + lr * ci + li * cr
            carry_scr[i:i + 1, 0:S5_NS] = cr
            carry_scr[i:i + 1, S5_NS:2 * S5_NS] = ci

    @pl.when(p == 1)
    def _():
        scan()
        for q in range(S5_Q):
            sre = jnp.concatenate([bu_scr[q * S5_QL + c] for c in range(S5_QL)], axis=1).astype(BF16)
            sim = jnp.concatenate([bu_scr[S5_SLABS + q * S5_QL + c] for c in range(S5_QL)], axis=1).astype(BF16)
            y_scr[:, q * S5_QC:(q + 1) * S5_QC] = (
                jnp.dot(sre, cre_ref[q], preferred_element_type=F32)
                + jnp.dot(sim, cim_ref[q], preferred_element_type=F32))
        cs = jnp.concatenate([y_scr[i * S5_PITCH:i * S5_PITCH + S5_KB, :] for i in range(S5_SEG)], axis=0)
        y = jax.nn.gelu(cs + d_ref[...] * u)
        z = jnp.dot(y.astype(BF16), gw_ref[...], preferred_element_type=F32) + gb_ref[...]
        o_ref[0] = _rms(y * jax.nn.sigmoid(z), go_ref[...]).reshape(S5_SEG, S5_KB, SSM_WIDTH)


def _s5_params(lam_re, lam_im, b_re, b_im, c_re, c_im, log_dt):
    lam = lax.complex(lam_re, lam_im)
    lam_bar = jnp.exp(lam * jnp.exp(log_dt)[:, None])
    b_bar = ((lam_bar - 1.0) / lam)[..., None] * lax.complex(b_re, b_im)
    gq = SSM_GROUPS // S5_Q
    eye = jnp.eye(gq, dtype=F32)

    def in_map(t):
        t = t.reshape(S5_Q, gq, SSM_STATE, SSM_CH)
        return jnp.einsum('qgpc,gh->qgchp', t, eye).reshape(S5_Q, S5_QC, S5_QS)

    def out_map(t):
        t = t.reshape(S5_Q, gq, SSM_CH, SSM_STATE)
        return jnp.einsum('qgcp,gh->qgphc', t, eye).reshape(S5_Q, S5_QS, S5_QC)

    wb = jnp.concatenate([in_map(b_bar.real), in_map(b_bar.imag)], axis=-1).astype(BF16)
    lam_flat = lam_bar.reshape(1, S5_NS)
    powers = jnp.concatenate([jnp.ones_like(lam_flat),
                              jnp.cumprod(jnp.broadcast_to(lam_flat, (S5_KB - 1, S5_NS)), axis=0)], axis=0)[::-1]
    return (wb, lam_flat.real, lam_flat.imag, powers.real, powers.imag,
            out_map(c_re).astype(BF16), out_map(-c_im).astype(BF16))


def _s5(proj4, wb, a_re, a_im, p_re, p_im, cre, cim, d_skip, glu_w, glu_b, out_gain):
    nblk = S5_SEGLEN // S5_KB
    fix2 = lambda b, p, j: (0, 0)
    fix3 = lambda b, p, j: (0, 0, 0)
    return pl.pallas_call(
        _s5_kernel,
        grid=(BATCH, 2, nblk),
        in_specs=[
            pl.BlockSpec((1, S5_SEG, S5_KB, SSM_WIDTH), lambda b, p, j: (b, 0, j, 0)),
            pl.BlockSpec((S5_Q, S5_QC, 2 * S5_QS), fix3),
            pl.BlockSpec((1, S5_NS), fix2),
            pl.BlockSpec((1, S5_NS), fix2),
            pl.BlockSpec((S5_KB, S5_NS), fix2),
            pl.BlockSpec((S5_KB, S5_NS), fix2),
            pl.BlockSpec((S5_Q, S5_QS, S5_QC), fix3),
            pl.BlockSpec((S5_Q, S5_QS, S5_QC), fix3),
            pl.BlockSpec((1, SSM_WIDTH), fix2),
            pl.BlockSpec((SSM_WIDTH, SSM_WIDTH), fix2),
            pl.BlockSpec((1, SSM_WIDTH), fix2),
            pl.BlockSpec((1, SSM_WIDTH), fix2),
        ],
        out_specs=pl.BlockSpec((1, S5_SEG, S5_KB, SSM_WIDTH), lambda b, p, j: (b, 0, j * p, 0)),
        out_shape=jax.ShapeDtypeStruct((BATCH, S5_SEG, S5_SEGLEN, SSM_WIDTH), F32),
        scratch_shapes=[
            pltpu.VMEM((2 * S5_SLABS, S5_PROWS, LANES), F32),
            pltpu.VMEM((S5_SEG, 2 * S5_NS), F32),
            pltpu.VMEM((S5_SEG, 2 * S5_NS), F32),
            pltpu.VMEM((S5_PROWS, SSM_WIDTH), F32),
        ],
        compiler_params=pltpu.CompilerParams(
            dimension_semantics=("parallel", "arbitrary", "arbitrary"), vmem_limit_bytes=VMEM_LIMIT),
        name="s5",
    )(proj4, wb, a_re, a_im, p_re, p_im, cre, cim, d_skip, glu_w, glu_b, out_gain)


def _s5_mixer(proj, p, l):
    params = _s5_params(p['ssm_lambda_re'][l], p['ssm_lambda_im'][l], p['ssm_b_re'][l], p['ssm_b_im'][l],
                        p['ssm_c_re'][l], p['ssm_c_im'][l], p['ssm_log_dt'][l])
    y = _s5(proj.reshape(BATCH, S5_SEG, S5_SEGLEN, N_IN_PAD), *params,
            p['ssm_d'][l].reshape(1, SSM_WIDTH), p['ssm_glu_w'][l].astype(BF16),
            p['ssm_glu_b'][l].reshape(1, SSM_WIDTH), p['ssm_out_gain'][l].reshape(1, SSM_WIDTH))
    return y.reshape(N_TOK, SSM_WIDTH)


AT_BLK = ATTN_BLOCK
AT_NBRANCH = len(DILATED_PAIRS)
AT_BLOCKS = SEQ // AT_BLK
AT_GROUP = 16
AT_ORDER = tuple(range(1, AT_NBRANCH)) + (0,)
AT_PAD_GROUP = 16
AT_PAD_PITCH = 24
AT_PAD_ROWS = SEQ // AT_PAD_GROUP * AT_PAD_PITCH
assert [d for _, d in DILATED_PAIRS].count(AT_PAD_GROUP) == 1 and DILATED_PAIRS[-1][1] == AT_PAD_GROUP
assert DILATED_PAIRS[0][1] == 1 and all(d > 1 for _, d in DILATED_PAIRS[1:])
assert all(w // d == AT_BLK for w, d in DILATED_PAIRS)
assert all((AT_BLOCKS // d) % AT_GROUP == 0 or AT_GROUP % (AT_BLOCKS // d) == 0 for _, d in DILATED_PAIRS)
AT_Q_COL0 = SSM_WIDTH // LANES
AT_K_COL0 = AT_Q_COL0 + ATTN_HEADS
AT_V_COL0 = AT_K_COL0 + ATTN_HEADS


def _at_padded_row(s):
    return (s // AT_PAD_GROUP) * AT_PAD_PITCH + s % AT_PAD_GROUP


def _attn_kernel(rb_ref, bkt_ref, q_ref, k_ref, v_ref, o_ref, bias_scr, o_scr, lse_scr,
                 qkv_pad, o_pad, lse_pad):
    h = pl.program_id(0)

    def to_padded(i, carry):
        src_rows = pl.ds(pl.multiple_of(i * AT_PAD_GROUP, AT_PAD_GROUP), AT_PAD_GROUP)
        dst_rows = pl.ds(pl.multiple_of(i * AT_PAD_PITCH, 8), AT_PAD_GROUP)
        for j, ref in enumerate((q_ref, k_ref, v_ref)):
            qkv_pad[j, dst_rows, :] = ref[src_rows, :]
        return carry

    lax.fori_loop(0, SEQ // AT_PAD_GROUP, to_padded, 0, unroll=8)

    @pl.when(pl.program_id(1) == 0)
    def _():
        for g in range(AT_NBRANCH):
            bkt = bkt_ref[g]
            bias = jnp.zeros((AT_BLK, 2 * AT_BLK), F32)
            for b in range(N_BUCKETS):
                bias = jnp.where(bkt == b, rb_ref[b, h], bias)
            bias_scr[g] = jnp.where(bkt < 0, NEG_INF, bias)

    lane = lax.broadcasted_iota(jnp.int32, (AT_BLK, 2 * AT_BLK), 1)
    scale = ATTN_HEAD_DIM ** -0.5
    for g in AT_ORDER:
        dil = DILATED_PAIRS[g][1]
        nb = AT_BLOCKS // dil

        def body(it, carry, g=g, dil=dil, nb=nb):
            padded = dil == AT_PAD_GROUP
            if padded:
                qs, ks, vs = qkv_pad.at[0], qkv_pad.at[1], qkv_pad.at[2]
            else:
                qs, ks, vs = q_ref, k_ref, v_ref

            def block_rows(start):
                if dil == 1:
                    return pl.ds(pl.multiple_of(start, AT_BLK), AT_BLK)
                if padded:
                    return pl.ds(_at_padded_row(start), AT_BLK, stride=AT_PAD_PITCH)
                return pl.ds(start, AT_BLK, stride=dil)

            group = range(AT_GROUP)
            ts = [it * AT_GROUP + i for i in group]
            if nb <= AT_GROUP:
                ns = [i % nb for i in group]
            else:
                ns = [None if i == 0 else i for i in group]
            n0 = ts[0] % nb
            rows = [block_rows(t // nb + (t % nb) * (AT_BLK * dil)) for t in ts]
            q = [(qs[r, :] * scale).astype(BF16) for r in rows]
            k_cur = [ks[r, :].astype(BF16) for r in rows]
            v_cur = [vs[r, :].astype(BF16) for r in rows]
            if ns[0] is None:
                prow = block_rows(ts[0] // nb + jnp.maximum(n0 - 1, 0) * (AT_BLK * dil))
                k_lead, v_lead = ks[prow, :].astype(BF16), vs[prow, :].astype(BF16)
            s, vc = [], []
            for i in group:
                if ns[i] == 0:
                    s.append(_dot_nt(q[i], k_cur[i]) + bias_scr[g, :, AT_BLK:])
                    vc.append(v_cur[i])
                    continue
                k_prev, v_prev = (k_lead, v_lead) if ns[i] is None else (k_cur[i - 1], v_cur[i - 1])
                si = _dot_nt(q[i], jnp.concatenate([k_prev, k_cur[i]], axis=0)) + bias_scr[g]
                if ns[i] is None:
                    si = jnp.where(jnp.logical_or(n0 > 0, lane >= AT_BLK), si, NEG_INF)
                s.append(si)
                vc.append(jnp.concatenate([v_prev, v_cur[i]], axis=0))
            m = [jnp.max(si, axis=1, keepdims=True) for si in s]
            p = [jnp.exp(si - mi) for si, mi in zip(s, m)]
            l = [jnp.sum(pi, axis=1, keepdims=True) for pi in p]
            o = [jnp.dot(pi.astype(BF16), vi, preferred_element_type=F32) / li for pi, vi, li in zip(p, vc, l)]
            lse = [jnp.broadcast_to(mi + jnp.log(li), (AT_BLK, LANES)) for mi, li in zip(m, l)]
            if dil > 1:
                for r, oi, li in zip(rows, o, lse):
                    if padded:
                        o_pad[r, :], lse_pad[r, :] = oi, li
                    else:
                        o_scr[g - 1, r, :], lse_scr[g - 1, r, :] = oi, li
                return carry

            def unpad(ref, start):
                base = (start // AT_PAD_GROUP) * AT_PAD_PITCH
                return jnp.concatenate(
                    [ref[pl.ds(pl.multiple_of(base + j * AT_PAD_PITCH, 8), AT_PAD_GROUP), :]
                     for j in range(AT_BLK // AT_PAD_GROUP)], axis=0)

            for t, r, oi, li in zip(ts, rows, o, lse):
                parked = [(unpad(o_pad, t * AT_BLK), unpad(lse_pad, t * AT_BLK))
                          if DILATED_PAIRS[j][1] == AT_PAD_GROUP else (o_scr[j - 1, r, :], lse_scr[j - 1, r, :])
                          for j in range(1, AT_NBRANCH)]
                top = functools.reduce(jnp.maximum, [lj for _, lj in parked], li)
                wi = jnp.exp(li - top)
                num, den = wi * oi, wi
                for oj, lj in parked:
                    wj = jnp.exp(lj - top)
                    num, den = num + wj * oj, den + wj
                o_ref[r, :] = num / den
            return carry

        lax.fori_loop(0, AT_BLOCKS // AT_GROUP, body, 0)


def _attn_bucket_table():
    qi = jnp.arange(AT_BLK)[:, None]
    kj = jnp.arange(2 * AT_BLK)[None, :]
    rel = AT_BLK + qi - kj
    max_exact = N_BUCKETS // 2
    tables = []
    for window, dil in DILATED_PAIRS:
        dist = jnp.maximum(rel, 0) * dil
        d = jnp.maximum(dist, 1).astype(F32)
        large = max_exact + jnp.log(d / max_exact) / math.log(REL_MAX_DIST / max_exact) * (N_BUCKETS - max_exact)
        large = jnp.minimum(large.astype(jnp.int32), N_BUCKETS - 1)
        bucket = jnp.where(dist < max_exact, dist, large)
        tables.append(jnp.where((rel >= 0) & (rel <= window // dil), bucket, -1))
    return jnp.stack(tables).astype(jnp.int32)


def _attn(proj, rel_bias, bucket_table):
    col = lambda c0: pl.BlockSpec((SEQ, LANES), lambda h, b: (b, c0 + h))
    return pl.pallas_call(
        _attn_kernel,
        grid=(ATTN_HEADS, BATCH),
        in_specs=[
            pl.BlockSpec(memory_space=pltpu.SMEM),
            pl.BlockSpec((AT_NBRANCH, AT_BLK, 2 * AT_BLK), lambda h, b: (0, 0, 0)),
            col(AT_Q_COL0), col(AT_K_COL0), col(AT_V_COL0),
        ],
        out_specs=pl.BlockSpec((SEQ, LANES), lambda h, b: (b, h)),
        out_shape=jax.ShapeDtypeStruct((N_TOK, ATTN_WIDTH), F32),
        scratch_shapes=[
            pltpu.VMEM((AT_NBRANCH, AT_BLK, 2 * AT_BLK), F32),
            pltpu.VMEM((AT_NBRANCH - 2, SEQ, LANES), F32),
            pltpu.VMEM((AT_NBRANCH - 2, SEQ, LANES), F32),
            pltpu.VMEM((3, AT_PAD_ROWS, LANES), F32),
            pltpu.VMEM((AT_PAD_ROWS, LANES), F32),
            pltpu.VMEM((AT_PAD_ROWS, LANES), F32),
        ],
        compiler_params=pltpu.CompilerParams(
            dimension_semantics=("parallel", "arbitrary"), vmem_limit_bytes=VMEM_LIMIT),
        name="dilated_attn",
    )(rel_bias, bucket_table, proj, proj, proj)


DN_TILE = 2 * DN_CHUNK
DN_NTILE = SEQ // DN_TILE
DN_BH = BATCH * DN_HEADS
DN_PAD = 8
DN_GROUP = 16
DN_SCAN_TILES = 4
DN_AKT_ROWS = DN_CHUNK + DN_HEAD_DIM
DN_INV_LEVELS = 6
assert 2 ** DN_INV_LEVELS == DN_CHUNK
DN_QKV_COL0 = (SSM_WIDTH + 3 * ATTN_WIDTH) // LANES
DN_Z_COL0 = DN_QKV_COL0 + 3 * DN_HEADS
DN_AB_COL = DN_Z_COL0 + DN_HEADS
HIGHEST = lax.Precision.HIGHEST


def _dot_nt(a, b):
    return lax.dot_general(a, b, (((1,), (1,)), ((), ())), preferred_element_type=F32)


def _split3(x):
    hi = x.astype(BF16)
    r = x - hi.astype(F32)
    mid = r.astype(BF16)
    return hi, mid, (r - mid.astype(F32)).astype(BF16)


def _dot_exact_lhs(x, m16):
    return sum(jnp.dot(piece, m16, preferred_element_type=F32) for piece in _split3(x))


def _dot_exact_rhs(m16, x):
    return sum(jnp.dot(m16, piece, preferred_element_type=F32) for piece in _split3(x))


def _dn_tile_masks():
    row = lax.broadcasted_iota(jnp.int32, (DN_TILE, DN_TILE), 0)
    col = lax.broadcasted_iota(jnp.int32, (DN_TILE, DN_TILE), 1)
    causal = jnp.logical_and((row // DN_CHUNK) == (col // DN_CHUNK), row >= col)
    return row, col, causal


DN_GATE_ROWS = 1024


def _dn_gates_kernel(ab_ref, alog_ref, dtb_ref, o_ref):
    _, col, causal = _dn_tile_masks()
    cumsum_mat = causal.astype(BF16)
    neg_a = -jnp.exp(alog_ref[...])
    for t in range(DN_GATE_ROWS // DN_TILE):
        rows = slice(t * DN_TILE, (t + 1) * DN_TILE)
        ab = jnp.where(col < 2 * DN_HEADS, ab_ref[rows, :], 0.0)
        x = ab + dtb_ref[...]
        g = neg_a * (jnp.maximum(x, 0.0) + jnp.log1p(jnp.exp(-jnp.abs(x))))
        gc = _dot_exact_rhs(cumsum_mat, g)
        o_ref[rows, :] = jnp.where(col < DN_HEADS, gc, jax.nn.sigmoid(ab))


def _dn_gates(proj, a_log, dt_bias):
    lanes = lambda v: jnp.pad(v, (0, LANES - DN_HEADS)).reshape(1, LANES)
    return pl.pallas_call(
        _dn_gates_kernel,
        grid=(N_TOK // DN_GATE_ROWS,),
        in_specs=[
            pl.BlockSpec((DN_GATE_ROWS, LANES), lambda i: (i, DN_AB_COL)),
            pl.BlockSpec((1, LANES), lambda i: (0, 0)),
            pl.BlockSpec((1, LANES), lambda i: (0, 0)),
        ],
        out_specs=pl.BlockSpec((DN_GATE_ROWS, LANES), lambda i: (i, 0)),
        out_shape=jax.ShapeDtypeStruct((N_TOK, LANES), F32),
        compiler_params=pltpu.CompilerParams(dimension_semantics=("parallel",), vmem_limit_bytes=VMEM_LIMIT),
        name="dn_gates",
    )(proj, lanes(a_log), lanes(dt_bias))


def _dn_prep_kernel(q_ref, k_ref, v_ref, z_ref, gate_ref, wq_ref, wk_ref, wv_ref,
                    u_o, wq_o, akt_o, dec_o, sz_o, qp, kp, vp):
    h = pl.program_id(1)
    for src, dst in ((q_ref, qp), (k_ref, kp), (v_ref, vp)):
        dst[0:DN_PAD, :] = jnp.zeros((DN_PAD, LANES), F32)
        dst[DN_PAD:DN_PAD + SEQ, :] = src[...]

    row, col, causal = _dn_tile_masks()
    pick = jnp.concatenate([row == h, row == h + DN_HEADS], axis=1).astype(BF16)
    eye = (row == col).astype(F32)
    pair_masks = [
        jnp.logical_and(jnp.logical_and((row // (2 * s)) == (col // (2 * s)), (row // s) % 2 == 1),
                        (col // s) % 2 == 0)
        for s in (2 ** i for i in range(DN_INV_LEVELS))]
    scale = DN_HEAD_DIM ** -0.5

    def conv_silu(pad_ref, w_ref, base):
        acc = None
        for j in range(DN_CONV):
            sh = DN_PAD - (DN_CONV - 1) + j
            term = w_ref[j:j + 1, :] * pad_ref[pl.ds(base + sh, DN_TILE), :]
            acc = term if acc is None else acc + term
        return _silu(acc)

    def l2n(x):
        return x * lax.rsqrt(jnp.sum(x * x, axis=1, keepdims=True) + NORM_EPS)

    def mm16(a, b):
        return jnp.dot(a.astype(BF16), b.astype(BF16), preferred_element_type=F32)

    def body(it, carry):
        tiles = [it * DN_GROUP + i for i in range(DN_GROUP)]
        bases = [pl.multiple_of(t * DN_TILE, DN_TILE) for t in tiles]
        q = [l2n(conv_silu(qp, wq_ref, b)) * scale for b in bases]
        k = [l2n(conv_silu(kp, wk_ref, b)) for b in bases]
        v = [conv_silu(vp, wv_ref, b) for b in bases]
        gate_rep = [_dot_exact_lhs(gate_ref[pl.ds(b, DN_TILE), :], pick) for b in bases]
        gc = [r[:, :DN_TILE] for r in gate_rep]
        beta = [r[:, DN_TILE:] for r in gate_rep]
        decay = [jnp.exp(jnp.where(causal, c - c.T, NEG_INF)) for c in gc]
        kb = [ki * bi for ki, bi in zip(k, beta)]
        k16 = [ki.astype(BF16) for ki in k]
        a_mat = [_dot_nt(kbi.astype(BF16), ki) * di for kbi, ki, di in zip(kb, k16, decay)]
        t_inv = [eye - jnp.where(pair_masks[0], am, 0.0) for am in a_mat]
        for mask in pair_masks[1:]:
            low = [jnp.where(mask, am, 0.0).astype(BF16) for am in a_mat]
            t_inv = [ti - mm16(mm16(ti, lo), ti) for ti, lo in zip(t_inv, low)]
        egc = [jnp.exp(c) for c in gc]
        uw = [mm16(ti, jnp.concatenate([vi * bi, kbi * ei], axis=1))
              for ti, vi, bi, kbi, ei in zip(t_inv, v, beta, kb, egc)]
        at = [(_dot_nt(qi.astype(BF16), ki) * di).astype(BF16) for qi, ki, di in zip(q, k16, decay)]
        qd = [(qi * ei).astype(BF16) for qi, ei in zip(q, egc)]
        c = DN_CHUNK
        for i, t in enumerate(tiles):
            rows = pl.ds(bases[i], DN_TILE)
            u_o[t] = uw[i][:, :DN_HEAD_DIM]
            w = uw[i][:, DN_HEAD_DIM:].astype(BF16)
            wq_o[t] = jnp.concatenate([w[:c], qd[i][:c], w[c:], qd[i][c:]], axis=0)
            gc_first, gc_second = gc[i][c - 1:c, :], gc[i][DN_TILE - 1:DN_TILE, :]
            gc_last = jnp.where(row < c, gc_first, gc_second)
            kt = (k[i] * jnp.exp(gc_last - gc[i])).T
            akt_o[t] = jnp.concatenate([at[i][:c] + at[i][c:], kt.astype(BF16)], axis=0)
            dec_o[t] = jnp.concatenate(
                [jnp.exp(gc_first), jnp.exp(gc_second), jnp.zeros((DN_PAD - 2, LANES), F32)], axis=0)
            z = z_ref[rows, :]
            sz_o[t] = _silu(z).astype(BF16)
        return carry

    lax.fori_loop(0, DN_NTILE // DN_GROUP, body, 0)


def _dn_prep(proj, gates, conv_w):
    col = lambda c0: pl.BlockSpec((SEQ, LANES), lambda b, h: (b, c0 + h))
    cw = lambda c0: pl.BlockSpec((DN_CONV, LANES), lambda b, h: (0, c0 + h))
    bh_rows = lambda n: pl.BlockSpec((DN_NTILE, None, n, LANES), lambda b, h: (0, b * DN_HEADS + h, 0, 0))
    rows_shape = lambda n, dt: jax.ShapeDtypeStruct((DN_NTILE, DN_BH, n, LANES), dt)
    return pl.pallas_call(
        _dn_prep_kernel,
        grid=(BATCH, DN_HEADS),
        in_specs=[
            col(DN_QKV_COL0), col(DN_QKV_COL0 + DN_HEADS), col(DN_QKV_COL0 + 2 * DN_HEADS), col(DN_Z_COL0),
            pl.BlockSpec((SEQ, LANES), lambda b, h: (b, 0)),
            cw(0), cw(DN_HEADS), cw(2 * DN_HEADS),
        ],
        out_specs=[bh_rows(DN_TILE), bh_rows(2 * DN_TILE), bh_rows(DN_AKT_ROWS), bh_rows(DN_PAD), bh_rows(DN_TILE)],
        out_shape=[
            rows_shape(DN_TILE, F32), rows_shape(2 * DN_TILE, BF16), rows_shape(DN_AKT_ROWS, BF16),
            rows_shape(DN_PAD, F32), rows_shape(DN_TILE, BF16),
        ],
        scratch_shapes=[pltpu.VMEM((SEQ + DN_PAD, LANES), F32)] * 3,
        compiler_params=pltpu.CompilerParams(
            dimension_semantics=("parallel", "parallel"), vmem_limit_bytes=VMEM_LIMIT),
        name="dn_prep",
    )(proj, proj, proj, proj, gates, conv_w, conv_w, conv_w)


def _dn_scan_kernel(u_ref, wq_ref, akt_ref, dec_ref, sz_ref, gain_ref, o_ref, s_scr):
    @pl.when(pl.program_id(0) == 0)
    def _():
        s_scr[...] = jnp.zeros_like(s_scr)

    gain = gain_ref[...]
    c = DN_CHUNK
    chains = range(DN_BH)
    second_chunk_lane = lax.broadcasted_iota(jnp.int32, (DN_AKT_ROWS, LANES), 1) >= c
    state = [s_scr[bh] for bh in chains]
    v_first = None
    for tt, j in ((tt, j) for tt in range(DN_SCAN_TILES) for j in range(2)):
        rows = slice(j * c, (j + 1) * c)
        out_rows = slice(tt * DN_TILE + j * c, tt * DN_TILE + (j + 1) * c)
        s16 = [s.astype(BF16) for s in state]
        ws_qs = [jnp.dot(wq_ref[tt, bh, j * DN_TILE:(j + 1) * DN_TILE, :], s16[bh],
                         preferred_element_type=F32) for bh in chains]
        v_new = [u_ref[tt, bh, rows, :] - ws_qs[bh][:c] for bh in chains]
        if j == 0:
            v_pair = [jnp.concatenate([v, jnp.zeros_like(v)], axis=0).astype(BF16) for v in v_new]
            v_first = v_new
            akt = [akt_ref[tt, bh] for bh in chains]
        else:
            v_pair = [jnp.concatenate([v0, v], axis=0).astype(BF16) for v0, v in zip(v_first, v_new)]
            akt = [jnp.where(second_chunk_lane, akt_ref[tt, bh], jnp.zeros((DN_AKT_ROWS, LANES), BF16))
                   for bh in chains]
        ov_kv = [jnp.dot(akt[bh], v_pair[bh], preferred_element_type=F32) for bh in chains]
        state = [state[bh] * dec_ref[tt, bh, j:j + 1, :] + ov_kv[bh][c:] for bh in chains]
        for bh in chains:
            b, h = divmod(bh, DN_HEADS)
            o = ws_qs[bh][c:] + ov_kv[bh][:c]
            o_ref[b, out_rows, h * DN_HEAD_DIM:(h + 1) * DN_HEAD_DIM] = (
                _rms(o, gain) * sz_ref[tt, bh, rows, :].astype(F32))
    for bh in chains:
        s_scr[bh] = state[bh]


def _dn_scan(u, wq, akt, dec, sz, gain):
    rows = lambda n: pl.BlockSpec((DN_SCAN_TILES, DN_BH, n, LANES), lambda t: (t, 0, 0, 0))
    return pl.pallas_call(
        _dn_scan_kernel,
        grid=(DN_NTILE // DN_SCAN_TILES,),
        in_specs=[rows(DN_TILE), rows(2 * DN_TILE), rows(DN_AKT_ROWS), rows(DN_PAD), rows(DN_TILE),
                  pl.BlockSpec((1, LANES), lambda t: (0, 0))],
        out_specs=pl.BlockSpec((BATCH, DN_SCAN_TILES * DN_TILE, DN_WIDTH), lambda t: (0, t, 0)),
        out_shape=jax.ShapeDtypeStruct((BATCH, SEQ, DN_WIDTH), F32),
        scratch_shapes=[pltpu.VMEM((DN_BH, DN_HEAD_DIM, DN_HEAD_DIM), F32)],
        compiler_params=pltpu.CompilerParams(
            dimension_semantics=("arbitrary",), vmem_limit_bytes=VMEM_LIMIT),
        name="dn_scan",
    )(u, wq, akt, dec, sz, gain)


def _dn_mixer(proj, p, l):
    gates = _dn_gates(proj, p['dn_a_log'][l], p['dn_dt_bias'][l])
    outs = _dn_prep(proj, gates, p['dn_conv_w'][l])
    y = _dn_scan(*outs, p['dn_norm_gain'][l].reshape(1, DN_HEAD_DIM))
    return y.reshape(N_TOK, DN_WIDTH)


def kernel(x, norm_gains, ffn_w_gate, ffn_w_up, ffn_w_down, w_in, w_out, ssm_lambda_re, ssm_lambda_im,
           ssm_b_re, ssm_b_im, ssm_c_re, ssm_c_im, ssm_d, ssm_log_dt, ssm_glu_w, ssm_glu_b, ssm_out_gain,
           dn_conv_w, dn_a_log, dn_dt_bias, dn_norm_gain, attn_out_gain, rel_bias):
    p = dict(ssm_lambda_re=ssm_lambda_re, ssm_lambda_im=ssm_lambda_im, ssm_b_re=ssm_b_re, ssm_b_im=ssm_b_im,
             ssm_c_re=ssm_c_re, ssm_c_im=ssm_c_im, ssm_d=ssm_d, ssm_log_dt=ssm_log_dt, ssm_glu_w=ssm_glu_w,
             ssm_glu_b=ssm_glu_b, ssm_out_gain=ssm_out_gain, dn_conv_w=dn_conv_w, dn_a_log=dn_a_log,
             dn_dt_bias=dn_dt_bias, dn_norm_gain=dn_norm_gain)
    wg, wu, wd = ffn_w_gate, ffn_w_up, ffn_w_down
    w_in_p = w_in.astype(BF16)
    w_out_b = w_out
    bucket_table = _attn_bucket_table()
    gains = norm_gains.reshape(DEPTH, 6, 1, D_MODEL)
    x = x.reshape(N_TOK, D_MODEL)
    for l in range(DEPTH):
        x = _ffn(x, gains, wg, wu, wd, l, 0)
        proj = _inproj(x, gains, w_in_p, l)
        y_ssm = _s5_mixer(proj, p, l)
        y_dn = _dn_mixer(proj, p, l)
        o_at = _attn(proj, rel_bias, bucket_table)
        x = _outproj(x, y_ssm, y_dn, o_at, attn_out_gain[l].reshape(1, ATTN_WIDTH), gains, w_out_b, l)
        x = _ffn(x, gains, wg, wu, wd, l, 1)
    return x.reshape(BATCH, SEQ, D_MODEL)
```

```python
import functools
import math

import jax
import jax.numpy as jnp
import numpy as np
from jax import lax
from jax.experimental import pallas as pl
from jax.experimental.pallas import tpu as pltpu

D_MODEL = 2048
BATCH = 2
SEQ = 4096
DEPTH = 4
SSM_GROUPS = 32
SSM_CH = 16
SSM_STATE = 64
SSM_WIDTH = SSM_GROUPS * SSM_CH
DN_HEADS = 6
DN_HEAD_DIM = 128
DN_WIDTH = DN_HEADS * DN_HEAD_DIM
DN_CONV = 4
DN_CHUNK = 64
ATTN_HEADS = 6
ATTN_HEAD_DIM = 128
ATTN_WIDTH = ATTN_HEADS * ATTN_HEAD_DIM
DILATED_PAIRS = ((128, 1), (512, 4), (2048, 16))
ATTN_BLOCK = 128
N_BUCKETS = 32
REL_MAX_DIST = 2048
D_MIX = SSM_WIDTH + DN_WIDTH + ATTN_WIDTH
IN_SPLITS = (SSM_WIDTH, ATTN_WIDTH, ATTN_WIDTH, ATTN_WIDTH, 3 * DN_WIDTH, DN_WIDTH, DN_HEADS, DN_HEADS)
N_IN_COLS = sum(IN_SPLITS)
D_FF = 5632
NORM_EPS = 1e-6
NEG_INF = -1e30

LANES = 128
N_IN_PAD = 6144
V7X_VMEM_BYTES = 64 * 1024 * 1024
VMEM_LIMIT = V7X_VMEM_BYTES - 4 * 1024 * 1024
N_TOK = BATCH * SEQ

BF16 = jnp.bfloat16
F32 = jnp.float32


def _rms(x, gain):
    return x * lax.rsqrt(jnp.mean(x * x, axis=-1, keepdims=True) + NORM_EPS) * gain


def _silu(x):
    half = 0.5 * x
    return half + half * jnp.tanh(half)


FFN_TM = 1024
FFN_TF = 256


def _ffn_kernel(x_ref, gpre_ref, gpost_ref, wg_ref, wu_ref, wd_ref, o_ref, h_scr):
    f = pl.program_id(1)
    last = pl.num_programs(1) - 1

    def down_proj(h):
        gate = jnp.dot(h, wg_ref[...].astype(BF16), preferred_element_type=F32)
        up = jnp.dot(h, wu_ref[...].astype(BF16), preferred_element_type=F32)
        act = (gate * jax.nn.sigmoid(gate) * up).astype(BF16)
        return jnp.dot(act, wd_ref[...].astype(BF16), preferred_element_type=F32)

    @pl.when(f == 0)
    def _():
        h = _rms(x_ref[...], gpre_ref[...]).astype(BF16)
        h_scr[...] = h
        o_ref[...] = down_proj(h)

    @pl.when(jnp.logical_and(f > 0, f < last))
    def _():
        o_ref[...] += down_proj(h_scr[...])

    @pl.when(f == last)
    def _():
        y = o_ref[...] + down_proj(h_scr[...])
        o_ref[...] = x_ref[...] + _rms(y, 0.5 * gpost_ref[...])


def _ffn(x, gains, wg, wu, wd, layer, half):
    return pl.pallas_call(
        _ffn_kernel,
        grid=(N_TOK // FFN_TM, D_FF // FFN_TF),
        in_specs=[
            pl.BlockSpec((FFN_TM, D_MODEL), lambda i, f: (i, 0)),
            pl.BlockSpec((None, None, 1, D_MODEL), lambda i, f: (layer, 4 * half, 0, 0)),
            pl.BlockSpec((None, None, 1, D_MODEL), lambda i, f: (layer, 4 * half + 1, 0, 0)),
            pl.BlockSpec((None, None, D_MODEL, FFN_TF), lambda i, f: (layer, half, 0, f)),
            pl.BlockSpec((None, None, D_MODEL, FFN_TF), lambda i, f: (layer, half, 0, f)),
            pl.BlockSpec((None, None, FFN_TF, D_MODEL), lambda i, f: (layer, half, f, 0)),
        ],
        out_specs=pl.BlockSpec((FFN_TM, D_MODEL), lambda i, f: (i, 0)),
        out_shape=jax.ShapeDtypeStruct((N_TOK, D_MODEL), F32),
        scratch_shapes=[pltpu.VMEM((FFN_TM, D_MODEL), BF16)],
        compiler_params=pltpu.CompilerParams(
            dimension_semantics=("parallel", "arbitrary"), vmem_limit_bytes=VMEM_LIMIT),
        name="ffn",
    )(x, gains, gains, wg, wu, wd)


INP_TM = 1024
INP_TN = 1536


def _inproj_kernel(x_ref, g_ref, w_ref, o_ref, h_scr):
    @pl.when(pl.program_id(1) == 0)
    def _():
        h = _rms(x_ref[...], g_ref[...]).astype(BF16)
        h_scr[...] = h
        o_ref[...] = jnp.dot(h, w_ref[...], preferred_element_type=F32)

    @pl.when(pl.program_id(1) > 0)
    def _():
        o_ref[...] = jnp.dot(h_scr[...], w_ref[...], preferred_element_type=F32)


def _inproj(x, gains, w, layer):
    return pl.pallas_call(
        _inproj_kernel,
        grid=(N_TOK // INP_TM, N_IN_PAD // INP_TN),
        in_specs=[
            pl.BlockSpec((INP_TM, D_MODEL), lambda i, n: (i, 0)),
            pl.BlockSpec((None, None, 1, D_MODEL), lambda i, n: (layer, 2, 0, 0)),
            pl.BlockSpec((None, D_MODEL, INP_TN), lambda i, n: (layer, 0, n)),
        ],
        out_specs=pl.BlockSpec((INP_TM, INP_TN), lambda i, n: (i, n)),
        out_shape=jax.ShapeDtypeStruct((N_TOK, N_IN_PAD), F32),
        scratch_shapes=[pltpu.VMEM((INP_TM, D_MODEL), BF16)],
        compiler_params=pltpu.CompilerParams(
            dimension_semantics=("parallel", "arbitrary"), vmem_limit_bytes=VMEM_LIMIT),
        name="inproj",
    )(x, gains, w)


OUT_TM = 512
OUT_SUB = 256


def _outproj_kernel(x_ref, ys_ref, yd_ref, oa_ref, ga_ref, gpost_ref, w_ref, o_ref, w16):
    @pl.when(pl.program_id(0) == 0)
    def _():
        w16[...] = w_ref[...].astype(BF16)

    halves = [slice(i * OUT_SUB, (i + 1) * OUT_SUB) for i in range(OUT_TM // OUT_SUB)]
    ya = [_rms(oa_ref[r, :], ga_ref[...]).astype(BF16) for r in halves]
    mix = [jnp.dot(ys_ref[r, :].astype(BF16), w16[0:SSM_WIDTH, :], preferred_element_type=F32) for r in halves]
    mix = [m + jnp.dot(yd_ref[r, :].astype(BF16), w16[SSM_WIDTH:SSM_WIDTH + DN_WIDTH, :],
                       preferred_element_type=F32) for m, r in zip(mix, halves)]
    mix = [m + jnp.dot(a, w16[SSM_WIDTH + DN_WIDTH:D_MIX, :], preferred_element_type=F32)
           for m, a in zip(mix, ya)]
    for r, m in zip(halves, mix):
        o_ref[r, :] = x_ref[r, :] + _rms(m, gpost_ref[...])


def _outproj(x, y_ssm, y_dn, o_at, g_attn, gains, w, layer):
    row = lambda i: (i, 0)
    fixed = lambda i: (0, 0)
    return pl.pallas_call(
        _outproj_kernel,
        grid=(N_TOK // OUT_TM,),
        in_specs=[
            pl.BlockSpec((OUT_TM, D_MODEL), row),
            pl.BlockSpec((OUT_TM, SSM_WIDTH), row),
            pl.BlockSpec((OUT_TM, DN_WIDTH), row),
            pl.BlockSpec((OUT_TM, ATTN_WIDTH), row),
            pl.BlockSpec((1, ATTN_WIDTH), fixed),
            pl.BlockSpec((None, None, 1, D_MODEL), lambda i: (layer, 3, 0, 0)),
            pl.BlockSpec((None, D_MIX, D_MODEL), lambda i: (layer, 0, 0), pipeline_mode=pl.Buffered(1)),
        ],
        out_specs=pl.BlockSpec((OUT_TM, D_MODEL), row),
        out_shape=jax.ShapeDtypeStruct((N_TOK, D_MODEL), F32),
        scratch_shapes=[pltpu.VMEM((D_MIX, D_MODEL), BF16)],
        compiler_params=pltpu.CompilerParams(
            dimension_semantics=("arbitrary",), vmem_limit_bytes=VMEM_LIMIT),
        name="outproj",
    )(x, y_ssm, y_dn, o_at, g_attn, gains, w)


S5_SEG = 8
S5_SEGLEN = SEQ // S5_SEG
S5_KB = 128
S5_ROWS = S5_KB * S5_SEG
S5_PITCH = S5_KB + 8
S5_PROWS = S5_PITCH * S5_SEG
S5_NS = SSM_GROUPS * SSM_STATE
S5_SLABS = S5_NS // LANES
S5_Q = 4
S5_QS = S5_NS // S5_Q
S5_QL = S5_QS // LANES
S5_QC = SSM_WIDTH // S5_Q
S5_LOG2_SEGLEN = 9
assert 1 << S5_LOG2_SEGLEN == S5_SEGLEN


def _s5_kernel(u_ref, wb_ref, are_ref, aim_ref, pre_ref, pim_ref, cre_ref, cim_ref, d_ref, gw_ref, gb_ref,
               go_ref, o_ref, bu_scr, st_scr, carry_scr, y_scr):
    p = pl.program_id(1)
    j = pl.program_id(2)
    u = u_ref[0].reshape(S5_ROWS, SSM_WIDTH)
    ub = u.astype(BF16)

    @pl.when(jnp.logical_and(p == 0, j == 0))
    def _():
        bu_scr[...] = jnp.zeros_like(bu_scr)
        st_scr[...] = jnp.zeros_like(st_scr)

    @pl.when(jnp.logical_and(p == 1, j == 0))
    def _():
        st_scr[...] = carry_scr[...]

    def bu_quarter(q):
        return jnp.dot(ub[:, q * S5_QC:(q + 1) * S5_QC], wb_ref[q], preferred_element_type=F32)

    def block_end_states():
        for q in range(S5_Q):
            r = bu_quarter(q)
            re_cols = slice(q * S5_QS, (q + 1) * S5_QS)
            im_cols = slice(S5_NS + q * S5_QS, S5_NS + (q + 1) * S5_QS)
            pr, pi = pre_ref[:, re_cols], pim_ref[:, re_cols]
            sums_re, sums_im = [], []
            for i in range(S5_SEG):
                rr = r[i * S5_KB:(i + 1) * S5_KB, :S5_QS]
                ri = r[i * S5_KB:(i + 1) * S5_KB, S5_QS:]
                sums_re.append(jnp.sum(pr * rr - pi * ri, axis=0, keepdims=True))
                sums_im.append(jnp.sum(pr * ri + pi * rr, axis=0, keepdims=True))
            ar, ai = are_ref[:, re_cols], aim_ref[:, re_cols]
            kr = ar * pr[0:1, :] - ai * pi[0:1, :]
            ki = ar * pi[0:1, :] + ai * pr[0:1, :]
            sr, si = st_scr[:, re_cols], st_scr[:, im_cols]
            st_scr[:, re_cols] = kr * sr - ki * si + jnp.concatenate(sums_re, axis=0)
            st_scr[:, im_cols] = kr * si + ki * sr + jnp.concatenate(sums_im, axis=0)

    def scatter_bu():
        for q in range(S5_Q):
            r = bu_quarter(q)
            for half in range(2):
                for c in range(S5_QL):
                    slab = half * S5_SLABS + q * S5_QL + c
                    lanes = slice((half * S5_QL + c) * LANES, (half * S5_QL + c + 1) * LANES)
                    for i in range(S5_SEG):
                        bu_scr[slab, i * S5_PITCH:i * S5_PITCH + S5_KB, :] = r[i * S5_KB:(i + 1) * S5_KB, lanes]

    def scan(store):
        for q in range(S5_Q):
            slabs = [q * S5_QL + c for c in range(S5_QL)]
            lanes = [slice(s * LANES, (s + 1) * LANES) for s in slabs]
            ilanes = [slice(S5_NS + s * LANES, S5_NS + (s + 1) * LANES) for s in slabs]
            ar = [jnp.broadcast_to(are_ref[:, ln], (S5_SEG, LANES)) for ln in lanes]
            ai = [jnp.broadcast_to(aim_ref[:, ln], (S5_SEG, LANES)) for ln in lanes]

            def body(k, carry):
                sr, si = carry
                rows = pl.ds(k, S5_SEG, stride=S5_PITCH)
                nr = tuple(ar[c] * sr[c] - ai[c] * si[c] + bu_scr[slabs[c], rows, :] for c in range(S5_QL))
                ni = tuple(ar[c] * si[c] + ai[c] * sr[c] + bu_scr[slabs[c] + S5_SLABS, rows, :]
                           for c in range(S5_QL))
                if store:
                    for c in range(S5_QL):
                        bu_scr[slabs[c], rows, :] = nr[c]
                        bu_scr[slabs[c] + S5_SLABS, rows, :] = ni[c]
                return nr, ni

            init = (tuple(st_scr[:, ln] for ln in lanes), tuple(st_scr[:, ln] for ln in ilanes))
            sr, si = lax.fori_loop(0, S5_KB, body, init, unroll=8)
            for c in range(S5_QL):
                st_scr[:, lanes[c]] = sr[c]
                st_scr[:, ilanes[c]] = si[c]

    @pl.when(p == 0)
    def _():
        block_end_states()

        @pl.when(j == pl.num_programs(2) - 1)
        def _():
            lr, li = are_ref[...], aim_ref[...]
            for _ in range(S5_LOG2_SEGLEN):
                lr, li = lr * lr - li * li, 2.0 * lr * li
            cr = jnp.zeros((1, S5_NS), F32)
            ci = jnp.zeros((1, S5_NS), F32)
            carry_scr[0:1, :] = jnp.zeros((1, 2 * S5_NS), F32)
            for i in range(1, S5_SEG):
                er = st_scr[i - 1:i, 0:S5_NS]
                ei = st_scr[i - 1:i, S5_NS:2 * S5_NS]
                cr, ci = er + lr * cr - li * ci, ei + lr * ci + li * cr
                carry_scr[i:i + 1, 0:S5_NS] = cr
                carry_scr[i:i + 1, S5_NS:2 * S5_NS] = ci

    @pl.when(p == 1)
    def _():
        scatter_bu()
        scan(True)
        for q in range(S5_Q):
            sre = jnp.concatenate([bu_scr[q * S5_QL + c] for c in range(S5_QL)], axis=1).astype(BF16)
            sim = jnp.concatenate([bu_scr[S5_SLABS + q * S5_QL + c] for c in range(S5_QL)], axis=1).astype(BF16)
            y_scr[:, q * S5_QC:(q + 1) * S5_QC] = (
                jnp.dot(sre, cre_ref[q], preferred_element_type=F32)
                + jnp.dot(sim, cim_ref[q], preferred_element_type=F32))
        cs = jnp.concatenate([y_scr[i * S5_PITCH:i * S5_PITCH + S5_KB, :] for i in range(S5_SEG)], axis=0)
        y = jax.nn.gelu(cs + d_ref[...] * u)
        z = jnp.dot(y.astype(BF16), gw_ref[...], preferred_element_type=F32) + gb_ref[...]
        o_ref[0] = _rms(y * jax.nn.sigmoid(z), go_ref[...]).reshape(S5_SEG, S5_KB, SSM_WIDTH)


def _s5_params(lam_re, lam_im, b_re, b_im, c_re, c_im, log_dt):
    lam = lax.complex(lam_re, lam_im)
    lam_bar = jnp.exp(lam * jnp.exp(log_dt)[:, None])
    b_bar = ((lam_bar - 1.0) / lam)[..., None] * lax.complex(b_re, b_im)
    gq = SSM_GROUPS // S5_Q
    eye = jnp.eye(gq, dtype=F32)

    def in_map(t):
        t = t.reshape(S5_Q, gq, SSM_STATE, SSM_CH)
        return jnp.einsum('qgpc,gh->qgchp', t, eye).reshape(S5_Q, S5_QC, S5_QS)

    def out_map(t):
        t = t.reshape(S5_Q, gq, SSM_CH, SSM_STATE)
        return jnp.einsum('qgcp,gh->qgphc', t, eye).reshape(S5_Q, S5_QS, S5_QC)

    wb = jnp.concatenate([in_map(b_bar.real), in_map(b_bar.imag)], axis=-1).astype(BF16)
    lam_flat = lam_bar.reshape(1, S5_NS)
    powers = jnp.concatenate([jnp.ones_like(lam_flat),
                              jnp.cumprod(jnp.broadcast_to(lam_flat, (S5_KB - 1, S5_NS)), axis=0)], axis=0)[::-1]
    return (wb, lam_flat.real, lam_flat.imag, powers.real, powers.imag,
            out_map(c_re).astype(BF16), out_map(-c_im).astype(BF16))


def _s5(proj4, wb, a_re, a_im, p_re, p_im, cre, cim, d_skip, glu_w, glu_b, out_gain):
    nblk = S5_SEGLEN // S5_KB
    fix2 = lambda b, p, j: (0, 0)
    fix3 = lambda b, p, j: (0, 0, 0)
    return pl.pallas_call(
        _s5_kernel,
        grid=(BATCH, 2, nblk),
        in_specs=[
            pl.BlockSpec((1, S5_SEG, S5_KB, SSM_WIDTH), lambda b, p, j: (b, 0, j, 0)),
            pl.BlockSpec((S5_Q, S5_QC, 2 * S5_QS), fix3),
            pl.BlockSpec((1, S5_NS), fix2),
            pl.BlockSpec((1, S5_NS), fix2),
            pl.BlockSpec((S5_KB, S5_NS), fix2),
            pl.BlockSpec((S5_KB, S5_NS), fix2),
            pl.BlockSpec((S5_Q, S5_QS, S5_QC), fix3),
            pl.BlockSpec((S5_Q, S5_QS, S5_QC), fix3),
            pl.BlockSpec((1, SSM_WIDTH), fix2),
            pl.BlockSpec((SSM_WIDTH, SSM_WIDTH), fix2),
            pl.BlockSpec((1, SSM_WIDTH), fix2),
            pl.BlockSpec((1, SSM_WIDTH), fix2),
        ],
        out_specs=pl.BlockSpec((1, S5_SEG, S5_KB, SSM_WIDTH), lambda b, p, j: (b, 0, j * p, 0)),
        out_shape=jax.ShapeDtypeStruct((BATCH, S5_SEG, S5_SEGLEN, SSM_WIDTH), F32),
        scratch_shapes=[
            pltpu.VMEM((2 * S5_SLABS, S5_PROWS, LANES), F32),
            pltpu.VMEM((S5_SEG, 2 * S5_NS), F32),
            pltpu.VMEM((S5_SEG, 2 * S5_NS), F32),
            pltpu.VMEM((S5_PROWS, SSM_WIDTH), F32),
        ],
        compiler_params=pltpu.CompilerParams(
            dimension_semantics=("parallel", "arbitrary", "arbitrary"), vmem_limit_bytes=VMEM_LIMIT),
        name="s5",
    )(proj4, wb, a_re, a_im, p_re, p_im, cre, cim, d_skip, glu_w, glu_b, out_gain)


def _s5_mixer(proj, p, l):
    params = _s5_params(p['ssm_lambda_re'][l], p['ssm_lambda_im'][l], p['ssm_b_re'][l], p['ssm_b_im'][l],
                        p['ssm_c_re'][l], p['ssm_c_im'][l], p['ssm_log_dt'][l])
    y = _s5(proj.reshape(BATCH, S5_SEG, S5_SEGLEN, N_IN_PAD), *params,
            p['ssm_d'][l].reshape(1, SSM_WIDTH), p['ssm_glu_w'][l].astype(BF16),
            p['ssm_glu_b'][l].reshape(1, SSM_WIDTH), p['ssm_out_gain'][l].reshape(1, SSM_WIDTH))
    return y.reshape(N_TOK, SSM_WIDTH)


AT_BLK = ATTN_BLOCK
AT_NBRANCH = len(DILATED_PAIRS)
AT_BLOCKS = SEQ // AT_BLK
AT_GROUP = 16
AT_ORDER = tuple(range(1, AT_NBRANCH)) + (0,)
AT_PAD_GROUP = 16
AT_PAD_PITCH = 24
AT_PAD_ROWS = SEQ // AT_PAD_GROUP * AT_PAD_PITCH
assert [d for _, d in DILATED_PAIRS].count(AT_PAD_GROUP) == 1 and DILATED_PAIRS[-1][1] == AT_PAD_GROUP
assert DILATED_PAIRS[0][1] == 1 and all(d > 1 for _, d in DILATED_PAIRS[1:])
assert all(w // d == AT_BLK for w, d in DILATED_PAIRS)
assert all((AT_BLOCKS // d) % AT_GROUP == 0 or AT_GROUP % (AT_BLOCKS // d) == 0 for _, d in DILATED_PAIRS)
AT_Q_COL0 = SSM_WIDTH // LANES
AT_K_COL0 = AT_Q_COL0 + ATTN_HEADS
AT_V_COL0 = AT_K_COL0 + ATTN_HEADS


def _at_padded_row(s):
    return (s // AT_PAD_GROUP) * AT_PAD_PITCH + s % AT_PAD_GROUP


def _attn_kernel(rb_ref, bkt_ref, q_ref, k_ref, v_ref, o_ref, bias_scr, o_scr, lse_scr,
                 qkv_pad, o_pad, lse_pad):
    h = pl.program_id(0)

    def to_padded(i, carry):
        src_rows = pl.ds(pl.multiple_of(i * AT_PAD_GROUP, AT_PAD_GROUP), AT_PAD_GROUP)
        dst_rows = pl.ds(pl.multiple_of(i * AT_PAD_PITCH, 8), AT_PAD_GROUP)
        for j, ref in enumerate((q_ref, k_ref, v_ref)):
            qkv_pad[j, dst_rows, :] = ref[src_rows, :]
        return carry

    lax.fori_loop(0, SEQ // AT_PAD_GROUP, to_padded, 0, unroll=8)

    @pl.when(pl.program_id(1) == 0)
    def _():
        for g in range(AT_NBRANCH):
            bkt = bkt_ref[g]
            bias = jnp.zeros((AT_BLK, 2 * AT_BLK), F32)
            for b in range(N_BUCKETS):
                bias = jnp.where(bkt == b, rb_ref[b, h], bias)
            bias_scr[g] = jnp.where(bkt < 0, NEG_INF, bias)

    lane = lax.broadcasted_iota(jnp.int32, (AT_BLK, 2 * AT_BLK), 1)
    scale = ATTN_HEAD_DIM ** -0.5
    for g in AT_ORDER:
        dil = DILATED_PAIRS[g][1]
        nb = AT_BLOCKS // dil

        def body(it, carry, g=g, dil=dil, nb=nb):
            padded = dil == AT_PAD_GROUP
            if padded:
                qs, ks, vs = qkv_pad.at[0], qkv_pad.at[1], qkv_pad.at[2]
            else:
                qs, ks, vs = q_ref, k_ref, v_ref

            def block_rows(start):
                if dil == 1:
                    return pl.ds(pl.multiple_of(start, AT_BLK), AT_BLK)
                if padded:
                    return pl.ds(_at_padded_row(start), AT_BLK, stride=AT_PAD_PITCH)
                return pl.ds(start, AT_BLK, stride=dil)

            group = range(AT_GROUP)
            ts = [it * AT_GROUP + i for i in group]
            if nb <= AT_GROUP:
                ns = [i % nb for i in group]
            else:
                ns = [None if i == 0 else i for i in group]
            n0 = ts[0] % nb
            rows = [block_rows(t // nb + (t % nb) * (AT_BLK * dil)) for t in ts]
            q = [(qs[r, :] * scale).astype(BF16) for r in rows]
            k_cur = [ks[r, :].astype(BF16) for r in rows]
            v_cur = [vs[r, :].astype(BF16) for r in rows]
            if ns[0] is None:
                prow = block_rows(ts[0] // nb + jnp.maximum(n0 - 1, 0) * (AT_BLK * dil))
                k_lead, v_lead = ks[prow, :].astype(BF16), vs[prow, :].astype(BF16)
            s, vc = [], []
            for i in group:
                if ns[i] == 0:
                    s.append(_dot_nt(q[i], k_cur[i]) + bias_scr[g, :, AT_BLK:])
                    vc.append(v_cur[i])
                    continue
                k_prev, v_prev = (k_lead, v_lead) if ns[i] is None else (k_cur[i - 1], v_cur[i - 1])
                si = _dot_nt(q[i], jnp.concatenate([k_prev, k_cur[i]], axis=0)) + bias_scr[g]
                if ns[i] is None:
                    si = jnp.where(jnp.logical_or(n0 > 0, lane >= AT_BLK), si, NEG_INF)
                s.append(si)
                vc.append(jnp.concatenate([v_prev, v_cur[i]], axis=0))
            m = [jnp.max(si, axis=1, keepdims=True) for si in s]
            p = [jnp.exp(si - mi) for si, mi in zip(s, m)]
            l = [jnp.sum(pi, axis=1, keepdims=True) for pi in p]
            o = [jnp.dot(pi.astype(BF16), vi, preferred_element_type=F32) / li for pi, vi, li in zip(p, vc, l)]
            lse = [jnp.broadcast_to(mi + jnp.log(li), (AT_BLK, LANES)) for mi, li in zip(m, l)]
            if dil > 1:
                for r, oi, li in zip(rows, o, lse):
                    if padded:
                        o_pad[r, :], lse_pad[r, :] = oi, li
                    else:
                        o_scr[g - 1, r, :], lse_scr[g - 1, r, :] = oi, li
                return carry

            def unpad(ref, start):
                base = (start // AT_PAD_GROUP) * AT_PAD_PITCH
                return jnp.concatenate(
                    [ref[pl.ds(pl.multiple_of(base + j * AT_PAD_PITCH, 8), AT_PAD_GROUP), :]
                     for j in range(AT_BLK // AT_PAD_GROUP)], axis=0)

            for t, r, oi, li in zip(ts, rows, o, lse):
                parked = [(unpad(o_pad, t * AT_BLK), unpad(lse_pad, t * AT_BLK))
                          if DILATED_PAIRS[j][1] == AT_PAD_GROUP else (o_scr[j - 1, r, :], lse_scr[j - 1, r, :])
                          for j in range(1, AT_NBRANCH)]
                top = functools.reduce(jnp.maximum, [lj for _, lj in parked], li)
                wi = jnp.exp(li - top)
                num, den = wi * oi, wi
                for oj, lj in parked:
                    wj = jnp.exp(lj - top)
                    num, den = num + wj * oj, den + wj
                o_ref[r, :] = num / den
            return carry

        lax.fori_loop(0, AT_BLOCKS // AT_GROUP, body, 0)


def _attn_bucket_table():
    qi = jnp.arange(AT_BLK)[:, None]
    kj = jnp.arange(2 * AT_BLK)[None, :]
    rel = AT_BLK + qi - kj
    max_exact = N_BUCKETS // 2
    tables = []
    for window, dil in DILATED_PAIRS:
        dist = jnp.maximum(rel, 0) * dil
        d = jnp.maximum(dist, 1).astype(F32)
        large = max_exact + jnp.log(d / max_exact) / math.log(REL_MAX_DIST / max_exact) * (N_BUCKETS - max_exact)
        large = jnp.minimum(large.astype(jnp.int32), N_BUCKETS - 1)
        bucket = jnp.where(dist < max_exact, dist, large)
        tables.append(jnp.where((rel >= 0) & (rel <= window // dil), bucket, -1))
    return jnp.stack(tables).astype(jnp.int32)


def _attn(proj, rel_bias, bucket_table):
    col = lambda c0: pl.BlockSpec((SEQ, LANES), lambda h, b: (b, c0 + h))
    return pl.pallas_call(
        _attn_kernel,
        grid=(ATTN_HEADS, BATCH),
        in_specs=[
            pl.BlockSpec(memory_space=pltpu.SMEM),
            pl.BlockSpec((AT_NBRANCH, AT_BLK, 2 * AT_BLK), lambda h, b: (0, 0, 0)),
            col(AT_Q_COL0), col(AT_K_COL0), col(AT_V_COL0),
        ],
        out_specs=pl.BlockSpec((SEQ, LANES), lambda h, b: (b, h)),
        out_shape=jax.ShapeDtypeStruct((N_TOK, ATTN_WIDTH), F32),
        scratch_shapes=[
            pltpu.VMEM((AT_NBRANCH, AT_BLK, 2 * AT_BLK), F32),
            pltpu.VMEM((AT_NBRANCH - 2, SEQ, LANES), F32),
            pltpu.VMEM((AT_NBRANCH - 2, SEQ, LANES), F32),
            pltpu.VMEM((3, AT_PAD_ROWS, LANES), F32),
            pltpu.VMEM((AT_PAD_ROWS, LANES), F32),
            pltpu.VMEM((AT_PAD_ROWS, LANES), F32),
        ],
        compiler_params=pltpu.CompilerParams(
            dimension_semantics=("parallel", "arbitrary"), vmem_limit_bytes=VMEM_LIMIT),
        name="dilated_attn",
    )(rel_bias, bucket_table, proj, proj, proj)


DN_TILE = 2 * DN_CHUNK
DN_NTILE = SEQ // DN_TILE
DN_BH = BATCH * DN_HEADS
DN_PAD = 8
DN_GROUP = 16
DN_SCAN_TILES = 4
DN_AKT_ROWS = DN_CHUNK + DN_HEAD_DIM
DN_INV_LEVELS = 6
assert 2 ** DN_INV_LEVELS == DN_CHUNK
DN_QKV_COL0 = (SSM_WIDTH + 3 * ATTN_WIDTH) // LANES
DN_Z_COL0 = DN_QKV_COL0 + 3 * DN_HEADS
DN_AB_COL = DN_Z_COL0 + DN_HEADS
HIGHEST = lax.Precision.HIGHEST


def _dot_nt(a, b):
    return lax.dot_general(a, b, (((1,), (1,)), ((), ())), preferred_element_type=F32)


def _split3(x):
    hi = x.astype(BF16)
    r = x - hi.astype(F32)
    mid = r.astype(BF16)
    return hi, mid, (r - mid.astype(F32)).astype(BF16)


def _dot_exact_lhs(x, m16):
    return sum(jnp.dot(piece, m16, preferred_element_type=F32) for piece in _split3(x))


def _dot_exact_rhs(m16, x):
    return sum(jnp.dot(m16, piece, preferred_element_type=F32) for piece in _split3(x))


def _dn_tile_masks():
    row = lax.broadcasted_iota(jnp.int32, (DN_TILE, DN_TILE), 0)
    col = lax.broadcasted_iota(jnp.int32, (DN_TILE, DN_TILE), 1)
    causal = jnp.logical_and((row // DN_CHUNK) == (col // DN_CHUNK), row >= col)
    return row, col, causal


DN_GATE_ROWS = 1024


def _dn_gates_kernel(ab_ref, alog_ref, dtb_ref, o_ref):
    _, col, causal = _dn_tile_masks()
    cumsum_mat = causal.astype(BF16)
    neg_a = -jnp.exp(alog_ref[...])
    for t in range(DN_GATE_ROWS // DN_TILE):
        rows = slice(t * DN_TILE, (t + 1) * DN_TILE)
        ab = jnp.where(col < 2 * DN_HEADS, ab_ref[rows, :], 0.0)
        x = ab + dtb_ref[...]
        g = neg_a * (jnp.maximum(x, 0.0) + jnp.log1p(jnp.exp(-jnp.abs(x))))
        gc = _dot_exact_rhs(cumsum_mat, g)
        o_ref[rows, :] = jnp.where(col < DN_HEADS, gc, jax.nn.sigmoid(ab))


def _dn_gates(proj, a_log, dt_bias):
    lanes = lambda v: jnp.pad(v, (0, LANES - DN_HEADS)).reshape(1, LANES)
    return pl.pallas_call(
        _dn_gates_kernel,
        grid=(N_TOK // DN_GATE_ROWS,),
        in_specs=[
            pl.BlockSpec((DN_GATE_ROWS, LANES), lambda i: (i, DN_AB_COL)),
            pl.BlockSpec((1, LANES), lambda i: (0, 0)),
            pl.BlockSpec((1, LANES), lambda i: (0, 0)),
        ],
        out_specs=pl.BlockSpec((DN_GATE_ROWS, LANES), lambda i: (i, 0)),
        out_shape=jax.ShapeDtypeStruct((N_TOK, LANES), F32),
        compiler_params=pltpu.CompilerParams(dimension_semantics=("parallel",), vmem_limit_bytes=VMEM_LIMIT),
        name="dn_gates",
    )(proj, lanes(a_log), lanes(dt_bias))


def _dn_prep_kernel(q_ref, k_ref, v_ref, z_ref, gate_ref, wq_ref, wk_ref, wv_ref,
                    u_o, wq_o, akt_o, dec_o, sz_o, qp, kp, vp):
    h = pl.program_id(1)
    for src, dst in ((q_ref, qp), (k_ref, kp), (v_ref, vp)):
        dst[0:DN_PAD, :] = jnp.zeros((DN_PAD, LANES), F32)
        dst[DN_PAD:DN_PAD + SEQ, :] = src[...]

    row, col, causal = _dn_tile_masks()
    pick = jnp.concatenate([row == h, row == h + DN_HEADS], axis=1).astype(BF16)
    eye = (row == col).astype(F32)
    pair_masks = [
        jnp.logical_and(jnp.logical_and((row // (2 * s)) == (col // (2 * s)), (row // s) % 2 == 1),
                        (col // s) % 2 == 0)
        for s in (2 ** i for i in range(DN_INV_LEVELS))]
    scale = DN_HEAD_DIM ** -0.5

    def conv_silu(pad_ref, w_ref, base):
        acc = None
        for j in range(DN_CONV):
            sh = DN_PAD - (DN_CONV - 1) + j
            term = w_ref[j:j + 1, :] * pad_ref[pl.ds(base + sh, DN_TILE), :]
            acc = term if acc is None else acc + term
        return _silu(acc)

    def l2n(x):
        return x * lax.rsqrt(jnp.sum(x * x, axis=1, keepdims=True) + NORM_EPS)

    def mm16(a, b):
        return jnp.dot(a.astype(BF16), b.astype(BF16), preferred_element_type=F32)

    def body(it, carry):
        tiles = [it * DN_GROUP + i for i in range(DN_GROUP)]
        bases = [pl.multiple_of(t * DN_TILE, DN_TILE) for t in tiles]
        q = [l2n(conv_silu(qp, wq_ref, b)) * scale for b in bases]
        k = [l2n(conv_silu(kp, wk_ref, b)) for b in bases]
        v = [conv_silu(vp, wv_ref, b) for b in bases]
        gate_rep = [_dot_exact_lhs(gate_ref[pl.ds(b, DN_TILE), :], pick) for b in bases]
        gc = [r[:, :DN_TILE] for r in gate_rep]
        beta = [r[:, DN_TILE:] for r in gate_rep]
        decay = [jnp.exp(jnp.where(causal, c - c.T, NEG_INF)) for c in gc]
        kb = [ki * bi for ki, bi in zip(k, beta)]
        k16 = [ki.astype(BF16) for ki in k]
        a_mat = [_dot_nt(kbi.astype(BF16), ki) * di for kbi, ki, di in zip(kb, k16, decay)]
        t_inv = [eye - jnp.where(pair_masks[0], am, 0.0) for am in a_mat]
        for mask in pair_masks[1:]:
            low = [jnp.where(mask, am, 0.0).astype(BF16) for am in a_mat]
            t_inv = [ti - mm16(mm16(ti, lo), ti) for ti, lo in zip(t_inv, low)]
        egc = [jnp.exp(c) for c in gc]
        uw = [mm16(ti, jnp.concatenate([vi * bi, kbi * ei], axis=1))
              for ti, vi, bi, kbi, ei in zip(t_inv, v, beta, kb, egc)]
        at = [(_dot_nt(qi.astype(BF16), ki) * di).astype(BF16) for qi, ki, di in zip(q, k16, decay)]
        qd = [(qi * ei).astype(BF16) for qi, ei in zip(q, egc)]
        c = DN_CHUNK
        for i, t in enumerate(tiles):
            rows = pl.ds(bases[i], DN_TILE)
            u_o[t] = uw[i][:, :DN_HEAD_DIM]
            w = uw[i][:, DN_HEAD_DIM:].astype(BF16)
            wq_o[t] = jnp.concatenate([w[:c], qd[i][:c], w[c:], qd[i][c:]], axis=0)
            gc_first, gc_second = gc[i][c - 1:c, :], gc[i][DN_TILE - 1:DN_TILE, :]
            gc_last = jnp.where(row < c, gc_first, gc_second)
            kt = (k[i] * jnp.exp(gc_last - gc[i])).T
            akt_o[t] = jnp.concatenate([at[i][:c] + at[i][c:], kt.astype(BF16)], axis=0)
            dec_o[t] = jnp.concatenate(
                [jnp.exp(gc_first), jnp.exp(gc_second), jnp.zeros((DN_PAD - 2, LANES), F32)], axis=0)
            z = z_ref[rows, :]
            sz_o[t] = _silu(z).astype(BF16)
        return carry

    lax.fori_loop(0, DN_NTILE // DN_GROUP, body, 0)


def _dn_prep(proj, gates, conv_w):
    col = lambda c0: pl.BlockSpec((SEQ, LANES), lambda b, h: (b, c0 + h))
    cw = lambda c0: pl.BlockSpec((DN_CONV, LANES), lambda b, h: (0, c0 + h))
    bh_rows = lambda n: pl.BlockSpec((DN_NTILE, None, n, LANES), lambda b, h: (0, b * DN_HEADS + h, 0, 0))
    rows_shape = lambda n, dt: jax.ShapeDtypeStruct((DN_NTILE, DN_BH, n, LANES), dt)
    return pl.pallas_call(
        _dn_prep_kernel,
        grid=(BATCH, DN_HEADS),
        in_specs=[
            col(DN_QKV_COL0), col(DN_QKV_COL0 + DN_HEADS), col(DN_QKV_COL0 + 2 * DN_HEADS), col(DN_Z_COL0),
            pl.BlockSpec((SEQ, LANES), lambda b, h: (b, 0)),
            cw(0), cw(DN_HEADS), cw(2 * DN_HEADS),
        ],
        out_specs=[bh_rows(DN_TILE), bh_rows(2 * DN_TILE), bh_rows(DN_AKT_ROWS), bh_rows(DN_PAD), bh_rows(DN_TILE)],
        out_shape=[
            rows_shape(DN_TILE, F32), rows_shape(2 * DN_TILE, BF16), rows_shape(DN_AKT_ROWS, BF16),
            rows_shape(DN_PAD, F32), rows_shape(DN_TILE, BF16),
        ],
        scratch_shapes=[pltpu.VMEM((SEQ + DN_PAD, LANES), F32)] * 3,
        compiler_params=pltpu.CompilerParams(
            dimension_semantics=("parallel", "parallel"), vmem_limit_bytes=VMEM_LIMIT),
        name="dn_prep",
    )(proj, proj, proj, proj, gates, conv_w, conv_w, conv_w)


def _dn_scan_kernel(u_ref, wq_ref, akt_ref, dec_ref, sz_ref, gain_ref, o_ref, s_scr):
    @pl.when(pl.program_id(0) == 0)
    def _():
        s_scr[...] = jnp.zeros_like(s_scr)

    gain = gain_ref[...]
    c = DN_CHUNK
    chains = range(DN_BH)
    second_chunk_lane = lax.broadcasted_iota(jnp.int32, (DN_AKT_ROWS, LANES), 1) >= c
    state = [s_scr[bh] for bh in chains]
    v_first = None
    for tt, j in ((tt, j) for tt in range(DN_SCAN_TILES) for j in range(2)):
        rows = slice(j * c, (j + 1) * c)
        out_rows = slice(tt * DN_TILE + j * c, tt * DN_TILE + (j + 1) * c)
        s16 = [s.astype(BF16) for s in state]
        ws_qs = [jnp.dot(wq_ref[tt, bh, j * DN_TILE:(j + 1) * DN_TILE, :], s16[bh],
                         preferred_element_type=F32) for bh in chains]
        v_new = [u_ref[tt, bh, rows, :] - ws_qs[bh][:c] for bh in chains]
        if j == 0:
            v_pair = [jnp.concatenate([v, jnp.zeros_like(v)], axis=0).astype(BF16) for v in v_new]
            v_first = v_new
            akt = [akt_ref[tt, bh] for bh in chains]
        else:
            v_pair = [jnp.concatenate([v0, v], axis=0).astype(BF16) for v0, v in zip(v_first, v_new)]
            akt = [jnp.where(second_chunk_lane, akt_ref[tt, bh], jnp.zeros((DN_AKT_ROWS, LANES), BF16))
                   for bh in chains]
        ov_kv = [jnp.dot(akt[bh], v_pair[bh], preferred_element_type=F32) for bh in chains]
        state = [state[bh] * dec_ref[tt, bh, j:j + 1, :] + ov_kv[bh][c:] for bh in chains]
        for bh in chains:
            b, h = divmod(bh, DN_HEADS)
            o = ws_qs[bh][c:] + ov_kv[bh][:c]
            o_ref[b, out_rows, h * DN_HEAD_DIM:(h + 1) * DN_HEAD_DIM] = (
                _rms(o, gain) * sz_ref[tt, bh, rows, :].astype(F32))
    for bh in chains:
        s_scr[bh] = state[bh]


def _dn_scan(u, wq, akt, dec, sz, gain):
    rows = lambda n: pl.BlockSpec((DN_SCAN_TILES, DN_BH, n, LANES), lambda t: (t, 0, 0, 0))
    return pl.pallas_call(
        _dn_scan_kernel,
        grid=(DN_NTILE // DN_SCAN_TILES,),
        in_specs=[rows(DN_TILE), rows(2 * DN_TILE), rows(DN_AKT_ROWS), rows(DN_PAD), rows(DN_TILE),
                  pl.BlockSpec((1, LANES), lambda t: (0, 0))],
        out_specs=pl.BlockSpec((BATCH, DN_SCAN_TILES * DN_TILE, DN_WIDTH), lambda t: (0, t, 0)),
        out_shape=jax.ShapeDtypeStruct((BATCH, SEQ, DN_WIDTH), F32),
        scratch_shapes=[pltpu.VMEM((DN_BH, DN_HEAD_DIM, DN_HEAD_DIM), F32)],
        compiler_params=pltpu.CompilerParams(
            dimension_semantics=("arbitrary",), vmem_limit_bytes=VMEM_LIMIT),
        name="dn_scan",
    )(u, wq, akt, dec, sz, gain)


def _dn_mixer(proj, p, l):
    gates = _dn_gates(proj, p['dn_a_log'][l], p['dn_dt_bias'][l])
    outs = _dn_prep(proj, gates, p['dn_conv_w'][l])
    y = _dn_scan(*outs, p['dn_norm_gain'][l].reshape(1, DN_HEAD_DIM))
    return y.reshape(N_TOK, DN_WIDTH)


def kernel(x, norm_gains, ffn_w_gate, ffn_w_up, ffn_w_down, w_in, w_out, ssm_lambda_re, ssm_lambda_im,
           ssm_b_re, ssm_b_im, ssm_c_re, ssm_c_im, ssm_d, ssm_log_dt, ssm_glu_w, ssm_glu_b, ssm_out_gain,
           dn_conv_w, dn_a_log, dn_dt_bias, dn_norm_gain, attn_out_gain, rel_bias):
    p = dict(ssm_lambda_re=ssm_lambda_re, ssm_lambda_im=ssm_lambda_im, ssm_b_re=ssm_b_re, ssm_b_im=ssm_b_im,
             ssm_c_re=ssm_c_re, ssm_c_im=ssm_c_im, ssm_d=ssm_d, ssm_log_dt=ssm_log_dt, ssm_glu_w=ssm_glu_w,
             ssm_glu_b=ssm_glu_b, ssm_out_gain=ssm_out_gain, dn_conv_w=dn_conv_w, dn_a_log=dn_a_log,
             dn_dt_bias=dn_dt_bias, dn_norm_gain=dn_norm_gain)
    wg, wu, wd = ffn_w_gate, ffn_w_up, ffn_w_down
    w_in_p = w_in.astype(BF16)
    w_out_b = w_out
    bucket_table = _attn_bucket_table()
    gains = norm_gains.reshape(DEPTH, 6, 1, D_MODEL)
    x = x.reshape(N_TOK, D_MODEL)
    for l in range(DEPTH):
        x = _ffn(x, gains, wg, wu, wd, l, 0)
        proj = _inproj(x, gains, w_in_p, l)
        y_ssm = _s5_mixer(proj, p, l)
        y_dn = _dn_mixer(proj, p, l)
        o_at = _attn(proj, rel_bias, bucket_table)
        x = _outproj(x, y_ssm, y_dn, o_at, attn_out_gain[l].reshape(1, ATTN_WIDTH), gains, w_out_b, l)
        x = _ffn(x, gains, wg, wu, wd, l, 1)
    return x.reshape(BATCH, SEQ, D_MODEL)
```

```python
import functools
import math

import jax
import jax.numpy as jnp
import numpy as np
from jax import lax
from jax.experimental import pallas as pl
from jax.experimental.pallas import tpu as pltpu

D_MODEL = 2048
BATCH = 2
SEQ = 4096
DEPTH = 4
SSM_GROUPS = 32
SSM_CH = 16
SSM_STATE = 64
SSM_WIDTH = SSM_GROUPS * SSM_CH
DN_HEADS = 6
DN_HEAD_DIM = 128
DN_WIDTH = DN_HEADS * DN_HEAD_DIM
DN_CONV = 4
DN_CHUNK = 64
ATTN_HEADS = 6
ATTN_HEAD_DIM = 128
ATTN_WIDTH = ATTN_HEADS * ATTN_HEAD_DIM
DILATED_PAIRS = ((128, 1), (512, 4), (2048, 16))
ATTN_BLOCK = 128
N_BUCKETS = 32
REL_MAX_DIST = 2048
D_MIX = SSM_WIDTH + DN_WIDTH + ATTN_WIDTH
IN_SPLITS = (SSM_WIDTH, ATTN_WIDTH, ATTN_WIDTH, ATTN_WIDTH, 3 * DN_WIDTH, DN_WIDTH, DN_HEADS, DN_HEADS)
N_IN_COLS = sum(IN_SPLITS)
D_FF = 5632
NORM_EPS = 1e-6
NEG_INF = -1e30

LANES = 128
N_IN_PAD = 6144
V7X_VMEM_BYTES = 64 * 1024 * 1024
VMEM_LIMIT = V7X_VMEM_BYTES - 4 * 1024 * 1024
N_TOK = BATCH * SEQ

BF16 = jnp.bfloat16
F32 = jnp.float32


def _rms(x, gain):
    return x * lax.rsqrt(jnp.mean(x * x, axis=-1, keepdims=True) + NORM_EPS) * gain


def _silu(x):
    half = 0.5 * x
    return half + half * jnp.tanh(half)


FFN_TM = 1024
FFN_TF = 256


def _ffn_kernel(x_ref, gpre_ref, gpost_ref, wg_ref, wu_ref, wd_ref, o_ref, h_scr):
    f = pl.program_id(1)
    last = pl.num_programs(1) - 1

    def down_proj(h):
        gate = jnp.dot(h, wg_ref[...].astype(BF16), preferred_element_type=F32)
        up = jnp.dot(h, wu_ref[...].astype(BF16), preferred_element_type=F32)
        act = (gate * jax.nn.sigmoid(gate) * up).astype(BF16)
        return jnp.dot(act, wd_ref[...].astype(BF16), preferred_element_type=F32)

    @pl.when(f == 0)
    def _():
        h = _rms(x_ref[...], gpre_ref[...]).astype(BF16)
        h_scr[...] = h
        o_ref[...] = down_proj(h)

    @pl.when(jnp.logical_and(f > 0, f < last))
    def _():
        o_ref[...] += down_proj(h_scr[...])

    @pl.when(f == last)
    def _():
        y = o_ref[...] + down_proj(h_scr[...])
        o_ref[...] = x_ref[...] + _rms(y, 0.5 * gpost_ref[...])


def _ffn(x, gains, wg, wu, wd, layer, half):
    return pl.pallas_call(
        _ffn_kernel,
        grid=(N_TOK // FFN_TM, D_FF // FFN_TF),
        in_specs=[
            pl.BlockSpec((FFN_TM, D_MODEL), lambda i, f: (i, 0)),
            pl.BlockSpec((None, None, 1, D_MODEL), lambda i, f: (layer, 4 * half, 0, 0)),
            pl.BlockSpec((None, None, 1, D_MODEL), lambda i, f: (layer, 4 * half + 1, 0, 0)),
            pl.BlockSpec((None, None, D_MODEL, FFN_TF), lambda i, f: (layer, half, 0, f)),
            pl.BlockSpec((None, None, D_MODEL, FFN_TF), lambda i, f: (layer, half, 0, f)),
            pl.BlockSpec((None, None, FFN_TF, D_MODEL), lambda i, f: (layer, half, f, 0)),
        ],
        out_specs=pl.BlockSpec((FFN_TM, D_MODEL), lambda i, f: (i, 0)),
        out_shape=jax.ShapeDtypeStruct((N_TOK, D_MODEL), F32),
        scratch_shapes=[pltpu.VMEM((FFN_TM, D_MODEL), BF16)],
        compiler_params=pltpu.CompilerParams(
            dimension_semantics=("parallel", "arbitrary"), vmem_limit_bytes=VMEM_LIMIT),
        name="ffn",
    )(x, gains, gains, wg, wu, wd)


INP_TM = 1024
INP_TN = 1536


def _inproj_kernel(x_ref, g_ref, w_ref, o_ref, h_scr):
    @pl.when(pl.program_id(1) == 0)
    def _():
        h = _rms(x_ref[...], g_ref[...]).astype(BF16)
        h_scr[...] = h
        o_ref[...] = jnp.dot(h, w_ref[...], preferred_element_type=F32)

    @pl.when(pl.program_id(1) > 0)
    def _():
        o_ref[...] = jnp.dot(h_scr[...], w_ref[...], preferred_element_type=F32)


def _inproj(x, gains, w, layer):
    return pl.pallas_call(
        _inproj_kernel,
        grid=(N_TOK // INP_TM, N_IN_PAD // INP_TN),
        in_specs=[
            pl.BlockSpec((INP_TM, D_MODEL), lambda i, n: (i, 0)),
            pl.BlockSpec((None, None, 1, D_MODEL), lambda i, n: (layer, 2, 0, 0)),
            pl.BlockSpec((None, D_MODEL, INP_TN), lambda i, n: (layer, 0, n)),
        ],
        out_specs=pl.BlockSpec((INP_TM, INP_TN), lambda i, n: (i, n)),
        out_shape=jax.ShapeDtypeStruct((N_TOK, N_IN_PAD), F32),
        scratch_shapes=[pltpu.VMEM((INP_TM, D_MODEL), BF16)],
        compiler_params=pltpu.CompilerParams(
            dimension_semantics=("parallel", "arbitrary"), vmem_limit_bytes=VMEM_LIMIT),
        name="inproj",
    )(x, gains, w)


OUT_TM = 512
OUT_SUB = 256


def _outproj_kernel(x_ref, ys_ref, yd_ref, oa_ref, ga_ref, gpost_ref, w_ref, o_ref, w16):
    @pl.when(pl.program_id(0) == 0)
    def _():
        w16[...] = w_ref[...].astype(BF16)

    halves = [slice(i * OUT_SUB, (i + 1) * OUT_SUB) for i in range(OUT_TM // OUT_SUB)]
    ya = [_rms(oa_ref[r, :], ga_ref[...]).astype(BF16) for r in halves]
    mix = [jnp.dot(ys_ref[r, :].astype(BF16), w16[0:SSM_WIDTH, :], preferred_element_type=F32) for r in halves]
    mix = [m + jnp.dot(yd_ref[r, :].astype(BF16), w16[SSM_WIDTH:SSM_WIDTH + DN_WIDTH, :],
                       preferred_element_type=F32) for m, r in zip(mix, halves)]
    mix = [m + jnp.dot(a, w16[SSM_WIDTH + DN_WIDTH:D_MIX, :], preferred_element_type=F32)
           for m, a in zip(mix, ya)]
    for r, m in zip(halves, mix):
        o_ref[r, :] = x_ref[r, :] + _rms(m, gpost_ref[...])


def _outproj(x, y_ssm, y_dn, o_at, g_attn, gains, w, layer):
    row = lambda i: (i, 0)
    fixed = lambda i: (0, 0)
    return pl.pallas_call(
        _outproj_kernel,
        grid=(N_TOK // OUT_TM,),
        in_specs=[
            pl.BlockSpec((OUT_TM, D_MODEL), row),
            pl.BlockSpec((OUT_TM, SSM_WIDTH), row),
            pl.BlockSpec((OUT_TM, DN_WIDTH), row),
            pl.BlockSpec((OUT_TM, ATTN_WIDTH), row),
            pl.BlockSpec((1, ATTN_WIDTH), fixed),
            pl.BlockSpec((None, None, 1, D_MODEL), lambda i: (layer, 3, 0, 0)),
            pl.BlockSpec((None, D_MIX, D_MODEL), lambda i: (layer, 0, 0), pipeline_mode=pl.Buffered(1)),
        ],
        out_specs=pl.BlockSpec((OUT_TM, D_MODEL), row),
        out_shape=jax.ShapeDtypeStruct((N_TOK, D_MODEL), F32),
        scratch_shapes=[pltpu.VMEM((D_MIX, D_MODEL), BF16)],
        compiler_params=pltpu.CompilerParams(
            dimension_semantics=("arbitrary",), vmem_limit_bytes=VMEM_LIMIT),
        name="outproj",
    )(x, y_ssm, y_dn, o_at, g_attn, gains, w)


S5_SEG = 8
S5_SEGLEN = SEQ // S5_SEG
S5_KB = 128
S5_ROWS = S5_KB * S5_SEG
S5_PITCH = S5_KB + 8
S5_PROWS = S5_PITCH * S5_SEG
S5_NS = SSM_GROUPS * SSM_STATE
S5_SLABS = S5_NS // LANES
S5_Q = 4
S5_QS = S5_NS // S5_Q
S5_QL = S5_QS // LANES
S5_QC = SSM_WIDTH // S5_Q
S5_LOG2_SEGLEN = 9
assert 1 << S5_LOG2_SEGLEN == S5_SEGLEN


def _s5_kernel(u_ref, wb_ref, are_ref, aim_ref, pre_ref, pim_ref, cre_ref, cim_ref, d_ref, gw_ref, gb_ref,
               go_ref, o_ref, bu_scr, st_scr, carry_scr, y_scr):
    p = pl.program_id(1)
    j = pl.program_id(2)
    u = u_ref[0].reshape(S5_ROWS, SSM_WIDTH)
    ub = u.astype(BF16)

    @pl.when(jnp.logical_and(p == 0, j == 0))
    def _():
        bu_scr[...] = jnp.zeros_like(bu_scr)
        st_scr[...] = jnp.zeros_like(st_scr)

    @pl.when(jnp.logical_and(p == 1, j == 0))
    def _():
        st_scr[...] = carry_scr[...]

    def bu_quarter(q):
        return jnp.dot(ub[:, q * S5_QC:(q + 1) * S5_QC], wb_ref[q], preferred_element_type=F32)

    def block_end_states():
        for q in range(S5_Q):
            r = bu_quarter(q)
            re_cols = slice(q * S5_QS, (q + 1) * S5_QS)
            im_cols = slice(S5_NS + q * S5_QS, S5_NS + (q + 1) * S5_QS)
            pr, pi = pre_ref[:, re_cols], pim_ref[:, re_cols]
            sums_re, sums_im = [], []
            for i in range(S5_SEG):
                rr = r[i * S5_KB:(i + 1) * S5_KB, :S5_QS]
                ri = r[i * S5_KB:(i + 1) * S5_KB, S5_QS:]
                sums_re.append(jnp.sum(pr * rr - pi * ri, axis=0, keepdims=True))
                sums_im.append(jnp.sum(pr * ri + pi * rr, axis=0, keepdims=True))
            ar, ai = are_ref[:, re_cols], aim_ref[:, re_cols]
            kr = ar * pr[0:1, :] - ai * pi[0:1, :]
            ki = ar * pi[0:1, :] + ai * pr[0:1, :]
            sr, si = st_scr[:, re_cols], st_scr[:, im_cols]
            st_scr[:, re_cols] = kr * sr - ki * si + jnp.concatenate(sums_re, axis=0)
            st_scr[:, im_cols] = kr * si + ki * sr + jnp.concatenate(sums_im, axis=0)

    def scatter_bu():
        for q in range(S5_Q):
            r = bu_quarter(q)
            for half in range(2):
                for c in range(S5_QL):
                    slab = half * S5_SLABS + q * S5_QL + c
                    lanes = slice((half * S5_QL + c) * LANES, (half * S5_QL + c + 1) * LANES)
                    for i in range(S5_SEG):
                        bu_scr[slab, i * S5_PITCH:i * S5_PITCH + S5_KB, :] = r[i * S5_KB:(i + 1) * S5_KB, lanes]

    def scan(store):
        for q in range(S5_Q):
            slabs = [q * S5_QL + c for c in range(S5_QL)]
            lanes = [slice(s * LANES, (s + 1) * LANES) for s in slabs]
            ilanes = [slice(S5_NS + s * LANES, S5_NS + (s + 1) * LANES) for s in slabs]
            ar = [jnp.broadcast_to(are_ref[:, ln], (S5_SEG, LANES)) for ln in lanes]
            ai = [jnp.broadcast_to(aim_ref[:, ln], (S5_SEG, LANES)) for ln in lanes]

            def body(k, carry):
                sr, si = carry
                rows = pl.ds(k, S5_SEG, stride=S5_PITCH)
                nr = tuple(ar[c] * sr[c] - ai[c] * si[c] + bu_scr[slabs[c], rows, :] for c in range(S5_QL))
                ni = tuple(ar[c] * si[c] + ai[c] * sr[c] + bu_scr[slabs[c] + S5_SLABS, rows, :]
                           for c in range(S5_QL))
                if store:
                    for c in range(S5_QL):
                        bu_scr[slabs[c], rows, :] = nr[c]
                        bu_scr[slabs[c] + S5_SLABS, rows, :] = ni[c]
                return nr, ni

            init = (tuple(st_scr[:, ln] for ln in lanes), tuple(st_scr[:, ln] for ln in ilanes))
            sr, si = lax.fori_loop(0, S5_KB, body, init, unroll=8)
            for c in range(S5_QL):
                st_scr[:, lanes[c]] = sr[c]
                st_scr[:, ilanes[c]] = si[c]

    @pl.when(p == 0)
    def _():
        block_end_states()

        @pl.when(j == pl.num_programs(2) - 1)
        def _():
            lr, li = are_ref[...], aim_ref[...]
            for _ in range(S5_LOG2_SEGLEN):
                lr, li = lr * lr - li * li, 2.0 * lr * li
            cr = jnp.zeros((1, S5_NS), F32)
            ci = jnp.zeros((1, S5_NS), F32)
            carry_scr[0:1, :] = jnp.zeros((1, 2 * S5_NS), F32)
            for i in range(1, S5_SEG):
                er = st_scr[i - 1:i, 0:S5_NS]
                ei = st_scr[i - 1:i, S5_NS:2 * S5_NS]
                cr, ci = er + lr * cr - li * ci, ei + lr * ci + li * cr
                carry_scr[i:i + 1, 0:S5_NS] = cr
                carry_scr[i:i + 1, S5_NS:2 * S5_NS] = ci

    @pl.when(p == 1)
    def _():
        scatter_bu()
        scan(True)
        for q in range(S5_Q):
            sre = jnp.concatenate([bu_scr[q * S5_QL + c] for c in range(S5_QL)], axis=1).astype(BF16)
            sim = jnp.concatenate([bu_scr[S5_SLABS + q * S5_QL + c] for c in range(S5_QL)], axis=1).astype(BF16)
            y_scr[:, q * S5_QC:(q + 1) * S5_QC] = (
                jnp.dot(sre, cre_ref[q], preferred_element_type=F32)
                + jnp.dot(sim, cim_ref[q], preferred_element_type=F32))
        cs = jnp.concatenate([y_scr[i * S5_PITCH:i * S5_PITCH + S5_KB, :] for i in range(S5_SEG)], axis=0)
        y = jax.nn.gelu(cs + d_ref[...] * u)
        z = jnp.dot(y.astype(BF16), gw_ref[...], preferred_element_type=F32) + gb_ref[...]
        o_ref[0] = _rms(y * jax.nn.sigmoid(z), go_ref[...]).reshape(S5_SEG, S5_KB, SSM_WIDTH)


def _s5_params(lam_re, lam_im, b_re, b_im, c_re, c_im, log_dt):
    lam = lax.complex(lam_re, lam_im)
    lam_bar = jnp.exp(lam * jnp.exp(log_dt)[:, None])
    b_bar = ((lam_bar - 1.0) / lam)[..., None] * lax.complex(b_re, b_im)
    gq = SSM_GROUPS // S5_Q
    eye = jnp.eye(gq, dtype=F32)

    def in_map(t):
        t = t.reshape(S5_Q, gq, SSM_STATE, SSM_CH)
        return jnp.einsum('qgpc,gh->qgchp', t, eye).reshape(S5_Q, S5_QC, S5_QS)

    def out_map(t):
        t = t.reshape(S5_Q, gq, SSM_CH, SSM_STATE)
        return jnp.einsum('qgcp,gh->qgphc', t, eye).reshape(S5_Q, S5_QS, S5_QC)

    wb = jnp.concatenate([in_map(b_bar.real), in_map(b_bar.imag)], axis=-1).astype(BF16)
    lam_flat = lam_bar.reshape(1, S5_NS)
    powers = jnp.concatenate([lax.cumprod(jnp.broadcast_to(lam_flat, (S5_KB - 1, S5_NS)), axis=0, reverse=True),
                              jnp.ones_like(lam_flat)], axis=0)
    return (wb, lam_flat.real, lam_flat.imag, powers.real, powers.imag,
            out_map(c_re).astype(BF16), out_map(-c_im).astype(BF16))


def _s5(proj4, wb, a_re, a_im, p_re, p_im, cre, cim, d_skip, glu_w, glu_b, out_gain):
    nblk = S5_SEGLEN // S5_KB
    fix2 = lambda b, p, j: (0, 0)
    fix3 = lambda b, p, j: (0, 0, 0)
    return pl.pallas_call(
        _s5_kernel,
        grid=(BATCH, 2, nblk),
        in_specs=[
            pl.BlockSpec((1, S5_SEG, S5_KB, SSM_WIDTH), lambda b, p, j: (b, 0, j, 0)),
            pl.BlockSpec((S5_Q, S5_QC, 2 * S5_QS), fix3),
            pl.BlockSpec((1, S5_NS), fix2),
            pl.BlockSpec((1, S5_NS), fix2),
            pl.BlockSpec((S5_KB, S5_NS), fix2),
            pl.BlockSpec((S5_KB, S5_NS), fix2),
            pl.BlockSpec((S5_Q, S5_QS, S5_QC), fix3),
            pl.BlockSpec((S5_Q, S5_QS, S5_QC), fix3),
            pl.BlockSpec((1, SSM_WIDTH), fix2),
            pl.BlockSpec((SSM_WIDTH, SSM_WIDTH), fix2),
            pl.BlockSpec((1, SSM_WIDTH), fix2),
            pl.BlockSpec((1, SSM_WIDTH), fix2),
        ],
        out_specs=pl.BlockSpec((1, S5_SEG, S5_KB, SSM_WIDTH), lambda b, p, j: (b, 0, j * p, 0)),
        out_shape=jax.ShapeDtypeStruct((BATCH, S5_SEG, S5_SEGLEN, SSM_WIDTH), F32),
        scratch_shapes=[
            pltpu.VMEM((2 * S5_SLABS, S5_PROWS, LANES), F32),
            pltpu.VMEM((S5_SEG, 2 * S5_NS), F32),
            pltpu.VMEM((S5_SEG, 2 * S5_NS), F32),
            pltpu.VMEM((S5_PROWS, SSM_WIDTH), F32),
        ],
        compiler_params=pltpu.CompilerParams(
            dimension_semantics=("parallel", "arbitrary", "arbitrary"), vmem_limit_bytes=VMEM_LIMIT),
        name="s5",
    )(proj4, wb, a_re, a_im, p_re, p_im, cre, cim, d_skip, glu_w, glu_b, out_gain)


def _s5_mixer(proj, p, l):
    params = _s5_params(p['ssm_lambda_re'][l], p['ssm_lambda_im'][l], p['ssm_b_re'][l], p['ssm_b_im'][l],
                        p['ssm_c_re'][l], p['ssm_c_im'][l], p['ssm_log_dt'][l])
    y = _s5(proj.reshape(BATCH, S5_SEG, S5_SEGLEN, N_IN_PAD), *params,
            p['ssm_d'][l].reshape(1, SSM_WIDTH), p['ssm_glu_w'][l].astype(BF16),
            p['ssm_glu_b'][l].reshape(1, SSM_WIDTH), p['ssm_out_gain'][l].reshape(1, SSM_WIDTH))
    return y.reshape(N_TOK, SSM_WIDTH)


AT_BLK = ATTN_BLOCK
AT_NBRANCH = len(DILATED_PAIRS)
AT_BLOCKS = SEQ // AT_BLK
AT_GROUP = 16
AT_ORDER = tuple(range(1, AT_NBRANCH)) + (0,)
AT_PAD_GROUP = 16
AT_PAD_PITCH = 24
AT_PAD_ROWS = SEQ // AT_PAD_GROUP * AT_PAD_PITCH
assert [d for _, d in DILATED_PAIRS].count(AT_PAD_GROUP) == 1 and DILATED_PAIRS[-1][1] == AT_PAD_GROUP
assert DILATED_PAIRS[0][1] == 1 and all(d > 1 for _, d in DILATED_PAIRS[1:])
assert all(w // d == AT_BLK for w, d in DILATED_PAIRS)
assert all((AT_BLOCKS // d) % AT_GROUP == 0 or AT_GROUP % (AT_BLOCKS // d) == 0 for _, d in DILATED_PAIRS)
AT_Q_COL0 = SSM_WIDTH // LANES
AT_K_COL0 = AT_Q_COL0 + ATTN_HEADS
AT_V_COL0 = AT_K_COL0 + ATTN_HEADS


def _at_padded_row(s):
    return (s // AT_PAD_GROUP) * AT_PAD_PITCH + s % AT_PAD_GROUP


def _attn_kernel(rb_ref, bkt_ref, q_ref, k_ref, v_ref, o_ref, bias_scr, o_scr, lse_scr,
                 qkv_pad, o_pad, lse_pad):
    h = pl.program_id(0)

    def to_padded(i, carry):
        src_rows = pl.ds(pl.multiple_of(i * AT_PAD_GROUP, AT_PAD_GROUP), AT_PAD_GROUP)
        dst_rows = pl.ds(pl.multiple_of(i * AT_PAD_PITCH, 8), AT_PAD_GROUP)
        for j, ref in enumerate((q_ref, k_ref, v_ref)):
            qkv_pad[j, dst_rows, :] = ref[src_rows, :]
        return carry

    lax.fori_loop(0, SEQ // AT_PAD_GROUP, to_padded, 0, unroll=8)

    @pl.when(pl.program_id(1) == 0)
    def _():
        for g in range(AT_NBRANCH):
            bkt = bkt_ref[g]
            bias = jnp.zeros((AT_BLK, 2 * AT_BLK), F32)
            for b in range(N_BUCKETS):
                bias = jnp.where(bkt == b, rb_ref[b, h], bias)
            bias_scr[g] = jnp.where(bkt < 0, NEG_INF, bias)

    lane = lax.broadcasted_iota(jnp.int32, (AT_BLK, 2 * AT_BLK), 1)
    scale = ATTN_HEAD_DIM ** -0.5
    for g in AT_ORDER:
        dil = DILATED_PAIRS[g][1]
        nb = AT_BLOCKS // dil

        def body(it, carry, g=g, dil=dil, nb=nb):
            padded = dil == AT_PAD_GROUP
            if padded:
                qs, ks, vs = qkv_pad.at[0], qkv_pad.at[1], qkv_pad.at[2]
            else:
                qs, ks, vs = q_ref, k_ref, v_ref

            def block_rows(start):
                if dil == 1:
                    return pl.ds(pl.multiple_of(start, AT_BLK), AT_BLK)
                if padded:
                    return pl.ds(_at_padded_row(start), AT_BLK, stride=AT_PAD_PITCH)
                return pl.ds(start, AT_BLK, stride=dil)

            group = range(AT_GROUP)
            ts = [it * AT_GROUP + i for i in group]
            if nb <= AT_GROUP:
                ns = [i % nb for i in group]
            else:
                ns = [None if i == 0 else i for i in group]
            n0 = ts[0] % nb
            rows = [block_rows(t // nb + (t % nb) * (AT_BLK * dil)) for t in ts]
            q = [(qs[r, :] * scale).astype(BF16) for r in rows]
            k_cur = [ks[r, :].astype(BF16) for r in rows]
            v_cur = [vs[r, :].astype(BF16) for r in rows]
            if ns[0] is None:
                prow = block_rows(ts[0] // nb + jnp.maximum(n0 - 1, 0) * (AT_BLK * dil))
                k_lead, v_lead = ks[prow, :].astype(BF16), vs[prow, :].astype(BF16)
            s, vc = [], []
            for i in group:
                if ns[i] == 0:
                    s.append(_dot_nt(q[i], k_cur[i]) + bias_scr[g, :, AT_BLK:])
                    vc.append(v_cur[i])
                    continue
                k_prev, v_prev = (k_lead, v_lead) if ns[i] is None else (k_cur[i - 1], v_cur[i - 1])
                si = _dot_nt(q[i], jnp.concatenate([k_prev, k_cur[i]], axis=0)) + bias_scr[g]
                if ns[i] is None:
                    si = jnp.where(jnp.logical_or(n0 > 0, lane >= AT_BLK), si, NEG_INF)
                s.append(si)
                vc.append(jnp.concatenate([v_prev, v_cur[i]], axis=0))
            m = [jnp.max(si, axis=1, keepdims=True) for si in s]
            p = [jnp.exp(si - mi) for si, mi in zip(s, m)]
            l = [jnp.sum(pi, axis=1, keepdims=True) for pi in p]
            o = [jnp.dot(pi.astype(BF16), vi, preferred_element_type=F32) / li for pi, vi, li in zip(p, vc, l)]
            lse = [jnp.broadcast_to(mi + jnp.log(li), (AT_BLK, LANES)) for mi, li in zip(m, l)]
            if dil > 1:
                for r, oi, li in zip(rows, o, lse):
                    if padded:
                        o_pad[r, :], lse_pad[r, :] = oi, li
                    else:
                        o_scr[g - 1, r, :], lse_scr[g - 1, r, :] = oi, li
                return carry

            def unpad(ref, start):
                base = (start // AT_PAD_GROUP) * AT_PAD_PITCH
                return jnp.concatenate(
                    [ref[pl.ds(pl.multiple_of(base + j * AT_PAD_PITCH, 8), AT_PAD_GROUP), :]
                     for j in range(AT_BLK // AT_PAD_GROUP)], axis=0)

            for t, r, oi, li in zip(ts, rows, o, lse):
                parked = [(unpad(o_pad, t * AT_BLK), unpad(lse_pad, t * AT_BLK))
                          if DILATED_PAIRS[j][1] == AT_PAD_GROUP else (o_scr[j - 1, r, :], lse_scr[j - 1, r, :])
                          for j in range(1, AT_NBRANCH)]
                top = functools.reduce(jnp.maximum, [lj for _, lj in parked], li)
                wi = jnp.exp(li - top)
                num, den = wi * oi, wi
                for oj, lj in parked:
                    wj = jnp.exp(lj - top)
                    num, den = num + wj * oj, den + wj
                o_ref[r, :] = num / den
            return carry

        lax.fori_loop(0, AT_BLOCKS // AT_GROUP, body, 0)


def _attn_bucket_table():
    qi = jnp.arange(AT_BLK)[:, None]
    kj = jnp.arange(2 * AT_BLK)[None, :]
    rel = AT_BLK + qi - kj
    max_exact = N_BUCKETS // 2
    tables = []
    for window, dil in DILATED_PAIRS:
        dist = jnp.maximum(rel, 0) * dil
        d = jnp.maximum(dist, 1).astype(F32)
        large = max_exact + jnp.log(d / max_exact) / math.log(REL_MAX_DIST / max_exact) * (N_BUCKETS - max_exact)
        large = jnp.minimum(large.astype(jnp.int32), N_BUCKETS - 1)
        bucket = jnp.where(dist < max_exact, dist, large)
        tables.append(jnp.where((rel >= 0) & (rel <= window // dil), bucket, -1))
    return jnp.stack(tables).astype(jnp.int32)


def _attn(proj, rel_bias, bucket_table):
    col = lambda c0: pl.BlockSpec((SEQ, LANES), lambda h, b: (b, c0 + h))
    return pl.pallas_call(
        _attn_kernel,
        grid=(ATTN_HEADS, BATCH),
        in_specs=[
            pl.BlockSpec(memory_space=pltpu.SMEM),
            pl.BlockSpec((AT_NBRANCH, AT_BLK, 2 * AT_BLK), lambda h, b: (0, 0, 0)),
            col(AT_Q_COL0), col(AT_K_COL0), col(AT_V_COL0),
        ],
        out_specs=pl.BlockSpec((SEQ, LANES), lambda h, b: (b, h)),
        out_shape=jax.ShapeDtypeStruct((N_TOK, ATTN_WIDTH), F32),
        scratch_shapes=[
            pltpu.VMEM((AT_NBRANCH, AT_BLK, 2 * AT_BLK), F32),
            pltpu.VMEM((AT_NBRANCH - 2, SEQ, LANES), F32),
            pltpu.VMEM((AT_NBRANCH - 2, SEQ, LANES), F32),
            pltpu.VMEM((3, AT_PAD_ROWS, LANES), F32),
            pltpu.VMEM((AT_PAD_ROWS, LANES), F32),
            pltpu.VMEM((AT_PAD_ROWS, LANES), F32),
        ],
        compiler_params=pltpu.CompilerParams(
            dimension_semantics=("parallel", "arbitrary"), vmem_limit_bytes=VMEM_LIMIT),
        name="dilated_attn",
    )(rel_bias, bucket_table, proj, proj, proj)


DN_TILE = 2 * DN_CHUNK
DN_NTILE = SEQ // DN_TILE
DN_BH = BATCH * DN_HEADS
DN_PAD = 8
DN_GROUP = 16
DN_SCAN_TILES = 4
DN_AKT_ROWS = DN_CHUNK + DN_HEAD_DIM
DN_INV_LEVELS = 6
assert 2 ** DN_INV_LEVELS == DN_CHUNK
DN_QKV_COL0 = (SSM_WIDTH + 3 * ATTN_WIDTH) // LANES
DN_Z_COL0 = DN_QKV_COL0 + 3 * DN_HEADS
DN_AB_COL = DN_Z_COL0 + DN_HEADS
HIGHEST = lax.Precision.HIGHEST


def _dot_nt(a, b):
    return lax.dot_general(a, b, (((1,), (1,)), ((), ())), preferred_element_type=F32)


def _split3(x):
    hi = x.astype(BF16)
    r = x - hi.astype(F32)
    mid = r.astype(BF16)
    return hi, mid, (r - mid.astype(F32)).astype(BF16)


def _dot_exact_lhs(x, m16):
    return sum(jnp.dot(piece, m16, preferred_element_type=F32) for piece in _split3(x))


def _dot_exact_rhs(m16, x):
    return sum(jnp.dot(m16, piece, preferred_element_type=F32) for piece in _split3(x))


def _dn_tile_masks():
    row = lax.broadcasted_iota(jnp.int32, (DN_TILE, DN_TILE), 0)
    col = lax.broadcasted_iota(jnp.int32, (DN_TILE, DN_TILE), 1)
    causal = jnp.logical_and((row // DN_CHUNK) == (col // DN_CHUNK), row >= col)
    return row, col, causal


DN_GATE_ROWS = 1024


def _dn_gates_kernel(ab_ref, alog_ref, dtb_ref, o_ref):
    _, col, causal = _dn_tile_masks()
    cumsum_mat = causal.astype(BF16)
    neg_a = -jnp.exp(alog_ref[...])
    for t in range(DN_GATE_ROWS // DN_TILE):
        rows = slice(t * DN_TILE, (t + 1) * DN_TILE)
        ab = jnp.where(col < 2 * DN_HEADS, ab_ref[rows, :], 0.0)
        x = ab + dtb_ref[...]
        g = neg_a * (jnp.maximum(x, 0.0) + jnp.log1p(jnp.exp(-jnp.abs(x))))
        gc = _dot_exact_rhs(cumsum_mat, g)
        o_ref[rows, :] = jnp.where(col < DN_HEADS, gc, jax.nn.sigmoid(ab))


def _dn_gates(proj, a_log, dt_bias):
    lanes = lambda v: jnp.pad(v, (0, LANES - DN_HEADS)).reshape(1, LANES)
    return pl.pallas_call(
        _dn_gates_kernel,
        grid=(N_TOK // DN_GATE_ROWS,),
        in_specs=[
            pl.BlockSpec((DN_GATE_ROWS, LANES), lambda i: (i, DN_AB_COL)),
            pl.BlockSpec((1, LANES), lambda i: (0, 0)),
            pl.BlockSpec((1, LANES), lambda i: (0, 0)),
        ],
        out_specs=pl.BlockSpec((DN_GATE_ROWS, LANES), lambda i: (i, 0)),
        out_shape=jax.ShapeDtypeStruct((N_TOK, LANES), F32),
        compiler_params=pltpu.CompilerParams(dimension_semantics=("parallel",), vmem_limit_bytes=VMEM_LIMIT),
        name="dn_gates",
    )(proj, lanes(a_log), lanes(dt_bias))


def _dn_prep_kernel(q_ref, k_ref, v_ref, z_ref, gate_ref, wq_ref, wk_ref, wv_ref,
                    u_o, wq_o, akt_o, dec_o, sz_o, qp, kp, vp):
    h = pl.program_id(1)
    for src, dst in ((q_ref, qp), (k_ref, kp), (v_ref, vp)):
        dst[0:DN_PAD, :] = jnp.zeros((DN_PAD, LANES), F32)
        dst[DN_PAD:DN_PAD + SEQ, :] = src[...]

    row, col, causal = _dn_tile_masks()
    pick = jnp.concatenate([row == h, row == h + DN_HEADS], axis=1).astype(BF16)
    eye = (row == col).astype(F32)
    pair_masks = [
        jnp.logical_and(jnp.logical_and((row // (2 * s)) == (col // (2 * s)), (row // s) % 2 == 1),
                        (col // s) % 2 == 0)
        for s in (2 ** i for i in range(DN_INV_LEVELS))]
    scale = DN_HEAD_DIM ** -0.5

    def conv_silu(pad_ref, w_ref, base):
        acc = None
        for j in range(DN_CONV):
            sh = DN_PAD - (DN_CONV - 1) + j
            term = w_ref[j:j + 1, :] * pad_ref[pl.ds(base + sh, DN_TILE), :]
            acc = term if acc is None else acc + term
        return _silu(acc)

    def l2n(x):
        return x * lax.rsqrt(jnp.sum(x * x, axis=1, keepdims=True) + NORM_EPS)

    def mm16(a, b):
        return jnp.dot(a.astype(BF16), b.astype(BF16), preferred_element_type=F32)

    def body(it, carry):
        tiles = [it * DN_GROUP + i for i in range(DN_GROUP)]
        bases = [pl.multiple_of(t * DN_TILE, DN_TILE) for t in tiles]
        q = [l2n(conv_silu(qp, wq_ref, b)) * scale for b in bases]
        k = [l2n(conv_silu(kp, wk_ref, b)) for b in bases]
        v = [conv_silu(vp, wv_ref, b) for b in bases]
        gate_rep = [_dot_exact_lhs(gate_ref[pl.ds(b, DN_TILE), :], pick) for b in bases]
        gc = [r[:, :DN_TILE] for r in gate_rep]
        beta = [r[:, DN_TILE:] for r in gate_rep]
        decay = [jnp.exp(jnp.where(causal, c - c.T, NEG_INF)) for c in gc]
        kb = [ki * bi for ki, bi in zip(k, beta)]
        k16 = [ki.astype(BF16) for ki in k]
        a_mat = [_dot_nt(kbi.astype(BF16), ki) * di for kbi, ki, di in zip(kb, k16, decay)]
        t_inv = [eye - jnp.where(pair_masks[0], am, 0.0) for am in a_mat]
        for mask in pair_masks[1:]:
            low = [jnp.where(mask, am, 0.0).astype(BF16) for am in a_mat]
            t_inv = [ti - mm16(mm16(ti, lo), ti) for ti, lo in zip(t_inv, low)]
        egc = [jnp.exp(c) for c in gc]
        uw = [mm16(ti, jnp.concatenate([vi * bi, kbi * ei], axis=1))
              for ti, vi, bi, kbi, ei in zip(t_inv, v, beta, kb, egc)]
        at = [(_dot_nt(qi.astype(BF16), ki) * di).astype(BF16) for qi, ki, di in zip(q, k16, decay)]
        qd = [(qi * ei).astype(BF16) for qi, ei in zip(q, egc)]
        c = DN_CHUNK
        for i, t in enumerate(tiles):
            rows = pl.ds(bases[i], DN_TILE)
            u_o[t] = uw[i][:, :DN_HEAD_DIM]
            w = uw[i][:, DN_HEAD_DIM:].astype(BF16)
            wq_o[t] = jnp.concatenate([w[:c], qd[i][:c], w[c:], qd[i][c:]], axis=0)
            gc_first, gc_second = gc[i][c - 1:c, :], gc[i][DN_TILE - 1:DN_TILE, :]
            gc_last = jnp.where(row < c, gc_first, gc_second)
            kt = (k[i] * jnp.exp(gc_last - gc[i])).T
            akt_o[t] = jnp.concatenate([at[i][:c] + at[i][c:], kt.astype(BF16)], axis=0)
            dec_o[t] = jnp.concatenate(
                [jnp.exp(gc_first), jnp.exp(gc_second), jnp.zeros((DN_PAD - 2, LANES), F32)], axis=0)
            z = z_ref[rows, :]
            sz_o[t] = _silu(z).astype(BF16)
        return carry

    lax.fori_loop(0, DN_NTILE // DN_GROUP, body, 0)


def _dn_prep(proj, gates, conv_w):
    col = lambda c0: pl.BlockSpec((SEQ, LANES), lambda b, h: (b, c0 + h))
    cw = lambda c0: pl.BlockSpec((DN_CONV, LANES), lambda b, h: (0, c0 + h))
    bh_rows = lambda n: pl.BlockSpec((DN_NTILE, None, n, LANES), lambda b, h: (0, b * DN_HEADS + h, 0, 0))
    rows_shape = lambda n, dt: jax.ShapeDtypeStruct((DN_NTILE, DN_BH, n, LANES), dt)
    return pl.pallas_call(
        _dn_prep_kernel,
        grid=(BATCH, DN_HEADS),
        in_specs=[
            col(DN_QKV_COL0), col(DN_QKV_COL0 + DN_HEADS), col(DN_QKV_COL0 + 2 * DN_HEADS), col(DN_Z_COL0),
            pl.BlockSpec((SEQ, LANES), lambda b, h: (b, 0)),
            cw(0), cw(DN_HEADS), cw(2 * DN_HEADS),
        ],
        out_specs=[bh_rows(DN_TILE), bh_rows(2 * DN_TILE), bh_rows(DN_AKT_ROWS), bh_rows(DN_PAD), bh_rows(DN_TILE)],
        out_shape=[
            rows_shape(DN_TILE, F32), rows_shape(2 * DN_TILE, BF16), rows_shape(DN_AKT_ROWS, BF16),
            rows_shape(DN_PAD, F32), rows_shape(DN_TILE, BF16),
        ],
        scratch_shapes=[pltpu.VMEM((SEQ + DN_PAD, LANES), F32)] * 3,
        compiler_params=pltpu.CompilerParams(
            dimension_semantics=("parallel", "parallel"), vmem_limit_bytes=VMEM_LIMIT),
        name="dn_prep",
    )(proj, proj, proj, proj, gates, conv_w, conv_w, conv_w)


def _dn_scan_kernel(u_ref, wq_ref, akt_ref, dec_ref, sz_ref, gain_ref, o_ref, s_scr):
    @pl.when(pl.program_id(0) == 0)
    def _():
        s_scr[...] = jnp.zeros_like(s_scr)

    gain = gain_ref[...]
    c = DN_CHUNK
    chains = range(DN_BH)
    second_chunk_lane = lax.broadcasted_iota(jnp.int32, (DN_AKT_ROWS, LANES), 1) >= c
    state = [s_scr[bh] for bh in chains]
    v_first = None
    for tt, j in ((tt, j) for tt in range(DN_SCAN_TILES) for j in range(2)):
        rows = slice(j * c, (j + 1) * c)
        out_rows = slice(tt * DN_TILE + j * c, tt * DN_TILE + (j + 1) * c)
        s16 = [s.astype(BF16) for s in state]
        ws_qs = [jnp.dot(wq_ref[tt, bh, j * DN_TILE:(j + 1) * DN_TILE, :], s16[bh],
                         preferred_element_type=F32) for bh in chains]
        v_new = [u_ref[tt, bh, rows, :] - ws_qs[bh][:c] for bh in chains]
        if j == 0:
            v_pair = [jnp.concatenate([v, jnp.zeros_like(v)], axis=0).astype(BF16) for v in v_new]
            v_first = v_new
            akt = [akt_ref[tt, bh] for bh in chains]
        else:
            v_pair = [jnp.concatenate([v0, v], axis=0).astype(BF16) for v0, v in zip(v_first, v_new)]
            akt = [jnp.where(second_chunk_lane, akt_ref[tt, bh], jnp.zeros((DN_AKT_ROWS, LANES), BF16))
                   for bh in chains]
        ov_kv = [jnp.dot(akt[bh], v_pair[bh], preferred_element_type=F32) for bh in chains]
        state = [state[bh] * dec_ref[tt, bh, j:j + 1, :] + ov_kv[bh][c:] for bh in chains]
        for bh in chains:
            b, h = divmod(bh, DN_HEADS)
            o = ws_qs[bh][c:] + ov_kv[bh][:c]
            o_ref[b, out_rows, h * DN_HEAD_DIM:(h + 1) * DN_HEAD_DIM] = (
                _rms(o, gain) * sz_ref[tt, bh, rows, :].astype(F32))
    for bh in chains:
        s_scr[bh] = state[bh]


def _dn_scan(u, wq, akt, dec, sz, gain):
    rows = lambda n: pl.BlockSpec((DN_SCAN_TILES, DN_BH, n, LANES), lambda t: (t, 0, 0, 0))
    return pl.pallas_call(
        _dn_scan_kernel,
        grid=(DN_NTILE // DN_SCAN_TILES,),
        in_specs=[rows(DN_TILE), rows(2 * DN_TILE), rows(DN_AKT_ROWS), rows(DN_PAD), rows(DN_TILE),
                  pl.BlockSpec((1, LANES), lambda t: (0, 0))],
        out_specs=pl.BlockSpec((BATCH, DN_SCAN_TILES * DN_TILE, DN_WIDTH), lambda t: (0, t, 0)),
        out_shape=jax.ShapeDtypeStruct((BATCH, SEQ, DN_WIDTH), F32),
        scratch_shapes=[pltpu.VMEM((DN_BH, DN_HEAD_DIM, DN_HEAD_DIM), F32)],
        compiler_params=pltpu.CompilerParams(
            dimension_semantics=("arbitrary",), vmem_limit_bytes=VMEM_LIMIT),
        name="dn_scan",
    )(u, wq, akt, dec, sz, gain)


def _dn_mixer(proj, p, l):
    gates = _dn_gates(proj, p['dn_a_log'][l], p['dn_dt_bias'][l])
    outs = _dn_prep(proj, gates, p['dn_conv_w'][l])
    y = _dn_scan(*outs, p['dn_norm_gain'][l].reshape(1, DN_HEAD_DIM))
    return y.reshape(N_TOK, DN_WIDTH)


def kernel(x, norm_gains, ffn_w_gate, ffn_w_up, ffn_w_down, w_in, w_out, ssm_lambda_re, ssm_lambda_im,
           ssm_b_re, ssm_b_im, ssm_c_re, ssm_c_im, ssm_d, ssm_log_dt, ssm_glu_w, ssm_glu_b, ssm_out_gain,
           dn_conv_w, dn_a_log, dn_dt_bias, dn_norm_gain, attn_out_gain, rel_bias):
    p = dict(ssm_lambda_re=ssm_lambda_re, ssm_lambda_im=ssm_lambda_im, ssm_b_re=ssm_b_re, ssm_b_im=ssm_b_im,
             ssm_c_re=ssm_c_re, ssm_c_im=ssm_c_im, ssm_d=ssm_d, ssm_log_dt=ssm_log_dt, ssm_glu_w=ssm_glu_w,
             ssm_glu_b=ssm_glu_b, ssm_out_gain=ssm_out_gain, dn_conv_w=dn_conv_w, dn_a_log=dn_a_log,
             dn_dt_bias=dn_dt_bias, dn_norm_gain=dn_norm_gain)
    wg, wu, wd = ffn_w_gate, ffn_w_up, ffn_w_down
    w_in_p = w_in.astype(BF16)
    w_out_b = w_out
    bucket_table = _attn_bucket_table()
    gains = norm_gains.reshape(DEPTH, 6, 1, D_MODEL)
    x = x.reshape(N_TOK, D_MODEL)
    for l in range(DEPTH):
        x = _ffn(x, gains, wg, wu, wd, l, 0)
        proj = _inproj(x, gains, w_in_p, l)
        y_ssm = _s5_mixer(proj, p, l)
        y_dn = _dn_mixer(proj, p, l)
        o_at = _attn(proj, rel_bias, bucket_table)
        x = _outproj(x, y_ssm, y_dn, o_at, attn_out_gain[l].reshape(1, ATTN_WIDTH), gains, w_out_b, l)
        x = _ffn(x, gains, wg, wu, wd, l, 1)
    return x.reshape(BATCH, SEQ, D_MODEL)
```
